```python
import jax, jax.numpy as jnp
from jax import lax
import numpy as np

D_MODEL = 1024
BATCH = 8
SEQ = 4096
DEPTH = 1
DEC_BATCH = 32
DEC_SEQ = 32
PAST_LEN = 4096

CHUNK = 64
GLA_HEADS = 4
GLA_DV = D_MODEL // 2 // GLA_HEADS
GLA_DK = GLA_DV // 2
GLA_GATE_RANK = 16
GLA_TAU = 16.0
GLA_WIDTH = GLA_HEADS * GLA_DV
SWA_HEAD_DIM = 64
SWA_Q_HEADS = D_MODEL // 2 // SWA_HEAD_DIM
SWA_KV_HEADS = 2
SWA_GROUP = SWA_Q_HEADS // SWA_KV_HEADS
SWA_WIDTH = SWA_Q_HEADS * SWA_HEAD_DIM
WINDOW = 128
WINDOW_CHUNKS = WINDOW // CHUNK
MIX_WIDTH = GLA_WIDTH + SWA_WIDTH
N_GROUPS = 4
EXPERTS_PER_GROUP = 8
EXPERT_FF = 256
TOP_K_INNER = 2
DEEPNORM_ALPHA = (2.0 * DEPTH) ** 0.25
DEEPNORM_BETA = (8.0 * DEPTH) ** -0.25
LN_EPS = 1e-5
NEG_INF = -1e30
IN_SPLITS = (GLA_HEADS * GLA_DK, GLA_HEADS * GLA_DK, GLA_WIDTH, GLA_WIDTH, GLA_GATE_RANK,
             SWA_WIDTH, SWA_KV_HEADS * SWA_HEAD_DIM, SWA_KV_HEADS * SWA_HEAD_DIM)
IN_WIDTH = sum(IN_SPLITS)

kernel_name = "hymba_gla_swa_sink_hmoe_deepnorm_adaln_stream_step"


def layer_norm(x, g, b):
    xf = x.astype(jnp.float32)
    mu = jnp.mean(xf, -1, keepdims=True)
    var = jnp.mean(jnp.square(xf - mu), -1, keepdims=True)
    return ((xf - mu) * lax.rsqrt(var + LN_EPS) * g + b).astype(x.dtype)


def alibi_slopes():
    s = 2.0 ** (-(8.0 / SWA_Q_HEADS) * np.arange(1, SWA_Q_HEADS + 1))
    return jnp.asarray(s.astype(np.float32).reshape(SWA_KV_HEADS, SWA_GROUP))


def adaln(c, ada_w, ada_b):
    mod = jax.nn.silu(c) @ ada_w + ada_b
    return [m[:, None, :] for m in jnp.split(mod, 6, axis=-1)]


def in_proj(h, w_in, a2_w, a2_b):
    B, L, _ = h.shape
    cuts = [int(i) for i in np.cumsum(IN_SPLITS)[:-1]]
    gq, gk, gv, gr, ga, sq, sk, sv = jnp.split(h @ w_in, cuts, axis=-1)
    loga = jax.nn.log_sigmoid((ga @ a2_w + a2_b).astype(jnp.float32)) / GLA_TAU
    return (gq.reshape(B, L, GLA_HEADS, GLA_DK) * GLA_DK ** -0.5,
            gk.reshape(B, L, GLA_HEADS, GLA_DK),
            gv.reshape(B, L, GLA_HEADS, GLA_DV),
            gr.reshape(B, L, GLA_HEADS, GLA_DV),
            loga.reshape(B, L, GLA_HEADS, GLA_DK),
            sq.reshape(B, L, SWA_KV_HEADS, SWA_GROUP, SWA_HEAD_DIM),
            sk.reshape(B, L, SWA_KV_HEADS, SWA_HEAD_DIM),
            sv.reshape(B, L, SWA_KV_HEADS, SWA_HEAD_DIM))


def gla_chunk_causal(q, k, v, loga, s0, chunk):
    f32 = jnp.float32
    B, L, H, _ = q.shape
    N = L // chunk
    q = q.astype(f32).reshape(B, N, chunk, H, GLA_DK)
    k = k.astype(f32).reshape(B, N, chunk, H, GLA_DK)
    v = v.astype(f32).reshape(B, N, chunk, H, GLA_DV)
    b = jnp.cumsum(loga.reshape(B, N, chunk, H, GLA_DK), axis=2)
    eb, ebn = jnp.exp(b), jnp.exp(-b)
    a_lo = jnp.einsum('bnthk,bnshk->bnhts', q * eb, k * ebn)
    a_up = jnp.einsum('bnthk,bnshk->bnhts', q * ebn, k * eb)
    pos = jnp.arange(chunk)
    a = jnp.where(pos[:, None] >= pos[None, :], a_lo, a_up)
    o = jnp.einsum('bnhts,bnshv->bnthv', a, v)
    b_end = b[:, :, -1]
    u = jnp.einsum('bnshk,bnshv->bnhkv', k * jnp.exp(b_end[:, :, None] - b), v)

    def step(s, gu):
        g, uc = gu
        return g[..., None] * s + uc, s

    s_fin, s_prev = lax.scan(step, s0.astype(f32),
                             (jnp.moveaxis(jnp.exp(b_end), 1, 0), jnp.moveaxis(u, 1, 0)))
    s_prev = jnp.moveaxis(s_prev, 0, 1)
    o = o + jnp.einsum('bnthk,bnhkv->bnthv', q * eb, s_prev)
    return o.reshape(B, L, H, GLA_DV), s_fin


def sink_attention(q, k, v, dist, valid, slopes, sinks):
    s = jnp.einsum('bnqhgd,bnkhd->bnhgqk', q, k).astype(jnp.float32) * SWA_HEAD_DIM ** -0.5
    s = s - slopes[:, :, None, None] * dist[None, None]
    s = jnp.where(valid[None, :, None, None], s, NEG_INF)
    sink = jnp.broadcast_to(sinks[:, :, None, None], s.shape[:-1] + (1,))
    p = jax.nn.softmax(jnp.concatenate([s, sink], axis=-1), axis=-1)[..., :-1]
    return jnp.einsum('bnhgqk,bnkhd->bnqhgd', p.astype(v.dtype), v)


def swa_prompt(sq, sk, sv, slopes, sinks):
    B, L = sq.shape[:2]
    N = L // CHUNK
    pad = WINDOW_CHUNKS * CHUNK
    kp = jnp.pad(sk, ((0, 0), (pad, 0), (0, 0), (0, 0))).reshape(B, N + WINDOW_CHUNKS, CHUNK, SWA_KV_HEADS, SWA_HEAD_DIM)
    vp = jnp.pad(sv, ((0, 0), (pad, 0), (0, 0), (0, 0))).reshape(B, N + WINDOW_CHUNKS, CHUNK, SWA_KV_HEADS, SWA_HEAD_DIM)
    kband = jnp.concatenate([kp[:, w:w + N] for w in range(WINDOW_CHUNKS + 1)], axis=2)
    vband = jnp.concatenate([vp[:, w:w + N] for w in range(WINDOW_CHUNKS + 1)], axis=2)
    qi = jnp.arange(CHUNK)[:, None]
    kj = jnp.arange((WINDOW_CHUNKS + 1) * CHUNK)[None, :]
    dist = jnp.abs(qi + pad - kj).astype(jnp.float32)
    valid = (jnp.arange(N)[:, None, None] * CHUNK - pad + kj[None]) >= 0
    qb = sq.reshape(B, N, CHUNK, SWA_KV_HEADS, SWA_GROUP, SWA_HEAD_DIM)
    o = sink_attention(qb, kband, vband, dist, valid, slopes, sinks)
    return o.reshape(B, L, SWA_KV_HEADS, SWA_GROUP, SWA_HEAD_DIM)


def swa_sample(sq, sk, sv, k_cache, v_cache, slopes, sinks):
    S = sq.shape[1]
    Lc = k_cache.shape[1]
    k_all = jnp.concatenate([k_cache.astype(sk.dtype), sk], axis=1)
    v_all = jnp.concatenate([v_cache.astype(sv.dtype), sv], axis=1)
    dist = jnp.abs(jnp.arange(S)[:, None] + Lc - jnp.arange(Lc + S)[None, :]).astype(jnp.float32)
    valid = jnp.ones((1, 1, Lc + S), dtype=bool)
    o = sink_attention(sq[:, None], k_all[:, None], v_all[:, None], dist, valid, slopes, sinks)[:, 0]
    return o, k_all[:, -Lc:], v_all[:, -Lc:]


def mixer_out(o_gla, gr, gla_norm_w, o_swa, w_o, dtype):
    B, L = o_gla.shape[:2]
    og = o_gla * lax.rsqrt(jnp.mean(jnp.square(o_gla), -1, keepdims=True) + LN_EPS) * gla_norm_w
    og = og * jax.nn.silu(gr.astype(jnp.float32))
    mixed = jnp.concatenate([og.reshape(B, L, GLA_WIDTH).astype(dtype),
                             o_swa.reshape(B, L, SWA_WIDTH).astype(dtype)], axis=-1)
    return mixed @ w_o


def hier_moe(h, rg_w, rg_b, re_w, re_b, w_gate, w_up, w_down):
    B, L, D = h.shape
    t = h.reshape(-1, D)
    g_logits = (t @ rg_w + rg_b).astype(jnp.float32)
    g_prob = jax.nn.softmax(g_logits, axis=-1)
    grp = jnp.argmax(g_logits, axis=-1)
    p_grp = jnp.take_along_axis(g_prob, grp[:, None], axis=-1)
    e_logits = (jnp.einsum('td,gde->tge', t, re_w) + re_b).astype(jnp.float32)
    e_sel = jnp.take_along_axis(e_logits, grp[:, None, None], axis=1)[:, 0]
    top_v, top_i = lax.top_k(e_sel, TOP_K_INNER)
    top_w = jax.nn.softmax(top_v, axis=-1) * p_grp
    exp_w = jnp.sum(jax.nn.one_hot(top_i, EXPERTS_PER_GROUP, dtype=jnp.float32) * top_w[..., None], axis=1)
    combine = (jax.nn.one_hot(grp, N_GROUPS, dtype=jnp.float32)[:, :, None] * exp_w[:, None, :]).astype(t.dtype)
    y = jnp.zeros_like(t)
    for g in range(N_GROUPS):
        hid = jax.nn.silu(jnp.einsum('td,edf->tef', t, w_gate[g])) * jnp.einsum('td,edf->tef', t, w_up[g])
        y = y + jnp.einsum('tef,efd->td', hid * combine[:, g, :, None], w_down[g])
    return y.reshape(B, L, D)


def trunk_layer(x, c, gla_s0, k_cache, v_cache, slopes, p):
    (ada_w, ada_b, w_in, a2_w, a2_b, gla_norm_w, sinks, w_o, ln1_g, ln1_b,
     rg_w, rg_b, re_w, re_b, w_gate, w_up, w_down, ln2_g, ln2_b) = p
    sh1, sc1, gt1, sh2, sc2, gt2 = adaln(c, ada_w, ada_b)
    h = x * (1 + sc1) + sh1
    gq, gk, gv, gr, loga, sq, sk, sv = in_proj(h, w_in, a2_w, a2_b)
    sinks = sinks.reshape(SWA_KV_HEADS, SWA_GROUP).astype(jnp.float32)
    if k_cache is None:
        s0 = jnp.zeros((x.shape[0], GLA_HEADS, GLA_DK, GLA_DV), jnp.float32)
        o_gla, s_new = gla_chunk_causal(gq, gk, gv, loga, s0, CHUNK)
        o_swa = swa_prompt(sq, sk, sv, slopes, sinks)
        k_new, v_new = sk[:, -WINDOW:], sv[:, -WINDOW:]
    else:
        o_gla, s_new = gla_chunk_causal(gq, gk, gv, loga, gla_s0, x.shape[1])
        o_swa, k_new, v_new = swa_sample(sq, sk, sv, k_cache, v_cache, slopes, sinks)
    mix = mixer_out(o_gla, gr, gla_norm_w, o_swa, w_o, x.dtype)
    x = layer_norm(DEEPNORM_ALPHA * x + gt1 * mix, ln1_g, ln1_b)
    h2 = x * (1 + sc2) + sh2
    x = layer_norm(DEEPNORM_ALPHA * x + gt2 * hier_moe(h2, rg_w, rg_b, re_w, re_b, w_gate, w_up, w_down), ln2_g, ln2_b)
    return x, s_new.astype(x.dtype), k_new, v_new


def setup_inputs(seed: int = 0) -> dict:
    key = jax.random.key(seed)
    ks = iter(jax.random.split(key, 40))
    nrm = lambda shape, scale=1.0: jax.random.normal(next(ks), shape, jnp.float32) * scale
    D = D_MODEL
    Lc = min(WINDOW, PAST_LEN)
    col_scale = np.concatenate([np.full(n, s, np.float32) for n, s in zip(
        IN_SPLITS, (1.0, 1.0, DEEPNORM_BETA, 1.0, 1.0, 1.0, 1.0, DEEPNORM_BETA))])
    return {
        "x_prompt": nrm((BATCH, SEQ, D)),
        "x_sample": nrm((DEC_BATCH, DEC_SEQ, D)),
        "c_prompt": nrm((BATCH, D)),
        "c_sample": nrm((DEC_BATCH, D)),
        "state_gla": nrm((DEPTH, DEC_BATCH, GLA_HEADS, GLA_DK, GLA_DV)),
        "cache_swa_k": nrm((DEPTH, DEC_BATCH, Lc, SWA_KV_HEADS, SWA_HEAD_DIM)),
        "cache_swa_v": nrm((DEPTH, DEC_BATCH, Lc, SWA_KV_HEADS, SWA_HEAD_DIM), DEEPNORM_BETA),
        "ada_w": nrm((DEPTH, D, 6 * D), D ** -0.5),
        "ada_b": nrm((DEPTH, 6 * D), 0.02),
        "w_in": nrm((DEPTH, D, IN_WIDTH), D ** -0.5) * jnp.asarray(col_scale),
        "gla_a2_w": nrm((DEPTH, GLA_GATE_RANK, GLA_HEADS * GLA_DK), GLA_GATE_RANK ** -0.5),
        "gla_a2_b": nrm((DEPTH, GLA_HEADS * GLA_DK), 0.1),
        "gla_norm_w": 1.0 + nrm((DEPTH, GLA_DV), 0.02),
        "swa_sinks": nrm((DEPTH, SWA_Q_HEADS), 0.5),
        "w_o": nrm((DEPTH, MIX_WIDTH, D), MIX_WIDTH ** -0.5 * DEEPNORM_BETA),
        "ln1_g": 1.0 + nrm((DEPTH, D), 0.02),
        "ln1_b": nrm((DEPTH, D), 0.02),
        "router_g_w": nrm((DEPTH, D, N_GROUPS), D ** -0.5),
        "router_g_b": nrm((DEPTH, N_GROUPS), 0.01),
        "router_e_w": nrm((DEPTH, N_GROUPS, D, EXPERTS_PER_GROUP), D ** -0.5),
        "router_e_b": nrm((DEPTH, N_GROUPS, EXPERTS_PER_GROUP), 0.01),
        "moe_w_gate": nrm((DEPTH, N_GROUPS, EXPERTS_PER_GROUP, D, EXPERT_FF), D ** -0.5 * DEEPNORM_BETA),
        "moe_w_up": nrm((DEPTH, N_GROUPS, EXPERTS_PER_GROUP, D, EXPERT_FF), D ** -0.5 * DEEPNORM_BETA),
        "moe_w_down": nrm((DEPTH, N_GROUPS, EXPERTS_PER_GROUP, EXPERT_FF, D), EXPERT_FF ** -0.5 * DEEPNORM_BETA),
        "ln2_g": 1.0 + nrm((DEPTH, D), 0.02),
        "ln2_b": nrm((DEPTH, D), 0.02),
    }


def reference(x_prompt, x_sample, c_prompt, c_sample, state_gla, cache_swa_k, cache_swa_v,
              ada_w, ada_b, w_in, gla_a2_w, gla_a2_b, gla_norm_w, swa_sinks, w_o, ln1_g, ln1_b,
              router_g_w, router_g_b, router_e_w, router_e_b, moe_w_gate, moe_w_up, moe_w_down,
              ln2_g, ln2_b):
    slopes = alibi_slopes()
    xp, xs = x_prompt, x_sample
    gla_p, kp, vp, gla_s, ksn, vsn = [], [], [], [], [], []
    for l in range(DEPTH):
        p = (ada_w[l], ada_b[l], w_in[l], gla_a2_w[l], gla_a2_b[l], gla_norm_w[l], swa_sinks[l], w_o[l],
             ln1_g[l], ln1_b[l], router_g_w[l], router_g_b[l], router_e_w[l], router_e_b[l],
             moe_w_gate[l], moe_w_up[l], moe_w_down[l], ln2_g[l], ln2_b[l])
        xp, s_p, k_p, v_p = trunk_layer(xp, c_prompt, None, None, None, slopes, p)
        xs, s_s, k_s, v_s = trunk_layer(xs, c_sample, state_gla[l], cache_swa_k[l], cache_swa_v[l], slopes, p)
        gla_p.append(s_p); kp.append(k_p); vp.append(v_p)
        gla_s.append(s_s); ksn.append(k_s); vsn.append(v_s)
    return (xp, xs, jnp.stack(gla_p), jnp.stack(kp), jnp.stack(vp),
            jnp.stack(gla_s), jnp.stack(ksn), jnp.stack(vsn))
```

```python
import functools

import jax
import jax.numpy as jnp
import numpy as np
from jax import lax
from jax.experimental import pallas as pl
from jax.experimental.pallas import tpu as pltpu

F32 = jnp.float32
BF16 = jnp.bfloat16

D_MODEL = 1024
CHUNK = 64
GLA_HEADS = 4
GLA_DK = 64
GLA_DV = 128
GLA_RANK = 16
GLA_TAU = 16.0
SWA_Q_HEADS = 8
SWA_KV_HEADS = 2
SWA_GROUP = 4
SWA_DH = 64
WINDOW = 128
N_GROUPS = 4
EPG = 8
EXPERT_FF = 256
DEEPNORM_ALPHA = 2.0 ** 0.25
LN_EPS = 1e-5
NEG_INF = -1e30

C_GQ, C_GK, C_GV, C_GR, C_SQ, C_SK, C_SV, C_GA = 0, 256, 512, 1024, 1536, 2048, 2176, 2304
PROJ_W = 2432
LANE = 128
R_EXP0 = 8

VMEM_LIMIT = 56 * 1024 * 1024


def _mm(a, b):
    return jnp.dot(a, b, preferred_element_type=F32)


def _mm_nt(a, b):
    return lax.dot_general(a, b, (((1,), (1,)), ((), ())), preferred_element_type=F32)


def _mm_tn(a, b):
    return lax.dot_general(a, b, (((0,), (0,)), ((), ())), preferred_element_type=F32)


def _split_bf16(a):
    hi = a.astype(BF16)
    lo = (a - hi.astype(F32)).astype(BF16)
    return hi, lo


def _sigmoid(x):
    return 1.0 / (1.0 + jnp.exp(-x))


def _layer_norm(y, g, b):
    mu = jnp.mean(y, axis=-1, keepdims=True)
    d = y - mu
    var = jnp.mean(d * d, axis=-1, keepdims=True)
    return d * lax.rsqrt(var + LN_EPS) * g + b


def _adaln_kernel(c_ref, w_ref, b_ref, o_ref):
    c = c_ref[...]
    a = c * _sigmoid(c)
    a_hi, a_lo = _split_bf16(a)
    w_hi, w_lo = _split_bf16(w_ref[...])
    o_ref[...] = _mm(a_hi, w_hi) + (_mm(a_hi, w_lo) + _mm(a_lo, w_hi)) + b_ref[...]


def _adaln(c_all, ada_w, ada_b):
    n = c_all.shape[0]
    bn = 1024
    return pl.pallas_call(
        _adaln_kernel,
        out_shape=jax.ShapeDtypeStruct((n, 6 * D_MODEL), F32),
        grid=(6 * D_MODEL // bn,),
        in_specs=[pl.BlockSpec((n, D_MODEL), lambda j: (0, 0)),
                  pl.BlockSpec((D_MODEL, bn), lambda j: (0, j)),
                  pl.BlockSpec((1, bn), lambda j: (0, j))],
        out_specs=pl.BlockSpec((n, bn), lambda j: (0, j)),
        compiler_params=pltpu.CompilerParams(dimension_semantics=("arbitrary",), vmem_limit_bytes=VMEM_LIMIT),
        name="adaln",
    )(c_all, ada_w, ada_b)


def _gla_chunk(q, k, v, la, st_ref, tri_bf, causal):
    c = q.shape[0]
    la_hi, la_lo = _split_bf16(la)
    b = _mm(tri_bf, la_hi) + _mm(tri_bf, la_lo)
    eb = jnp.exp(b)
    ebn = jnp.exp(-b)
    b_end = b[c - 1:c, :]
    wk = jnp.exp(b_end - b)
    g_end = jnp.exp(b_end)
    qe = (q * eb).astype(BF16)
    qn = (q * ebn).astype(BF16)
    ke = (k * eb).astype(BF16)
    kn = (k * ebn).astype(BF16)
    kw = (k * wk).astype(BF16)
    vb = v.astype(BF16)
    outs = []
    for h in range(GLA_HEADS):
        ks = slice(h * GLA_DK, (h + 1) * GLA_DK)
        vs = slice(h * GLA_DV, (h + 1) * GLA_DV)
        a_lo = _mm_nt(qe[:, ks], kn[:, ks])
        a_up = _mm_nt(qn[:, ks], ke[:, ks])
        a = jnp.where(causal, a_lo, a_up).astype(BF16)
        st = st_ref[h]
        o = _mm(a, vb[:, vs]) + _mm_nt(qe[:, ks], st.astype(BF16))
        ut = _mm_tn(vb[:, vs], kw[:, ks])
        st_ref[h] = g_end[:, ks] * st + ut
        outs.append(o)
    return outs


def _gla_post(outs, gr, gnw):
    res = []
    for h in range(GLA_HEADS):
        o = outs[h]
        ms = jnp.mean(o * o, axis=-1, keepdims=True)
        og = o * lax.rsqrt(ms + LN_EPS) * gnw
        r = gr[:, h * GLA_DV:(h + 1) * GLA_DV]
        res.append((og * (r * _sigmoid(r))).astype(BF16))
    return res


def _swa_chunk(sq, kwin, vwin, dist, valid, sinks_ref):
    qb = (sq * (SWA_DH ** -0.5)).astype(BF16)
    kb = kwin.astype(BF16)
    vb = vwin.astype(BF16)
    res = []
    for hk in range(SWA_KV_HEADS):
        kh = kb[:, hk * SWA_DH:(hk + 1) * SWA_DH]
        vh = vb[:, hk * SWA_DH:(hk + 1) * SWA_DH]
        for g in range(SWA_GROUP):
            hq = hk * SWA_GROUP + g
            slope = float(2.0 ** (-(hq + 1)))
            s = _mm_nt(qb[:, hq * SWA_DH:(hq + 1) * SWA_DH], kh) - slope * dist
            if valid is not None:
                s = jnp.where(valid, s, NEG_INF)
            sink = sinks_ref[hq]
            m = jnp.maximum(jnp.max(s, axis=-1, keepdims=True), sink)
            p = jnp.exp(s - m)
            denom = jnp.sum(p, axis=-1, keepdims=True) + jnp.exp(sink - m)
            o = _mm(p.astype(BF16), vh) / denom
            res.append(o.astype(BF16))
    return res


def _in_proj(x, mod, win_ref, a2w_ref, a2b_ref, proj_ref, loga_ref):
    sh1 = mod[:, 0:D_MODEL]
    sc1 = mod[:, D_MODEL:2 * D_MODEL]
    h = (x * (1.0 + sc1) + sh1).astype(BF16)
    proj_ref[...] = _mm(h, win_ref[...])
    ga = proj_ref[:, C_GA:C_GA + LANE].astype(BF16)
    z = _mm(ga, a2w_ref[...]) + a2b_ref[...]
    lsig = -(jnp.maximum(-z, 0.0) + jnp.log(1.0 + jnp.exp(-jnp.abs(z))))
    loga_ref[...] = lsig * (1.0 / GLA_TAU)


def _out_proj_ln(x, mod, mixed_ref, wo_ref, ln_g, ln_b):
    gt1 = mod[:, 2 * D_MODEL:3 * D_MODEL]
    mix = _mm(mixed_ref[...], wo_ref[...])
    return _layer_norm(DEEPNORM_ALPHA * x + gt1 * mix, ln_g, ln_b)


def _tri(c):
    r = lax.broadcasted_iota(jnp.int32, (c, c), 0)
    s = lax.broadcasted_iota(jnp.int32, (c, c), 1)
    return r >= s


def _mixer_prompt_kernel(sinks_ref, x_ref, mod_ref, win_ref, a2w_ref, a2b_ref, gnw_ref, wo_ref, lng_ref, lnb_ref,
                         x1_ref, s_out_ref, k_out_ref, v_out_ref,
                         proj_ref, loga_ref, kbuf, vbuf, mixed_ref, st_ref, *, tl):
    j = pl.program_id(1)
    nj = pl.num_programs(1)
    nkeys = WINDOW + CHUNK

    @pl.when(j == 0)
    def _():
        st_ref[...] = jnp.zeros_like(st_ref)
        kbuf[0:WINDOW, :] = jnp.zeros((WINDOW, LANE), F32)
        vbuf[0:WINDOW, :] = jnp.zeros((WINDOW, LANE), F32)

    x = x_ref[0]
    mod = mod_ref[0]
    _in_proj(x, mod, win_ref, a2w_ref, a2b_ref, proj_ref, loga_ref)
    kbuf[WINDOW:WINDOW + tl, :] = proj_ref[:, C_SK:C_SK + LANE]
    vbuf[WINDOW:WINDOW + tl, :] = proj_ref[:, C_SV:C_SV + LANE]

    causal = _tri(CHUNK)
    tri_bf = causal.astype(F32).astype(BF16)
    qi = lax.broadcasted_iota(jnp.int32, (CHUNK, nkeys), 0)
    kj = lax.broadcasted_iota(jnp.int32, (CHUNK, nkeys), 1)
    dist = jnp.abs(qi + WINDOW - kj).astype(F32)
    gnw = gnw_ref[...]

    def body(c, carry):
        r = pl.multiple_of(c * CHUNK, CHUNK)
        rows = pl.ds(r, CHUNK)
        q = proj_ref[rows, C_GQ:C_GQ + 256] * (GLA_DK ** -0.5)
        k = proj_ref[rows, C_GK:C_GK + 256]
        v = proj_ref[rows, C_GV:C_GV + 512]
        gr = proj_ref[rows, C_GR:C_GR + 512]
        la = loga_ref[rows, :]
        outs = _gla_chunk(q, k, v, la, st_ref, tri_bf, causal)
        og = _gla_post(outs, gr, gnw)
        for h in range(GLA_HEADS):
            mixed_ref[rows, h * GLA_DV:(h + 1) * GLA_DV] = og[h]
        sq = proj_ref[rows, C_SQ:C_SQ + 512]
        kwin = kbuf[pl.ds(r, nkeys), :]
        vwin = vbuf[pl.ds(r, nkeys), :]
        valid = kj >= (WINDOW - (j * tl + r))
        os_ = _swa_chunk(sq, kwin, vwin, dist, valid, sinks_ref)
        for hp in range(SWA_Q_HEADS // 2):
            mixed_ref[rows, 512 + hp * LANE:512 + (hp + 1) * LANE] = jnp.concatenate(
                [os_[2 * hp], os_[2 * hp + 1]], axis=-1)
        return carry

    lax.fori_loop(0, tl // CHUNK, body, 0)

    x1_ref[0] = _out_proj_ln(x, mod, mixed_ref, wo_ref, lng_ref[...], lnb_ref[...])

    kbuf[0:WINDOW, :] = kbuf[tl:tl + WINDOW, :]
    vbuf[0:WINDOW, :] = vbuf[tl:tl + WINDOW, :]

    @pl.when(j == nj - 1)
    def _():
        for h in range(GLA_HEADS):
            s_out_ref[0, h] = st_ref[h].T
        k_out_ref[0] = kbuf[0:WINDOW, :]
        v_out_ref[0] = vbuf[0:WINDOW, :]


def _mixer_prompt(x, mod, sinks, win, a2w, a2b, gnw, wo, lng, lnb, *, tl=512):
    b, l, d = x.shape
    nj = l // tl
    const2 = lambda i, j, s: (0, 0)
    grid_spec = pltpu.PrefetchScalarGridSpec(
        num_scalar_prefetch=1,
        grid=(b, nj),
        in_specs=[
            pl.BlockSpec((1, tl, d), lambda i, j, s: (i, j, 0)),
            pl.BlockSpec((1, 1, 6 * d), lambda i, j, s: (i, 0, 0)),
            pl.BlockSpec((d, PROJ_W), const2),
            pl.BlockSpec((LANE, 256), const2),
            pl.BlockSpec((1, 256), const2),
            pl.BlockSpec((1, GLA_DV), const2),
            pl.BlockSpec((d, d), const2),
            pl.BlockSpec((1, d), const2),
            pl.BlockSpec((1, d), const2),
        ],
        out_specs=[
            pl.BlockSpec((1, tl, d), lambda i, j, s: (i, j, 0)),
            pl.BlockSpec((1, GLA_HEADS, GLA_DK, GLA_DV), lambda i, j, s: (i, 0, 0, 0)),
            pl.BlockSpec((1, WINDOW, LANE), lambda i, j, s: (i, 0, 0)),
            pl.BlockSpec((1, WINDOW, LANE), lambda i, j, s: (i, 0, 0)),
        ],
        scratch_shapes=[
            pltpu.VMEM((tl, PROJ_W), F32),
            pltpu.VMEM((tl, 256), F32),
            pltpu.VMEM((WINDOW + tl, LANE), F32),
            pltpu.VMEM((WINDOW + tl, LANE), F32),
            pltpu.VMEM((tl, d), BF16),
            pltpu.VMEM((GLA_HEADS, GLA_DV, GLA_DK), F32),
        ],
    )
    return pl.pallas_call(
        functools.partial(_mixer_prompt_kernel, tl=tl),
        out_shape=[
            jax.ShapeDtypeStruct((b, l, d), F32),
            jax.ShapeDtypeStruct((b, GLA_HEADS, GLA_DK, GLA_DV), F32),
            jax.ShapeDtypeStruct((b, WINDOW, LANE), F32),
            jax.ShapeDtypeStruct((b, WINDOW, LANE), F32),
        ],
        grid_spec=grid_spec,
        compiler_params=pltpu.CompilerParams(dimension_semantics=("arbitrary", "arbitrary"),
                                             vmem_limit_bytes=VMEM_LIMIT),
        name="mixer_prompt",
    )(sinks, x, mod, win, a2w, a2b, gnw, wo, lng, lnb)


def _mixer_sample_kernel(sinks_ref, x_ref, mod_ref, s0_ref, kc_ref, vc_ref, win_ref, a2w_ref, a2b_ref, gnw_ref,
                         wo_ref, lng_ref, lnb_ref,
                         x1_ref, s_out_ref, k_out_ref, v_out_ref,
                         proj_ref, loga_ref, kbuf, vbuf, mixed_ref, st_ref, xm_ref, *, nb, s):
    nkeys = WINDOW + s
    d = D_MODEL
    for bb in range(nb):
        m = mod_ref[bb]
        xm_ref[bb * s:(bb + 1) * s, :] = x_ref[bb] * (1.0 + m[:, d:2 * d]) + m[:, 0:d]
    h = xm_ref[...].astype(BF16)
    proj_ref[...] = _mm(h, win_ref[...])
    ga = proj_ref[:, C_GA:C_GA + LANE].astype(BF16)
    z = _mm(ga, a2w_ref[...]) + a2b_ref[...]
    lsig = -(jnp.maximum(-z, 0.0) + jnp.log(1.0 + jnp.exp(-jnp.abs(z))))
    loga_ref[...] = lsig * (1.0 / GLA_TAU)

    causal = _tri(s)
    tri_bf = causal.astype(F32).astype(BF16)
    qi = lax.broadcasted_iota(jnp.int32, (s, nkeys), 0)
    kj = lax.broadcasted_iota(jnp.int32, (s, nkeys), 1)
    dist = jnp.abs(qi + WINDOW - kj).astype(F32)
    gnw = gnw_ref[...]

    def body(bb, carry):
        r = pl.multiple_of(bb * s, s)
        rows = pl.ds(r, s)
        for hh in range(GLA_HEADS):
            st_ref[hh] = s0_ref[bb, hh].T
        q = proj_ref[rows, C_GQ:C_GQ + 256] * (GLA_DK ** -0.5)
        k = proj_ref[rows, C_GK:C_GK + 256]
        v = proj_ref[rows, C_GV:C_GV + 512]
        gr = proj_ref[rows, C_GR:C_GR + 512]
        la = loga_ref[rows, :]
        outs = _gla_chunk(q, k, v, la, st_ref, tri_bf, causal)
        og = _gla_post(outs, gr, gnw)
        for hh in range(GLA_HEADS):
            mixed_ref[rows, hh * GLA_DV:(hh + 1) * GLA_DV] = og[hh]
            s_out_ref[bb, hh] = st_ref[hh].T
        kbuf[0:WINDOW, :] = kc_ref[bb]
        vbuf[0:WINDOW, :] = vc_ref[bb]
        kbuf[WINDOW:nkeys, :] = proj_ref[rows, C_SK:C_SK + LANE]
        vbuf[WINDOW:nkeys, :] = proj_ref[rows, C_SV:C_SV + LANE]
        sq = proj_ref[rows, C_SQ:C_SQ + 512]
        os_ = _swa_chunk(sq, kbuf[...], vbuf[...], dist, None, sinks_ref)
        for hp in range(SWA_Q_HEADS // 2):
            mixed_ref[rows, 512 + hp * LANE:512 + (hp + 1) * LANE] = jnp.concatenate(
                [os_[2 * hp], os_[2 * hp + 1]], axis=-1)
        k_out_ref[bb] = kbuf[s:nkeys, :]
        v_out_ref[bb] = vbuf[s:nkeys, :]
        return carry

    lax.fori_loop(0, nb, body, 0)

    mix = _mm(mixed_ref[...], wo_ref[...])
    lng = lng_ref[...]
    lnb = lnb_ref[...]
    for bb in range(nb):
        m = mod_ref[bb]
        y = DEEPNORM_ALPHA * x_ref[bb] + m[:, 2 * d:3 * d] * mix[bb * s:(bb + 1) * s, :]
        x1_ref[bb] = _layer_norm(y, lng, lnb)


def _mixer_sample(x, mod, s0, kc, vc, sinks, win, a2w, a2b, gnw, wo, lng, lnb, *, nb=8):
    b, s, d = x.shape
    const2 = lambda i, sk: (0, 0)
    rows = nb * s
    grid_spec = pltpu.PrefetchScalarGridSpec(
        num_scalar_prefetch=1,
        grid=(b // nb,),
        in_specs=[
            pl.BlockSpec((nb, s, d), lambda i, sk: (i, 0, 0)),
            pl.BlockSpec((nb, 1, 6 * d), lambda i, sk: (i, 0, 0)),
            pl.BlockSpec((nb, GLA_HEADS, GLA_DK, GLA_DV), lambda i, sk: (i, 0, 0, 0)),
            pl.BlockSpec((nb, WINDOW, LANE), lambda i, sk: (i, 0, 0)),
            pl.BlockSpec((nb, WINDOW, LANE), lambda i, sk: (i, 0, 0)),
            pl.BlockSpec((d, PROJ_W), const2),
            pl.BlockSpec((LANE, 256), const2),
            pl.BlockSpec((1, 256), const2),
            pl.BlockSpec((1, GLA_DV), const2),
            pl.BlockSpec((d, d), const2),
            pl.BlockSpec((1, d), const2),
            pl.BlockSpec((1, d), const2),
        ],
        out_specs=[
            pl.BlockSpec((nb, s, d), lambda i, sk: (i, 0, 0)),
            pl.BlockSpec((nb, GLA_HEADS, GLA_DK, GLA_DV), lambda i, sk: (i, 0, 0, 0)),
            pl.BlockSpec((nb, WINDOW, LANE), lambda i, sk: (i, 0, 0)),
            pl.BlockSpec((nb, WINDOW, LANE), lambda i, sk: (i, 0, 0)),
        ],
        scratch_shapes=[
            pltpu.VMEM((rows, PROJ_W), F32),
            pltpu.VMEM((rows, 256), F32),
            pltpu.VMEM((WINDOW + s, LANE), F32),
            pltpu.VMEM((WINDOW + s, LANE), F32),
            pltpu.VMEM((rows, d), BF16),
            pltpu.VMEM((GLA_HEADS, GLA_DV, GLA_DK), F32),
            pltpu.VMEM((rows, d), F32),
        ],
    )
    return pl.pallas_call(
        functools.partial(_mixer_sample_kernel, nb=nb, s=s),
        out_shape=[
            jax.ShapeDtypeStruct((b, s, d), F32),
            jax.ShapeDtypeStruct((b, GLA_HEADS, GLA_DK, GLA_DV), F32),
            jax.ShapeDtypeStruct((b, WINDOW, LANE), F32),
            jax.ShapeDtypeStruct((b, WINDOW, LANE), F32),
        ],
        grid_spec=grid_spec,
        compiler_params=pltpu.CompilerParams(dimension_semantics=("arbitrary",),
                                             vmem_limit_bytes=VMEM_LIMIT),
        name="mixer_sample",
    )(sinks, x, mod, s0, kc, vc, win, a2w, a2b, gnw, wo, lng, lnb)


def _route(logits):
    t = logits.shape[0]
    lane = lax.broadcasted_iota(jnp.int32, (t, LANE), 1)
    lane_f = lane.astype(F32)
    big = float(LANE)
    gmask = lane < N_GROUPS
    gl = jnp.where(gmask, logits, -jnp.inf)
    gmax = jnp.max(gl, axis=-1, keepdims=True)
    grp = jnp.min(jnp.where(gl == gmax, lane_f, big), axis=-1, keepdims=True)
    p_grp = 1.0 / jnp.sum(jnp.where(gmask, jnp.exp(logits - gmax), 0.0), axis=-1, keepdims=True)
    lo = R_EXP0 + EPG * grp
    emask = (lane_f >= lo) & (lane_f < lo + EPG)
    el = jnp.where(emask, logits, -jnp.inf)
    v1 = jnp.max(el, axis=-1, keepdims=True)
    i1 = jnp.min(jnp.where(el == v1, lane_f, big), axis=-1, keepdims=True)
    el2 = jnp.where(lane_f == i1, -jnp.inf, el)
    v2 = jnp.max(el2, axis=-1, keepdims=True)
    i2 = jnp.min(jnp.where(el2 == v2, lane_f, big), axis=-1, keepdims=True)
    e2 = jnp.exp(v2 - v1)
    w1 = p_grp / (1.0 + e2)
    w2 = p_grp * e2 / (1.0 + e2)
    return jnp.where(lane_f == i1, w1, 0.0) + jnp.where(lane_f == i2, w2, 0.0)


def _moe_kernel(x1_ref, mod_ref, wr_ref, br_ref, wg_ref, wu_ref, wd_ref, lng_ref, lnb_ref,
                out_ref, t_ref, cw_ref, y_ref, hid_ref, *, nb, r):
    g = pl.program_id(1)
    tm = nb * r
    d = D_MODEL

    @pl.when(g == 0)
    def _():
        mod = mod_ref[...]
        t3 = x1_ref[...] * (1.0 + mod[:, :, 4 * d:5 * d]) + mod[:, :, 3 * d:4 * d]
        t = t3.reshape(tm, d).astype(BF16)
        t_ref[...] = t
        cw = _route(_mm(t, wr_ref[...]) + br_ref[...])
        lane = lax.broadcasted_iota(jnp.int32, (tm, LANE), 1)
        for gg in range(N_GROUPS):
            shifted = pltpu.roll(cw, LANE - (R_EXP0 + EPG * gg), axis=1)
            cw_ref[gg] = jnp.where(lane < EPG, shifted, 0.0)
        y_ref[...] = jnp.zeros_like(y_ref)

    t = t_ref[...]
    cwg = cw_ref[g]
    for e in range(EPG):
        gg_ = _mm(t, wg_ref[0, e])
        uu = _mm(t, wu_ref[0, e])
        hid = gg_ * _sigmoid(gg_) * uu * cwg[:, e:e + 1]
        hid_ref[:, e * EXPERT_FF:(e + 1) * EXPERT_FF] = hid.astype(BF16)
    y_ref[...] += _mm(hid_ref[...], wd_ref[0].reshape(EPG * EXPERT_FF, d))

    @pl.when(g == N_GROUPS - 1)
    def _():
        mod = mod_ref[...]
        y = DEEPNORM_ALPHA * x1_ref[...] + mod[:, :, 5 * d:6 * d] * y_ref[...].reshape(nb, r, d)
        out_ref[...] = _layer_norm(y, lng_ref[...], lnb_ref[...])


def _moe(x1, mod, wr, br, wg, wu, wd, lng, lnb, *, nb, r):
    b, l, d = x1.shape
    tpb = l // r
    ntiles = (b // nb) * tpb
    tm = nb * r
    xmap = lambda i, g: (i // tpb, i % tpb, 0)
    mmap = lambda i, g: (i // tpb, 0, 0)
    const2 = lambda i, g: (0, 0)
    return pl.pallas_call(
        functools.partial(_moe_kernel, nb=nb, r=r),
        out_shape=jax.ShapeDtypeStruct((b, l, d), F32),
        grid=(ntiles, N_GROUPS),
        in_specs=[
            pl.BlockSpec((nb, r, d), xmap),
            pl.BlockSpec((nb, 1, 6 * d), mmap),
            pl.BlockSpec((d, LANE), const2),
            pl.BlockSpec((1, LANE), const2),
            pl.BlockSpec((1, EPG, d, EXPERT_FF), lambda i, g: (g, 0, 0, 0)),
            pl.BlockSpec((1, EPG, d, EXPERT_FF), lambda i, g: (g, 0, 0, 0)),
            pl.BlockSpec((1, EPG, EXPERT_FF, d), lambda i, g: (g, 0, 0, 0)),
            pl.BlockSpec((1, d), const2),
            pl.BlockSpec((1, d), const2),
        ],
        out_specs=pl.BlockSpec((nb, r, d), xmap),
        scratch_shapes=[
            pltpu.VMEM((tm, d), BF16),
            pltpu.VMEM((N_GROUPS, tm, LANE), F32),
            pltpu.VMEM((tm, d), F32),
            pltpu.VMEM((tm, EPG * EXPERT_FF), BF16),
        ],
        compiler_params=pltpu.CompilerParams(dimension_semantics=("arbitrary", "arbitrary"),
                                             vmem_limit_bytes=VMEM_LIMIT),
        name="moe",
    )(x1, mod, wr, br, wg, wu, wd, lng, lnb)


def kernel(x_prompt, x_sample, c_prompt, c_sample, state_gla, cache_swa_k, cache_swa_v, ada_w, ada_b, w_in,
           gla_a2_w, gla_a2_b, gla_norm_w, swa_sinks, w_o, ln1_g, ln1_b, router_g_w, router_g_b, router_e_w,
           router_e_b, moe_w_gate, moe_w_up, moe_w_down, ln2_g, ln2_b):
    assert ada_w.shape[0] == 1
    bp = x_prompt.shape[0]
    bs, ss, d = x_sample.shape
    lc = cache_swa_k.shape[2]

    w = w_in[0]
    zpad = jnp.zeros((d, LANE - GLA_RANK), F32)
    win = jnp.concatenate([w[:, 0:1536], w[:, 1552:2320], w[:, 1536:1552], zpad], axis=1).astype(BF16)
    a2w = jnp.concatenate([gla_a2_w[0], jnp.zeros((LANE - GLA_RANK, 256), F32)], axis=0).astype(BF16)
    a2b = gla_a2_b[0].reshape(1, 256)
    gnw = gla_norm_w[0].reshape(1, GLA_DV)
    wo = w_o[0].astype(BF16)
    sinks = swa_sinks[0]
    wr = jnp.concatenate([router_g_w[0], jnp.zeros((d, R_EXP0 - N_GROUPS), F32),
                          jnp.transpose(router_e_w[0], (1, 0, 2)).reshape(d, N_GROUPS * EPG),
                          jnp.zeros((d, LANE - R_EXP0 - N_GROUPS * EPG), F32)], axis=1).astype(BF16)
    br = jnp.concatenate([router_g_b[0], jnp.zeros((R_EXP0 - N_GROUPS,), F32), router_e_b[0].reshape(-1),
                          jnp.zeros((LANE - R_EXP0 - N_GROUPS * EPG,), F32)]).reshape(1, LANE)
    wg = moe_w_gate[0].astype(BF16)
    wu = moe_w_up[0].astype(BF16)
    wd = moe_w_down[0].astype(BF16)
    lng1, lnb1 = ln1_g[0].reshape(1, d), ln1_b[0].reshape(1, d)
    lng2, lnb2 = ln2_g[0].reshape(1, d), ln2_b[0].reshape(1, d)

    mod = _adaln(jnp.concatenate([c_prompt, c_sample], axis=0), ada_w[0], ada_b[0].reshape(1, 6 * d))
    mod = mod.reshape(bp + bs, 1, 6 * d)
    mod_p, mod_s = mod[:bp], mod[bp:]

    x1p, s_p, k_p, v_p = _mixer_prompt(x_prompt, mod_p, sinks, win, a2w, a2b, gnw, wo, lng1, lnb1)
    x1s, s_s, k_s, v_s = _mixer_sample(
        x_sample, mod_s, state_gla[0], cache_swa_k[0].reshape(bs, lc, LANE), cache_swa_v[0].reshape(bs, lc, LANE),
        sinks, win, a2w, a2b, gnw, wo, lng1, lnb1)

    yp = _moe(x1p, mod_p, wr, br, wg, wu, wd, lng2, lnb2, nb=1, r=512)
    ys = _moe(x1s, mod_s, wr, br, wg, wu, wd, lng2, lnb2, nb=bs, r=ss)

    kv_shape_p = (1, bp, WINDOW, SWA_KV_HEADS, SWA_DH)
    kv_shape_s = (1, bs, lc, SWA_KV_HEADS, SWA_DH)
    return (yp, ys, s_p[None], k_p.reshape(kv_shape_p), v_p.reshape(kv_shape_p),
            s_s[None], k_s.reshape(kv_shape_s), v_s.reshape(kv_shape_s))
```

```python
import functools

import jax
import jax.numpy as jnp
import numpy as np
from jax import lax
from jax.experimental import pallas as pl
from jax.experimental.pallas import tpu as pltpu

F32 = jnp.float32
BF16 = jnp.bfloat16

D_MODEL = 1024
CHUNK = 64
GLA_HEADS = 4
GLA_DK = 64
GLA_DV = 128
GLA_RANK = 16
GLA_TAU = 16.0
SWA_Q_HEADS = 8
SWA_KV_HEADS = 2
SWA_GROUP = 4
SWA_DH = 64
WINDOW = 128
N_GROUPS = 4
EPG = 8
EXPERT_FF = 256
DEEPNORM_ALPHA = 2.0 ** 0.25
LN_EPS = 1e-5
NEG_INF = -1e30

C_GQ, C_GK, C_GV, C_GR, C_SQ, C_SK, C_SV, C_GA = 0, 256, 512, 1024, 1536, 2048, 2176, 2304
PROJ_W = 2432
LANE = 128
R_EXP0 = 8

VMEM_LIMIT = 56 * 1024 * 1024


def _mm(a, b):
    return jnp.dot(a, b, preferred_element_type=F32)


def _mm_nt(a, b):
    return lax.dot_general(a, b, (((1,), (1,)), ((), ())), preferred_element_type=F32)


def _mm_tn(a, b):
    return lax.dot_general(a, b, (((0,), (0,)), ((), ())), preferred_element_type=F32)


def _split_bf16(a):
    hi = a.astype(BF16)
    lo = (a - hi.astype(F32)).astype(BF16)
    return hi, lo


def _sigmoid(x):
    return 1.0 / (1.0 + jnp.exp(-x))


def _layer_norm(y, g, b):
    mu = jnp.mean(y, axis=-1, keepdims=True)
    d = y - mu
    var = jnp.mean(d * d, axis=-1, keepdims=True)
    return d * lax.rsqrt(var + LN_EPS) * g + b


def _adaln_kernel(c_ref, w_ref, b_ref, o_ref):
    c = c_ref[...]
    a = c * _sigmoid(c)
    a_hi, a_lo = _split_bf16(a)
    w_hi, w_lo = _split_bf16(w_ref[...])
    o_ref[...] = _mm(a_hi, w_hi) + (_mm(a_hi, w_lo) + _mm(a_lo, w_hi)) + b_ref[...]


def _adaln(c_all, ada_w, ada_b):
    n = c_all.shape[0]
    bn = 1024
    return pl.pallas_call(
        _adaln_kernel,
        out_shape=jax.ShapeDtypeStruct((n, 6 * D_MODEL), F32),
        grid=(6 * D_MODEL // bn,),
        in_specs=[pl.BlockSpec((n, D_MODEL), lambda j: (0, 0)),
                  pl.BlockSpec((D_MODEL, bn), lambda j: (0, j)),
                  pl.BlockSpec((1, bn), lambda j: (0, j))],
        out_specs=pl.BlockSpec((n, bn), lambda j: (0, j)),
        compiler_params=pltpu.CompilerParams(dimension_semantics=("arbitrary",), vmem_limit_bytes=VMEM_LIMIT),
        name="adaln",
    )(c_all, ada_w, ada_b)


def _gla_chunk(q, k, v, la, st_ref, tri_bf, causal):
    c = q.shape[0]
    la_hi, la_lo = _split_bf16(la)
    b = _mm(tri_bf, la_hi) + _mm(tri_bf, la_lo)
    eb = jnp.exp(b)
    ebn = jnp.exp(-b)
    b_end = b[c - 1:c, :]
    wk = jnp.exp(b_end - b)
    g_end = jnp.exp(b_end)
    qe = (q * eb).astype(BF16)
    qn = (q * ebn).astype(BF16)
    ke = (k * eb).astype(BF16)
    kn = (k * ebn).astype(BF16)
    kw = (k * wk).astype(BF16)
    vb = v.astype(BF16)
    outs = []
    for h in range(GLA_HEADS):
        ks = slice(h * GLA_DK, (h + 1) * GLA_DK)
        vs = slice(h * GLA_DV, (h + 1) * GLA_DV)
        a_lo = _mm_nt(qe[:, ks], kn[:, ks])
        a_up = _mm_nt(qn[:, ks], ke[:, ks])
        a = jnp.where(causal, a_lo, a_up).astype(BF16)
        st = st_ref[h]
        o = _mm(a, vb[:, vs]) + _mm_nt(qe[:, ks], st.astype(BF16))
        ut = _mm_tn(vb[:, vs], kw[:, ks])
        st_ref[h] = g_end[:, ks] * st + ut
        outs.append(o)
    return outs


def _gla_post(outs, gr, gnw):
    res = []
    for h in range(GLA_HEADS):
        o = outs[h]
        ms = jnp.mean(o * o, axis=-1, keepdims=True)
        og = o * lax.rsqrt(ms + LN_EPS) * gnw
        r = gr[:, h * GLA_DV:(h + 1) * GLA_DV]
        res.append((og * (r * _sigmoid(r))).astype(BF16))
    return res


def _swa_chunk(sq, kwin, vwin, dist, valid, sinks_ref):
    qb = (sq * (SWA_DH ** -0.5)).astype(BF16)
    kb = kwin.astype(BF16)
    vb = vwin.astype(BF16)
    res = []
    for hk in range(SWA_KV_HEADS):
        kh = kb[:, hk * SWA_DH:(hk + 1) * SWA_DH]
        vh = vb[:, hk * SWA_DH:(hk + 1) * SWA_DH]
        for g in range(SWA_GROUP):
            hq = hk * SWA_GROUP + g
            slope = float(2.0 ** (-(hq + 1)))
            s = _mm_nt(qb[:, hq * SWA_DH:(hq + 1) * SWA_DH], kh) - slope * dist
            if valid is not None:
                s = jnp.where(valid, s, NEG_INF)
            sink = sinks_ref[hq]
            m = jnp.maximum(jnp.max(s, axis=-1, keepdims=True), sink)
            p = jnp.exp(s - m)
            denom = jnp.sum(p, axis=-1, keepdims=True) + jnp.exp(sink - m)
            o = _mm(p.astype(BF16), vh) / denom
            res.append(o.astype(BF16))
    return res


def _in_proj(x, mod, win_ref, a2w_ref, a2b_ref, proj_ref, loga_ref):
    sh1 = mod[:, 0:D_MODEL]
    sc1 = mod[:, D_MODEL:2 * D_MODEL]
    h = (x * (1.0 + sc1) + sh1).astype(BF16)
    proj_ref[...] = _mm(h, win_ref[...])
    ga = proj_ref[:, C_GA:C_GA + LANE].astype(BF16)
    z = _mm(ga, a2w_ref[...]) + a2b_ref[...]
    lsig = -(jnp.maximum(-z, 0.0) + jnp.log(1.0 + jnp.exp(-jnp.abs(z))))
    loga_ref[...] = lsig * (1.0 / GLA_TAU)


def _out_proj_ln(x, mod, mixed_ref, wo_ref, ln_g, ln_b):
    gt1 = mod[:, 2 * D_MODEL:3 * D_MODEL]
    mix = _mm(mixed_ref[...], wo_ref[...])
    return _layer_norm(DEEPNORM_ALPHA * x + gt1 * mix, ln_g, ln_b)


def _tri(c):
    r = lax.broadcasted_iota(jnp.int32, (c, c), 0)
    s = lax.broadcasted_iota(jnp.int32, (c, c), 1)
    return r >= s


def _mixer_prompt_kernel(sinks_ref, x_ref, mod_ref, win_ref, a2w_ref, a2b_ref, gnw_ref, wo_ref, lng_ref, lnb_ref,
                         x1_ref, s_out_ref, k_out_ref, v_out_ref,
                         proj_ref, loga_ref, kbuf, vbuf, mixed_ref, st_ref, *, tl):
    j = pl.program_id(1)
    nj = pl.num_programs(1)
    nkeys = WINDOW + CHUNK

    @pl.when(j == 0)
    def _():
        st_ref[...] = jnp.zeros_like(st_ref)
        kbuf[0:WINDOW, :] = jnp.zeros((WINDOW, LANE), F32)
        vbuf[0:WINDOW, :] = jnp.zeros((WINDOW, LANE), F32)

    x = x_ref[0]
    mod = mod_ref[0]
    _in_proj(x, mod, win_ref, a2w_ref, a2b_ref, proj_ref, loga_ref)
    kbuf[WINDOW:WINDOW + tl, :] = proj_ref[:, C_SK:C_SK + LANE]
    vbuf[WINDOW:WINDOW + tl, :] = proj_ref[:, C_SV:C_SV + LANE]

    causal = _tri(CHUNK)
    tri_bf = causal.astype(F32).astype(BF16)
    qi = lax.broadcasted_iota(jnp.int32, (CHUNK, nkeys), 0)
    kj = lax.broadcasted_iota(jnp.int32, (CHUNK, nkeys), 1)
    dist = jnp.abs(qi + WINDOW - kj).astype(F32)
    gnw = gnw_ref[...]

    def body(c, carry):
        r = pl.multiple_of(c * CHUNK, CHUNK)
        rows = pl.ds(r, CHUNK)
        q = proj_ref[rows, C_GQ:C_GQ + 256] * (GLA_DK ** -0.5)
        k = proj_ref[rows, C_GK:C_GK + 256]
        v = proj_ref[rows, C_GV:C_GV + 512]
        gr = proj_ref[rows, C_GR:C_GR + 512]
        la = loga_ref[rows, :]
        outs = _gla_chunk(q, k, v, la, st_ref, tri_bf, causal)
        og = _gla_post(outs, gr, gnw)
        for h in range(GLA_HEADS):
            mixed_ref[rows, h * GLA_DV:(h + 1) * GLA_DV] = og[h]
        sq = proj_ref[rows, C_SQ:C_SQ + 512]
        kwin = kbuf[pl.ds(r, nkeys), :]
        vwin = vbuf[pl.ds(r, nkeys), :]
        valid = kj >= (WINDOW - (j * tl + r))
        os_ = _swa_chunk(sq, kwin, vwin, dist, valid, sinks_ref)
        for hp in range(SWA_Q_HEADS // 2):
            mixed_ref[rows, 512 + hp * LANE:512 + (hp + 1) * LANE] = jnp.concatenate(
                [os_[2 * hp], os_[2 * hp + 1]], axis=-1)
        return carry

    lax.fori_loop(0, tl // CHUNK, body, 0)

    x1_ref[0] = _out_proj_ln(x, mod, mixed_ref, wo_ref, lng_ref[...], lnb_ref[...])

    kbuf[0:WINDOW, :] = kbuf[tl:tl + WINDOW, :]
    vbuf[0:WINDOW, :] = vbuf[tl:tl + WINDOW, :]

    @pl.when(j == nj - 1)
    def _():
        for h in range(GLA_HEADS):
            s_out_ref[0, h] = st_ref[h].T
        k_out_ref[0] = kbuf[0:WINDOW, :]
        v_out_ref[0] = vbuf[0:WINDOW, :]


def _mixer_prompt(x, mod, sinks, win, a2w, a2b, gnw, wo, lng, lnb, *, tl=512):
    b, l, d = x.shape
    nj = l // tl
    const2 = lambda i, j, s: (0, 0)
    grid_spec = pltpu.PrefetchScalarGridSpec(
        num_scalar_prefetch=1,
        grid=(b, nj),
        in_specs=[
            pl.BlockSpec((1, tl, d), lambda i, j, s: (i, j, 0)),
            pl.BlockSpec((1, 1, 6 * d), lambda i, j, s: (i, 0, 0)),
            pl.BlockSpec((d, PROJ_W), const2),
            pl.BlockSpec((LANE, 256), const2),
            pl.BlockSpec((1, 256), const2),
            pl.BlockSpec((1, GLA_DV), const2),
            pl.BlockSpec((d, d), const2),
            pl.BlockSpec((1, d), const2),
            pl.BlockSpec((1, d), const2),
        ],
        out_specs=[
            pl.BlockSpec((1, tl, d), lambda i, j, s: (i, j, 0)),
            pl.BlockSpec((1, GLA_HEADS, GLA_DK, GLA_DV), lambda i, j, s: (i, 0, 0, 0)),
            pl.BlockSpec((1, WINDOW, LANE), lambda i, j, s: (i, 0, 0)),
            pl.BlockSpec((1, WINDOW, LANE), lambda i, j, s: (i, 0, 0)),
        ],
        scratch_shapes=[
            pltpu.VMEM((tl, PROJ_W), F32),
            pltpu.VMEM((tl, 256), F32),
            pltpu.VMEM((WINDOW + tl, LANE), F32),
            pltpu.VMEM((WINDOW + tl, LANE), F32),
            pltpu.VMEM((tl, d), BF16),
            pltpu.VMEM((GLA_HEADS, GLA_DV, GLA_DK), F32),
        ],
    )
    return pl.pallas_call(
        functools.partial(_mixer_prompt_kernel, tl=tl),
        out_shape=[
            jax.ShapeDtypeStruct((b, l, d), F32),
            jax.ShapeDtypeStruct((b, GLA_HEADS, GLA_DK, GLA_DV), F32),
            jax.ShapeDtypeStruct((b, WINDOW, LANE), F32),
            jax.ShapeDtypeStruct((b, WINDOW, LANE), F32),
        ],
        grid_spec=grid_spec,
        compiler_params=pltpu.CompilerParams(dimension_semantics=("arbitrary", "arbitrary"),
                                             vmem_limit_bytes=VMEM_LIMIT),
        name="mixer_prompt",
    )(sinks, x, mod, win, a2w, a2b, gnw, wo, lng, lnb)


def _mixer_sample_kernel(sinks_ref, x_ref, mod_ref, s0_ref, kc_ref, vc_ref, win_ref, a2w_ref, a2b_ref, gnw_ref,
                         wo_ref, lng_ref, lnb_ref,
                         x1_ref, s_out_ref, k_out_ref, v_out_ref,
                         proj_ref, loga_ref, kbuf, vbuf, mixed_ref, st_ref, xm_ref, *, nb, s):
    nkeys = WINDOW + s
    d = D_MODEL
    for bb in range(nb):
        m = mod_ref[bb]
        xm_ref[bb * s:(bb + 1) * s, :] = x_ref[bb] * (1.0 + m[:, d:2 * d]) + m[:, 0:d]
    h = xm_ref[...].astype(BF16)
    proj_ref[...] = _mm(h, win_ref[...])
    ga = proj_ref[:, C_GA:C_GA + LANE].astype(BF16)
    z = _mm(ga, a2w_ref[...]) + a2b_ref[...]
    lsig = -(jnp.maximum(-z, 0.0) + jnp.log(1.0 + jnp.exp(-jnp.abs(z))))
    loga_ref[...] = lsig * (1.0 / GLA_TAU)

    causal = _tri(s)
    tri_bf = causal.astype(F32).astype(BF16)
    qi = lax.broadcasted_iota(jnp.int32, (s, nkeys), 0)
    kj = lax.broadcasted_iota(jnp.int32, (s, nkeys), 1)
    dist = jnp.abs(qi + WINDOW - kj).astype(F32)
    gnw = gnw_ref[...]

    def body(bb, carry):
        r = pl.multiple_of(bb * s, s)
        rows = pl.ds(r, s)
        for hh in range(GLA_HEADS):
            st_ref[hh] = s0_ref[bb, hh].T
        q = proj_ref[rows, C_GQ:C_GQ + 256] * (GLA_DK ** -0.5)
        k = proj_ref[rows, C_GK:C_GK + 256]
        v = proj_ref[rows, C_GV:C_GV + 512]
        gr = proj_ref[rows, C_GR:C_GR + 512]
        la = loga_ref[rows, :]
        outs = _gla_chunk(q, k, v, la, st_ref, tri_bf, causal)
        og = _gla_post(outs, gr, gnw)
        for hh in range(GLA_HEADS):
            mixed_ref[rows, hh * GLA_DV:(hh + 1) * GLA_DV] = og[hh]
            s_out_ref[bb, hh] = st_ref[hh].T
        kbuf[0:WINDOW, :] = kc_ref[bb]
        vbuf[0:WINDOW, :] = vc_ref[bb]
        kbuf[WINDOW:nkeys, :] = proj_ref[rows, C_SK:C_SK + LANE]
        vbuf[WINDOW:nkeys, :] = proj_ref[rows, C_SV:C_SV + LANE]
        sq = proj_ref[rows, C_SQ:C_SQ + 512]
        os_ = _swa_chunk(sq, kbuf[...], vbuf[...], dist, None, sinks_ref)
        for hp in range(SWA_Q_HEADS // 2):
            mixed_ref[rows, 512 + hp * LANE:512 + (hp + 1) * LANE] = jnp.concatenate(
                [os_[2 * hp], os_[2 * hp + 1]], axis=-1)
        k_out_ref[bb] = kbuf[s:nkeys, :]
        v_out_ref[bb] = vbuf[s:nkeys, :]
        return carry

    lax.fori_loop(0, nb, body, 0)

    mix = _mm(mixed_ref[...], wo_ref[...])
    lng = lng_ref[...]
    lnb = lnb_ref[...]
    for bb in range(nb):
        m = mod_ref[bb]
        y = DEEPNORM_ALPHA * x_ref[bb] + m[:, 2 * d:3 * d] * mix[bb * s:(bb + 1) * s, :]
        x1_ref[bb] = _layer_norm(y, lng, lnb)


def _mixer_sample(x, mod, s0, kc, vc, sinks, win, a2w, a2b, gnw, wo, lng, lnb, *, nb=8):
    b, s, d = x.shape
    const2 = lambda i, sk: (0, 0)
    rows = nb * s
    grid_spec = pltpu.PrefetchScalarGridSpec(
        num_scalar_prefetch=1,
        grid=(b // nb,),
        in_specs=[
            pl.BlockSpec((nb, s, d), lambda i, sk: (i, 0, 0)),
            pl.BlockSpec((nb, 1, 6 * d), lambda i, sk: (i, 0, 0)),
            pl.BlockSpec((nb, GLA_HEADS, GLA_DK, GLA_DV), lambda i, sk: (i, 0, 0, 0)),
            pl.BlockSpec((nb, WINDOW, LANE), lambda i, sk: (i, 0, 0)),
            pl.BlockSpec((nb, WINDOW, LANE), lambda i, sk: (i, 0, 0)),
            pl.BlockSpec((d, PROJ_W), const2),
            pl.BlockSpec((LANE, 256), const2),
            pl.BlockSpec((1, 256), const2),
            pl.BlockSpec((1, GLA_DV), const2),
            pl.BlockSpec((d, d), const2),
            pl.BlockSpec((1, d), const2),
            pl.BlockSpec((1, d), const2),
        ],
        out_specs=[
            pl.BlockSpec((nb, s, d), lambda i, sk: (i, 0, 0)),
            pl.BlockSpec((nb, GLA_HEADS, GLA_DK, GLA_DV), lambda i, sk: (i, 0, 0, 0)),
            pl.BlockSpec((nb, WINDOW, LANE), lambda i, sk: (i, 0, 0)),
            pl.BlockSpec((nb, WINDOW, LANE), lambda i, sk: (i, 0, 0)),
        ],
        scratch_shapes=[
            pltpu.VMEM((rows, PROJ_W), F32),
            pltpu.VMEM((rows, 256), F32),
            pltpu.VMEM((WINDOW + s, LANE), F32),
            pltpu.VMEM((WINDOW + s, LANE), F32),
            pltpu.VMEM((rows, d), BF16),
            pltpu.VMEM((GLA_HEADS, GLA_DV, GLA_DK), F32),
            pltpu.VMEM((rows, d), F32),
        ],
    )
    return pl.pallas_call(
        functools.partial(_mixer_sample_kernel, nb=nb, s=s),
        out_shape=[
            jax.ShapeDtypeStruct((b, s, d), F32),
            jax.ShapeDtypeStruct((b, GLA_HEADS, GLA_DK, GLA_DV), F32),
            jax.ShapeDtypeStruct((b, WINDOW, LANE), F32),
            jax.ShapeDtypeStruct((b, WINDOW, LANE), F32),
        ],
        grid_spec=grid_spec,
        compiler_params=pltpu.CompilerParams(dimension_semantics=("arbitrary",),
                                             vmem_limit_bytes=VMEM_LIMIT),
        name="mixer_sample",
    )(sinks, x, mod, s0, kc, vc, win, a2w, a2b, gnw, wo, lng, lnb)


MOE_BLK = 128
MOE_HALF = EPG // 2


def _route_t(logits_t):
    t = logits_t.shape[1]
    row = lax.broadcasted_iota(jnp.int32, (EPG, t), 0).astype(F32)
    big = 99.0
    gl = jnp.where(row < N_GROUPS, logits_t[0:EPG, :], -jnp.inf)
    gmax = jnp.max(gl, axis=0, keepdims=True)
    grp = jnp.min(jnp.where(gl == gmax, row, big), axis=0, keepdims=True)
    p_grp = 1.0 / jnp.sum(jnp.exp(gl - gmax), axis=0, keepdims=True)
    el = jnp.zeros((EPG, t), F32)
    for g in range(N_GROUPS):
        el = el + jnp.where(grp == float(g), logits_t[R_EXP0 + EPG * g:R_EXP0 + EPG * (g + 1), :], 0.0)
    v1 = jnp.max(el, axis=0, keepdims=True)
    i1 = jnp.min(jnp.where(el == v1, row, big), axis=0, keepdims=True)
    el2 = jnp.where(row == i1, -jnp.inf, el)
    v2 = jnp.max(el2, axis=0, keepdims=True)
    i2 = jnp.min(jnp.where(el2 == v2, row, big), axis=0, keepdims=True)
    e2 = jnp.exp(v2 - v1)
    w1 = p_grp / (1.0 + e2)
    w2 = p_grp * e2 / (1.0 + e2)
    cw = jnp.where(row == i1, w1, 0.0) + jnp.where(row == i2, w2, 0.0)
    return grp, cw


def _moe_kernel(x1_ref, mod_ref, wrt_ref, brt_ref, wg_ref, wu_ref, wd_ref, lng_ref, lnb_ref,
                out_ref, t_ref, xs_ref, ys_ref, cws_ref, posc_ref, sm_ref, *, nb, r):
    s = pl.program_id(1)
    g = lax.shift_right_logical(s, 1)
    half = jnp.bitwise_and(s, 1)
    tm = nb * r
    tmp = tm + N_GROUPS * MOE_BLK
    d = D_MODEL
    pc = 256

    @pl.when(s == 0)
    def _():
        mod = mod_ref[...]
        t3 = x1_ref[...] * (1.0 + mod[:, :, 4 * d:5 * d]) + mod[:, :, 3 * d:4 * d]
        t = t3.reshape(tm, d).astype(BF16)
        t_ref[...] = t
        grp, cw = _route_t(_mm_nt(wrt_ref[...], t) + brt_ref[...])
        row = lax.broadcasted_iota(jnp.int32, (EPG, tm), 0).astype(F32)
        onehot_g = jnp.where(row == grp, 1.0, 0.0)
        upper = (lax.broadcasted_iota(jnp.int32, (tm, tm), 0)
                 < lax.broadcasted_iota(jnp.int32, (tm, tm), 1)).astype(F32).astype(BF16)
        rank = _mm(onehot_g.astype(BF16), upper)
        cnt = jnp.sum(onehot_g, axis=1, keepdims=True)
        nblk = jnp.floor((cnt + (MOE_BLK - 1)) * (1.0 / MOE_BLK))
        padded = nblk * MOE_BLK
        rowc = lax.broadcasted_iota(jnp.int32, (EPG, 1), 0)
        off = jnp.zeros((EPG, 1), F32)
        for gg in range(N_GROUPS - 1):
            off = off + jnp.where(rowc > gg, padded[gg:gg + 1, :], 0.0)
        pos = jnp.sum(onehot_g * (off + rank), axis=0, keepdims=True)
        off_i = off.astype(jnp.int32)
        nblk_i = nblk.astype(jnp.int32)
        for gg in range(N_GROUPS):
            sm_ref[gg] = off_i[gg, 0]
            sm_ref[N_GROUPS + gg] = nblk_i[gg, 0]
        posc_ref[...] = jnp.broadcast_to(pos, (LANE, tm)).T
        cw_pad = jnp.concatenate([cw, jnp.zeros((LANE - EPG, tm), F32)], axis=0)
        cw_hi, cw_lo = _split_bf16(cw_pad)
        for c in range(tmp // pc):
            slot = (lax.broadcasted_iota(jnp.int32, (pc, tm), 0) + c * pc).astype(F32)
            perm = jnp.where(slot == pos, 1.0, 0.0).astype(BF16)
            xs_ref[c * pc:(c + 1) * pc, :] = _mm(perm, t).astype(BF16)
            cws = _mm_nt(perm, cw_hi) + _mm_nt(perm, cw_lo)
            cws_ref[0, c * pc:(c + 1) * pc, :] = cws
            cws_ref[1, c * pc:(c + 1) * pc, :] = pltpu.roll(cws, LANE - MOE_HALF, axis=1)
        ys_ref[...] = jnp.zeros_like(ys_ref)

    off_g = sm_ref[g]
    nblk_g = sm_ref[N_GROUPS + g]
    wd = wd_ref[0].reshape(MOE_HALF * EXPERT_FF, d)

    def blk(k, carry):
        rows = pl.ds(pl.multiple_of(off_g + k * MOE_BLK, MOE_BLK), MOE_BLK)
        xb = xs_ref[rows, :]
        cwb = cws_ref[half, rows, :]
        hs = []
        for e in range(MOE_HALF):
            gg_ = _mm(xb, wg_ref[0, e])
            uu = _mm(xb, wu_ref[0, e])
            hs.append((gg_ * _sigmoid(gg_) * uu * cwb[:, e:e + 1]).astype(BF16))
        ys_ref[rows, :] += _mm(jnp.concatenate(hs, axis=1), wd)
        return carry

    lax.fori_loop(0, nblk_g, blk, 0)

    @pl.when(s == 2 * N_GROUPS - 1)
    def _():
        ysb = ys_ref[...].astype(BF16)
        lng = lng_ref[...]
        lnb = lnb_ref[...]
        for c in range(tm // pc):
            posc = posc_ref[c * pc:(c + 1) * pc, 0:1]
            slot = lax.broadcasted_iota(jnp.int32, (pc, tmp), 1).astype(F32)
            unperm = jnp.where(slot == posc, 1.0, 0.0).astype(BF16)
            y = _mm(unperm, ysb)
            if nb == 1:
                x1c = x1_ref[0, c * pc:(c + 1) * pc, :]
                gt2 = mod_ref[0][:, 5 * d:6 * d]
                out_ref[0, c * pc:(c + 1) * pc, :] = _layer_norm(DEEPNORM_ALPHA * x1c + gt2 * y, lng, lnb)
            else:
                cb = pc // r
                x1c = x1_ref[c * cb:(c + 1) * cb]
                gt2 = mod_ref[c * cb:(c + 1) * cb][:, :, 5 * d:6 * d]
                yy = DEEPNORM_ALPHA * x1c + gt2 * y.reshape(cb, r, d)
                out_ref[c * cb:(c + 1) * cb] = _layer_norm(yy, lng, lnb)


def _moe(x1, mod, wrt, brt, wg, wu, wd, lng, lnb, *, nb, r):
    b, l, d = x1.shape
    tpb = l // r
    ntiles = (b // nb) * tpb
    tm = nb * r
    tmp = tm + N_GROUPS * MOE_BLK
    xmap = lambda i, s: (i // tpb, i % tpb, 0)
    mmap = lambda i, s: (i // tpb, 0, 0)
    const2 = lambda i, s: (0, 0)
    wmap = lambda i, s: (s, 0, 0, 0)
    return pl.pallas_call(
        functools.partial(_moe_kernel, nb=nb, r=r),
        out_shape=jax.ShapeDtypeStruct((b, l, d), F32),
        grid=(ntiles, 2 * N_GROUPS),
        in_specs=[
            pl.BlockSpec((nb, r, d), xmap),
            pl.BlockSpec((nb, 1, 6 * d), mmap),
            pl.BlockSpec((LANE, d), const2),
            pl.BlockSpec((LANE, 1), const2),
            pl.BlockSpec((1, MOE_HALF, d, EXPERT_FF), wmap),
            pl.BlockSpec((1, MOE_HALF, d, EXPERT_FF), wmap),
            pl.BlockSpec((1, MOE_HALF, EXPERT_FF, d), wmap),
            pl.BlockSpec((1, d), const2),
            pl.BlockSpec((1, d), const2),
        ],
        out_specs=pl.BlockSpec((nb, r, d), xmap),
        scratch_shapes=[
            pltpu.VMEM((tm, d), BF16),
            pltpu.VMEM((tmp, d), BF16),
            pltpu.VMEM((tmp, d), F32),
            pltpu.VMEM((2, tmp, LANE), F32),
            pltpu.VMEM((tm, LANE), F32),
            pltpu.SMEM((2 * N_GROUPS,), jnp.int32),
        ],
        compiler_params=pltpu.CompilerParams(dimension_semantics=("arbitrary", "arbitrary"),
                                             vmem_limit_bytes=VMEM_LIMIT),
        name="moe",
    )(x1, mod, wrt, brt, wg, wu, wd, lng, lnb)


def kernel(x_prompt, x_sample, c_prompt, c_sample, state_gla, cache_swa_k, cache_swa_v, ada_w, ada_b, w_in,
           gla_a2_w, gla_a2_b, gla_norm_w, swa_sinks, w_o, ln1_g, ln1_b, router_g_w, router_g_b, router_e_w,
           router_e_b, moe_w_gate, moe_w_up, moe_w_down, ln2_g, ln2_b):
    assert ada_w.shape[0] == 1
    bp = x_prompt.shape[0]
    bs, ss, d = x_sample.shape
    lc = cache_swa_k.shape[2]

    w = w_in[0]
    zpad = jnp.zeros((d, LANE - GLA_RANK), F32)
    win = jnp.concatenate([w[:, 0:1536], w[:, 1552:2320], w[:, 1536:1552], zpad], axis=1).astype(BF16)
    a2w = jnp.concatenate([gla_a2_w[0], jnp.zeros((LANE - GLA_RANK, 256), F32)], axis=0).astype(BF16)
    a2b = gla_a2_b[0].reshape(1, 256)
    gnw = gla_norm_w[0].reshape(1, GLA_DV)
    wo = w_o[0].astype(BF16)
    sinks = swa_sinks[0]
    wrt = jnp.concatenate([router_g_w[0], jnp.zeros((d, R_EXP0 - N_GROUPS), F32),
                           jnp.transpose(router_e_w[0], (1, 0, 2)).reshape(d, N_GROUPS * EPG),
                           jnp.zeros((d, LANE - R_EXP0 - N_GROUPS * EPG), F32)], axis=1).T.astype(BF16)
    brt = jnp.concatenate([router_g_b[0], jnp.zeros((R_EXP0 - N_GROUPS,), F32), router_e_b[0].reshape(-1),
                           jnp.zeros((LANE - R_EXP0 - N_GROUPS * EPG,), F32)]).reshape(LANE, 1)
    nhalf = 2 * N_GROUPS
    wg = moe_w_gate[0].astype(BF16).reshape(nhalf, MOE_HALF, d, EXPERT_FF)
    wu = moe_w_up[0].astype(BF16).reshape(nhalf, MOE_HALF, d, EXPERT_FF)
    wd = moe_w_down[0].astype(BF16).reshape(nhalf, MOE_HALF, EXPERT_FF, d)
    lng1, lnb1 = ln1_g[0].reshape(1, d), ln1_b[0].reshape(1, d)
    lng2, lnb2 = ln2_g[0].reshape(1, d), ln2_b[0].reshape(1, d)

    mod = _adaln(jnp.concatenate([c_prompt, c_sample], axis=0), ada_w[0], ada_b[0].reshape(1, 6 * d))
    mod = mod.reshape(bp + bs, 1, 6 * d)
    mod_p, mod_s = mod[:bp], mod[bp:]

    x1p, s_p, k_p, v_p = _mixer_prompt(x_prompt, mod_p, sinks, win, a2w, a2b, gnw, wo, lng1, lnb1)
    x1s, s_s, k_s, v_s = _mixer_sample(
        x_sample, mod_s, state_gla[0], cache_swa_k[0].reshape(bs, lc, LANE), cache_swa_v[0].reshape(bs, lc, LANE),
        sinks, win, a2w, a2b, gnw, wo, lng1, lnb1)

    yp = _moe(x1p, mod_p, wrt, brt, wg, wu, wd, lng2, lnb2, nb=1, r=1024)
    ys = _moe(x1s, mod_s, wrt, brt, wg, wu, wd, lng2, lnb2, nb=bs, r=ss)

    kv_shape_p = (1, bp, WINDOW, SWA_KV_HEADS, SWA_DH)
    kv_shape_s = (1, bs, lc, SWA_KV_HEADS, SWA_DH)
    return (yp, ys, s_p[None], k_p.reshape(kv_shape_p), v_p.reshape(kv_shape_p),
            s_s[None], k_s.reshape(kv_shape_s), v_s.reshape(kv_shape_s))
```

```python
import functools

import jax
import jax.numpy as jnp
import numpy as np
from jax import lax
from jax.experimental import pallas as pl
from jax.experimental.pallas import tpu as pltpu

F32 = jnp.float32
BF16 = jnp.bfloat16

D_MODEL = 1024
CHUNK = 64
GLA_HEADS = 4
GLA_DK = 64
GLA_DV = 128
GLA_KW = GLA_HEADS * GLA_DK
GLA_VW = GLA_HEADS * GLA_DV
GLA_RANK = 16
GLA_TAU = 16.0
SWA_Q_HEADS = 8
SWA_KV_HEADS = 2
SWA_GROUP = 4
SWA_DH = 64
SWA_W = SWA_Q_HEADS * SWA_DH
WINDOW = 128
N_GROUPS = 4
EPG = 8
EXPERT_FF = 256
DEEPNORM_ALPHA = 2.0 ** 0.25
LN_EPS = 1e-5
NEG_INF = -1e30

C_GQ, C_GK, C_GV, C_GR, C_SQ, C_SK, C_SV, C_GA = 0, 256, 512, 1024, 1536, 2048, 2176, 2304
PROJ_W = 2432
LANE = 128
R_EXP0 = 8

VMEM_LIMIT = 56 * 1024 * 1024


def _mm(a, b):
    return jnp.dot(a, b, preferred_element_type=F32)


def _mm_nt(a, b):
    return lax.dot_general(a, b, (((1,), (1,)), ((), ())), preferred_element_type=F32)


def _mm_tn(a, b):
    return lax.dot_general(a, b, (((0,), (0,)), ((), ())), preferred_element_type=F32)


def _split_bf16(a):
    hi = a.astype(BF16)
    lo = (a - hi.astype(F32)).astype(BF16)
    return hi, lo


def _sigmoid(x):
    return 1.0 / (1.0 + jnp.exp(-x))


def _layer_norm(y, g, b):
    mu = jnp.mean(y, axis=-1, keepdims=True)
    d = y - mu
    var = jnp.mean(d * d, axis=-1, keepdims=True)
    return d * lax.rsqrt(var + LN_EPS) * g + b


def _adaln_kernel(c_ref, w_ref, b_ref, o_ref):
    c = c_ref[...]
    a = c * _sigmoid(c)
    a_hi, a_lo = _split_bf16(a)
    w_hi, w_lo = _split_bf16(w_ref[...])
    o_ref[...] = _mm(a_hi, w_hi) + (_mm(a_hi, w_lo) + _mm(a_lo, w_hi)) + b_ref[...]


def _adaln(c_all, ada_w, ada_b):
    n = c_all.shape[0]
    bn = 1024
    return pl.pallas_call(
        _adaln_kernel,
        out_shape=jax.ShapeDtypeStruct((n, 6 * D_MODEL), F32),
        grid=(6 * D_MODEL // bn,),
        in_specs=[pl.BlockSpec((n, D_MODEL), lambda j: (0, 0)),
                  pl.BlockSpec((D_MODEL, bn), lambda j: (0, j)),
                  pl.BlockSpec((1, bn), lambda j: (0, j))],
        out_specs=pl.BlockSpec((n, bn), lambda j: (0, j)),
        compiler_params=pltpu.CompilerParams(dimension_semantics=("arbitrary",), vmem_limit_bytes=VMEM_LIMIT),
        name="adaln",
    )(c_all, ada_w, ada_b)


def _mixer_consts(nrows, chunk, qb, kw):
    r = np.arange(nrows)
    tri = ((r[:, None] // chunk == r[None, :] // chunk) & (r[:, None] >= r[None, :])).astype(np.float32)
    hs = np.arange(GLA_HEADS * chunk)
    mkk = (hs[:, None] // chunk == np.arange(GLA_KW)[None, :] // GLA_DK).astype(np.float32)
    mv = (hs[:, None] // chunk == np.arange(GLA_VW)[None, :] // GLA_DV).astype(np.float32)
    ms = (np.arange(GLA_VW)[:, None] // GLA_DV == np.arange(GLA_KW)[None, :] // GLA_DK).astype(np.float32)
    caus = (np.arange(chunk)[:, None] >= (hs[None, :] % chunk)).astype(np.float32)
    t = np.arange(qb)
    kj = np.arange(kw)
    cs = (t // chunk) * chunk
    kpos = kj[None, :] - WINDOW
    vis = (kpos >= cs[:, None] - WINDOW) & (kpos < cs[:, None] + chunk)
    dist = np.abs(t[:, None] + WINDOW - kj[None, :]).astype(np.float32)
    bias = np.zeros((SWA_KV_HEADS, SWA_GROUP * qb, kw), np.float32)
    for hk in range(SWA_KV_HEADS):
        for g in range(SWA_GROUP):
            slope = np.float32(2.0 ** (-(hk * SWA_GROUP + g + 1)))
            bias[hk, g * qb:(g + 1) * qb] = np.where(vis, -slope * dist, np.float32(2.0 * NEG_INF))
    return (jnp.asarray(tri, BF16), jnp.asarray(bias), jnp.asarray(mkk, BF16), jnp.asarray(mv, BF16),
            jnp.asarray(ms), jnp.asarray(caus))


def _in_proj(h, win_ref, a2w_ref, a2b_ref, proj_ref):
    proj_ref[...] = _mm(h, win_ref[...])
    ga = proj_ref[:, C_GA:C_GA + LANE].astype(BF16)
    z = _mm(ga, a2w_ref[...]) + a2b_ref[...]
    lsig = -(jnp.maximum(-z, 0.0) + jnp.log(1.0 + jnp.exp(-jnp.abs(z))))
    return lsig * (1.0 / GLA_TAU)


def _gla_prep(loga, tri_ref, proj_ref, chunk, qe_ref, qn_ref, ke_ref, kn_ref, kw_ref, eb_ref, vb_ref):
    nrows = loga.shape[0]
    la_hi, la_lo = _split_bf16(loga)
    tri = tri_ref[...]
    b = _mm(tri, la_hi) + _mm(tri, la_lo)
    b_end = jnp.concatenate(
        [jnp.broadcast_to(b[c * chunk + chunk - 1:(c + 1) * chunk, :], (chunk, GLA_KW))
         for c in range(nrows // chunk)], axis=0)
    eb = jnp.exp(b)
    ebn = jnp.exp(-b)
    wk = jnp.exp(b_end - b)
    q = proj_ref[:, C_GQ:C_GQ + GLA_KW] * (GLA_DK ** -0.5)
    k = proj_ref[:, C_GK:C_GK + GLA_KW]
    eb_ref[...] = eb
    qe_ref[...] = (q * eb).astype(BF16)
    qn_ref[...] = (q * ebn).astype(BF16)
    ke_ref[...] = (k * eb).astype(BF16)
    kn_ref[...] = (k * ebn).astype(BF16)
    kw_ref[...] = (k * wk).astype(BF16)
    vb_ref[...] = proj_ref[:, C_GV:C_GV + GLA_VW].astype(BF16)


def _gla_chunk(r0, chunk, qe_ref, qn_ref, ke_ref, kn_ref, kw_ref, eb_ref, vb_ref, sbt_ref, mkk, mv, ms, caus):
    rows = slice(r0, r0 + chunk)
    qe = qe_ref[rows, :]
    qn = qn_ref[rows, :]
    zero = jnp.zeros((), BF16)
    kn4 = jnp.where(mkk != 0, jnp.concatenate([kn_ref[rows, :]] * GLA_HEADS, axis=0), zero)
    ke4 = jnp.where(mkk != 0, jnp.concatenate([ke_ref[rows, :]] * GLA_HEADS, axis=0), zero)
    a_lo = _mm_nt(qe, kn4)
    a_up = _mm_nt(qn, ke4)
    a = jnp.where(caus != 0.0, a_lo, a_up).astype(BF16)
    v = vb_ref[rows, :]
    v4 = jnp.where(mv != 0, jnp.concatenate([v] * GLA_HEADS, axis=0), zero)
    sbt = sbt_ref[...]
    o = _mm(a, v4) + _mm_nt(qe, sbt.astype(BF16))
    ut = _mm_tn(v, kw_ref[rows, :])
    g_end = eb_ref[r0 + chunk - 1:r0 + chunk, :]
    sbt_ref[...] = g_end * sbt + jnp.where(ms != 0.0, ut, 0.0)
    return o


def _gla_post(o, gr, gnw):
    res = []
    for h in range(GLA_HEADS):
        oh = o[:, h * GLA_DV:(h + 1) * GLA_DV]
        ms_ = jnp.mean(oh * oh, axis=-1, keepdims=True)
        og = oh * lax.rsqrt(ms_ + LN_EPS) * gnw
        r = gr[:, h * GLA_DV:(h + 1) * GLA_DV]
        res.append((og * (r * _sigmoid(r))).astype(BF16))
    return res


def _swa_problem(qs, kh, vh, bias, sink_col, extra_valid):
    s = _mm_nt(qs, kh)
    ok = bias > -1e29
    if extra_valid is not None:
        ok = ok & extra_valid
    s = jnp.where(ok, s + bias, NEG_INF)
    m = jnp.maximum(jnp.max(s, axis=-1, keepdims=True), sink_col)
    p = jnp.exp(s - m)
    denom = jnp.sum(p, axis=-1, keepdims=True) + jnp.exp(sink_col - m)
    return _mm(p.astype(BF16), vh) / denom


def _sink_cols(sinks_ref, qb):
    cols = []
    for hk in range(SWA_KV_HEADS):
        cols.append(jnp.concatenate(
            [jnp.full((qb, 1), sinks_ref[hk * SWA_GROUP + g], F32) for g in range(SWA_GROUP)], axis=0))
    return cols


def _swa_block(sq, kwin, vwin, bias_ref, sink_cols, extra_valid, qb):
    sqb = (sq * (SWA_DH ** -0.5)).astype(BF16)
    blocks = []
    for hk in range(SWA_KV_HEADS):
        qs = jnp.concatenate([sqb[:, (hk * SWA_GROUP + g) * SWA_DH:(hk * SWA_GROUP + g + 1) * SWA_DH]
                              for g in range(SWA_GROUP)], axis=0)
        o = _swa_problem(qs, kwin[:, hk * SWA_DH:(hk + 1) * SWA_DH], vwin[:, hk * SWA_DH:(hk + 1) * SWA_DH],
                         bias_ref[hk], sink_cols[hk], extra_valid).astype(BF16)
        for gp in range(SWA_GROUP // 2):
            blocks.append(jnp.concatenate([o[(2 * gp) * qb:(2 * gp + 1) * qb, :],
                                           o[(2 * gp + 1) * qb:(2 * gp + 2) * qb, :]], axis=1))
    return blocks


def _out_proj_ln(x, gt1, mixed_ref, wo_ref, ln_g, ln_b):
    mix = _mm(mixed_ref[...], wo_ref[...])
    return _layer_norm(DEEPNORM_ALPHA * x + gt1 * mix, ln_g, ln_b)


P_QB = 128
P_KW = WINDOW + P_QB


def _mixer_prompt_kernel(sinks_ref, x_ref, mod_ref, win_ref, a2w_ref, a2b_ref, gnw_ref, wo_ref, lng_ref, lnb_ref,
                         tri_ref, bias_ref, mkk_ref, mv_ref, ms_ref, caus_ref,
                         x1_ref, s_out_ref, k_out_ref, v_out_ref,
                         proj_ref, qe_ref, qn_ref, ke_ref, kn_ref, kw_ref, eb_ref, vb_ref, kbuf, vbuf, mixed_ref,
                         sbt_ref, *, tl):
    j = pl.program_id(1)
    nj = pl.num_programs(1)
    d = D_MODEL

    @pl.when(j == 0)
    def _():
        sbt_ref[...] = jnp.zeros_like(sbt_ref)
        kbuf[0:WINDOW, :] = jnp.zeros((WINDOW, LANE), BF16)
        vbuf[0:WINDOW, :] = jnp.zeros((WINDOW, LANE), BF16)

    x = x_ref[0]
    mod = mod_ref[0]
    h = (x * (1.0 + mod[:, d:2 * d]) + mod[:, 0:d]).astype(BF16)
    loga = _in_proj(h, win_ref, a2w_ref, a2b_ref, proj_ref)
    _gla_prep(loga, tri_ref, proj_ref, CHUNK, qe_ref, qn_ref, ke_ref, kn_ref, kw_ref, eb_ref, vb_ref)
    kbuf[WINDOW:WINDOW + tl, :] = proj_ref[:, C_SK:C_SK + LANE].astype(BF16)
    vbuf[WINDOW:WINDOW + tl, :] = proj_ref[:, C_SV:C_SV + LANE].astype(BF16)

    mkk = mkk_ref[...]
    mv = mv_ref[...]
    ms = ms_ref[...]
    caus = caus_ref[...]
    gnw = gnw_ref[...]
    sink_cols = _sink_cols(sinks_ref, P_QB)
    kj = lax.broadcasted_iota(jnp.int32, (SWA_GROUP * P_QB, P_KW), 1)
    first_valid = kj >= jnp.where(j > 0, 0, WINDOW)

    for p in range(tl // P_QB):
        q0 = p * P_QB
        blocks = _swa_block(proj_ref[q0:q0 + P_QB, C_SQ:C_SQ + SWA_W], kbuf[q0:q0 + P_KW, :], vbuf[q0:q0 + P_KW, :],
                            bias_ref, sink_cols, first_valid if p == 0 else None, P_QB)
        for i, blk in enumerate(blocks):
            mixed_ref[q0:q0 + P_QB, GLA_VW + i * LANE:GLA_VW + (i + 1) * LANE] = blk
        for c in range(P_QB // CHUNK):
            r0 = q0 + c * CHUNK
            o = _gla_chunk(r0, CHUNK, qe_ref, qn_ref, ke_ref, kn_ref, kw_ref, eb_ref, vb_ref, sbt_ref,
                           mkk, mv, ms, caus)
            og = _gla_post(o, proj_ref[r0:r0 + CHUNK, C_GR:C_GR + GLA_VW], gnw)
            for hh in range(GLA_HEADS):
                mixed_ref[r0:r0 + CHUNK, hh * GLA_DV:(hh + 1) * GLA_DV] = og[hh]

    x1_ref[0] = _out_proj_ln(x, mod[:, 2 * d:3 * d], mixed_ref, wo_ref, lng_ref[...], lnb_ref[...])

    kbuf[0:WINDOW, :] = kbuf[tl:tl + WINDOW, :]
    vbuf[0:WINDOW, :] = vbuf[tl:tl + WINDOW, :]

    @pl.when(j == nj - 1)
    def _():
        for hh in range(GLA_HEADS):
            s_out_ref[0, hh] = sbt_ref[hh * GLA_DV:(hh + 1) * GLA_DV, hh * GLA_DK:(hh + 1) * GLA_DK].T
        k_out_ref[0] = proj_ref[tl - WINDOW:tl, C_SK:C_SK + LANE]
        v_out_ref[0] = proj_ref[tl - WINDOW:tl, C_SV:C_SV + LANE]


def _mixer_prompt(x, mod, sinks, win, a2w, a2b, gnw, wo, lng, lnb, *, tl=512):
    b, l, d = x.shape
    nj = l // tl
    consts = _mixer_consts(tl, CHUNK, P_QB, P_KW)
    const2 = lambda i, j, s: (0, 0)
    const3 = lambda i, j, s: (0, 0, 0)
    grid_spec = pltpu.PrefetchScalarGridSpec(
        num_scalar_prefetch=1,
        grid=(b, nj),
        in_specs=[
            pl.BlockSpec((1, tl, d), lambda i, j, s: (i, j, 0)),
            pl.BlockSpec((1, 1, 6 * d), lambda i, j, s: (i, 0, 0)),
            pl.BlockSpec((d, PROJ_W), const2),
            pl.BlockSpec((LANE, GLA_KW), const2),
            pl.BlockSpec((1, GLA_KW), const2),
            pl.BlockSpec((1, GLA_DV), const2),
            pl.BlockSpec((d, d), const2),
            pl.BlockSpec((1, d), const2),
            pl.BlockSpec((1, d), const2),
            pl.BlockSpec((tl, tl), const2),
            pl.BlockSpec((SWA_KV_HEADS, SWA_GROUP * P_QB, P_KW), const3),
            pl.BlockSpec((GLA_HEADS * CHUNK, GLA_KW), const2),
            pl.BlockSpec((GLA_HEADS * CHUNK, GLA_VW), const2),
            pl.BlockSpec((GLA_VW, GLA_KW), const2),
            pl.BlockSpec((CHUNK, GLA_HEADS * CHUNK), const2),
        ],
        out_specs=[
            pl.BlockSpec((1, tl, d), lambda i, j, s: (i, j, 0)),
            pl.BlockSpec((1, GLA_HEADS, GLA_DK, GLA_DV), lambda i, j, s: (i, 0, 0, 0)),
            pl.BlockSpec((1, WINDOW, LANE), lambda i, j, s: (i, 0, 0)),
            pl.BlockSpec((1, WINDOW, LANE), lambda i, j, s: (i, 0, 0)),
        ],
        scratch_shapes=[
            pltpu.VMEM((tl, PROJ_W), F32),
            pltpu.VMEM((tl, GLA_KW), BF16),
            pltpu.VMEM((tl, GLA_KW), BF16),
            pltpu.VMEM((tl, GLA_KW), BF16),
            pltpu.VMEM((tl, GLA_KW), BF16),
            pltpu.VMEM((tl, GLA_KW), BF16),
            pltpu.VMEM((tl, GLA_KW), F32),
            pltpu.VMEM((tl, GLA_VW), BF16),
            pltpu.VMEM((WINDOW + tl, LANE), BF16),
            pltpu.VMEM((WINDOW + tl, LANE), BF16),
            pltpu.VMEM((tl, d), BF16),
            pltpu.VMEM((GLA_VW, GLA_KW), F32),
        ],
    )
    return pl.pallas_call(
        functools.partial(_mixer_prompt_kernel, tl=tl),
        out_shape=[
            jax.ShapeDtypeStruct((b, l, d), F32),
            jax.ShapeDtypeStruct((b, GLA_HEADS, GLA_DK, GLA_DV), F32),
            jax.ShapeDtypeStruct((b, WINDOW, LANE), F32),
            jax.ShapeDtypeStruct((b, WINDOW, LANE), F32),
        ],
        grid_spec=grid_spec,
        compiler_params=pltpu.CompilerParams(dimension_semantics=("arbitrary", "arbitrary"),
                                             vmem_limit_bytes=VMEM_LIMIT),
        name="mixer_prompt",
    )(sinks, x, mod, win, a2w, a2b, gnw, wo, lng, lnb, *consts)


def _mixer_sample_kernel(sinks_ref, x_ref, mod_ref, s0_ref, kc_ref, vc_ref, win_ref, a2w_ref, a2b_ref, gnw_ref,
                         wo_ref, lng_ref, lnb_ref, tri_ref, bias_ref, mkk_ref, mv_ref, ms_ref, caus_ref,
                         x1_ref, s_out_ref, k_out_ref, v_out_ref,
                         proj_ref, qe_ref, qn_ref, ke_ref, kn_ref, kw_ref, eb_ref, vb_ref, kbuf, vbuf, mixed_ref,
                         sbt_ref, xm_ref, *, nb, s):
    nkeys = WINDOW + s
    d = D_MODEL
    for bb in range(nb):
        m = mod_ref[bb]
        xm_ref[bb * s:(bb + 1) * s, :] = x_ref[bb] * (1.0 + m[:, d:2 * d]) + m[:, 0:d]
    loga = _in_proj(xm_ref[...].astype(BF16), win_ref, a2w_ref, a2b_ref, proj_ref)
    _gla_prep(loga, tri_ref, proj_ref, s, qe_ref, qn_ref, ke_ref, kn_ref, kw_ref, eb_ref, vb_ref)

    mkk = mkk_ref[...]
    mv = mv_ref[...]
    ms = ms_ref[...]
    caus = caus_ref[...]
    gnw = gnw_ref[...]
    sink_cols = _sink_cols(sinks_ref, s)

    for bb in range(nb):
        r0 = bb * s
        rows = slice(r0, r0 + s)
        sbt_ref[...] = jnp.zeros_like(sbt_ref)
        for hh in range(GLA_HEADS):
            sbt_ref[hh * GLA_DV:(hh + 1) * GLA_DV, hh * GLA_DK:(hh + 1) * GLA_DK] = s0_ref[bb, hh].T
        kbuf[0:WINDOW, :] = kc_ref[bb].astype(BF16)
        vbuf[0:WINDOW, :] = vc_ref[bb].astype(BF16)
        kbuf[WINDOW:nkeys, :] = proj_ref[rows, C_SK:C_SK + LANE].astype(BF16)
        vbuf[WINDOW:nkeys, :] = proj_ref[rows, C_SV:C_SV + LANE].astype(BF16)
        blocks = _swa_block(proj_ref[rows, C_SQ:C_SQ + SWA_W], kbuf[...], vbuf[...], bias_ref, sink_cols, None, s)
        for i, blk in enumerate(blocks):
            mixed_ref[rows, GLA_VW + i * LANE:GLA_VW + (i + 1) * LANE] = blk
        o = _gla_chunk(r0, s, qe_ref, qn_ref, ke_ref, kn_ref, kw_ref, eb_ref, vb_ref, sbt_ref, mkk, mv, ms, caus)
        og = _gla_post(o, proj_ref[rows, C_GR:C_GR + GLA_VW], gnw)
        for hh in range(GLA_HEADS):
            mixed_ref[rows, hh * GLA_DV:(hh + 1) * GLA_DV] = og[hh]
            s_out_ref[bb, hh] = sbt_ref[hh * GLA_DV:(hh + 1) * GLA_DV, hh * GLA_DK:(hh + 1) * GLA_DK].T
        k_out_ref[bb, 0:WINDOW - s, :] = kc_ref[bb, s:WINDOW, :]
        v_out_ref[bb, 0:WINDOW - s, :] = vc_ref[bb, s:WINDOW, :]
        k_out_ref[bb, WINDOW - s:WINDOW, :] = proj_ref[rows, C_SK:C_SK + LANE]
        v_out_ref[bb, WINDOW - s:WINDOW, :] = proj_ref[rows, C_SV:C_SV + LANE]

    mix = _mm(mixed_ref[...], wo_ref[...])
    lng = lng_ref[...]
    lnb = lnb_ref[...]
    for bb in range(nb):
        m = mod_ref[bb]
        y = DEEPNORM_ALPHA * x_ref[bb] + m[:, 2 * d:3 * d] * mix[bb * s:(bb + 1) * s, :]
        x1_ref[bb] = _layer_norm(y, lng, lnb)


def _mixer_sample(x, mod, s0, kc, vc, sinks, win, a2w, a2b, gnw, wo, lng, lnb, *, nb=8):
    b, s, d = x.shape
    assert kc.shape[1] == WINDOW and s <= WINDOW
    rows = nb * s
    nkeys = WINDOW + s
    consts = _mixer_consts(rows, s, s, nkeys)
    const2 = lambda i, sk: (0, 0)
    const3 = lambda i, sk: (0, 0, 0)
    grid_spec = pltpu.PrefetchScalarGridSpec(
        num_scalar_prefetch=1,
        grid=(b // nb,),
        in_specs=[
            pl.BlockSpec((nb, s, d), lambda i, sk: (i, 0, 0)),
            pl.BlockSpec((nb, 1, 6 * d), lambda i, sk: (i, 0, 0)),
            pl.BlockSpec((nb, GLA_HEADS, GLA_DK, GLA_DV), lambda i, sk: (i, 0, 0, 0)),
            pl.BlockSpec((nb, WINDOW, LANE), lambda i, sk: (i, 0, 0)),
            pl.BlockSpec((nb, WINDOW, LANE), lambda i, sk: (i, 0, 0)),
            pl.BlockSpec((d, PROJ_W), const2),
            pl.BlockSpec((LANE, GLA_KW), const2),
            pl.BlockSpec((1, GLA_KW), const2),
            pl.BlockSpec((1, GLA_DV), const2),
            pl.BlockSpec((d, d), const2),
            pl.BlockSpec((1, d), const2),
            pl.BlockSpec((1, d), const2),
            pl.BlockSpec((rows, rows), const2),
            pl.BlockSpec((SWA_KV_HEADS, SWA_GROUP * s, nkeys), const3),
            pl.BlockSpec((GLA_HEADS * s, GLA_KW), const2),
            pl.BlockSpec((GLA_HEADS * s, GLA_VW), const2),
            pl.BlockSpec((GLA_VW, GLA_KW), const2),
            pl.BlockSpec((s, GLA_HEADS * s), const2),
        ],
        out_specs=[
            pl.BlockSpec((nb, s, d), lambda i, sk: (i, 0, 0)),
            pl.BlockSpec((nb, GLA_HEADS, GLA_DK, GLA_DV), lambda i, sk: (i, 0, 0, 0)),
            pl.BlockSpec((nb, WINDOW, LANE), lambda i, sk: (i, 0, 0)),
            pl.BlockSpec((nb, WINDOW, LANE), lambda i, sk: (i, 0, 0)),
        ],
        scratch_shapes=[
            pltpu.VMEM((rows, PROJ_W), F32),
            pltpu.VMEM((rows, GLA_KW), BF16),
            pltpu.VMEM((rows, GLA_KW), BF16),
            pltpu.VMEM((rows, GLA_KW), BF16),
            pltpu.VMEM((rows, GLA_KW), BF16),
            pltpu.VMEM((rows, GLA_KW), BF16),
            pltpu.VMEM((rows, GLA_KW), F32),
            pltpu.VMEM((rows, GLA_VW), BF16),
            pltpu.VMEM((nkeys, LANE), BF16),
            pltpu.VMEM((nkeys, LANE), BF16),
            pltpu.VMEM((rows, d), BF16),
            pltpu.VMEM((GLA_VW, GLA_KW), F32),
            pltpu.VMEM((rows, d), F32),
        ],
    )
    return pl.pallas_call(
        functools.partial(_mixer_sample_kernel, nb=nb, s=s),
        out_shape=[
            jax.ShapeDtypeStruct((b, s, d), F32),
            jax.ShapeDtypeStruct((b, GLA_HEADS, GLA_DK, GLA_DV), F32),
            jax.ShapeDtypeStruct((b, WINDOW, LANE), F32),
            jax.ShapeDtypeStruct((b, WINDOW, LANE), F32),
        ],
        grid_spec=grid_spec,
        compiler_params=pltpu.CompilerParams(dimension_semantics=("arbitrary",),
                                             vmem_limit_bytes=VMEM_LIMIT),
        name="mixer_sample",
    )(sinks, x, mod, s0, kc, vc, win, a2w, a2b, gnw, wo, lng, lnb, *consts)


MOE_BLK = 128
MOE_HALF = EPG // 2


def _route_t(logits_t):
    t = logits_t.shape[1]
    row = lax.broadcasted_iota(jnp.int32, (EPG, t), 0).astype(F32)
    big = 99.0
    gl = jnp.where(row < N_GROUPS, logits_t[0:EPG, :], -jnp.inf)
    gmax = jnp.max(gl, axis=0, keepdims=True)
    grp = jnp.min(jnp.where(gl == gmax, row, big), axis=0, keepdims=True)
    p_grp = 1.0 / jnp.sum(jnp.exp(gl - gmax), axis=0, keepdims=True)
    el = jnp.zeros((EPG, t), F32)
    for g in range(N_GROUPS):
        el = el + jnp.where(grp == float(g), logits_t[R_EXP0 + EPG * g:R_EXP0 + EPG * (g + 1), :], 0.0)
    v1 = jnp.max(el, axis=0, keepdims=True)
    i1 = jnp.min(jnp.where(el == v1, row, big), axis=0, keepdims=True)
    el2 = jnp.where(row == i1, -jnp.inf, el)
    v2 = jnp.max(el2, axis=0, keepdims=True)
    i2 = jnp.min(jnp.where(el2 == v2, row, big), axis=0, keepdims=True)
    e2 = jnp.exp(v2 - v1)
    w1 = p_grp / (1.0 + e2)
    w2 = p_grp * e2 / (1.0 + e2)
    cw = jnp.where(row == i1, w1, 0.0) + jnp.where(row == i2, w2, 0.0)
    return grp, cw


def _moe_kernel(x1_ref, mod_ref, wrt_ref, brt_ref, wg_ref, wu_ref, wd_ref, lng_ref, lnb_ref,
                out_ref, t_ref, xs_ref, ys_ref, cws_ref, posc_ref, sm_ref, *, nb, r):
    s = pl.program_id(1)
    g = lax.shift_right_logical(s, 1)
    half = jnp.bitwise_and(s, 1)
    tm = nb * r
    tmp = tm + N_GROUPS * MOE_BLK
    d = D_MODEL
    pc = 256

    @pl.when(s == 0)
    def _():
        mod = mod_ref[...]
        t3 = x1_ref[...] * (1.0 + mod[:, :, 4 * d:5 * d]) + mod[:, :, 3 * d:4 * d]
        t = t3.reshape(tm, d).astype(BF16)
        t_ref[...] = t
        grp, cw = _route_t(_mm_nt(wrt_ref[...], t) + brt_ref[...])
        row = lax.broadcasted_iota(jnp.int32, (EPG, tm), 0).astype(F32)
        onehot_g = jnp.where(row == grp, 1.0, 0.0)
        upper = (lax.broadcasted_iota(jnp.int32, (tm, tm), 0)
                 < lax.broadcasted_iota(jnp.int32, (tm, tm), 1)).astype(F32).astype(BF16)
        rank = _mm(onehot_g.astype(BF16), upper)
        cnt = jnp.sum(onehot_g, axis=1, keepdims=True)
        nblk = jnp.floor((cnt + (MOE_BLK - 1)) * (1.0 / MOE_BLK))
        padded = nblk * MOE_BLK
        rowc = lax.broadcasted_iota(jnp.int32, (EPG, 1), 0)
        off = jnp.zeros((EPG, 1), F32)
        for gg in range(N_GROUPS - 1):
            off = off + jnp.where(rowc > gg, padded[gg:gg + 1, :], 0.0)
        pos = jnp.sum(onehot_g * (off + rank), axis=0, keepdims=True)
        off_i = off.astype(jnp.int32)
        nblk_i = nblk.astype(jnp.int32)
        for gg in range(N_GROUPS):
            sm_ref[gg] = off_i[gg, 0]
            sm_ref[N_GROUPS + gg] = nblk_i[gg, 0]
        posc_ref[...] = jnp.broadcast_to(pos, (LANE, tm)).T
        cw_pad = jnp.concatenate([cw, jnp.zeros((LANE - EPG, tm), F32)], axis=0)
        cw_hi, cw_lo = _split_bf16(cw_pad)
        for c in range(tmp // pc):
            slot = (lax.broadcasted_iota(jnp.int32, (pc, tm), 0) + c * pc).astype(F32)
            perm = jnp.where(slot == pos, 1.0, 0.0).astype(BF16)
            xs_ref[c * pc:(c + 1) * pc, :] = _mm(perm, t).astype(BF16)
            cws = _mm_nt(perm, cw_hi) + _mm_nt(perm, cw_lo)
            cws_ref[0, c * pc:(c + 1) * pc, :] = cws
            cws_ref[1, c * pc:(c + 1) * pc, :] = pltpu.roll(cws, LANE - MOE_HALF, axis=1)
        ys_ref[...] = jnp.zeros_like(ys_ref)

    off_g = sm_ref[g]
    nblk_g = sm_ref[N_GROUPS + g]
    wd = wd_ref[0].reshape(MOE_HALF * EXPERT_FF, d)

    def blk(k, carry):
        rows = pl.ds(pl.multiple_of(off_g + k * MOE_BLK, MOE_BLK), MOE_BLK)
        xb = xs_ref[rows, :]
        cwb = cws_ref[half, rows, :]
        hs = []
        for e in range(MOE_HALF):
            gg_ = _mm(xb, wg_ref[0, e])
            uu = _mm(xb, wu_ref[0, e])
            hs.append((gg_ * _sigmoid(gg_) * uu * cwb[:, e:e + 1]).astype(BF16))
        ys_ref[rows, :] += _mm(jnp.concatenate(hs, axis=1), wd)
        return carry

    lax.fori_loop(0, nblk_g, blk, 0)

    @pl.when(s == 2 * N_GROUPS - 1)
    def _():
        ysb = ys_ref[...].astype(BF16)
        lng = lng_ref[...]
        lnb = lnb_ref[...]
        for c in range(tm // pc):
            posc = posc_ref[c * pc:(c + 1) * pc, 0:1]
            slot = lax.broadcasted_iota(jnp.int32, (pc, tmp), 1).astype(F32)
            unperm = jnp.where(slot == posc, 1.0, 0.0).astype(BF16)
            y = _mm(unperm, ysb)
            if nb == 1:
                x1c = x1_ref[0, c * pc:(c + 1) * pc, :]
                gt2 = mod_ref[0][:, 5 * d:6 * d]
                out_ref[0, c * pc:(c + 1) * pc, :] = _layer_norm(DEEPNORM_ALPHA * x1c + gt2 * y, lng, lnb)
            else:
                cb = pc // r
                x1c = x1_ref[c * cb:(c + 1) * cb]
                gt2 = mod_ref[c * cb:(c + 1) * cb][:, :, 5 * d:6 * d]
                yy = DEEPNORM_ALPHA * x1c + gt2 * y.reshape(cb, r, d)
                out_ref[c * cb:(c + 1) * cb] = _layer_norm(yy, lng, lnb)


def _moe(x1, mod, wrt, brt, wg, wu, wd, lng, lnb, *, nb, r):
    b, l, d = x1.shape
    tpb = l // r
    ntiles = (b // nb) * tpb
    tm = nb * r
    tmp = tm + N_GROUPS * MOE_BLK
    xmap = lambda i, s: (i // tpb, i % tpb, 0)
    mmap = lambda i, s: (i // tpb, 0, 0)
    const2 = lambda i, s: (0, 0)
    wmap = lambda i, s: (s, 0, 0, 0)
    return pl.pallas_call(
        functools.partial(_moe_kernel, nb=nb, r=r),
        out_shape=jax.ShapeDtypeStruct((b, l, d), F32),
        grid=(ntiles, 2 * N_GROUPS),
        in_specs=[
            pl.BlockSpec((nb, r, d), xmap),
            pl.BlockSpec((nb, 1, 6 * d), mmap),
            pl.BlockSpec((LANE, d), const2),
            pl.BlockSpec((LANE, 1), const2),
            pl.BlockSpec((1, MOE_HALF, d, EXPERT_FF), wmap),
            pl.BlockSpec((1, MOE_HALF, d, EXPERT_FF), wmap),
            pl.BlockSpec((1, MOE_HALF, EXPERT_FF, d), wmap),
            pl.BlockSpec((1, d), const2),
            pl.BlockSpec((1, d), const2),
        ],
        out_specs=pl.BlockSpec((nb, r, d), xmap),
        scratch_shapes=[
            pltpu.VMEM((tm, d), BF16),
            pltpu.VMEM((tmp, d), BF16),
            pltpu.VMEM((tmp, d), F32),
            pltpu.VMEM((2, tmp, LANE), F32),
            pltpu.VMEM((tm, LANE), F32),
            pltpu.SMEM((2 * N_GROUPS,), jnp.int32),
        ],
        compiler_params=pltpu.CompilerParams(dimension_semantics=("arbitrary", "arbitrary"),
                                             vmem_limit_bytes=VMEM_LIMIT),
        name="moe",
    )(x1, mod, wrt, brt, wg, wu, wd, lng, lnb)


def kernel(x_prompt, x_sample, c_prompt, c_sample, state_gla, cache_swa_k, cache_swa_v, ada_w, ada_b, w_in,
           gla_a2_w, gla_a2_b, gla_norm_w, swa_sinks, w_o, ln1_g, ln1_b, router_g_w, router_g_b, router_e_w,
           router_e_b, moe_w_gate, moe_w_up, moe_w_down, ln2_g, ln2_b):
    assert ada_w.shape[0] == 1
    bp = x_prompt.shape[0]
    bs, ss, d = x_sample.shape
    lc = cache_swa_k.shape[2]

    w = w_in[0]
    zpad = jnp.zeros((d, LANE - GLA_RANK), F32)
    win = jnp.concatenate([w[:, 0:1536], w[:, 1552:2320], w[:, 1536:1552], zpad], axis=1).astype(BF16)
    a2w = jnp.concatenate([gla_a2_w[0], jnp.zeros((LANE - GLA_RANK, GLA_KW), F32)], axis=0).astype(BF16)
    a2b = gla_a2_b[0].reshape(1, GLA_KW)
    gnw = gla_norm_w[0].reshape(1, GLA_DV)
    wo = w_o[0].astype(BF16)
    sinks = swa_sinks[0]
    wrt = jnp.concatenate([router_g_w[0], jnp.zeros((d, R_EXP0 - N_GROUPS), F32),
                           jnp.transpose(router_e_w[0], (1, 0, 2)).reshape(d, N_GROUPS * EPG),
                           jnp.zeros((d, LANE - R_EXP0 - N_GROUPS * EPG), F32)], axis=1).T.astype(BF16)
    brt = jnp.concatenate([router_g_b[0], jnp.zeros((R_EXP0 - N_GROUPS,), F32), router_e_b[0].reshape(-1),
                           jnp.zeros((LANE - R_EXP0 - N_GROUPS * EPG,), F32)]).reshape(LANE, 1)
    nhalf = 2 * N_GROUPS
    wg = moe_w_gate[0].astype(BF16).reshape(nhalf, MOE_HALF, d, EXPERT_FF)
    wu = moe_w_up[0].astype(BF16).reshape(nhalf, MOE_HALF, d, EXPERT_FF)
    wd = moe_w_down[0].astype(BF16).reshape(nhalf, MOE_HALF, EXPERT_FF, d)
    lng1, lnb1 = ln1_g[0].reshape(1, d), ln1_b[0].reshape(1, d)
    lng2, lnb2 = ln2_g[0].reshape(1, d), ln2_b[0].reshape(1, d)

    mod = _adaln(jnp.concatenate([c_prompt, c_sample], axis=0), ada_w[0], ada_b[0].reshape(1, 6 * d))
    mod = mod.reshape(bp + bs, 1, 6 * d)
    mod_p, mod_s = mod[:bp], mod[bp:]

    x1p, s_p, k_p, v_p = _mixer_prompt(x_prompt, mod_p, sinks, win, a2w, a2b, gnw, wo, lng1, lnb1)
    x1s, s_s, k_s, v_s = _mixer_sample(
        x_sample, mod_s, state_gla[0], cache_swa_k[0].reshape(bs, lc, LANE), cache_swa_v[0].reshape(bs, lc, LANE),
        sinks, win, a2w, a2b, gnw, wo, lng1, lnb1)

    yp = _moe(x1p, mod_p, wrt, brt, wg, wu, wd, lng2, lnb2, nb=1, r=1024)
    ys = _moe(x1s, mod_s, wrt, brt, wg, wu, wd, lng2, lnb2, nb=bs, r=ss)

    kv_shape_p = (1, bp, WINDOW, SWA_KV_HEADS, SWA_DH)
    kv_shape_s = (1, bs, lc, SWA_KV_HEADS, SWA_DH)
    return (yp, ys, s_p[None], k_p.reshape(kv_shape_p), v_p.reshape(kv_shape_p),
            s_s[None], k_s.reshape(kv_shape_s), v_s.reshape(kv_shape_s))
```

```python
import functools

import jax
import jax.numpy as jnp
import numpy as np
from jax import lax
from jax.experimental import pallas as pl
from jax.experimental.pallas import tpu as pltpu

F32 = jnp.float32
BF16 = jnp.bfloat16

D_MODEL = 1024
CHUNK = 64
GLA_HEADS = 4
GLA_DK = 64
GLA_DV = 128
GLA_KW = GLA_HEADS * GLA_DK
GLA_VW = GLA_HEADS * GLA_DV
GLA_RANK = 16
GLA_TAU = 16.0
SWA_Q_HEADS = 8
SWA_KV_HEADS = 2
SWA_GROUP = 4
SWA_DH = 64
SWA_W = SWA_Q_HEADS * SWA_DH
WINDOW = 128
N_GROUPS = 4
EPG = 8
EXPERT_FF = 256
DEEPNORM_ALPHA = 2.0 ** 0.25
LN_EPS = 1e-5
NEG_INF = -1e30

C_GQ, C_GK, C_GV, C_GR, C_SQ, C_SK, C_SV, C_GA = 0, 256, 512, 1024, 1536, 2048, 2176, 2304
PROJ_W = 2432
LANE = 128
R_EXP0 = 8

VMEM_LIMIT = 56 * 1024 * 1024


def _mm(a, b):
    return jnp.dot(a, b, preferred_element_type=F32)


def _mm_nt(a, b):
    return lax.dot_general(a, b, (((1,), (1,)), ((), ())), preferred_element_type=F32)


def _mm_tn(a, b):
    return lax.dot_general(a, b, (((0,), (0,)), ((), ())), preferred_element_type=F32)


def _split_bf16(a):
    hi = a.astype(BF16)
    lo = (a - hi.astype(F32)).astype(BF16)
    return hi, lo


def _sigmoid(x):
    return 1.0 / (1.0 + jnp.exp(-x))


def _layer_norm(y, g, b):
    mu = jnp.mean(y, axis=-1, keepdims=True)
    d = y - mu
    var = jnp.mean(d * d, axis=-1, keepdims=True)
    return d * lax.rsqrt(var + LN_EPS) * g + b


def _adaln_kernel(c_ref, w_ref, b_ref, o_ref):
    c = c_ref[...]
    a = c * _sigmoid(c)
    a_hi, a_lo = _split_bf16(a)
    w_hi, w_lo = _split_bf16(w_ref[...])
    o_ref[...] = _mm(a_hi, w_hi) + (_mm(a_hi, w_lo) + _mm(a_lo, w_hi)) + b_ref[...]


def _adaln(c_all, ada_w, ada_b):
    n = c_all.shape[0]
    bn = 1024
    return pl.pallas_call(
        _adaln_kernel,
        out_shape=jax.ShapeDtypeStruct((n, 6 * D_MODEL), F32),
        grid=(6 * D_MODEL // bn,),
        in_specs=[pl.BlockSpec((n, D_MODEL), lambda j: (0, 0)),
                  pl.BlockSpec((D_MODEL, bn), lambda j: (0, j)),
                  pl.BlockSpec((1, bn), lambda j: (0, j))],
        out_specs=pl.BlockSpec((n, bn), lambda j: (0, j)),
        compiler_params=pltpu.CompilerParams(dimension_semantics=("arbitrary",), vmem_limit_bytes=VMEM_LIMIT),
        name="adaln",
    )(c_all, ada_w, ada_b)


def _mixer_consts(nrows, chunk, qb, kw):
    r = np.arange(nrows)
    tri = ((r[:, None] // chunk == r[None, :] // chunk) & (r[:, None] >= r[None, :])).astype(np.float32)
    hs = np.arange(GLA_HEADS * chunk)
    mkk = (hs[:, None] // chunk == np.arange(GLA_KW)[None, :] // GLA_DK).astype(np.float32)
    mv = (hs[:, None] // chunk == np.arange(GLA_VW)[None, :] // GLA_DV).astype(np.float32)
    ms = (np.arange(GLA_VW)[:, None] // GLA_DV == np.arange(GLA_KW)[None, :] // GLA_DK).astype(np.float32)
    caus = (np.arange(chunk)[:, None] >= (hs[None, :] % chunk)).astype(np.float32)
    t = np.arange(qb)
    kj = np.arange(kw)
    cs = (t // chunk) * chunk
    kpos = kj[None, :] - WINDOW
    vis = (kpos >= cs[:, None] - WINDOW) & (kpos < cs[:, None] + chunk)
    dist = np.abs(t[:, None] + WINDOW - kj[None, :]).astype(np.float32)
    bias = np.zeros((SWA_KV_HEADS, SWA_GROUP * qb, kw), np.float32)
    for hk in range(SWA_KV_HEADS):
        for g in range(SWA_GROUP):
            slope = np.float32(2.0 ** (-(hk * SWA_GROUP + g + 1)))
            bias[hk, g * qb:(g + 1) * qb] = np.where(vis, -slope * dist, np.float32(2.0 * NEG_INF))
    return (jnp.asarray(tri, BF16), jnp.asarray(bias), jnp.asarray(mkk, BF16), jnp.asarray(mv, BF16),
            jnp.asarray(ms), jnp.asarray(caus))


def _in_proj(h, win_ref, a2w_ref, a2b_ref, proj_ref):
    proj_ref[...] = _mm(h, win_ref[...])
    ga = proj_ref[:, C_GA:C_GA + LANE].astype(BF16)
    z = _mm(ga, a2w_ref[...]) + a2b_ref[...]
    lsig = -(jnp.maximum(-z, 0.0) + jnp.log(1.0 + jnp.exp(-jnp.abs(z))))
    return lsig * (1.0 / GLA_TAU)


def _gla_prep(loga, tri_ref, proj_ref, chunk, qe_ref, qn_ref, ke_ref, kn_ref, kw_ref, eb_ref, vb_ref):
    nrows = loga.shape[0]
    la_hi, la_lo = _split_bf16(loga)
    tri = tri_ref[...]
    b = _mm(tri, la_hi) + _mm(tri, la_lo)
    b_end = jnp.concatenate(
        [jnp.broadcast_to(b[c * chunk + chunk - 1:(c + 1) * chunk, :], (chunk, GLA_KW))
         for c in range(nrows // chunk)], axis=0)
    eb = jnp.exp(b)
    ebn = jnp.exp(-b)
    wk = jnp.exp(b_end - b)
    q = proj_ref[:, C_GQ:C_GQ + GLA_KW] * (GLA_DK ** -0.5)
    k = proj_ref[:, C_GK:C_GK + GLA_KW]
    eb_ref[...] = eb
    qe_ref[...] = (q * eb).astype(BF16)
    qn_ref[...] = (q * ebn).astype(BF16)
    ke_ref[...] = (k * eb).astype(BF16)
    kn_ref[...] = (k * ebn).astype(BF16)
    kw_ref[...] = (k * wk).astype(BF16)
    vb_ref[...] = proj_ref[:, C_GV:C_GV + GLA_VW].astype(BF16)


def _gla_chunk(r0, chunk, qe_ref, qn_ref, ke_ref, kn_ref, kw_ref, eb_ref, vb_ref, sbt_ref, mkk, mv, ms, caus):
    rows = slice(r0, r0 + chunk)
    qe = qe_ref[rows, :]
    qn = qn_ref[rows, :]
    zero = jnp.zeros((), BF16)
    kn4 = jnp.where(mkk != 0, jnp.concatenate([kn_ref[rows, :]] * GLA_HEADS, axis=0), zero)
    ke4 = jnp.where(mkk != 0, jnp.concatenate([ke_ref[rows, :]] * GLA_HEADS, axis=0), zero)
    a_lo = _mm_nt(qe, kn4)
    a_up = _mm_nt(qn, ke4)
    a = jnp.where(caus != 0.0, a_lo, a_up).astype(BF16)
    v = vb_ref[rows, :]
    v4 = jnp.where(mv != 0, jnp.concatenate([v] * GLA_HEADS, axis=0), zero)
    sbt = sbt_ref[...]
    o = _mm(a, v4) + _mm_nt(qe, sbt.astype(BF16))
    ut = _mm_tn(v, kw_ref[rows, :])
    g_end = eb_ref[r0 + chunk - 1:r0 + chunk, :]
    sbt_ref[...] = g_end * sbt + jnp.where(ms != 0.0, ut, 0.0)
    return o


def _gla_post(o, gr, gnw):
    res = []
    for h in range(GLA_HEADS):
        oh = o[:, h * GLA_DV:(h + 1) * GLA_DV]
        ms_ = jnp.mean(oh * oh, axis=-1, keepdims=True)
        og = oh * lax.rsqrt(ms_ + LN_EPS) * gnw
        r = gr[:, h * GLA_DV:(h + 1) * GLA_DV]
        res.append((og * (r * _sigmoid(r))).astype(BF16))
    return res


def _swa_problem(qs, kwin, vwin, bias, sinks_ref, hk, extra_valid, qb):
    s = _mm_nt(qs, kwin)
    ok = bias > -1e29
    if extra_valid is not None:
        ok = ok & extra_valid
    s = jnp.where(ok, s + bias, NEG_INF)
    ps, denoms = [], []
    for g in range(SWA_GROUP):
        sg = s[g * qb:(g + 1) * qb, :]
        sink = sinks_ref[hk * SWA_GROUP + g]
        m = jnp.maximum(jnp.max(sg, axis=-1, keepdims=True), sink)
        pg = jnp.exp(sg - m)
        denoms.append(jnp.sum(pg, axis=-1, keepdims=True) + jnp.exp(sink - m))
        ps.append(pg.astype(BF16))
    o = _mm(jnp.concatenate(ps, axis=0), vwin)
    return o / jnp.concatenate(denoms, axis=0)


def _swa_block(sq, kwin, vwin, bias_ref, sinks_ref, extra_valid, qb):
    low = lax.broadcasted_iota(jnp.int32, (qb, LANE), 1) < SWA_DH
    outs = []
    for hk in range(SWA_KV_HEADS):
        keep = low if hk == 0 else jnp.logical_not(low)
        qs = jnp.concatenate([jnp.where(keep, sq[:, g * LANE:(g + 1) * LANE] * (SWA_DH ** -0.5), 0.0)
                              for g in range(SWA_GROUP)], axis=0).astype(BF16)
        outs.append(_swa_problem(qs, kwin, vwin, bias_ref[hk], sinks_ref, hk, extra_valid, qb))
    return [jnp.where(low, outs[0][g * qb:(g + 1) * qb, :], outs[1][g * qb:(g + 1) * qb, :]).astype(BF16)
            for g in range(SWA_GROUP)]


def _out_proj_ln(x, gt1, mixed_ref, wo_ref, ln_g, ln_b):
    mix = _mm(mixed_ref[...], wo_ref[...])
    return _layer_norm(DEEPNORM_ALPHA * x + gt1 * mix, ln_g, ln_b)


P_QB = 128
P_KW = WINDOW + P_QB


def _mixer_prompt_kernel(sinks_ref, x_ref, mod_ref, win_ref, a2w_ref, a2b_ref, gnw_ref, wo_ref, lng_ref, lnb_ref,
                         tri_ref, bias_ref, mkk_ref, mv_ref, ms_ref, caus_ref,
                         x1_ref, s_out_ref, k_out_ref, v_out_ref,
                         proj_ref, qe_ref, qn_ref, ke_ref, kn_ref, kw_ref, eb_ref, vb_ref, kbuf, vbuf, mixed_ref,
                         sbt_ref, *, tl):
    j = pl.program_id(1)
    nj = pl.num_programs(1)
    d = D_MODEL

    @pl.when(j == 0)
    def _():
        sbt_ref[...] = jnp.zeros_like(sbt_ref)
        kbuf[0:WINDOW, :] = jnp.zeros((WINDOW, LANE), BF16)
        vbuf[0:WINDOW, :] = jnp.zeros((WINDOW, LANE), BF16)

    x = x_ref[0]
    mod = mod_ref[0]
    h = (x * (1.0 + mod[:, d:2 * d]) + mod[:, 0:d]).astype(BF16)
    loga = _in_proj(h, win_ref, a2w_ref, a2b_ref, proj_ref)
    _gla_prep(loga, tri_ref, proj_ref, CHUNK, qe_ref, qn_ref, ke_ref, kn_ref, kw_ref, eb_ref, vb_ref)
    kbuf[WINDOW:WINDOW + tl, :] = proj_ref[:, C_SK:C_SK + LANE].astype(BF16)
    vbuf[WINDOW:WINDOW + tl, :] = proj_ref[:, C_SV:C_SV + LANE].astype(BF16)

    mkk = mkk_ref[...]
    mv = mv_ref[...]
    ms = ms_ref[...]
    caus = caus_ref[...]
    gnw = gnw_ref[...]
    kj = lax.broadcasted_iota(jnp.int32, (SWA_GROUP * P_QB, P_KW), 1)
    first_valid = kj >= jnp.where(j > 0, 0, WINDOW)

    for p in range(tl // P_QB):
        q0 = p * P_QB
        blocks = _swa_block(proj_ref[q0:q0 + P_QB, C_SQ:C_SQ + SWA_W], kbuf[q0:q0 + P_KW, :], vbuf[q0:q0 + P_KW, :],
                            bias_ref, sinks_ref, first_valid if p == 0 else None, P_QB)
        for i, blk in enumerate(blocks):
            mixed_ref[q0:q0 + P_QB, GLA_VW + i * LANE:GLA_VW + (i + 1) * LANE] = blk
        for c in range(P_QB // CHUNK):
            r0 = q0 + c * CHUNK
            o = _gla_chunk(r0, CHUNK, qe_ref, qn_ref, ke_ref, kn_ref, kw_ref, eb_ref, vb_ref, sbt_ref,
                           mkk, mv, ms, caus)
            og = _gla_post(o, proj_ref[r0:r0 + CHUNK, C_GR:C_GR + GLA_VW], gnw)
            for hh in range(GLA_HEADS):
                mixed_ref[r0:r0 + CHUNK, hh * GLA_DV:(hh + 1) * GLA_DV] = og[hh]

    x1_ref[0] = _out_proj_ln(x, mod[:, 2 * d:3 * d], mixed_ref, wo_ref, lng_ref[...], lnb_ref[...])

    kbuf[0:WINDOW, :] = kbuf[tl:tl + WINDOW, :]
    vbuf[0:WINDOW, :] = vbuf[tl:tl + WINDOW, :]

    @pl.when(j == nj - 1)
    def _():
        for hh in range(GLA_HEADS):
            s_out_ref[0, hh] = sbt_ref[hh * GLA_DV:(hh + 1) * GLA_DV, hh * GLA_DK:(hh + 1) * GLA_DK].T
        k_out_ref[0] = proj_ref[tl - WINDOW:tl, C_SK:C_SK + LANE]
        v_out_ref[0] = proj_ref[tl - WINDOW:tl, C_SV:C_SV + LANE]


def _mixer_prompt(x, mod, sinks, win, a2w, a2b, gnw, wo, lng, lnb, *, tl=512):
    b, l, d = x.shape
    nj = l // tl
    consts = _mixer_consts(tl, CHUNK, P_QB, P_KW)
    const2 = lambda i, j, s: (0, 0)
    const3 = lambda i, j, s: (0, 0, 0)
    grid_spec = pltpu.PrefetchScalarGridSpec(
        num_scalar_prefetch=1,
        grid=(b, nj),
        in_specs=[
            pl.BlockSpec((1, tl, d), lambda i, j, s: (i, j, 0)),
            pl.BlockSpec((1, 1, 6 * d), lambda i, j, s: (i, 0, 0)),
            pl.BlockSpec((d, PROJ_W), const2),
            pl.BlockSpec((LANE, GLA_KW), const2),
            pl.BlockSpec((1, GLA_KW), const2),
            pl.BlockSpec((1, GLA_DV), const2),
            pl.BlockSpec((d, d), const2),
            pl.BlockSpec((1, d), const2),
            pl.BlockSpec((1, d), const2),
            pl.BlockSpec((tl, tl), const2),
            pl.BlockSpec((SWA_KV_HEADS, SWA_GROUP * P_QB, P_KW), const3),
            pl.BlockSpec((GLA_HEADS * CHUNK, GLA_KW), const2),
            pl.BlockSpec((GLA_HEADS * CHUNK, GLA_VW), const2),
            pl.BlockSpec((GLA_VW, GLA_KW), const2),
            pl.BlockSpec((CHUNK, GLA_HEADS * CHUNK), const2),
        ],
        out_specs=[
            pl.BlockSpec((1, tl, d), lambda i, j, s: (i, j, 0)),
            pl.BlockSpec((1, GLA_HEADS, GLA_DK, GLA_DV), lambda i, j, s: (i, 0, 0, 0)),
            pl.BlockSpec((1, WINDOW, LANE), lambda i, j, s: (i, 0, 0)),
            pl.BlockSpec((1, WINDOW, LANE), lambda i, j, s: (i, 0, 0)),
        ],
        scratch_shapes=[
            pltpu.VMEM((tl, PROJ_W), F32),
            pltpu.VMEM((tl, GLA_KW), BF16),
            pltpu.VMEM((tl, GLA_KW), BF16),
            pltpu.VMEM((tl, GLA_KW), BF16),
            pltpu.VMEM((tl, GLA_KW), BF16),
            pltpu.VMEM((tl, GLA_KW), BF16),
            pltpu.VMEM((tl, GLA_KW), F32),
            pltpu.VMEM((tl, GLA_VW), BF16),
            pltpu.VMEM((WINDOW + tl, LANE), BF16),
            pltpu.VMEM((WINDOW + tl, LANE), BF16),
            pltpu.VMEM((tl, d), BF16),
            pltpu.VMEM((GLA_VW, GLA_KW), F32),
        ],
    )
    return pl.pallas_call(
        functools.partial(_mixer_prompt_kernel, tl=tl),
        out_shape=[
            jax.ShapeDtypeStruct((b, l, d), F32),
            jax.ShapeDtypeStruct((b, GLA_HEADS, GLA_DK, GLA_DV), F32),
            jax.ShapeDtypeStruct((b, WINDOW, LANE), F32),
            jax.ShapeDtypeStruct((b, WINDOW, LANE), F32),
        ],
        grid_spec=grid_spec,
        compiler_params=pltpu.CompilerParams(dimension_semantics=("arbitrary", "arbitrary"),
                                             vmem_limit_bytes=VMEM_LIMIT),
        name="mixer_prompt",
    )(sinks, x, mod, win, a2w, a2b, gnw, wo, lng, lnb, *consts)


def _mixer_sample_kernel(sinks_ref, x_ref, mod_ref, s0_ref, kc_ref, vc_ref, win_ref, a2w_ref, a2b_ref, gnw_ref,
                         wo_ref, lng_ref, lnb_ref, tri_ref, bias_ref, mkk_ref, mv_ref, ms_ref, caus_ref,
                         x1_ref, s_out_ref, k_out_ref, v_out_ref,
                         proj_ref, qe_ref, qn_ref, ke_ref, kn_ref, kw_ref, eb_ref, vb_ref, kbuf, vbuf, mixed_ref,
                         sbt_ref, xm_ref, *, nb, s):
    nkeys = WINDOW + s
    d = D_MODEL
    for bb in range(nb):
        m = mod_ref[bb]
        xm_ref[bb * s:(bb + 1) * s, :] = x_ref[bb] * (1.0 + m[:, d:2 * d]) + m[:, 0:d]
    loga = _in_proj(xm_ref[...].astype(BF16), win_ref, a2w_ref, a2b_ref, proj_ref)
    _gla_prep(loga, tri_ref, proj_ref, s, qe_ref, qn_ref, ke_ref, kn_ref, kw_ref, eb_ref, vb_ref)

    mkk = mkk_ref[...]
    mv = mv_ref[...]
    ms = ms_ref[...]
    caus = caus_ref[...]
    gnw = gnw_ref[...]

    for bb in range(nb):
        r0 = bb * s
        rows = slice(r0, r0 + s)
        sbt_ref[...] = jnp.zeros_like(sbt_ref)
        for hh in range(GLA_HEADS):
            sbt_ref[hh * GLA_DV:(hh + 1) * GLA_DV, hh * GLA_DK:(hh + 1) * GLA_DK] = s0_ref[bb, hh].T
        kbuf[0:WINDOW, :] = kc_ref[bb].astype(BF16)
        vbuf[0:WINDOW, :] = vc_ref[bb].astype(BF16)
        kbuf[WINDOW:nkeys, :] = proj_ref[rows, C_SK:C_SK + LANE].astype(BF16)
        vbuf[WINDOW:nkeys, :] = proj_ref[rows, C_SV:C_SV + LANE].astype(BF16)
        blocks = _swa_block(proj_ref[rows, C_SQ:C_SQ + SWA_W], kbuf[...], vbuf[...], bias_ref, sinks_ref, None, s)
        for i, blk in enumerate(blocks):
            mixed_ref[rows, GLA_VW + i * LANE:GLA_VW + (i + 1) * LANE] = blk
        o = _gla_chunk(r0, s, qe_ref, qn_ref, ke_ref, kn_ref, kw_ref, eb_ref, vb_ref, sbt_ref, mkk, mv, ms, caus)
        og = _gla_post(o, proj_ref[rows, C_GR:C_GR + GLA_VW], gnw)
        for hh in range(GLA_HEADS):
            mixed_ref[rows, hh * GLA_DV:(hh + 1) * GLA_DV] = og[hh]
            s_out_ref[bb, hh] = sbt_ref[hh * GLA_DV:(hh + 1) * GLA_DV, hh * GLA_DK:(hh + 1) * GLA_DK].T
        k_out_ref[bb, 0:WINDOW - s, :] = kc_ref[bb, s:WINDOW, :]
        v_out_ref[bb, 0:WINDOW - s, :] = vc_ref[bb, s:WINDOW, :]
        k_out_ref[bb, WINDOW - s:WINDOW, :] = proj_ref[rows, C_SK:C_SK + LANE]
        v_out_ref[bb, WINDOW - s:WINDOW, :] = proj_ref[rows, C_SV:C_SV + LANE]

    mix = _mm(mixed_ref[...], wo_ref[...])
    lng = lng_ref[...]
    lnb = lnb_ref[...]
    for bb in range(nb):
        m = mod_ref[bb]
        y = DEEPNORM_ALPHA * x_ref[bb] + m[:, 2 * d:3 * d] * mix[bb * s:(bb + 1) * s, :]
        x1_ref[bb] = _layer_norm(y, lng, lnb)


def _mixer_sample(x, mod, s0, kc, vc, sinks, win, a2w, a2b, gnw, wo, lng, lnb, *, nb=8):
    b, s, d = x.shape
    assert kc.shape[1] == WINDOW and s <= WINDOW
    rows = nb * s
    nkeys = WINDOW + s
    consts = _mixer_consts(rows, s, s, nkeys)
    const2 = lambda i, sk: (0, 0)
    const3 = lambda i, sk: (0, 0, 0)
    grid_spec = pltpu.PrefetchScalarGridSpec(
        num_scalar_prefetch=1,
        grid=(b // nb,),
        in_specs=[
            pl.BlockSpec((nb, s, d), lambda i, sk: (i, 0, 0)),
            pl.BlockSpec((nb, 1, 6 * d), lambda i, sk: (i, 0, 0)),
            pl.BlockSpec((nb, GLA_HEADS, GLA_DK, GLA_DV), lambda i, sk: (i, 0, 0, 0)),
            pl.BlockSpec((nb, WINDOW, LANE), lambda i, sk: (i, 0, 0)),
            pl.BlockSpec((nb, WINDOW, LANE), lambda i, sk: (i, 0, 0)),
            pl.BlockSpec((d, PROJ_W), const2),
            pl.BlockSpec((LANE, GLA_KW), const2),
            pl.BlockSpec((1, GLA_KW), const2),
            pl.BlockSpec((1, GLA_DV), const2),
            pl.BlockSpec((d, d), const2),
            pl.BlockSpec((1, d), const2),
            pl.BlockSpec((1, d), const2),
            pl.BlockSpec((rows, rows), const2),
            pl.BlockSpec((SWA_KV_HEADS, SWA_GROUP * s, nkeys), const3),
            pl.BlockSpec((GLA_HEADS * s, GLA_KW), const2),
            pl.BlockSpec((GLA_HEADS * s, GLA_VW), const2),
            pl.BlockSpec((GLA_VW, GLA_KW), const2),
            pl.BlockSpec((s, GLA_HEADS * s), const2),
        ],
        out_specs=[
            pl.BlockSpec((nb, s, d), lambda i, sk: (i, 0, 0)),
            pl.BlockSpec((nb, GLA_HEADS, GLA_DK, GLA_DV), lambda i, sk: (i, 0, 0, 0)),
            pl.BlockSpec((nb, WINDOW, LANE), lambda i, sk: (i, 0, 0)),
            pl.BlockSpec((nb, WINDOW, LANE), lambda i, sk: (i, 0, 0)),
        ],
        scratch_shapes=[
            pltpu.VMEM((rows, PROJ_W), F32),
            pltpu.VMEM((rows, GLA_KW), BF16),
            pltpu.VMEM((rows, GLA_KW), BF16),
            pltpu.VMEM((rows, GLA_KW), BF16),
            pltpu.VMEM((rows, GLA_KW), BF16),
            pltpu.VMEM((rows, GLA_KW), BF16),
            pltpu.VMEM((rows, GLA_KW), F32),
            pltpu.VMEM((rows, GLA_VW), BF16),
            pltpu.VMEM((nkeys, LANE), BF16),
            pltpu.VMEM((nkeys, LANE), BF16),
            pltpu.VMEM((rows, d), BF16),
            pltpu.VMEM((GLA_VW, GLA_KW), F32),
            pltpu.VMEM((rows, d), F32),
        ],
    )
    return pl.pallas_call(
        functools.partial(_mixer_sample_kernel, nb=nb, s=s),
        out_shape=[
            jax.ShapeDtypeStruct((b, s, d), F32),
            jax.ShapeDtypeStruct((b, GLA_HEADS, GLA_DK, GLA_DV), F32),
            jax.ShapeDtypeStruct((b, WINDOW, LANE), F32),
            jax.ShapeDtypeStruct((b, WINDOW, LANE), F32),
        ],
        grid_spec=grid_spec,
        compiler_params=pltpu.CompilerParams(dimension_semantics=("arbitrary",),
                                             vmem_limit_bytes=VMEM_LIMIT),
        name="mixer_sample",
    )(sinks, x, mod, s0, kc, vc, win, a2w, a2b, gnw, wo, lng, lnb, *consts)


MOE_BLK = 128
MOE_HALF = EPG // 2


def _route_t(logits_t):
    t = logits_t.shape[1]
    row = lax.broadcasted_iota(jnp.int32, (EPG, t), 0).astype(F32)
    big = 99.0
    gl = jnp.where(row < N_GROUPS, logits_t[0:EPG, :], -jnp.inf)
    gmax = jnp.max(gl, axis=0, keepdims=True)
    grp = jnp.min(jnp.where(gl == gmax, row, big), axis=0, keepdims=True)
    p_grp = 1.0 / jnp.sum(jnp.exp(gl - gmax), axis=0, keepdims=True)
    el = jnp.zeros((EPG, t), F32)
    for g in range(N_GROUPS):
        el = el + jnp.where(grp == float(g), logits_t[R_EXP0 + EPG * g:R_EXP0 + EPG * (g + 1), :], 0.0)
    v1 = jnp.max(el, axis=0, keepdims=True)
    i1 = jnp.min(jnp.where(el == v1, row, big), axis=0, keepdims=True)
    el2 = jnp.where(row == i1, -jnp.inf, el)
    v2 = jnp.max(el2, axis=0, keepdims=True)
    i2 = jnp.min(jnp.where(el2 == v2, row, big), axis=0, keepdims=True)
    e2 = jnp.exp(v2 - v1)
    w1 = p_grp / (1.0 + e2)
    w2 = p_grp * e2 / (1.0 + e2)
    cw = jnp.where(row == i1, w1, 0.0) + jnp.where(row == i2, w2, 0.0)
    return grp, cw


def _moe_kernel(x1_ref, mod_ref, wrt_ref, brt_ref, wg_ref, wu_ref, wd_ref, lng_ref, lnb_ref,
                out_ref, t_ref, xs_ref, ys_ref, cws_ref, posc_ref, sm_ref, *, nb, r):
    s = pl.program_id(1)
    g = lax.shift_right_logical(s, 1)
    half = jnp.bitwise_and(s, 1)
    tm = nb * r
    tmp = tm + N_GROUPS * MOE_BLK
    d = D_MODEL
    pc = 256

    @pl.when(s == 0)
    def _():
        mod = mod_ref[...]
        t3 = x1_ref[...] * (1.0 + mod[:, :, 4 * d:5 * d]) + mod[:, :, 3 * d:4 * d]
        t = t3.reshape(tm, d).astype(BF16)
        t_ref[...] = t
        grp, cw = _route_t(_mm_nt(wrt_ref[...], t) + brt_ref[...])
        row = lax.broadcasted_iota(jnp.int32, (EPG, tm), 0).astype(F32)
        onehot_g = jnp.where(row == grp, 1.0, 0.0)
        upper = (lax.broadcasted_iota(jnp.int32, (tm, tm), 0)
                 < lax.broadcasted_iota(jnp.int32, (tm, tm), 1)).astype(F32).astype(BF16)
        rank = _mm(onehot_g.astype(BF16), upper)
        cnt = jnp.sum(onehot_g, axis=1, keepdims=True)
        nblk = jnp.floor((cnt + (MOE_BLK - 1)) * (1.0 / MOE_BLK))
        padded = nblk * MOE_BLK
        rowc = lax.broadcasted_iota(jnp.int32, (EPG, 1), 0)
        off = jnp.zeros((EPG, 1), F32)
        for gg in range(N_GROUPS - 1):
            off = off + jnp.where(rowc > gg, padded[gg:gg + 1, :], 0.0)
        pos = jnp.sum(onehot_g * (off + rank), axis=0, keepdims=True)
        off_i = off.astype(jnp.int32)
        nblk_i = nblk.astype(jnp.int32)
        for gg in range(N_GROUPS):
            sm_ref[gg] = off_i[gg, 0]
            sm_ref[N_GROUPS + gg] = nblk_i[gg, 0]
        posc_ref[...] = jnp.broadcast_to(pos, (LANE, tm)).T
        cw_pad = jnp.concatenate([cw, jnp.zeros((LANE - EPG, tm), F32)], axis=0)
        cw_hi, cw_lo = _split_bf16(cw_pad)
        for c in range(tmp // pc):
            slot = (lax.broadcasted_iota(jnp.int32, (pc, tm), 0) + c * pc).astype(F32)
            perm = jnp.where(slot == pos, 1.0, 0.0).astype(BF16)
            xs_ref[c * pc:(c + 1) * pc, :] = _mm(perm, t).astype(BF16)
            cws = _mm_nt(perm, cw_hi) + _mm_nt(perm, cw_lo)
            cws_ref[0, c * pc:(c + 1) * pc, :] = cws
            cws_ref[1, c * pc:(c + 1) * pc, :] = pltpu.roll(cws, LANE - MOE_HALF, axis=1)
        ys_ref[...] = jnp.zeros_like(ys_ref)

    off_g = sm_ref[g]
    nblk_g = sm_ref[N_GROUPS + g]
    wd = wd_ref[0].reshape(MOE_HALF * EXPERT_FF, d)

    def blk(k, carry):
        rows = pl.ds(pl.multiple_of(off_g + k * MOE_BLK, MOE_BLK), MOE_BLK)
        xb = xs_ref[rows, :]
        cwb = cws_ref[half, rows, :]
        hs = []
        for e in range(MOE_HALF):
            gg_ = _mm(xb, wg_ref[0, e])
            uu = _mm(xb, wu_ref[0, e])
            hs.append((gg_ * _sigmoid(gg_) * uu * cwb[:, e:e + 1]).astype(BF16))
        ys_ref[rows, :] += _mm(jnp.concatenate(hs, axis=1), wd)
        return carry

    lax.fori_loop(0, nblk_g, blk, 0)

    @pl.when(s == 2 * N_GROUPS - 1)
    def _():
        ysb = ys_ref[...].astype(BF16)
        lng = lng_ref[...]
        lnb = lnb_ref[...]
        for c in range(tm // pc):
            posc = posc_ref[c * pc:(c + 1) * pc, 0:1]
            slot = lax.broadcasted_iota(jnp.int32, (pc, tmp), 1).astype(F32)
            unperm = jnp.where(slot == posc, 1.0, 0.0).astype(BF16)
            y = _mm(unperm, ysb)
            if nb == 1:
                x1c = x1_ref[0, c * pc:(c + 1) * pc, :]
                gt2 = mod_ref[0][:, 5 * d:6 * d]
                out_ref[0, c * pc:(c + 1) * pc, :] = _layer_norm(DEEPNORM_ALPHA * x1c + gt2 * y, lng, lnb)
            else:
                cb = pc // r
                x1c = x1_ref[c * cb:(c + 1) * cb]
                gt2 = mod_ref[c * cb:(c + 1) * cb][:, :, 5 * d:6 * d]
                yy = DEEPNORM_ALPHA * x1c + gt2 * y.reshape(cb, r, d)
                out_ref[c * cb:(c + 1) * cb] = _layer_norm(yy, lng, lnb)


def _moe(x1, mod, wrt, brt, wg, wu, wd, lng, lnb, *, nb, r):
    b, l, d = x1.shape
    tpb = l // r
    ntiles = (b // nb) * tpb
    tm = nb * r
    tmp = tm + N_GROUPS * MOE_BLK
    xmap = lambda i, s: (i // tpb, i % tpb, 0)
    mmap = lambda i, s: (i // tpb, 0, 0)
    const2 = lambda i, s: (0, 0)
    wmap = lambda i, s: (s, 0, 0, 0)
    return pl.pallas_call(
        functools.partial(_moe_kernel, nb=nb, r=r),
        out_shape=jax.ShapeDtypeStruct((b, l, d), F32),
        grid=(ntiles, 2 * N_GROUPS),
        in_specs=[
            pl.BlockSpec((nb, r, d), xmap),
            pl.BlockSpec((nb, 1, 6 * d), mmap),
            pl.BlockSpec((LANE, d), const2),
            pl.BlockSpec((LANE, 1), const2),
            pl.BlockSpec((1, MOE_HALF, d, EXPERT_FF), wmap),
            pl.BlockSpec((1, MOE_HALF, d, EXPERT_FF), wmap),
            pl.BlockSpec((1, MOE_HALF, EXPERT_FF, d), wmap),
            pl.BlockSpec((1, d), const2),
            pl.BlockSpec((1, d), const2),
        ],
        out_specs=pl.BlockSpec((nb, r, d), xmap),
        scratch_shapes=[
            pltpu.VMEM((tm, d), BF16),
            pltpu.VMEM((tmp, d), BF16),
            pltpu.VMEM((tmp, d), F32),
            pltpu.VMEM((2, tmp, LANE), F32),
            pltpu.VMEM((tm, LANE), F32),
            pltpu.SMEM((2 * N_GROUPS,), jnp.int32),
        ],
        compiler_params=pltpu.CompilerParams(dimension_semantics=("arbitrary", "arbitrary"),
                                             vmem_limit_bytes=VMEM_LIMIT),
        name="moe",
    )(x1, mod, wrt, brt, wg, wu, wd, lng, lnb)


def kernel(x_prompt, x_sample, c_prompt, c_sample, state_gla, cache_swa_k, cache_swa_v, ada_w, ada_b, w_in,
           gla_a2_w, gla_a2_b, gla_norm_w, swa_sinks, w_o, ln1_g, ln1_b, router_g_w, router_g_b, router_e_w,
           router_e_b, moe_w_gate, moe_w_up, moe_w_down, ln2_g, ln2_b):
    assert ada_w.shape[0] == 1
    bp = x_prompt.shape[0]
    bs, ss, d = x_sample.shape
    lc = cache_swa_k.shape[2]

    w = w_in[0]
    zpad = jnp.zeros((d, LANE - GLA_RANK), F32)
    n = np.arange(SWA_W)
    swa_perm = ((n // SWA_DH) % SWA_KV_HEADS) * (SWA_GROUP * SWA_DH) + (n // LANE) * SWA_DH + n % SWA_DH
    win = jnp.concatenate([w[:, 0:1536], w[:, 1552:2064][:, swa_perm], w[:, 2064:2320], w[:, 1536:1552], zpad],
                          axis=1).astype(BF16)
    a2w = jnp.concatenate([gla_a2_w[0], jnp.zeros((LANE - GLA_RANK, GLA_KW), F32)], axis=0).astype(BF16)
    a2b = gla_a2_b[0].reshape(1, GLA_KW)
    gnw = gla_norm_w[0].reshape(1, GLA_DV)
    wo = jnp.concatenate([w_o[0][:GLA_VW], w_o[0][GLA_VW:][swa_perm]], axis=0).astype(BF16)
    sinks = swa_sinks[0]
    wrt = jnp.concatenate([router_g_w[0], jnp.zeros((d, R_EXP0 - N_GROUPS), F32),
                           jnp.transpose(router_e_w[0], (1, 0, 2)).reshape(d, N_GROUPS * EPG),
                           jnp.zeros((d, LANE - R_EXP0 - N_GROUPS * EPG), F32)], axis=1).T.astype(BF16)
    brt = jnp.concatenate([router_g_b[0], jnp.zeros((R_EXP0 - N_GROUPS,), F32), router_e_b[0].reshape(-1),
                           jnp.zeros((LANE - R_EXP0 - N_GROUPS * EPG,), F32)]).reshape(LANE, 1)
    nhalf = 2 * N_GROUPS
    wg = moe_w_gate[0].astype(BF16).reshape(nhalf, MOE_HALF, d, EXPERT_FF)
    wu = moe_w_up[0].astype(BF16).reshape(nhalf, MOE_HALF, d, EXPERT_FF)
    wd = moe_w_down[0].astype(BF16).reshape(nhalf, MOE_HALF, EXPERT_FF, d)
    lng1, lnb1 = ln1_g[0].reshape(1, d), ln1_b[0].reshape(1, d)
    lng2, lnb2 = ln2_g[0].reshape(1, d), ln2_b[0].reshape(1, d)

    mod = _adaln(jnp.concatenate([c_prompt, c_sample], axis=0), ada_w[0], ada_b[0].reshape(1, 6 * d))
    mod = mod.reshape(bp + bs, 1, 6 * d)
    mod_p, mod_s = mod[:bp], mod[bp:]

    x1p, s_p, k_p, v_p = _mixer_prompt(x_prompt, mod_p, sinks, win, a2w, a2b, gnw, wo, lng1, lnb1)
    x1s, s_s, k_s, v_s = _mixer_sample(
        x_sample, mod_s, state_gla[0], cache_swa_k[0].reshape(bs, lc, LANE), cache_swa_v[0].reshape(bs, lc, LANE),
        sinks, win, a2w, a2b, gnw, wo, lng1, lnb1)

    yp = _moe(x1p, mod_p, wrt, brt, wg, wu, wd, lng2, lnb2, nb=1, r=1024)
    ys = _moe(x1s, mod_s, wrt, brt, wg, wu, wd, lng2, lnb2, nb=bs, r=ss)

    kv_shape_p = (1, bp, WINDOW, SWA_KV_HEADS, SWA_DH)
    kv_shape_s = (1, bs, lc, SWA_KV_HEADS, SWA_DH)
    return (yp, ys, s_p[None], k_p.reshape(kv_shape_p), v_p.reshape(kv_shape_p),
            s_s[None], k_s.reshape(kv_shape_s), v_s.reshape(kv_shape_s))
```

```python
import functools

import jax
import jax.numpy as jnp
import numpy as np
from jax import lax
from jax.experimental import pallas as pl
from jax.experimental.pallas import tpu as pltpu

F32 = jnp.float32
BF16 = jnp.bfloat16

D_MODEL = 1024
CHUNK = 64
GLA_HEADS = 4
GLA_DK = 64
GLA_DV = 128
GLA_KW = GLA_HEADS * GLA_DK
GLA_VW = GLA_HEADS * GLA_DV
GLA_RANK = 16
GLA_TAU = 16.0
SWA_Q_HEADS = 8
SWA_KV_HEADS = 2
SWA_GROUP = 4
SWA_DH = 64
SWA_W = SWA_Q_HEADS * SWA_DH
WINDOW = 128
N_GROUPS = 4
EPG = 8
EXPERT_FF = 256
DEEPNORM_ALPHA = 2.0 ** 0.25
LN_EPS = 1e-5
NEG_INF = -1e30

C_GQ, C_GK, C_GV, C_GR, C_SQ, C_SK, C_SV, C_GA = 0, 256, 512, 1024, 1536, 2048, 2176, 2304
PROJ_W = 2432
LANE = 128
R_EXP0 = 8

VMEM_LIMIT = 56 * 1024 * 1024


def _mm(a, b):
    return jnp.dot(a, b, preferred_element_type=F32)


def _mm_nt(a, b):
    return lax.dot_general(a, b, (((1,), (1,)), ((), ())), preferred_element_type=F32)


def _mm_tn(a, b):
    return lax.dot_general(a, b, (((0,), (0,)), ((), ())), preferred_element_type=F32)


def _split_bf16(a):
    hi = a.astype(BF16)
    lo = (a - hi.astype(F32)).astype(BF16)
    return hi, lo


def _sigmoid(x):
    return 1.0 / (1.0 + jnp.exp(-x))


def _layer_norm(y, g, b):
    mu = jnp.mean(y, axis=-1, keepdims=True)
    d = y - mu
    var = jnp.mean(d * d, axis=-1, keepdims=True)
    return d * lax.rsqrt(var + LN_EPS) * g + b


def _adaln_kernel(c_ref, w_ref, b_ref, o_ref):
    c = c_ref[...]
    a = c * _sigmoid(c)
    a_hi, a_lo = _split_bf16(a)
    w_hi, w_lo = _split_bf16(w_ref[...])
    o_ref[...] = _mm(a_hi, w_hi) + (_mm(a_hi, w_lo) + _mm(a_lo, w_hi)) + b_ref[...]


def _adaln(c_all, ada_w, ada_b):
    n = c_all.shape[0]
    bn = 1024
    return pl.pallas_call(
        _adaln_kernel,
        out_shape=jax.ShapeDtypeStruct((n, 6 * D_MODEL), F32),
        grid=(6 * D_MODEL // bn,),
        in_specs=[pl.BlockSpec((n, D_MODEL), lambda j: (0, 0)),
                  pl.BlockSpec((D_MODEL, bn), lambda j: (0, j)),
                  pl.BlockSpec((1, bn), lambda j: (0, j))],
        out_specs=pl.BlockSpec((n, bn), lambda j: (0, j)),
        compiler_params=pltpu.CompilerParams(dimension_semantics=("arbitrary",), vmem_limit_bytes=VMEM_LIMIT),
        name="adaln",
    )(c_all, ada_w, ada_b)


def _mixer_consts(nrows, chunk, qb, kw):
    r = np.arange(nrows)
    tri = ((r[:, None] // chunk == r[None, :] // chunk) & (r[:, None] >= r[None, :])).astype(np.float32)
    hs = np.arange(GLA_HEADS * chunk)
    mkk = (hs[:, None] // chunk == np.arange(GLA_KW)[None, :] // GLA_DK).astype(np.float32)
    mv = (hs[:, None] // chunk == np.arange(GLA_VW)[None, :] // GLA_DV).astype(np.float32)
    ms = (np.arange(GLA_VW)[:, None] // GLA_DV == np.arange(GLA_KW)[None, :] // GLA_DK).astype(np.float32)
    caus = (np.arange(chunk)[:, None] >= (hs[None, :] % chunk)).astype(np.float32)
    t = np.arange(qb)
    kj = np.arange(kw)
    cs = (t // chunk) * chunk
    kpos = kj[None, :] - WINDOW
    vis = (kpos >= cs[:, None] - WINDOW) & (kpos < cs[:, None] + chunk)
    dist = np.abs(t[:, None] + WINDOW - kj[None, :]).astype(np.float32)
    bias = np.zeros((SWA_KV_HEADS, SWA_GROUP * qb, kw), np.float32)
    for hk in range(SWA_KV_HEADS):
        for g in range(SWA_GROUP):
            slope = np.float32(2.0 ** (-(hk * SWA_GROUP + g + 1)))
            bias[hk, g * qb:(g + 1) * qb] = np.where(vis, -slope * dist, np.float32(2.0 * NEG_INF))
    return (jnp.asarray(tri, BF16), jnp.asarray(bias), jnp.asarray(mkk, BF16), jnp.asarray(mv, BF16),
            jnp.asarray(ms), jnp.asarray(caus))


def _in_proj(h, win_ref, a2w_ref, a2b_ref, proj_ref):
    proj_ref[...] = _mm(h, win_ref[...])
    ga = proj_ref[:, C_GA:C_GA + LANE].astype(BF16)
    z = _mm(ga, a2w_ref[...]) + a2b_ref[...]
    lsig = -(jnp.maximum(-z, 0.0) + jnp.log(1.0 + jnp.exp(-jnp.abs(z))))
    return lsig * (1.0 / GLA_TAU)


def _gla_prep(loga, tri_ref, proj_ref, chunk, qe_ref, qn_ref, ke_ref, kn_ref, kw_ref, eb_ref, vb_ref):
    nrows = loga.shape[0]
    la_hi, la_lo = _split_bf16(loga)
    tri = tri_ref[...]
    b = _mm(tri, la_hi) + _mm(tri, la_lo)
    b_end = jnp.concatenate(
        [jnp.broadcast_to(b[c * chunk + chunk - 1:(c + 1) * chunk, :], (chunk, GLA_KW))
         for c in range(nrows // chunk)], axis=0)
    eb = jnp.exp(b)
    ebn = jnp.exp(-b)
    wk = jnp.exp(b_end - b)
    q = proj_ref[:, C_GQ:C_GQ + GLA_KW] * (GLA_DK ** -0.5)
    k = proj_ref[:, C_GK:C_GK + GLA_KW]
    eb_ref[...] = eb
    qe_ref[...] = (q * eb).astype(BF16)
    qn_ref[...] = (q * ebn).astype(BF16)
    ke_ref[...] = (k * eb).astype(BF16)
    kn_ref[...] = (k * ebn).astype(BF16)
    kw_ref[...] = (k * wk).astype(BF16)
    vb_ref[...] = proj_ref[:, C_GV:C_GV + GLA_VW].astype(BF16)


def _gla_chunk(r0, chunk, qe_ref, qn_ref, ke_ref, kn_ref, kw_ref, eb_ref, vb_ref, sbt_ref, mkk, mv, ms, caus):
    rows = slice(r0, r0 + chunk)
    qe = qe_ref[rows, :]
    qn = qn_ref[rows, :]
    zero = jnp.zeros((), BF16)
    kn4 = jnp.where(mkk != 0, jnp.concatenate([kn_ref[rows, :]] * GLA_HEADS, axis=0), zero)
    ke4 = jnp.where(mkk != 0, jnp.concatenate([ke_ref[rows, :]] * GLA_HEADS, axis=0), zero)
    a_lo = _mm_nt(qe, kn4)
    a_up = _mm_nt(qn, ke4)
    a = jnp.where(caus != 0.0, a_lo, a_up).astype(BF16)
    v = vb_ref[rows, :]
    v4 = jnp.where(mv != 0, jnp.concatenate([v] * GLA_HEADS, axis=0), zero)
    sbt = sbt_ref[...]
    o = _mm(a, v4) + _mm_nt(qe, sbt.astype(BF16))
    ut = _mm_tn(v, kw_ref[rows, :])
    g_end = eb_ref[r0 + chunk - 1:r0 + chunk, :]
    sbt_ref[...] = g_end * sbt + jnp.where(ms != 0.0, ut, 0.0)
    return o


def _gla_post(o, gr, gnw):
    res = []
    for h in range(GLA_HEADS):
        oh = o[:, h * GLA_DV:(h + 1) * GLA_DV]
        ms_ = jnp.mean(oh * oh, axis=-1, keepdims=True)
        og = oh * lax.rsqrt(ms_ + LN_EPS) * gnw
        r = gr[:, h * GLA_DV:(h + 1) * GLA_DV]
        res.append((og * (r * _sigmoid(r))).astype(BF16))
    return res


def _swa_problem(qs, kwin, vwin, bias, sinks_ref, hk, extra_valid, qb):
    s = _mm_nt(qs, kwin)
    ok = bias > -1e29
    if extra_valid is not None:
        ok = ok & extra_valid
    s = jnp.where(ok, s + bias, NEG_INF)
    ps, denoms = [], []
    for g in range(SWA_GROUP):
        sg = s[g * qb:(g + 1) * qb, :]
        sink = sinks_ref[hk * SWA_GROUP + g]
        m = jnp.maximum(jnp.max(sg, axis=-1, keepdims=True), sink)
        pg = jnp.exp(sg - m)
        denoms.append(jnp.sum(pg, axis=-1, keepdims=True) + jnp.exp(sink - m))
        ps.append(pg.astype(BF16))
    o = _mm(jnp.concatenate(ps, axis=0), vwin)
    return o / jnp.concatenate(denoms, axis=0)


def _swa_block(sq, kwin, vwin, bias_ref, sinks_ref, extra_valid, qb):
    low = lax.broadcasted_iota(jnp.int32, (qb, LANE), 1) < SWA_DH
    outs = []
    for hk in range(SWA_KV_HEADS):
        keep = low if hk == 0 else jnp.logical_not(low)
        qs = jnp.concatenate([jnp.where(keep, sq[:, g * LANE:(g + 1) * LANE] * (SWA_DH ** -0.5), 0.0)
                              for g in range(SWA_GROUP)], axis=0).astype(BF16)
        outs.append(_swa_problem(qs, kwin, vwin, bias_ref[hk], sinks_ref, hk, extra_valid, qb))
    return [jnp.where(low, outs[0][g * qb:(g + 1) * qb, :], outs[1][g * qb:(g + 1) * qb, :]).astype(BF16)
            for g in range(SWA_GROUP)]


def _out_proj_ln(x, gt1, mixed_ref, wo_ref, ln_g, ln_b):
    mix = _mm(mixed_ref[...], wo_ref[...])
    return _layer_norm(DEEPNORM_ALPHA * x + gt1 * mix, ln_g, ln_b)


P_QB = 128
P_KW = WINDOW + P_QB


def _mixer_prompt_kernel(sinks_ref, x_ref, mod_ref, win_ref, a2w_ref, a2b_ref, gnw_ref, wo_ref, lng_ref, lnb_ref,
                         tri_ref, bias_ref, mkk_ref, mv_ref, ms_ref, caus_ref,
                         x1_ref, s_out_ref, k_out_ref, v_out_ref,
                         proj_ref, qe_ref, qn_ref, ke_ref, kn_ref, kw_ref, eb_ref, vb_ref, kbuf, vbuf, mixed_ref,
                         sbt_ref, *, tl):
    j = pl.program_id(1)
    nj = pl.num_programs(1)
    d = D_MODEL

    @pl.when(j == 0)
    def _():
        sbt_ref[...] = jnp.zeros_like(sbt_ref)
        kbuf[0:WINDOW, :] = jnp.zeros((WINDOW, LANE), BF16)
        vbuf[0:WINDOW, :] = jnp.zeros((WINDOW, LANE), BF16)

    x = x_ref[0]
    mod = mod_ref[0]
    h = (x * (1.0 + mod[:, d:2 * d]) + mod[:, 0:d]).astype(BF16)
    loga = _in_proj(h, win_ref, a2w_ref, a2b_ref, proj_ref)
    _gla_prep(loga, tri_ref, proj_ref, CHUNK, qe_ref, qn_ref, ke_ref, kn_ref, kw_ref, eb_ref, vb_ref)
    kbuf[WINDOW:WINDOW + tl, :] = proj_ref[:, C_SK:C_SK + LANE].astype(BF16)
    vbuf[WINDOW:WINDOW + tl, :] = proj_ref[:, C_SV:C_SV + LANE].astype(BF16)

    mkk = mkk_ref[...]
    mv = mv_ref[...]
    ms = ms_ref[...]
    caus = caus_ref[...]
    gnw = gnw_ref[...]
    kj = lax.broadcasted_iota(jnp.int32, (SWA_GROUP * P_QB, P_KW), 1)
    first_valid = kj >= jnp.where(j > 0, 0, WINDOW)

    for p in range(tl // P_QB):
        q0 = p * P_QB
        blocks = _swa_block(proj_ref[q0:q0 + P_QB, C_SQ:C_SQ + SWA_W], kbuf[q0:q0 + P_KW, :], vbuf[q0:q0 + P_KW, :],
                            bias_ref, sinks_ref, first_valid if p == 0 else None, P_QB)
        for i, blk in enumerate(blocks):
            mixed_ref[q0:q0 + P_QB, GLA_VW + i * LANE:GLA_VW + (i + 1) * LANE] = blk
        for c in range(P_QB // CHUNK):
            r0 = q0 + c * CHUNK
            o = _gla_chunk(r0, CHUNK, qe_ref, qn_ref, ke_ref, kn_ref, kw_ref, eb_ref, vb_ref, sbt_ref,
                           mkk, mv, ms, caus)
            og = _gla_post(o, proj_ref[r0:r0 + CHUNK, C_GR:C_GR + GLA_VW], gnw)
            for hh in range(GLA_HEADS):
                mixed_ref[r0:r0 + CHUNK, hh * GLA_DV:(hh + 1) * GLA_DV] = og[hh]

    x1_ref[0] = _out_proj_ln(x, mod[:, 2 * d:3 * d], mixed_ref, wo_ref, lng_ref[...], lnb_ref[...])

    kbuf[0:WINDOW, :] = kbuf[tl:tl + WINDOW, :]
    vbuf[0:WINDOW, :] = vbuf[tl:tl + WINDOW, :]

    @pl.when(j == nj - 1)
    def _():
        for hh in range(GLA_HEADS):
            s_out_ref[0, hh] = sbt_ref[hh * GLA_DV:(hh + 1) * GLA_DV, hh * GLA_DK:(hh + 1) * GLA_DK].T
        k_out_ref[0] = proj_ref[tl - WINDOW:tl, C_SK:C_SK + LANE]
        v_out_ref[0] = proj_ref[tl - WINDOW:tl, C_SV:C_SV + LANE]


def _mixer_prompt(x, mod, sinks, win, a2w, a2b, gnw, wo, lng, lnb, *, tl=512):
    b, l, d = x.shape
    nj = l // tl
    consts = _mixer_consts(tl, CHUNK, P_QB, P_KW)
    const2 = lambda i, j, s: (0, 0)
    const3 = lambda i, j, s: (0, 0, 0)
    grid_spec = pltpu.PrefetchScalarGridSpec(
        num_scalar_prefetch=1,
        grid=(b, nj),
        in_specs=[
            pl.BlockSpec((1, tl, d), lambda i, j, s: (i, j, 0)),
            pl.BlockSpec((1, 1, 6 * d), lambda i, j, s: (i, 0, 0)),
            pl.BlockSpec((d, PROJ_W), const2),
            pl.BlockSpec((LANE, GLA_KW), const2),
            pl.BlockSpec((1, GLA_KW), const2),
            pl.BlockSpec((1, GLA_DV), const2),
            pl.BlockSpec((d, d), const2),
            pl.BlockSpec((1, d), const2),
            pl.BlockSpec((1, d), const2),
            pl.BlockSpec((tl, tl), const2),
            pl.BlockSpec((SWA_KV_HEADS, SWA_GROUP * P_QB, P_KW), const3),
            pl.BlockSpec((GLA_HEADS * CHUNK, GLA_KW), const2),
            pl.BlockSpec((GLA_HEADS * CHUNK, GLA_VW), const2),
            pl.BlockSpec((GLA_VW, GLA_KW), const2),
            pl.BlockSpec((CHUNK, GLA_HEADS * CHUNK), const2),
        ],
        out_specs=[
            pl.BlockSpec((1, tl, d), lambda i, j, s: (i, j, 0)),
            pl.BlockSpec((1, GLA_HEADS, GLA_DK, GLA_DV), lambda i, j, s: (i, 0, 0, 0)),
            pl.BlockSpec((1, WINDOW, LANE), lambda i, j, s: (i, 0, 0)),
            pl.BlockSpec((1, WINDOW, LANE), lambda i, j, s: (i, 0, 0)),
        ],
        scratch_shapes=[
            pltpu.VMEM((tl, PROJ_W), F32),
            pltpu.VMEM((tl, GLA_KW), BF16),
            pltpu.VMEM((tl, GLA_KW), BF16),
            pltpu.VMEM((tl, GLA_KW), BF16),
            pltpu.VMEM((tl, GLA_KW), BF16),
            pltpu.VMEM((tl, GLA_KW), BF16),
            pltpu.VMEM((tl, GLA_KW), F32),
            pltpu.VMEM((tl, GLA_VW), BF16),
            pltpu.VMEM((WINDOW + tl, LANE), BF16),
            pltpu.VMEM((WINDOW + tl, LANE), BF16),
            pltpu.VMEM((tl, d), BF16),
            pltpu.VMEM((GLA_VW, GLA_KW), F32),
        ],
    )
    return pl.pallas_call(
        functools.partial(_mixer_prompt_kernel, tl=tl),
        out_shape=[
            jax.ShapeDtypeStruct((b, l, d), F32),
            jax.ShapeDtypeStruct((b, GLA_HEADS, GLA_DK, GLA_DV), F32),
            jax.ShapeDtypeStruct((b, WINDOW, LANE), F32),
            jax.ShapeDtypeStruct((b, WINDOW, LANE), F32),
        ],
        grid_spec=grid_spec,
        compiler_params=pltpu.CompilerParams(dimension_semantics=("arbitrary", "arbitrary"),
                                             vmem_limit_bytes=VMEM_LIMIT),
        name="mixer_prompt",
    )(sinks, x, mod, win, a2w, a2b, gnw, wo, lng, lnb, *consts)


def _mixer_sample_kernel(sinks_ref, x_ref, mod_ref, s0_ref, kc_ref, vc_ref, win_ref, a2w_ref, a2b_ref, gnw_ref,
                         wo_ref, lng_ref, lnb_ref, tri_ref, bias_ref, mkk_ref, mv_ref, ms_ref, caus_ref,
                         x1_ref, s_out_ref, k_out_ref, v_out_ref,
                         proj_ref, qe_ref, qn_ref, ke_ref, kn_ref, kw_ref, eb_ref, vb_ref, kbuf, vbuf, mixed_ref,
                         sbt_ref, xm_ref, *, nb, s):
    nkeys = WINDOW + s
    d = D_MODEL
    for bb in range(nb):
        m = mod_ref[bb]
        xm_ref[bb * s:(bb + 1) * s, :] = x_ref[bb] * (1.0 + m[:, d:2 * d]) + m[:, 0:d]
    loga = _in_proj(xm_ref[...].astype(BF16), win_ref, a2w_ref, a2b_ref, proj_ref)
    _gla_prep(loga, tri_ref, proj_ref, s, qe_ref, qn_ref, ke_ref, kn_ref, kw_ref, eb_ref, vb_ref)

    mkk = mkk_ref[...]
    mv = mv_ref[...]
    ms = ms_ref[...]
    caus = caus_ref[...]
    gnw = gnw_ref[...]

    for bb in range(nb):
        r0 = bb * s
        rows = slice(r0, r0 + s)
        sbt_ref[...] = jnp.zeros_like(sbt_ref)
        for hh in range(GLA_HEADS):
            sbt_ref[hh * GLA_DV:(hh + 1) * GLA_DV, hh * GLA_DK:(hh + 1) * GLA_DK] = s0_ref[bb, hh].T
        kbuf[0:WINDOW, :] = kc_ref[bb].astype(BF16)
        vbuf[0:WINDOW, :] = vc_ref[bb].astype(BF16)
        kbuf[WINDOW:nkeys, :] = proj_ref[rows, C_SK:C_SK + LANE].astype(BF16)
        vbuf[WINDOW:nkeys, :] = proj_ref[rows, C_SV:C_SV + LANE].astype(BF16)
        blocks = _swa_block(proj_ref[rows, C_SQ:C_SQ + SWA_W], kbuf[...], vbuf[...], bias_ref, sinks_ref, None, s)
        for i, blk in enumerate(blocks):
            mixed_ref[rows, GLA_VW + i * LANE:GLA_VW + (i + 1) * LANE] = blk
        o = _gla_chunk(r0, s, qe_ref, qn_ref, ke_ref, kn_ref, kw_ref, eb_ref, vb_ref, sbt_ref, mkk, mv, ms, caus)
        og = _gla_post(o, proj_ref[rows, C_GR:C_GR + GLA_VW], gnw)
        for hh in range(GLA_HEADS):
            mixed_ref[rows, hh * GLA_DV:(hh + 1) * GLA_DV] = og[hh]
            s_out_ref[bb, hh] = sbt_ref[hh * GLA_DV:(hh + 1) * GLA_DV, hh * GLA_DK:(hh + 1) * GLA_DK].T
        k_out_ref[bb, 0:WINDOW - s, :] = kc_ref[bb, s:WINDOW, :]
        v_out_ref[bb, 0:WINDOW - s, :] = vc_ref[bb, s:WINDOW, :]
        k_out_ref[bb, WINDOW - s:WINDOW, :] = proj_ref[rows, C_SK:C_SK + LANE]
        v_out_ref[bb, WINDOW - s:WINDOW, :] = proj_ref[rows, C_SV:C_SV + LANE]

    mix = _mm(mixed_ref[...], wo_ref[...])
    lng = lng_ref[...]
    lnb = lnb_ref[...]
    for bb in range(nb):
        m = mod_ref[bb]
        y = DEEPNORM_ALPHA * x_ref[bb] + m[:, 2 * d:3 * d] * mix[bb * s:(bb + 1) * s, :]
        x1_ref[bb] = _layer_norm(y, lng, lnb)


def _mixer_sample(x, mod, s0, kc, vc, sinks, win, a2w, a2b, gnw, wo, lng, lnb, *, nb=8):
    b, s, d = x.shape
    assert kc.shape[1] == WINDOW and s <= WINDOW
    rows = nb * s
    nkeys = WINDOW + s
    consts = _mixer_consts(rows, s, s, nkeys)
    const2 = lambda i, sk: (0, 0)
    const3 = lambda i, sk: (0, 0, 0)
    grid_spec = pltpu.PrefetchScalarGridSpec(
        num_scalar_prefetch=1,
        grid=(b // nb,),
        in_specs=[
            pl.BlockSpec((nb, s, d), lambda i, sk: (i, 0, 0)),
            pl.BlockSpec((nb, 1, 6 * d), lambda i, sk: (i, 0, 0)),
            pl.BlockSpec((nb, GLA_HEADS, GLA_DK, GLA_DV), lambda i, sk: (i, 0, 0, 0)),
            pl.BlockSpec((nb, WINDOW, LANE), lambda i, sk: (i, 0, 0)),
            pl.BlockSpec((nb, WINDOW, LANE), lambda i, sk: (i, 0, 0)),
            pl.BlockSpec((d, PROJ_W), const2),
            pl.BlockSpec((LANE, GLA_KW), const2),
            pl.BlockSpec((1, GLA_KW), const2),
            pl.BlockSpec((1, GLA_DV), const2),
            pl.BlockSpec((d, d), const2),
            pl.BlockSpec((1, d), const2),
            pl.BlockSpec((1, d), const2),
            pl.BlockSpec((rows, rows), const2),
            pl.BlockSpec((SWA_KV_HEADS, SWA_GROUP * s, nkeys), const3),
            pl.BlockSpec((GLA_HEADS * s, GLA_KW), const2),
            pl.BlockSpec((GLA_HEADS * s, GLA_VW), const2),
            pl.BlockSpec((GLA_VW, GLA_KW), const2),
            pl.BlockSpec((s, GLA_HEADS * s), const2),
        ],
        out_specs=[
            pl.BlockSpec((nb, s, d), lambda i, sk: (i, 0, 0)),
            pl.BlockSpec((nb, GLA_HEADS, GLA_DK, GLA_DV), lambda i, sk: (i, 0, 0, 0)),
            pl.BlockSpec((nb, WINDOW, LANE), lambda i, sk: (i, 0, 0)),
            pl.BlockSpec((nb, WINDOW, LANE), lambda i, sk: (i, 0, 0)),
        ],
        scratch_shapes=[
            pltpu.VMEM((rows, PROJ_W), F32),
            pltpu.VMEM((rows, GLA_KW), BF16),
            pltpu.VMEM((rows, GLA_KW), BF16),
            pltpu.VMEM((rows, GLA_KW), BF16),
            pltpu.VMEM((rows, GLA_KW), BF16),
            pltpu.VMEM((rows, GLA_KW), BF16),
            pltpu.VMEM((rows, GLA_KW), F32),
            pltpu.VMEM((rows, GLA_VW), BF16),
            pltpu.VMEM((nkeys, LANE), BF16),
            pltpu.VMEM((nkeys, LANE), BF16),
            pltpu.VMEM((rows, d), BF16),
            pltpu.VMEM((GLA_VW, GLA_KW), F32),
            pltpu.VMEM((rows, d), F32),
        ],
    )
    return pl.pallas_call(
        functools.partial(_mixer_sample_kernel, nb=nb, s=s),
        out_shape=[
            jax.ShapeDtypeStruct((b, s, d), F32),
            jax.ShapeDtypeStruct((b, GLA_HEADS, GLA_DK, GLA_DV), F32),
            jax.ShapeDtypeStruct((b, WINDOW, LANE), F32),
            jax.ShapeDtypeStruct((b, WINDOW, LANE), F32),
        ],
        grid_spec=grid_spec,
        compiler_params=pltpu.CompilerParams(dimension_semantics=("arbitrary",),
                                             vmem_limit_bytes=VMEM_LIMIT),
        name="mixer_sample",
    )(sinks, x, mod, s0, kc, vc, win, a2w, a2b, gnw, wo, lng, lnb, *consts)


MOE_BLK = 128


def _route_t(logits_t):
    t = logits_t.shape[1]
    row = lax.broadcasted_iota(jnp.int32, (EPG, t), 0).astype(F32)
    big = 99.0
    gl = jnp.where(row < N_GROUPS, logits_t[0:EPG, :], -jnp.inf)
    gmax = jnp.max(gl, axis=0, keepdims=True)
    grp = jnp.min(jnp.where(gl == gmax, row, big), axis=0, keepdims=True)
    p_grp = 1.0 / jnp.sum(jnp.exp(gl - gmax), axis=0, keepdims=True)
    el = jnp.zeros((EPG, t), F32)
    for g in range(N_GROUPS):
        el = el + jnp.where(grp == float(g), logits_t[R_EXP0 + EPG * g:R_EXP0 + EPG * (g + 1), :], 0.0)
    v1 = jnp.max(el, axis=0, keepdims=True)
    i1 = jnp.min(jnp.where(el == v1, row, big), axis=0, keepdims=True)
    el2 = jnp.where(row == i1, -jnp.inf, el)
    v2 = jnp.max(el2, axis=0, keepdims=True)
    i2 = jnp.min(jnp.where(el2 == v2, row, big), axis=0, keepdims=True)
    e2 = jnp.exp(v2 - v1)
    w1 = p_grp / (1.0 + e2)
    w2 = p_grp * e2 / (1.0 + e2)
    cw = jnp.where(row == i1, w1, 0.0) + jnp.where(row == i2, w2, 0.0)
    return grp, cw


MOE_PC = 256


def _moe_sort_kernel(x1_ref, mod_ref, wrt_ref, brt_ref, upper_ref, xs_ref, cws_ref, pos_ref, cnt_ref, *, nb, r):
    tm = nb * r
    tmp = tm + N_GROUPS * MOE_BLK
    d = D_MODEL
    mod = mod_ref[...]
    t3 = x1_ref[...] * (1.0 + mod[:, :, 4 * d:5 * d]) + mod[:, :, 3 * d:4 * d]
    t = t3.reshape(tm, d).astype(BF16)
    grp, cw = _route_t(_mm_nt(wrt_ref[...], t) + brt_ref[...])
    row = lax.broadcasted_iota(jnp.int32, (EPG, tm), 0).astype(F32)
    onehot_g = jnp.where(row == grp, 1.0, 0.0)
    rank = _mm(onehot_g.astype(BF16), upper_ref[...])
    cnt = jnp.sum(onehot_g, axis=1, keepdims=True)
    padded = jnp.floor((cnt + (MOE_BLK - 1)) * (1.0 / MOE_BLK)) * MOE_BLK
    rowc = lax.broadcasted_iota(jnp.int32, (EPG, 1), 0)
    off = jnp.zeros((EPG, 1), F32)
    for gg in range(N_GROUPS - 1):
        off = off + jnp.where(rowc > gg, padded[gg:gg + 1, :], 0.0)
    pos = jnp.sum(onehot_g * (off + rank), axis=0, keepdims=True)
    pos_ref[0] = jnp.broadcast_to(pos, (EPG, tm))
    cnt_ref[0] = jnp.broadcast_to(cnt, (EPG, LANE))
    cw_pad = jnp.concatenate([cw, jnp.zeros((LANE - EPG, tm), F32)], axis=0)
    cw_hi, cw_lo = _split_bf16(cw_pad)
    for c in range(tmp // MOE_PC):
        slot = (lax.broadcasted_iota(jnp.int32, (MOE_PC, tm), 0) + c * MOE_PC).astype(F32)
        perm = jnp.where(slot == pos, 1.0, 0.0).astype(BF16)
        xs_ref[c * MOE_PC:(c + 1) * MOE_PC, :] = _mm(perm, t).astype(BF16)
        cws_ref[c * MOE_PC:(c + 1) * MOE_PC, :] = _mm_nt(perm, cw_hi) + _mm_nt(perm, cw_lo)


def _moe_expert_kernel(blk_row_ref, blk_grp_ref, nvalid_ref, xs_ref, cws_ref, wg_ref, wu_ref, wd_ref, ys_ref):
    del blk_row_ref, blk_grp_ref

    @pl.when(pl.program_id(0) < nvalid_ref[0])
    def _():
        xb = xs_ref[...]
        cwb = cws_ref[...]
        hs = []
        for e in range(EPG):
            gg_ = _mm(xb, wg_ref[0, e])
            uu = _mm(xb, wu_ref[0, e])
            hs.append((gg_ * _sigmoid(gg_) * uu * cwb[:, e:e + 1]).astype(BF16))
        y = _mm(jnp.concatenate(hs, axis=1), wd_ref[0].reshape(EPG * EXPERT_FF, D_MODEL))
        ys_ref[...] = y.astype(BF16)


def _moe_unsort_kernel(used_ref, ys_ref, pos_ref, x1_ref, mod_ref, lng_ref, lnb_ref, out_ref, *, nb, r):
    tm = nb * r
    tmp = tm + N_GROUPS * MOE_BLK
    d = D_MODEL
    used = used_ref[pl.program_id(0)]
    zero = jnp.zeros((), BF16)
    ysb = jnp.concatenate([jnp.where(k * MOE_BLK < used, ys_ref[k * MOE_BLK:(k + 1) * MOE_BLK, :], zero)
                           for k in range(tmp // MOE_BLK)], axis=0)
    posc = jnp.broadcast_to(pos_ref[0][0:1, :], (LANE, tm)).T
    lng = lng_ref[...]
    lnb = lnb_ref[...]
    for c in range(tm // MOE_PC):
        slot = lax.broadcasted_iota(jnp.int32, (MOE_PC, tmp), 1).astype(F32)
        unperm = jnp.where(slot == posc[c * MOE_PC:(c + 1) * MOE_PC, 0:1], 1.0, 0.0).astype(BF16)
        y = _mm(unperm, ysb)
        if nb == 1:
            x1c = x1_ref[0, c * MOE_PC:(c + 1) * MOE_PC, :]
            gt2 = mod_ref[0][:, 5 * d:6 * d]
            out_ref[0, c * MOE_PC:(c + 1) * MOE_PC, :] = _layer_norm(DEEPNORM_ALPHA * x1c + gt2 * y, lng, lnb)
        else:
            cb = MOE_PC // r
            x1c = x1_ref[c * cb:(c + 1) * cb]
            gt2 = mod_ref[c * cb:(c + 1) * cb][:, :, 5 * d:6 * d]
            yy = DEEPNORM_ALPHA * x1c + gt2 * y.reshape(cb, r, d)
            out_ref[c * cb:(c + 1) * cb] = _layer_norm(yy, lng, lnb)


def _moe(x1, mod, wrt, brt, wg, wu, wd, lng, lnb, *, nb, r):
    b, l, d = x1.shape
    tpb = l // r
    ntiles = (b // nb) * tpb
    tm = nb * r
    tmp = tm + N_GROUPS * MOE_BLK
    bpt = tmp // MOE_BLK
    xmap = lambda i: (i // tpb, i % tpb, 0)
    mmap = lambda i: (i // tpb, 0, 0)
    const2 = lambda i: (0, 0)
    params = pltpu.CompilerParams(dimension_semantics=("arbitrary",), vmem_limit_bytes=VMEM_LIMIT)
    upper = jnp.asarray(np.triu(np.ones((tm, tm), np.float32), 1), BF16)

    xs, cws, pos, cnt = pl.pallas_call(
        functools.partial(_moe_sort_kernel, nb=nb, r=r),
        out_shape=[jax.ShapeDtypeStruct((ntiles * tmp, d), BF16),
                   jax.ShapeDtypeStruct((ntiles * tmp, LANE), F32),
                   jax.ShapeDtypeStruct((ntiles, EPG, tm), F32),
                   jax.ShapeDtypeStruct((ntiles, EPG, LANE), F32)],
        grid=(ntiles,),
        in_specs=[
            pl.BlockSpec((nb, r, d), xmap),
            pl.BlockSpec((nb, 1, 6 * d), mmap),
            pl.BlockSpec((LANE, d), const2),
            pl.BlockSpec((LANE, 1), const2),
            pl.BlockSpec((tm, tm), const2),
        ],
        out_specs=[pl.BlockSpec((tmp, d), lambda i: (i, 0)),
                   pl.BlockSpec((tmp, LANE), lambda i: (i, 0)),
                   pl.BlockSpec((1, EPG, tm), lambda i: (i, 0, 0)),
                   pl.BlockSpec((1, EPG, LANE), lambda i: (i, 0, 0))],
        compiler_params=params,
        name="moe_sort",
    )(x1, mod, wrt, brt, upper)

    nblk = ((cnt[:, :N_GROUPS, 0].astype(jnp.int32) + (MOE_BLK - 1)) // MOE_BLK)
    boff = jnp.cumsum(nblk, axis=1) - nblk
    used = jnp.sum(nblk, axis=1) * MOE_BLK
    n_gt = nblk.T.reshape(-1)
    first = (jnp.arange(ntiles, dtype=jnp.int32)[None, :] * bpt + boff.T).reshape(-1)
    start = jnp.cumsum(n_gt) - n_gt
    total = jnp.sum(n_gt)
    nb_max = ntiles * bpt
    slot = jnp.arange(nb_max, dtype=jnp.int32)
    seg = jnp.clip(jnp.searchsorted(start + n_gt, slot, side="right"), 0, N_GROUPS * ntiles - 1).astype(jnp.int32)
    row_blk = first[seg] + (slot - start[seg])
    grp_blk = seg // ntiles
    last = jnp.maximum(total - 1, 0)
    valid = slot < total
    row_blk = jnp.where(valid, row_blk, row_blk[last]).astype(jnp.int32)
    grp_blk = jnp.where(valid, grp_blk, grp_blk[last]).astype(jnp.int32)

    wmap = lambda i, rb, gb, nv: (gb[i], 0, 0, 0)
    ys = pl.pallas_call(
        _moe_expert_kernel,
        out_shape=jax.ShapeDtypeStruct((ntiles * tmp, d), BF16),
        grid_spec=pltpu.PrefetchScalarGridSpec(
            num_scalar_prefetch=3,
            grid=(nb_max,),
            in_specs=[
                pl.BlockSpec((MOE_BLK, d), lambda i, rb, gb, nv: (rb[i], 0)),
                pl.BlockSpec((MOE_BLK, LANE), lambda i, rb, gb, nv: (rb[i], 0)),
                pl.BlockSpec((1, EPG, d, EXPERT_FF), wmap),
                pl.BlockSpec((1, EPG, d, EXPERT_FF), wmap),
                pl.BlockSpec((1, EPG, EXPERT_FF, d), wmap),
            ],
            out_specs=pl.BlockSpec((MOE_BLK, d), lambda i, rb, gb, nv: (rb[i], 0)),
        ),
        compiler_params=params,
        name="moe_experts",
    )(row_blk, grp_blk, total.reshape(1).astype(jnp.int32), xs, cws, wg, wu, wd)

    return pl.pallas_call(
        functools.partial(_moe_unsort_kernel, nb=nb, r=r),
        out_shape=jax.ShapeDtypeStruct((b, l, d), F32),
        grid_spec=pltpu.PrefetchScalarGridSpec(
            num_scalar_prefetch=1,
            grid=(ntiles,),
            in_specs=[
                pl.BlockSpec((tmp, d), lambda i, u: (i, 0)),
                pl.BlockSpec((1, EPG, tm), lambda i, u: (i, 0, 0)),
                pl.BlockSpec((nb, r, d), lambda i, u: (i // tpb, i % tpb, 0)),
                pl.BlockSpec((nb, 1, 6 * d), lambda i, u: (i // tpb, 0, 0)),
                pl.BlockSpec((1, d), lambda i, u: (0, 0)),
                pl.BlockSpec((1, d), lambda i, u: (0, 0)),
            ],
            out_specs=pl.BlockSpec((nb, r, d), lambda i, u: (i // tpb, i % tpb, 0)),
        ),
        compiler_params=params,
        name="moe_unsort",
    )(used.astype(jnp.int32), ys, pos, x1, mod, lng, lnb)


def kernel(x_prompt, x_sample, c_prompt, c_sample, state_gla, cache_swa_k, cache_swa_v, ada_w, ada_b, w_in,
           gla_a2_w, gla_a2_b, gla_norm_w, swa_sinks, w_o, ln1_g, ln1_b, router_g_w, router_g_b, router_e_w,
           router_e_b, moe_w_gate, moe_w_up, moe_w_down, ln2_g, ln2_b):
    assert ada_w.shape[0] == 1
    bp = x_prompt.shape[0]
    bs, ss, d = x_sample.shape
    lc = cache_swa_k.shape[2]

    w = w_in[0]
    zpad = jnp.zeros((d, LANE - GLA_RANK), F32)
    n = np.arange(SWA_W)
    swa_perm = ((n // SWA_DH) % SWA_KV_HEADS) * (SWA_GROUP * SWA_DH) + (n // LANE) * SWA_DH + n % SWA_DH
    win = jnp.concatenate([w[:, 0:1536], w[:, 1552:2064][:, swa_perm], w[:, 2064:2320], w[:, 1536:1552], zpad],
                          axis=1).astype(BF16)
    a2w = jnp.concatenate([gla_a2_w[0], jnp.zeros((LANE - GLA_RANK, GLA_KW), F32)], axis=0).astype(BF16)
    a2b = gla_a2_b[0].reshape(1, GLA_KW)
    gnw = gla_norm_w[0].reshape(1, GLA_DV)
    wo = jnp.concatenate([w_o[0][:GLA_VW], w_o[0][GLA_VW:][swa_perm]], axis=0).astype(BF16)
    sinks = swa_sinks[0]
    wrt = jnp.concatenate([router_g_w[0], jnp.zeros((d, R_EXP0 - N_GROUPS), F32),
                           jnp.transpose(router_e_w[0], (1, 0, 2)).reshape(d, N_GROUPS * EPG),
                           jnp.zeros((d, LANE - R_EXP0 - N_GROUPS * EPG), F32)], axis=1).T.astype(BF16)
    brt = jnp.concatenate([router_g_b[0], jnp.zeros((R_EXP0 - N_GROUPS,), F32), router_e_b[0].reshape(-1),
                           jnp.zeros((LANE - R_EXP0 - N_GROUPS * EPG,), F32)]).reshape(LANE, 1)
    wg = moe_w_gate[0].astype(BF16)
    wu = moe_w_up[0].astype(BF16)
    wd = moe_w_down[0].astype(BF16)
    lng1, lnb1 = ln1_g[0].reshape(1, d), ln1_b[0].reshape(1, d)
    lng2, lnb2 = ln2_g[0].reshape(1, d), ln2_b[0].reshape(1, d)

    mod = _adaln(jnp.concatenate([c_prompt, c_sample], axis=0), ada_w[0], ada_b[0].reshape(1, 6 * d))
    mod = mod.reshape(bp + bs, 1, 6 * d)
    mod_p, mod_s = mod[:bp], mod[bp:]

    x1p, s_p, k_p, v_p = _mixer_prompt(x_prompt, mod_p, sinks, win, a2w, a2b, gnw, wo, lng1, lnb1)
    x1s, s_s, k_s, v_s = _mixer_sample(
        x_sample, mod_s, state_gla[0], cache_swa_k[0].reshape(bs, lc, LANE), cache_swa_v[0].reshape(bs, lc, LANE),
        sinks, win, a2w, a2b, gnw, wo, lng1, lnb1)

    yp = _moe(x1p, mod_p, wrt, brt, wg, wu, wd, lng2, lnb2, nb=1, r=1024)
    ys = _moe(x1s, mod_s, wrt, brt, wg, wu, wd, lng2, lnb2, nb=bs, r=ss)

    kv_shape_p = (1, bp, WINDOW, SWA_KV_HEADS, SWA_DH)
    kv_shape_s = (1, bs, lc, SWA_KV_HEADS, SWA_DH)
    return (yp, ys, s_p[None], k_p.reshape(kv_shape_p), v_p.reshape(kv_shape_p),
            s_s[None], k_s.reshape(kv_shape_s), v_s.reshape(kv_shape_s))
```

```python
import functools

import jax
import jax.numpy as jnp
import numpy as np
from jax import lax
from jax.experimental import pallas as pl
from jax.experimental.pallas import tpu as pltpu

F32 = jnp.float32
BF16 = jnp.bfloat16

D_MODEL = 1024
CHUNK = 64
GLA_HEADS = 4
GLA_DK = 64
GLA_DV = 128
GLA_KW = GLA_HEADS * GLA_DK
GLA_VW = GLA_HEADS * GLA_DV
GLA_RANK = 16
GLA_TAU = 16.0
SWA_Q_HEADS = 8
SWA_KV_HEADS = 2
SWA_GROUP = 4
SWA_DH = 64
SWA_W = SWA_Q_HEADS * SWA_DH
WINDOW = 128
N_GROUPS = 4
EPG = 8
EXPERT_FF = 256
DEEPNORM_ALPHA = 2.0 ** 0.25
LN_EPS = 1e-5
NEG_INF = -1e30

C_GQ, C_GK, C_GV, C_GR, C_SQ, C_SK, C_SV, C_GA = 0, 256, 512, 1024, 1536, 2048, 2176, 2304
PROJ_W = 2432
LANE = 128
R_EXP0 = 8

VMEM_LIMIT = 56 * 1024 * 1024


def _mm(a, b):
    return jnp.dot(a, b, preferred_element_type=F32)


def _mm_nt(a, b):
    return lax.dot_general(a, b, (((1,), (1,)), ((), ())), preferred_element_type=F32)


def _mm_tn(a, b):
    return lax.dot_general(a, b, (((0,), (0,)), ((), ())), preferred_element_type=F32)


def _split_bf16(a):
    hi = a.astype(BF16)
    lo = (a - hi.astype(F32)).astype(BF16)
    return hi, lo


def _sigmoid(x):
    return 1.0 / (1.0 + jnp.exp(-x))


def _layer_norm(y, g, b):
    mu = jnp.mean(y, axis=-1, keepdims=True)
    d = y - mu
    var = jnp.mean(d * d, axis=-1, keepdims=True)
    return d * lax.rsqrt(var + LN_EPS) * g + b


def _adaln_kernel(c_ref, w_ref, b_ref, o_ref):
    c = c_ref[...]
    a = c * _sigmoid(c)
    a_hi, a_lo = _split_bf16(a)
    w_hi, w_lo = _split_bf16(w_ref[...])
    o_ref[...] = _mm(a_hi, w_hi) + (_mm(a_hi, w_lo) + _mm(a_lo, w_hi)) + b_ref[...]


def _adaln(c_all, ada_w, ada_b):
    n = c_all.shape[0]
    bn = 1024
    return pl.pallas_call(
        _adaln_kernel,
        out_shape=jax.ShapeDtypeStruct((n, 6 * D_MODEL), F32),
        grid=(6 * D_MODEL // bn,),
        in_specs=[pl.BlockSpec((n, D_MODEL), lambda j: (0, 0)),
                  pl.BlockSpec((D_MODEL, bn), lambda j: (0, j)),
                  pl.BlockSpec((1, bn), lambda j: (0, j))],
        out_specs=pl.BlockSpec((n, bn), lambda j: (0, j)),
        compiler_params=pltpu.CompilerParams(dimension_semantics=("arbitrary",), vmem_limit_bytes=VMEM_LIMIT),
        name="adaln",
    )(c_all, ada_w, ada_b)


def _mixer_consts(nrows, chunk, qb, kw):
    r = np.arange(nrows)
    tri = ((r[:, None] // chunk == r[None, :] // chunk) & (r[:, None] >= r[None, :])).astype(np.float32)
    hs = np.arange(GLA_HEADS * chunk)
    mkk = (hs[:, None] // chunk == np.arange(GLA_KW)[None, :] // GLA_DK).astype(np.float32)
    mv = (hs[:, None] // chunk == np.arange(GLA_VW)[None, :] // GLA_DV).astype(np.float32)
    ms = (np.arange(GLA_VW)[:, None] // GLA_DV == np.arange(GLA_KW)[None, :] // GLA_DK).astype(np.float32)
    caus = (np.arange(chunk)[:, None] >= (hs[None, :] % chunk)).astype(np.float32)
    t = np.arange(qb)
    kj = np.arange(kw)
    cs = (t // chunk) * chunk
    kpos = kj[None, :] - WINDOW
    vis = (kpos >= cs[:, None] - WINDOW) & (kpos < cs[:, None] + chunk)
    dist = np.abs(t[:, None] + WINDOW - kj[None, :]).astype(np.float32)
    bias = np.zeros((SWA_KV_HEADS, SWA_GROUP * qb, kw), np.float32)
    for hk in range(SWA_KV_HEADS):
        for g in range(SWA_GROUP):
            slope = np.float32(2.0 ** (-(hk * SWA_GROUP + g + 1)))
            bias[hk, g * qb:(g + 1) * qb] = np.where(vis, -slope * dist, np.float32(2.0 * NEG_INF))
    return (jnp.asarray(tri, BF16), jnp.asarray(bias), jnp.asarray(mkk, BF16), jnp.asarray(mv, BF16),
            jnp.asarray(ms), jnp.asarray(caus))


def _in_proj(h, win_ref, a2w_ref, a2b_ref, proj_ref):
    proj_ref[...] = _mm(h, win_ref[...])
    ga = proj_ref[:, C_GA:C_GA + LANE].astype(BF16)
    z = _mm(ga, a2w_ref[...]) + a2b_ref[...]
    lsig = -(jnp.maximum(-z, 0.0) + jnp.log(1.0 + jnp.exp(-jnp.abs(z))))
    return lsig * (1.0 / GLA_TAU)


def _gla_prep(loga, tri_ref, proj_ref, chunk, qe_ref, qn_ref, ke_ref, kn_ref, kw_ref, eb_ref, vb_ref):
    nrows = loga.shape[0]
    la_hi, la_lo = _split_bf16(loga)
    tri = tri_ref[...]
    b = _mm(tri, la_hi) + _mm(tri, la_lo)
    b_end = jnp.concatenate(
        [jnp.broadcast_to(b[c * chunk + chunk - 1:(c + 1) * chunk, :], (chunk, GLA_KW))
         for c in range(nrows // chunk)], axis=0)
    eb = jnp.exp(b)
    ebn = jnp.exp(-b)
    wk = jnp.exp(b_end - b)
    q = proj_ref[:, C_GQ:C_GQ + GLA_KW] * (GLA_DK ** -0.5)
    k = proj_ref[:, C_GK:C_GK + GLA_KW]
    eb_ref[...] = eb
    qe_ref[...] = (q * eb).astype(BF16)
    qn_ref[...] = (q * ebn).astype(BF16)
    ke_ref[...] = (k * eb).astype(BF16)
    kn_ref[...] = (k * ebn).astype(BF16)
    kw_ref[...] = (k * wk).astype(BF16)
    vb_ref[...] = proj_ref[:, C_GV:C_GV + GLA_VW].astype(BF16)


def _gla_chunk(r0, chunk, qe_ref, qn_ref, ke_ref, kn_ref, kw_ref, eb_ref, vb_ref, sbt_ref, mkk, mv, ms, caus):
    rows = slice(r0, r0 + chunk)
    qe = qe_ref[rows, :]
    qn = qn_ref[rows, :]
    zero = jnp.zeros((), BF16)
    kn4 = jnp.where(mkk != 0, jnp.concatenate([kn_ref[rows, :]] * GLA_HEADS, axis=0), zero)
    ke4 = jnp.where(mkk != 0, jnp.concatenate([ke_ref[rows, :]] * GLA_HEADS, axis=0), zero)
    a_lo = _mm_nt(qe, kn4)
    a_up = _mm_nt(qn, ke4)
    a = jnp.where(caus != 0.0, a_lo, a_up).astype(BF16)
    v = vb_ref[rows, :]
    v4 = jnp.where(mv != 0, jnp.concatenate([v] * GLA_HEADS, axis=0), zero)
    sbt = sbt_ref[...]
    o = _mm(a, v4) + _mm_nt(qe, sbt.astype(BF16))
    ut = _mm_tn(v, kw_ref[rows, :])
    g_end = eb_ref[r0 + chunk - 1:r0 + chunk, :]
    sbt_ref[...] = g_end * sbt + jnp.where(ms != 0.0, ut, 0.0)
    return o


def _gla_post(o, gr, gnw):
    res = []
    for h in range(GLA_HEADS):
        oh = o[:, h * GLA_DV:(h + 1) * GLA_DV]
        ms_ = jnp.mean(oh * oh, axis=-1, keepdims=True)
        og = oh * lax.rsqrt(ms_ + LN_EPS) * gnw
        r = gr[:, h * GLA_DV:(h + 1) * GLA_DV]
        res.append((og * (r * _sigmoid(r))).astype(BF16))
    return res


def _swa_problem(qs, kwin, vwin, bias, sinks_ref, hk, extra_valid, qb):
    s = _mm_nt(qs, kwin)
    ok = bias > -1e29
    if extra_valid is not None:
        ok = ok & extra_valid
    s = jnp.where(ok, s + bias, NEG_INF)
    ps, denoms = [], []
    for g in range(SWA_GROUP):
        sg = s[g * qb:(g + 1) * qb, :]
        sink = sinks_ref[hk * SWA_GROUP + g]
        m = jnp.maximum(jnp.max(sg, axis=-1, keepdims=True), sink)
        pg = jnp.exp(sg - m)
        denoms.append(jnp.sum(pg, axis=-1, keepdims=True) + jnp.exp(sink - m))
        ps.append(pg.astype(BF16))
    o = _mm(jnp.concatenate(ps, axis=0), vwin)
    return o / jnp.concatenate(denoms, axis=0)


def _swa_block(sq, kwin, vwin, bias_ref, sinks_ref, extra_valid, qb):
    low = lax.broadcasted_iota(jnp.int32, (qb, LANE), 1) < SWA_DH
    outs = []
    for hk in range(SWA_KV_HEADS):
        keep = low if hk == 0 else jnp.logical_not(low)
        qs = jnp.concatenate([jnp.where(keep, sq[:, g * LANE:(g + 1) * LANE] * (SWA_DH ** -0.5), 0.0)
                              for g in range(SWA_GROUP)], axis=0).astype(BF16)
        outs.append(_swa_problem(qs, kwin, vwin, bias_ref[hk], sinks_ref, hk, extra_valid, qb))
    return [jnp.where(low, outs[0][g * qb:(g + 1) * qb, :], outs[1][g * qb:(g + 1) * qb, :]).astype(BF16)
            for g in range(SWA_GROUP)]


def _out_proj_ln(x, gt1, mixed_ref, wo_ref, ln_g, ln_b):
    mix = _mm(mixed_ref[...], wo_ref[...])
    return _layer_norm(DEEPNORM_ALPHA * x + gt1 * mix, ln_g, ln_b)


P_QB = 128
P_KW = WINDOW + P_QB


def _mixer_prompt_kernel(sinks_ref, x_ref, mod_ref, win_ref, a2w_ref, a2b_ref, gnw_ref, wo_ref, lng_ref, lnb_ref,
                         tri_ref, bias_ref, mkk_ref, mv_ref, ms_ref, caus_ref,
                         x1_ref, s_out_ref, k_out_ref, v_out_ref,
                         proj_ref, qe_ref, qn_ref, ke_ref, kn_ref, kw_ref, eb_ref, vb_ref, kbuf, vbuf, mixed_ref,
                         sbt_ref, *, tl):
    j = pl.program_id(1)
    nj = pl.num_programs(1)
    d = D_MODEL

    @pl.when(j == 0)
    def _():
        sbt_ref[...] = jnp.zeros_like(sbt_ref)
        kbuf[0:WINDOW, :] = jnp.zeros((WINDOW, LANE), BF16)
        vbuf[0:WINDOW, :] = jnp.zeros((WINDOW, LANE), BF16)

    x = x_ref[0]
    mod = mod_ref[0]
    h = (x * (1.0 + mod[:, d:2 * d]) + mod[:, 0:d]).astype(BF16)
    loga = _in_proj(h, win_ref, a2w_ref, a2b_ref, proj_ref)
    _gla_prep(loga, tri_ref, proj_ref, CHUNK, qe_ref, qn_ref, ke_ref, kn_ref, kw_ref, eb_ref, vb_ref)
    kbuf[WINDOW:WINDOW + tl, :] = proj_ref[:, C_SK:C_SK + LANE].astype(BF16)
    vbuf[WINDOW:WINDOW + tl, :] = proj_ref[:, C_SV:C_SV + LANE].astype(BF16)

    mkk = mkk_ref[...]
    mv = mv_ref[...]
    ms = ms_ref[...]
    caus = caus_ref[...]
    gnw = gnw_ref[...]
    kj = lax.broadcasted_iota(jnp.int32, (SWA_GROUP * P_QB, P_KW), 1)
    first_valid = kj >= jnp.where(j > 0, 0, WINDOW)

    for p in range(tl // P_QB):
        q0 = p * P_QB
        blocks = _swa_block(proj_ref[q0:q0 + P_QB, C_SQ:C_SQ + SWA_W], kbuf[q0:q0 + P_KW, :], vbuf[q0:q0 + P_KW, :],
                            bias_ref, sinks_ref, first_valid if p == 0 else None, P_QB)
        for i, blk in enumerate(blocks):
            mixed_ref[q0:q0 + P_QB, GLA_VW + i * LANE:GLA_VW + (i + 1) * LANE] = blk
        for c in range(P_QB // CHUNK):
            r0 = q0 + c * CHUNK
            o = _gla_chunk(r0, CHUNK, qe_ref, qn_ref, ke_ref, kn_ref, kw_ref, eb_ref, vb_ref, sbt_ref,
                           mkk, mv, ms, caus)
            og = _gla_post(o, proj_ref[r0:r0 + CHUNK, C_GR:C_GR + GLA_VW], gnw)
            for hh in range(GLA_HEADS):
                mixed_ref[r0:r0 + CHUNK, hh * GLA_DV:(hh + 1) * GLA_DV] = og[hh]

    x1_ref[0] = _out_proj_ln(x, mod[:, 2 * d:3 * d], mixed_ref, wo_ref, lng_ref[...], lnb_ref[...])

    kbuf[0:WINDOW, :] = kbuf[tl:tl + WINDOW, :]
    vbuf[0:WINDOW, :] = vbuf[tl:tl + WINDOW, :]

    @pl.when(j == nj - 1)
    def _():
        for hh in range(GLA_HEADS):
            s_out_ref[0, hh] = sbt_ref[hh * GLA_DV:(hh + 1) * GLA_DV, hh * GLA_DK:(hh + 1) * GLA_DK].T
        k_out_ref[0] = proj_ref[tl - WINDOW:tl, C_SK:C_SK + LANE]
        v_out_ref[0] = proj_ref[tl - WINDOW:tl, C_SV:C_SV + LANE]


def _mixer_prompt(x, mod, sinks, win, a2w, a2b, gnw, wo, lng, lnb, *, tl=512):
    b, l, d = x.shape
    nj = l // tl
    consts = _mixer_consts(tl, CHUNK, P_QB, P_KW)
    const2 = lambda i, j, s: (0, 0)
    const3 = lambda i, j, s: (0, 0, 0)
    grid_spec = pltpu.PrefetchScalarGridSpec(
        num_scalar_prefetch=1,
        grid=(b, nj),
        in_specs=[
            pl.BlockSpec((1, tl, d), lambda i, j, s: (i, j, 0)),
            pl.BlockSpec((1, 1, 6 * d), lambda i, j, s: (i, 0, 0)),
            pl.BlockSpec((d, PROJ_W), const2),
            pl.BlockSpec((LANE, GLA_KW), const2),
            pl.BlockSpec((1, GLA_KW), const2),
            pl.BlockSpec((1, GLA_DV), const2),
            pl.BlockSpec((d, d), const2),
            pl.BlockSpec((1, d), const2),
            pl.BlockSpec((1, d), const2),
            pl.BlockSpec((tl, tl), const2),
            pl.BlockSpec((SWA_KV_HEADS, SWA_GROUP * P_QB, P_KW), const3),
            pl.BlockSpec((GLA_HEADS * CHUNK, GLA_KW), const2),
            pl.BlockSpec((GLA_HEADS * CHUNK, GLA_VW), const2),
            pl.BlockSpec((GLA_VW, GLA_KW), const2),
            pl.BlockSpec((CHUNK, GLA_HEADS * CHUNK), const2),
        ],
        out_specs=[
            pl.BlockSpec((1, tl, d), lambda i, j, s: (i, j, 0)),
            pl.BlockSpec((1, GLA_HEADS, GLA_DK, GLA_DV), lambda i, j, s: (i, 0, 0, 0)),
            pl.BlockSpec((1, WINDOW, LANE), lambda i, j, s: (i, 0, 0)),
            pl.BlockSpec((1, WINDOW, LANE), lambda i, j, s: (i, 0, 0)),
        ],
        scratch_shapes=[
            pltpu.VMEM((tl, PROJ_W), F32),
            pltpu.VMEM((tl, GLA_KW), BF16),
            pltpu.VMEM((tl, GLA_KW), BF16),
            pltpu.VMEM((tl, GLA_KW), BF16),
            pltpu.VMEM((tl, GLA_KW), BF16),
            pltpu.VMEM((tl, GLA_KW), BF16),
            pltpu.VMEM((tl, GLA_KW), F32),
            pltpu.VMEM((tl, GLA_VW), BF16),
            pltpu.VMEM((WINDOW + tl, LANE), BF16),
            pltpu.VMEM((WINDOW + tl, LANE), BF16),
            pltpu.VMEM((tl, d), BF16),
            pltpu.VMEM((GLA_VW, GLA_KW), F32),
        ],
    )
    return pl.pallas_call(
        functools.partial(_mixer_prompt_kernel, tl=tl),
        out_shape=[
            jax.ShapeDtypeStruct((b, l, d), F32),
            jax.ShapeDtypeStruct((b, GLA_HEADS, GLA_DK, GLA_DV), F32),
            jax.ShapeDtypeStruct((b, WINDOW, LANE), F32),
            jax.ShapeDtypeStruct((b, WINDOW, LANE), F32),
        ],
        grid_spec=grid_spec,
        compiler_params=pltpu.CompilerParams(dimension_semantics=("arbitrary", "arbitrary"),
                                             vmem_limit_bytes=VMEM_LIMIT),
        name="mixer_prompt",
    )(sinks, x, mod, win, a2w, a2b, gnw, wo, lng, lnb, *consts)


def _mixer_sample_kernel(sinks_ref, x_ref, mod_ref, s0_ref, kc_ref, vc_ref, win_ref, a2w_ref, a2b_ref, gnw_ref,
                         wo_ref, lng_ref, lnb_ref, tri_ref, bias_ref, mkk_ref, mv_ref, ms_ref, caus_ref,
                         x1_ref, s_out_ref, k_out_ref, v_out_ref,
                         proj_ref, qe_ref, qn_ref, ke_ref, kn_ref, kw_ref, eb_ref, vb_ref, kbuf, vbuf, mixed_ref,
                         sbt_ref, xm_ref, *, nb, s):
    nkeys = WINDOW + s
    d = D_MODEL
    for bb in range(nb):
        m = mod_ref[bb]
        xm_ref[bb * s:(bb + 1) * s, :] = x_ref[bb] * (1.0 + m[:, d:2 * d]) + m[:, 0:d]
    loga = _in_proj(xm_ref[...].astype(BF16), win_ref, a2w_ref, a2b_ref, proj_ref)
    _gla_prep(loga, tri_ref, proj_ref, s, qe_ref, qn_ref, ke_ref, kn_ref, kw_ref, eb_ref, vb_ref)

    mkk = mkk_ref[...]
    mv = mv_ref[...]
    ms = ms_ref[...]
    caus = caus_ref[...]
    gnw = gnw_ref[...]

    for bb in range(nb):
        r0 = bb * s
        rows = slice(r0, r0 + s)
        sbt_ref[...] = jnp.zeros_like(sbt_ref)
        for hh in range(GLA_HEADS):
            sbt_ref[hh * GLA_DV:(hh + 1) * GLA_DV, hh * GLA_DK:(hh + 1) * GLA_DK] = s0_ref[bb, hh].T
        kbuf[0:WINDOW, :] = kc_ref[bb].astype(BF16)
        vbuf[0:WINDOW, :] = vc_ref[bb].astype(BF16)
        kbuf[WINDOW:nkeys, :] = proj_ref[rows, C_SK:C_SK + LANE].astype(BF16)
        vbuf[WINDOW:nkeys, :] = proj_ref[rows, C_SV:C_SV + LANE].astype(BF16)
        blocks = _swa_block(proj_ref[rows, C_SQ:C_SQ + SWA_W], kbuf[...], vbuf[...], bias_ref, sinks_ref, None, s)
        for i, blk in enumerate(blocks):
            mixed_ref[rows, GLA_VW + i * LANE:GLA_VW + (i + 1) * LANE] = blk
        o = _gla_chunk(r0, s, qe_ref, qn_ref, ke_ref, kn_ref, kw_ref, eb_ref, vb_ref, sbt_ref, mkk, mv, ms, caus)
        og = _gla_post(o, proj_ref[rows, C_GR:C_GR + GLA_VW], gnw)
        for hh in range(GLA_HEADS):
            mixed_ref[rows, hh * GLA_DV:(hh + 1) * GLA_DV] = og[hh]
            s_out_ref[bb, hh] = sbt_ref[hh * GLA_DV:(hh + 1) * GLA_DV, hh * GLA_DK:(hh + 1) * GLA_DK].T
        k_out_ref[bb, 0:WINDOW - s, :] = kc_ref[bb, s:WINDOW, :]
        v_out_ref[bb, 0:WINDOW - s, :] = vc_ref[bb, s:WINDOW, :]
        k_out_ref[bb, WINDOW - s:WINDOW, :] = proj_ref[rows, C_SK:C_SK + LANE]
        v_out_ref[bb, WINDOW - s:WINDOW, :] = proj_ref[rows, C_SV:C_SV + LANE]

    mix = _mm(mixed_ref[...], wo_ref[...])
    lng = lng_ref[...]
    lnb = lnb_ref[...]
    for bb in range(nb):
        m = mod_ref[bb]
        y = DEEPNORM_ALPHA * x_ref[bb] + m[:, 2 * d:3 * d] * mix[bb * s:(bb + 1) * s, :]
        x1_ref[bb] = _layer_norm(y, lng, lnb)


def _mixer_sample(x, mod, s0, kc, vc, sinks, win, a2w, a2b, gnw, wo, lng, lnb, *, nb=8):
    b, s, d = x.shape
    assert kc.shape[1] == WINDOW and s <= WINDOW
    rows = nb * s
    nkeys = WINDOW + s
    consts = _mixer_consts(rows, s, s, nkeys)
    const2 = lambda i, sk: (0, 0)
    const3 = lambda i, sk: (0, 0, 0)
    grid_spec = pltpu.PrefetchScalarGridSpec(
        num_scalar_prefetch=1,
        grid=(b // nb,),
        in_specs=[
            pl.BlockSpec((nb, s, d), lambda i, sk: (i, 0, 0)),
            pl.BlockSpec((nb, 1, 6 * d), lambda i, sk: (i, 0, 0)),
            pl.BlockSpec((nb, GLA_HEADS, GLA_DK, GLA_DV), lambda i, sk: (i, 0, 0, 0)),
            pl.BlockSpec((nb, WINDOW, LANE), lambda i, sk: (i, 0, 0)),
            pl.BlockSpec((nb, WINDOW, LANE), lambda i, sk: (i, 0, 0)),
            pl.BlockSpec((d, PROJ_W), const2),
            pl.BlockSpec((LANE, GLA_KW), const2),
            pl.BlockSpec((1, GLA_KW), const2),
            pl.BlockSpec((1, GLA_DV), const2),
            pl.BlockSpec((d, d), const2),
            pl.BlockSpec((1, d), const2),
            pl.BlockSpec((1, d), const2),
            pl.BlockSpec((rows, rows), const2),
            pl.BlockSpec((SWA_KV_HEADS, SWA_GROUP * s, nkeys), const3),
            pl.BlockSpec((GLA_HEADS * s, GLA_KW), const2),
            pl.BlockSpec((GLA_HEADS * s, GLA_VW), const2),
            pl.BlockSpec((GLA_VW, GLA_KW), const2),
            pl.BlockSpec((s, GLA_HEADS * s), const2),
        ],
        out_specs=[
            pl.BlockSpec((nb, s, d), lambda i, sk: (i, 0, 0)),
            pl.BlockSpec((nb, GLA_HEADS, GLA_DK, GLA_DV), lambda i, sk: (i, 0, 0, 0)),
            pl.BlockSpec((nb, WINDOW, LANE), lambda i, sk: (i, 0, 0)),
            pl.BlockSpec((nb, WINDOW, LANE), lambda i, sk: (i, 0, 0)),
        ],
        scratch_shapes=[
            pltpu.VMEM((rows, PROJ_W), F32),
            pltpu.VMEM((rows, GLA_KW), BF16),
            pltpu.VMEM((rows, GLA_KW), BF16),
            pltpu.VMEM((rows, GLA_KW), BF16),
            pltpu.VMEM((rows, GLA_KW), BF16),
            pltpu.VMEM((rows, GLA_KW), BF16),
            pltpu.VMEM((rows, GLA_KW), F32),
            pltpu.VMEM((rows, GLA_VW), BF16),
            pltpu.VMEM((nkeys, LANE), BF16),
            pltpu.VMEM((nkeys, LANE), BF16),
            pltpu.VMEM((rows, d), BF16),
            pltpu.VMEM((GLA_VW, GLA_KW), F32),
            pltpu.VMEM((rows, d), F32),
        ],
    )
    return pl.pallas_call(
        functools.partial(_mixer_sample_kernel, nb=nb, s=s),
        out_shape=[
            jax.ShapeDtypeStruct((b, s, d), F32),
            jax.ShapeDtypeStruct((b, GLA_HEADS, GLA_DK, GLA_DV), F32),
            jax.ShapeDtypeStruct((b, WINDOW, LANE), F32),
            jax.ShapeDtypeStruct((b, WINDOW, LANE), F32),
        ],
        grid_spec=grid_spec,
        compiler_params=pltpu.CompilerParams(dimension_semantics=("arbitrary",),
                                             vmem_limit_bytes=VMEM_LIMIT),
        name="mixer_sample",
    )(sinks, x, mod, s0, kc, vc, win, a2w, a2b, gnw, wo, lng, lnb, *consts)


MOE_BLK = 128


def _route_t(logits_t):
    t = logits_t.shape[1]
    row = lax.broadcasted_iota(jnp.int32, (EPG, t), 0).astype(F32)
    big = 99.0
    gl = jnp.where(row < N_GROUPS, logits_t[0:EPG, :], -jnp.inf)
    gmax = jnp.max(gl, axis=0, keepdims=True)
    grp = jnp.min(jnp.where(gl == gmax, row, big), axis=0, keepdims=True)
    p_grp = 1.0 / jnp.sum(jnp.exp(gl - gmax), axis=0, keepdims=True)
    el = jnp.zeros((EPG, t), F32)
    for g in range(N_GROUPS):
        el = el + jnp.where(grp == float(g), logits_t[R_EXP0 + EPG * g:R_EXP0 + EPG * (g + 1), :], 0.0)
    v1 = jnp.max(el, axis=0, keepdims=True)
    i1 = jnp.min(jnp.where(el == v1, row, big), axis=0, keepdims=True)
    el2 = jnp.where(row == i1, -jnp.inf, el)
    v2 = jnp.max(el2, axis=0, keepdims=True)
    i2 = jnp.min(jnp.where(el2 == v2, row, big), axis=0, keepdims=True)
    e2 = jnp.exp(v2 - v1)
    w1 = p_grp / (1.0 + e2)
    w2 = p_grp * e2 / (1.0 + e2)
    cw = jnp.where(row == i1, w1, 0.0) + jnp.where(row == i2, w2, 0.0)
    return grp, cw


MOE_PC = 256


def _moe_sort_kernel(x1_ref, mod_ref, wrt_ref, brt_ref, xs_ref, cws_ref, pos_ref, cnt_ref, *, nb, r):
    tm = nb * r
    tmp = tm + N_GROUPS * MOE_BLK
    d = D_MODEL
    mod = mod_ref[...]
    t3 = x1_ref[...] * (1.0 + mod[:, :, 4 * d:5 * d]) + mod[:, :, 3 * d:4 * d]
    t = t3.reshape(tm, d).astype(BF16)
    grp, cw = _route_t(_mm_nt(wrt_ref[...], t) + brt_ref[...])
    row = lax.broadcasted_iota(jnp.int32, (EPG, tm), 0).astype(F32)
    onehot_g = jnp.where(row == grp, 1.0, 0.0)
    nch = tm // LANE
    strict = (lax.broadcasted_iota(jnp.int32, (LANE, LANE), 0)
              < lax.broadcasted_iota(jnp.int32, (LANE, LANE), 1)).astype(F32).astype(BF16)
    stacked = jnp.concatenate([onehot_g[:, c * LANE:(c + 1) * LANE] for c in range(nch)], axis=0)
    pref = _mm(stacked.astype(BF16), strict)
    tot = jnp.sum(stacked, axis=1, keepdims=True)
    cnt = jnp.zeros((EPG, 1), F32)
    ranks = []
    for c in range(nch):
        ranks.append(pref[c * EPG:(c + 1) * EPG, :] + cnt)
        cnt = cnt + tot[c * EPG:(c + 1) * EPG, :]
    rank = jnp.concatenate(ranks, axis=1)
    padded = jnp.floor((cnt + (MOE_BLK - 1)) * (1.0 / MOE_BLK)) * MOE_BLK
    rowc = lax.broadcasted_iota(jnp.int32, (EPG, 1), 0)
    off = jnp.zeros((EPG, 1), F32)
    for gg in range(N_GROUPS - 1):
        off = off + jnp.where(rowc > gg, padded[gg:gg + 1, :], 0.0)
    pos = jnp.sum(onehot_g * (off + rank), axis=0, keepdims=True)
    pos_ref[0] = jnp.broadcast_to(pos, (EPG, tm))
    cnt_ref[0] = jnp.broadcast_to(cnt, (EPG, LANE))
    cw_hi, cw_lo = _split_bf16(jnp.concatenate([cw, jnp.zeros((LANE - EPG, tm), F32)], axis=0).T)
    t_aug = jnp.concatenate([t, cw_hi, cw_lo], axis=1)
    for c in range(tmp // MOE_PC):
        slot = (lax.broadcasted_iota(jnp.int32, (MOE_PC, tm), 0) + c * MOE_PC).astype(F32)
        perm = jnp.where(slot == pos, 1.0, 0.0).astype(BF16)
        moved = _mm(perm, t_aug)
        xs_ref[c * MOE_PC:(c + 1) * MOE_PC, :] = moved[:, 0:d].astype(BF16)
        cws_ref[c * MOE_PC:(c + 1) * MOE_PC, :] = moved[:, d:d + LANE] + moved[:, d + LANE:d + 2 * LANE]


def _moe_expert_kernel(blk_row_ref, blk_grp_ref, nvalid_ref, xs_ref, cws_ref, wg_ref, wu_ref, wd_ref, ys_ref):
    del blk_row_ref, blk_grp_ref

    @pl.when(pl.program_id(0) < nvalid_ref[0])
    def _():
        xb = xs_ref[...]
        cwb = cws_ref[...]
        hs = []
        for e in range(EPG):
            gg_ = _mm(xb, wg_ref[0, e])
            uu = _mm(xb, wu_ref[0, e])
            hs.append((gg_ * _sigmoid(gg_) * uu * cwb[:, e:e + 1]).astype(BF16))
        y = _mm(jnp.concatenate(hs, axis=1), wd_ref[0].reshape(EPG * EXPERT_FF, D_MODEL))
        ys_ref[...] = y.astype(BF16)


def _moe_unsort_kernel(used_ref, ys_ref, pos_ref, x1_ref, mod_ref, lng_ref, lnb_ref, out_ref, *, nb, r):
    tm = nb * r
    tmp = tm + N_GROUPS * MOE_BLK
    d = D_MODEL
    used = used_ref[pl.program_id(0)]
    zero = jnp.zeros((), BF16)
    ysb = jnp.concatenate([jnp.where(k * MOE_BLK < used, ys_ref[k * MOE_BLK:(k + 1) * MOE_BLK, :], zero)
                           for k in range(tmp // MOE_BLK)], axis=0)
    posc = jnp.broadcast_to(pos_ref[0][0:1, :], (LANE, tm)).T
    lng = lng_ref[...]
    lnb = lnb_ref[...]
    for c in range(tm // MOE_PC):
        slot = lax.broadcasted_iota(jnp.int32, (MOE_PC, tmp), 1).astype(F32)
        unperm = jnp.where(slot == posc[c * MOE_PC:(c + 1) * MOE_PC, 0:1], 1.0, 0.0).astype(BF16)
        y = _mm(unperm, ysb)
        if nb == 1:
            x1c = x1_ref[0, c * MOE_PC:(c + 1) * MOE_PC, :]
            gt2 = mod_ref[0][:, 5 * d:6 * d]
            out_ref[0, c * MOE_PC:(c + 1) * MOE_PC, :] = _layer_norm(DEEPNORM_ALPHA * x1c + gt2 * y, lng, lnb)
        else:
            cb = MOE_PC // r
            x1c = x1_ref[c * cb:(c + 1) * cb]
            gt2 = mod_ref[c * cb:(c + 1) * cb][:, :, 5 * d:6 * d]
            yy = DEEPNORM_ALPHA * x1c + gt2 * y.reshape(cb, r, d)
            out_ref[c * cb:(c + 1) * cb] = _layer_norm(yy, lng, lnb)


def _moe(x1, mod, wrt, brt, wg, wu, wd, lng, lnb, *, nb, r):
    b, l, d = x1.shape
    tpb = l // r
    ntiles = (b // nb) * tpb
    tm = nb * r
    tmp = tm + N_GROUPS * MOE_BLK
    bpt = tmp // MOE_BLK
    xmap = lambda i: (i // tpb, i % tpb, 0)
    mmap = lambda i: (i // tpb, 0, 0)
    const2 = lambda i: (0, 0)
    params = pltpu.CompilerParams(dimension_semantics=("arbitrary",), vmem_limit_bytes=VMEM_LIMIT)

    xs, cws, pos, cnt = pl.pallas_call(
        functools.partial(_moe_sort_kernel, nb=nb, r=r),
        out_shape=[jax.ShapeDtypeStruct((ntiles * tmp, d), BF16),
                   jax.ShapeDtypeStruct((ntiles * tmp, LANE), F32),
                   jax.ShapeDtypeStruct((ntiles, EPG, tm), F32),
                   jax.ShapeDtypeStruct((ntiles, EPG, LANE), F32)],
        grid=(ntiles,),
        in_specs=[
            pl.BlockSpec((nb, r, d), xmap),
            pl.BlockSpec((nb, 1, 6 * d), mmap),
            pl.BlockSpec((LANE, d), const2),
            pl.BlockSpec((LANE, 1), const2),
        ],
        out_specs=[pl.BlockSpec((tmp, d), lambda i: (i, 0)),
                   pl.BlockSpec((tmp, LANE), lambda i: (i, 0)),
                   pl.BlockSpec((1, EPG, tm), lambda i: (i, 0, 0)),
                   pl.BlockSpec((1, EPG, LANE), lambda i: (i, 0, 0))],
        compiler_params=params,
        name="moe_sort",
    )(x1, mod, wrt, brt)

    nblk = ((cnt[:, :N_GROUPS, 0].astype(jnp.int32) + (MOE_BLK - 1)) // MOE_BLK)
    boff = jnp.cumsum(nblk, axis=1) - nblk
    used = jnp.sum(nblk, axis=1) * MOE_BLK
    n_gt = nblk.T.reshape(-1)
    first = (jnp.arange(ntiles, dtype=jnp.int32)[None, :] * bpt + boff.T).reshape(-1)
    start = jnp.cumsum(n_gt) - n_gt
    total = jnp.sum(n_gt)
    nb_max = ntiles * bpt
    slot = jnp.arange(nb_max, dtype=jnp.int32)
    slot_c = jnp.minimum(slot, jnp.maximum(total - 1, 0))
    in_seg = ((slot_c[:, None] >= start[None, :]) & (slot_c[:, None] < (start + n_gt)[None, :])).astype(jnp.int32)
    row_blk = jnp.sum(in_seg * (first - start)[None, :], axis=1) + slot_c
    grp_blk = jnp.sum(in_seg * (jnp.arange(N_GROUPS * ntiles, dtype=jnp.int32) // ntiles)[None, :], axis=1)

    wmap = lambda i, rb, gb, nv: (gb[i], 0, 0, 0)
    ys = pl.pallas_call(
        _moe_expert_kernel,
        out_shape=jax.ShapeDtypeStruct((ntiles * tmp, d), BF16),
        grid_spec=pltpu.PrefetchScalarGridSpec(
            num_scalar_prefetch=3,
            grid=(nb_max,),
            in_specs=[
                pl.BlockSpec((MOE_BLK, d), lambda i, rb, gb, nv: (rb[i], 0)),
                pl.BlockSpec((MOE_BLK, LANE), lambda i, rb, gb, nv: (rb[i], 0)),
                pl.BlockSpec((1, EPG, d, EXPERT_FF), wmap),
                pl.BlockSpec((1, EPG, d, EXPERT_FF), wmap),
                pl.BlockSpec((1, EPG, EXPERT_FF, d), wmap),
            ],
            out_specs=pl.BlockSpec((MOE_BLK, d), lambda i, rb, gb, nv: (rb[i], 0)),
        ),
        compiler_params=params,
        name="moe_experts",
    )(row_blk, grp_blk, total.reshape(1).astype(jnp.int32), xs, cws, wg, wu, wd)

    return pl.pallas_call(
        functools.partial(_moe_unsort_kernel, nb=nb, r=r),
        out_shape=jax.ShapeDtypeStruct((b, l, d), F32),
        grid_spec=pltpu.PrefetchScalarGridSpec(
            num_scalar_prefetch=1,
            grid=(ntiles,),
            in_specs=[
                pl.BlockSpec((tmp, d), lambda i, u: (i, 0)),
                pl.BlockSpec((1, EPG, tm), lambda i, u: (i, 0, 0)),
                pl.BlockSpec((nb, r, d), lambda i, u: (i // tpb, i % tpb, 0)),
                pl.BlockSpec((nb, 1, 6 * d), lambda i, u: (i // tpb, 0, 0)),
                pl.BlockSpec((1, d), lambda i, u: (0, 0)),
                pl.BlockSpec((1, d), lambda i, u: (0, 0)),
            ],
            out_specs=pl.BlockSpec((nb, r, d), lambda i, u: (i // tpb, i % tpb, 0)),
        ),
        compiler_params=params,
        name="moe_unsort",
    )(used.astype(jnp.int32), ys, pos, x1, mod, lng, lnb)


def kernel(x_prompt, x_sample, c_prompt, c_sample, state_gla, cache_swa_k, cache_swa_v, ada_w, ada_b, w_in,
           gla_a2_w, gla_a2_b, gla_norm_w, swa_sinks, w_o, ln1_g, ln1_b, router_g_w, router_g_b, router_e_w,
           router_e_b, moe_w_gate, moe_w_up, moe_w_down, ln2_g, ln2_b):
    assert ada_w.shape[0] == 1
    bp = x_prompt.shape[0]
    bs, ss, d = x_sample.shape
    lc = cache_swa_k.shape[2]

    w = w_in[0]
    zpad = jnp.zeros((d, LANE - GLA_RANK), F32)
    w_sq = w[:, 1552:2064].reshape(d, SWA_KV_HEADS, SWA_GROUP, SWA_DH).transpose(0, 2, 1, 3).reshape(d, SWA_W)
    win = jnp.concatenate([w[:, 0:1536], w_sq, w[:, 2064:2320], w[:, 1536:1552], zpad], axis=1).astype(BF16)
    a2w = jnp.concatenate([gla_a2_w[0], jnp.zeros((LANE - GLA_RANK, GLA_KW), F32)], axis=0).astype(BF16)
    a2b = gla_a2_b[0].reshape(1, GLA_KW)
    gnw = gla_norm_w[0].reshape(1, GLA_DV)
    wo_swa = w_o[0][GLA_VW:].reshape(SWA_KV_HEADS, SWA_GROUP, SWA_DH, d).transpose(1, 0, 2, 3).reshape(SWA_W, d)
    wo = jnp.concatenate([w_o[0][:GLA_VW], wo_swa], axis=0).astype(BF16)
    sinks = swa_sinks[0]
    wrt = jnp.concatenate([router_g_w[0], jnp.zeros((d, R_EXP0 - N_GROUPS), F32),
                           jnp.transpose(router_e_w[0], (1, 0, 2)).reshape(d, N_GROUPS * EPG),
                           jnp.zeros((d, LANE - R_EXP0 - N_GROUPS * EPG), F32)], axis=1).T.astype(BF16)
    brt = jnp.concatenate([router_g_b[0], jnp.zeros((R_EXP0 - N_GROUPS,), F32), router_e_b[0].reshape(-1),
                           jnp.zeros((LANE - R_EXP0 - N_GROUPS * EPG,), F32)]).reshape(LANE, 1)
    wg = moe_w_gate[0].astype(BF16)
    wu = moe_w_up[0].astype(BF16)
    wd = moe_w_down[0].astype(BF16)
    lng1, lnb1 = ln1_g[0].reshape(1, d), ln1_b[0].reshape(1, d)
    lng2, lnb2 = ln2_g[0].reshape(1, d), ln2_b[0].reshape(1, d)

    mod = _adaln(jnp.concatenate([c_prompt, c_sample], axis=0), ada_w[0], ada_b[0].reshape(1, 6 * d))
    mod = mod.reshape(bp + bs, 1, 6 * d)
    mod_p, mod_s = mod[:bp], mod[bp:]

    x1p, s_p, k_p, v_p = _mixer_prompt(x_prompt, mod_p, sinks, win, a2w, a2b, gnw, wo, lng1, lnb1)
    x1s, s_s, k_s, v_s = _mixer_sample(
        x_sample, mod_s, state_gla[0], cache_swa_k[0].reshape(bs, lc, LANE), cache_swa_v[0].reshape(bs, lc, LANE),
        sinks, win, a2w, a2b, gnw, wo, lng1, lnb1)

    yp = _moe(x1p, mod_p, wrt, brt, wg, wu, wd, lng2, lnb2, nb=1, r=1024)
    ys = _moe(x1s, mod_s, wrt, brt, wg, wu, wd, lng2, lnb2, nb=bs, r=ss)

    kv_shape_p = (1, bp, WINDOW, SWA_KV_HEADS, SWA_DH)
    kv_shape_s = (1, bs, lc, SWA_KV_HEADS, SWA_DH)
    return (yp, ys, s_p[None], k_p.reshape(kv_shape_p), v_p.reshape(kv_shape_p),
            s_s[None], k_s.reshape(kv_shape_s), v_s.reshape(kv_shape_s))
```

```python
import functools

import jax
import jax.numpy as jnp
import numpy as np
from jax import lax
from jax.experimental import pallas as pl
from jax.experimental.pallas import tpu as pltpu

F32 = jnp.float32
BF16 = jnp.bfloat16

D_MODEL = 1024
CHUNK = 64
GLA_HEADS = 4
GLA_DK = 64
GLA_DV = 128
GLA_KW = GLA_HEADS * GLA_DK
GLA_VW = GLA_HEADS * GLA_DV
GLA_RANK = 16
GLA_TAU = 16.0
SWA_Q_HEADS = 8
SWA_KV_HEADS = 2
SWA_GROUP = 4
SWA_DH = 64
SWA_W = SWA_Q_HEADS * SWA_DH
WINDOW = 128
N_GROUPS = 4
EPG = 8
EXPERT_FF = 256
DEEPNORM_ALPHA = 2.0 ** 0.25
LN_EPS = 1e-5
NEG_INF = -1e30

C_GQ, C_GK, C_GV, C_GR, C_SQ, C_SK, C_SV, C_GA = 0, 256, 512, 1024, 1536, 2048, 2176, 2304
PROJ_W = 2432
LANE = 128
R_EXP0 = 8

VMEM_LIMIT = 56 * 1024 * 1024


def _mm(a, b):
    return jnp.dot(a, b, preferred_element_type=F32)


def _mm_nt(a, b):
    return lax.dot_general(a, b, (((1,), (1,)), ((), ())), preferred_element_type=F32)


def _mm_tn(a, b):
    return lax.dot_general(a, b, (((0,), (0,)), ((), ())), preferred_element_type=F32)


def _split_bf16(a):
    hi = a.astype(BF16)
    lo = (a - hi.astype(F32)).astype(BF16)
    return hi, lo


def _sigmoid(x):
    return 1.0 / (1.0 + jnp.exp(-x))


def _layer_norm(y, g, b):
    mu = jnp.mean(y, axis=-1, keepdims=True)
    d = y - mu
    var = jnp.mean(d * d, axis=-1, keepdims=True)
    return d * lax.rsqrt(var + LN_EPS) * g + b


def _adaln_kernel(c_ref, w_ref, b_ref, o_ref):
    c = c_ref[...]
    a = c * _sigmoid(c)
    a_hi, a_lo = _split_bf16(a)
    w_hi, w_lo = _split_bf16(w_ref[...])
    o_ref[...] = _mm(a_hi, w_hi) + (_mm(a_hi, w_lo) + _mm(a_lo, w_hi)) + b_ref[...]


def _adaln(c_all, ada_w, ada_b):
    n = c_all.shape[0]
    bn = 1024
    return pl.pallas_call(
        _adaln_kernel,
        out_shape=jax.ShapeDtypeStruct((n, 6 * D_MODEL), F32),
        grid=(6 * D_MODEL // bn,),
        in_specs=[pl.BlockSpec((n, D_MODEL), lambda j: (0, 0)),
                  pl.BlockSpec((D_MODEL, bn), lambda j: (0, j)),
                  pl.BlockSpec((1, bn), lambda j: (0, j))],
        out_specs=pl.BlockSpec((n, bn), lambda j: (0, j)),
        compiler_params=pltpu.CompilerParams(dimension_semantics=("arbitrary",), vmem_limit_bytes=VMEM_LIMIT),
        name="adaln",
    )(c_all, ada_w, ada_b)


def _mixer_consts(nrows, chunk, qb, kw):
    r = np.arange(nrows)
    tri = ((r[:, None] // chunk == r[None, :] // chunk) & (r[:, None] >= r[None, :])).astype(np.float32)
    hs = np.arange(GLA_HEADS * chunk)
    mkk = (hs[:, None] // chunk == np.arange(GLA_KW)[None, :] // GLA_DK).astype(np.float32)
    mv = (hs[:, None] // chunk == np.arange(GLA_VW)[None, :] // GLA_DV).astype(np.float32)
    ms = (np.arange(GLA_VW)[:, None] // GLA_DV == np.arange(GLA_KW)[None, :] // GLA_DK).astype(np.float32)
    caus = (np.arange(chunk)[:, None] >= (hs[None, :] % chunk)).astype(np.float32)
    t = np.arange(qb)
    kj = np.arange(kw)
    cs = (t // chunk) * chunk
    kpos = kj[None, :] - WINDOW
    vis = (kpos >= cs[:, None] - WINDOW) & (kpos < cs[:, None] + chunk)
    dist = np.abs(t[:, None] + WINDOW - kj[None, :]).astype(np.float32)
    bias = np.zeros((SWA_KV_HEADS, SWA_GROUP * qb, kw), np.float32)
    for hk in range(SWA_KV_HEADS):
        for g in range(SWA_GROUP):
            slope = np.float32(2.0 ** (-(hk * SWA_GROUP + g + 1)))
            bias[hk, g * qb:(g + 1) * qb] = np.where(vis, -slope * dist, np.float32(2.0 * NEG_INF))
    return (jnp.asarray(tri, BF16), jnp.asarray(bias), jnp.asarray(mkk, BF16), jnp.asarray(mv, BF16),
            jnp.asarray(ms), jnp.asarray(caus))


def _in_proj(h, win_ref, a2w_ref, a2b_ref, proj_ref):
    proj_ref[...] = _mm(h, win_ref[...])
    ga = proj_ref[:, C_GA:C_GA + LANE].astype(BF16)
    z = _mm(ga, a2w_ref[...]) + a2b_ref[...]
    lsig = -(jnp.maximum(-z, 0.0) + jnp.log(1.0 + jnp.exp(-jnp.abs(z))))
    return lsig * (1.0 / GLA_TAU)


def _gla_prep(loga, tri_ref, proj_ref, chunk, qe_ref, qn_ref, ke_ref, kn_ref, kw_ref, eb_ref, vb_ref):
    nrows = loga.shape[0]
    la_hi, la_lo = _split_bf16(loga)
    tri = tri_ref[...]
    b = _mm(tri, la_hi) + _mm(tri, la_lo)
    b_end = jnp.concatenate(
        [jnp.broadcast_to(b[c * chunk + chunk - 1:(c + 1) * chunk, :], (chunk, GLA_KW))
         for c in range(nrows // chunk)], axis=0)
    eb = jnp.exp(b)
    ebn = jnp.exp(-b)
    wk = jnp.exp(b_end - b)
    q = proj_ref[:, C_GQ:C_GQ + GLA_KW] * (GLA_DK ** -0.5)
    k = proj_ref[:, C_GK:C_GK + GLA_KW]
    eb_ref[...] = eb
    qe_ref[...] = (q * eb).astype(BF16)
    qn_ref[...] = (q * ebn).astype(BF16)
    ke_ref[...] = (k * eb).astype(BF16)
    kn_ref[...] = (k * ebn).astype(BF16)
    kw_ref[...] = (k * wk).astype(BF16)
    vb_ref[...] = proj_ref[:, C_GV:C_GV + GLA_VW].astype(BF16)


def _gla_chunk(r0, chunk, qe_ref, qn_ref, ke_ref, kn_ref, kw_ref, eb_ref, vb_ref, sbt_ref, mkk, mv, ms, caus):
    rows = slice(r0, r0 + chunk)
    qe = qe_ref[rows, :]
    qn = qn_ref[rows, :]
    zero = jnp.zeros((), BF16)
    kn4 = jnp.where(mkk != 0, jnp.concatenate([kn_ref[rows, :]] * GLA_HEADS, axis=0), zero)
    ke4 = jnp.where(mkk != 0, jnp.concatenate([ke_ref[rows, :]] * GLA_HEADS, axis=0), zero)
    a_lo = _mm_nt(qe, kn4)
    a_up = _mm_nt(qn, ke4)
    a = jnp.where(caus != 0.0, a_lo, a_up).astype(BF16)
    v = vb_ref[rows, :]
    v4 = jnp.where(mv != 0, jnp.concatenate([v] * GLA_HEADS, axis=0), zero)
    sbt = sbt_ref[...]
    o = _mm(a, v4) + _mm_nt(qe, sbt.astype(BF16))
    ut = _mm_tn(v, kw_ref[rows, :])
    g_end = eb_ref[r0 + chunk - 1:r0 + chunk, :]
    sbt_ref[...] = g_end * sbt + jnp.where(ms != 0.0, ut, 0.0)
    return o


def _gla_post(o, gr, gnw):
    res = []
    for h in range(GLA_HEADS):
        oh = o[:, h * GLA_DV:(h + 1) * GLA_DV]
        ms_ = jnp.mean(oh * oh, axis=-1, keepdims=True)
        og = oh * lax.rsqrt(ms_ + LN_EPS) * gnw
        r = gr[:, h * GLA_DV:(h + 1) * GLA_DV]
        res.append((og * (r * _sigmoid(r))).astype(BF16))
    return res


def _swa_problem(qs, kwin, vwin, bias, sinks_ref, hk, extra_valid, qb):
    s = _mm_nt(qs, kwin)
    ok = bias > -1e29
    if extra_valid is not None:
        ok = ok & extra_valid
    s = jnp.where(ok, s + bias, NEG_INF)
    ps, denoms = [], []
    for g in range(SWA_GROUP):
        sg = s[g * qb:(g + 1) * qb, :]
        sink = sinks_ref[hk * SWA_GROUP + g]
        m = jnp.maximum(jnp.max(sg, axis=-1, keepdims=True), sink)
        pg = jnp.exp(sg - m)
        denoms.append(jnp.sum(pg, axis=-1, keepdims=True) + jnp.exp(sink - m))
        ps.append(pg.astype(BF16))
    o = _mm(jnp.concatenate(ps, axis=0), vwin)
    return o / jnp.concatenate(denoms, axis=0)


def _swa_block(sq, kwin, vwin, bias_ref, sinks_ref, extra_valid, qb):
    low = lax.broadcasted_iota(jnp.int32, (qb, LANE), 1) < SWA_DH
    outs = []
    for hk in range(SWA_KV_HEADS):
        keep = low if hk == 0 else jnp.logical_not(low)
        qs = jnp.concatenate([jnp.where(keep, sq[:, g * LANE:(g + 1) * LANE] * (SWA_DH ** -0.5), 0.0)
                              for g in range(SWA_GROUP)], axis=0).astype(BF16)
        outs.append(_swa_problem(qs, kwin, vwin, bias_ref[hk], sinks_ref, hk, extra_valid, qb))
    return [jnp.where(low, outs[0][g * qb:(g + 1) * qb, :], outs[1][g * qb:(g + 1) * qb, :]).astype(BF16)
            for g in range(SWA_GROUP)]


def _out_proj_ln(x, gt1, mixed_ref, wo_ref, ln_g, ln_b):
    mix = _mm(mixed_ref[...], wo_ref[...])
    return _layer_norm(DEEPNORM_ALPHA * x + gt1 * mix, ln_g, ln_b)


P_QB = 128
P_KW = WINDOW + P_QB


def _mixer_prompt_kernel(sinks_ref, x_ref, mod_ref, win_ref, a2w_ref, a2b_ref, gnw_ref, wo_ref, lng_ref, lnb_ref,
                         tri_ref, bias_ref, mkk_ref, mv_ref, ms_ref, caus_ref,
                         x1_ref, s_out_ref, k_out_ref, v_out_ref,
                         proj_ref, qe_ref, qn_ref, ke_ref, kn_ref, kw_ref, eb_ref, vb_ref, kbuf, vbuf, mixed_ref,
                         sbt_ref, *, tl):
    j = pl.program_id(1)
    nj = pl.num_programs(1)
    d = D_MODEL

    @pl.when(j == 0)
    def _():
        sbt_ref[...] = jnp.zeros_like(sbt_ref)
        kbuf[0:WINDOW, :] = jnp.zeros((WINDOW, LANE), BF16)
        vbuf[0:WINDOW, :] = jnp.zeros((WINDOW, LANE), BF16)

    x = x_ref[0]
    mod = mod_ref[0]
    h = (x * (1.0 + mod[:, d:2 * d]) + mod[:, 0:d]).astype(BF16)
    loga = _in_proj(h, win_ref, a2w_ref, a2b_ref, proj_ref)
    _gla_prep(loga, tri_ref, proj_ref, CHUNK, qe_ref, qn_ref, ke_ref, kn_ref, kw_ref, eb_ref, vb_ref)
    kbuf[WINDOW:WINDOW + tl, :] = proj_ref[:, C_SK:C_SK + LANE].astype(BF16)
    vbuf[WINDOW:WINDOW + tl, :] = proj_ref[:, C_SV:C_SV + LANE].astype(BF16)

    mkk = mkk_ref[...]
    mv = mv_ref[...]
    ms = ms_ref[...]
    caus = caus_ref[...]
    gnw = gnw_ref[...]
    kj = lax.broadcasted_iota(jnp.int32, (SWA_GROUP * P_QB, P_KW), 1)
    first_valid = kj >= jnp.where(j > 0, 0, WINDOW)

    for p in range(tl // P_QB):
        q0 = p * P_QB
        blocks = _swa_block(proj_ref[q0:q0 + P_QB, C_SQ:C_SQ + SWA_W], kbuf[q0:q0 + P_KW, :], vbuf[q0:q0 + P_KW, :],
                            bias_ref, sinks_ref, first_valid if p == 0 else None, P_QB)
        for i, blk in enumerate(blocks):
            mixed_ref[q0:q0 + P_QB, GLA_VW + i * LANE:GLA_VW + (i + 1) * LANE] = blk
        for c in range(P_QB // CHUNK):
            r0 = q0 + c * CHUNK
            o = _gla_chunk(r0, CHUNK, qe_ref, qn_ref, ke_ref, kn_ref, kw_ref, eb_ref, vb_ref, sbt_ref,
                           mkk, mv, ms, caus)
            og = _gla_post(o, proj_ref[r0:r0 + CHUNK, C_GR:C_GR + GLA_VW], gnw)
            for hh in range(GLA_HEADS):
                mixed_ref[r0:r0 + CHUNK, hh * GLA_DV:(hh + 1) * GLA_DV] = og[hh]

    x1_ref[0] = _out_proj_ln(x, mod[:, 2 * d:3 * d], mixed_ref, wo_ref, lng_ref[...], lnb_ref[...])

    kbuf[0:WINDOW, :] = kbuf[tl:tl + WINDOW, :]
    vbuf[0:WINDOW, :] = vbuf[tl:tl + WINDOW, :]

    @pl.when(j == nj - 1)
    def _():
        for hh in range(GLA_HEADS):
            s_out_ref[0, hh] = sbt_ref[hh * GLA_DV:(hh + 1) * GLA_DV, hh * GLA_DK:(hh + 1) * GLA_DK].T
        k_out_ref[0] = proj_ref[tl - WINDOW:tl, C_SK:C_SK + LANE]
        v_out_ref[0] = proj_ref[tl - WINDOW:tl, C_SV:C_SV + LANE]


def _mixer_prompt(x, mod, sinks, win, a2w, a2b, gnw, wo, lng, lnb, *, tl=512):
    b, l, d = x.shape
    nj = l // tl
    consts = _mixer_consts(tl, CHUNK, P_QB, P_KW)
    const2 = lambda i, j, s: (0, 0)
    const3 = lambda i, j, s: (0, 0, 0)
    grid_spec = pltpu.PrefetchScalarGridSpec(
        num_scalar_prefetch=1,
        grid=(b, nj),
        in_specs=[
            pl.BlockSpec((1, tl, d), lambda i, j, s: (i, j, 0)),
            pl.BlockSpec((1, 1, 6 * d), lambda i, j, s: (i, 0, 0)),
            pl.BlockSpec((d, PROJ_W), const2),
            pl.BlockSpec((LANE, GLA_KW), const2),
            pl.BlockSpec((1, GLA_KW), const2),
            pl.BlockSpec((1, GLA_DV), const2),
            pl.BlockSpec((d, d), const2),
            pl.BlockSpec((1, d), const2),
            pl.BlockSpec((1, d), const2),
            pl.BlockSpec((tl, tl), const2),
            pl.BlockSpec((SWA_KV_HEADS, SWA_GROUP * P_QB, P_KW), const3),
            pl.BlockSpec((GLA_HEADS * CHUNK, GLA_KW), const2),
            pl.BlockSpec((GLA_HEADS * CHUNK, GLA_VW), const2),
            pl.BlockSpec((GLA_VW, GLA_KW), const2),
            pl.BlockSpec((CHUNK, GLA_HEADS * CHUNK), const2),
        ],
        out_specs=[
            pl.BlockSpec((1, tl, d), lambda i, j, s: (i, j, 0)),
            pl.BlockSpec((1, GLA_HEADS, GLA_DK, GLA_DV), lambda i, j, s: (i, 0, 0, 0)),
            pl.BlockSpec((1, WINDOW, LANE), lambda i, j, s: (i, 0, 0)),
            pl.BlockSpec((1, WINDOW, LANE), lambda i, j, s: (i, 0, 0)),
        ],
        scratch_shapes=[
            pltpu.VMEM((tl, PROJ_W), F32),
            pltpu.VMEM((tl, GLA_KW), BF16),
            pltpu.VMEM((tl, GLA_KW), BF16),
            pltpu.VMEM((tl, GLA_KW), BF16),
            pltpu.VMEM((tl, GLA_KW), BF16),
            pltpu.VMEM((tl, GLA_KW), BF16),
            pltpu.VMEM((tl, GLA_KW), F32),
            pltpu.VMEM((tl, GLA_VW), BF16),
            pltpu.VMEM((WINDOW + tl, LANE), BF16),
            pltpu.VMEM((WINDOW + tl, LANE), BF16),
            pltpu.VMEM((tl, d), BF16),
            pltpu.VMEM((GLA_VW, GLA_KW), F32),
        ],
    )
    return pl.pallas_call(
        functools.partial(_mixer_prompt_kernel, tl=tl),
        out_shape=[
            jax.ShapeDtypeStruct((b, l, d), F32),
            jax.ShapeDtypeStruct((b, GLA_HEADS, GLA_DK, GLA_DV), F32),
            jax.ShapeDtypeStruct((b, WINDOW, LANE), F32),
            jax.ShapeDtypeStruct((b, WINDOW, LANE), F32),
        ],
        grid_spec=grid_spec,
        compiler_params=pltpu.CompilerParams(dimension_semantics=("arbitrary", "arbitrary"),
                                             vmem_limit_bytes=VMEM_LIMIT),
        name="mixer_prompt",
    )(sinks, x, mod, win, a2w, a2b, gnw, wo, lng, lnb, *consts)


def _mixer_sample_kernel(sinks_ref, x_ref, mod_ref, s0_ref, kc_ref, vc_ref, win_ref, a2w_ref, a2b_ref, gnw_ref,
                         wo_ref, lng_ref, lnb_ref, tri_ref, bias_ref, mkk_ref, mv_ref, ms_ref, caus_ref,
                         x1_ref, s_out_ref, k_out_ref, v_out_ref,
                         proj_ref, qe_ref, qn_ref, ke_ref, kn_ref, kw_ref, eb_ref, vb_ref, kbuf, vbuf, mixed_ref,
                         sbt_ref, xm_ref, *, nb, s):
    nkeys = WINDOW + s
    d = D_MODEL
    for bb in range(nb):
        m = mod_ref[bb]
        xm_ref[bb * s:(bb + 1) * s, :] = x_ref[bb] * (1.0 + m[:, d:2 * d]) + m[:, 0:d]
    loga = _in_proj(xm_ref[...].astype(BF16), win_ref, a2w_ref, a2b_ref, proj_ref)
    _gla_prep(loga, tri_ref, proj_ref, s, qe_ref, qn_ref, ke_ref, kn_ref, kw_ref, eb_ref, vb_ref)

    mkk = mkk_ref[...]
    mv = mv_ref[...]
    ms = ms_ref[...]
    caus = caus_ref[...]
    gnw = gnw_ref[...]

    for bb in range(nb):
        r0 = bb * s
        rows = slice(r0, r0 + s)
        sbt_ref[...] = jnp.zeros_like(sbt_ref)
        for hh in range(GLA_HEADS):
            sbt_ref[hh * GLA_DV:(hh + 1) * GLA_DV, hh * GLA_DK:(hh + 1) * GLA_DK] = s0_ref[bb, hh].T
        kbuf[0:WINDOW, :] = kc_ref[bb].astype(BF16)
        vbuf[0:WINDOW, :] = vc_ref[bb].astype(BF16)
        kbuf[WINDOW:nkeys, :] = proj_ref[rows, C_SK:C_SK + LANE].astype(BF16)
        vbuf[WINDOW:nkeys, :] = proj_ref[rows, C_SV:C_SV + LANE].astype(BF16)
        blocks = _swa_block(proj_ref[rows, C_SQ:C_SQ + SWA_W], kbuf[...], vbuf[...], bias_ref, sinks_ref, None, s)
        for i, blk in enumerate(blocks):
            mixed_ref[rows, GLA_VW + i * LANE:GLA_VW + (i + 1) * LANE] = blk
        o = _gla_chunk(r0, s, qe_ref, qn_ref, ke_ref, kn_ref, kw_ref, eb_ref, vb_ref, sbt_ref, mkk, mv, ms, caus)
        og = _gla_post(o, proj_ref[rows, C_GR:C_GR + GLA_VW], gnw)
        for hh in range(GLA_HEADS):
            mixed_ref[rows, hh * GLA_DV:(hh + 1) * GLA_DV] = og[hh]
            s_out_ref[bb, hh] = sbt_ref[hh * GLA_DV:(hh + 1) * GLA_DV, hh * GLA_DK:(hh + 1) * GLA_DK].T
        k_out_ref[bb, 0:WINDOW - s, :] = kc_ref[bb, s:WINDOW, :]
        v_out_ref[bb, 0:WINDOW - s, :] = vc_ref[bb, s:WINDOW, :]
        k_out_ref[bb, WINDOW - s:WINDOW, :] = proj_ref[rows, C_SK:C_SK + LANE]
        v_out_ref[bb, WINDOW - s:WINDOW, :] = proj_ref[rows, C_SV:C_SV + LANE]

    mix = _mm(mixed_ref[...], wo_ref[...])
    lng = lng_ref[...]
    lnb = lnb_ref[...]
    for bb in range(nb):
        m = mod_ref[bb]
        y = DEEPNORM_ALPHA * x_ref[bb] + m[:, 2 * d:3 * d] * mix[bb * s:(bb + 1) * s, :]
        x1_ref[bb] = _layer_norm(y, lng, lnb)


def _mixer_sample(x, mod, s0, kc, vc, sinks, win, a2w, a2b, gnw, wo, lng, lnb, *, nb=8):
    b, s, d = x.shape
    assert kc.shape[1] == WINDOW and s <= WINDOW
    rows = nb * s
    nkeys = WINDOW + s
    consts = _mixer_consts(rows, s, s, nkeys)
    const2 = lambda i, sk: (0, 0)
    const3 = lambda i, sk: (0, 0, 0)
    grid_spec = pltpu.PrefetchScalarGridSpec(
        num_scalar_prefetch=1,
        grid=(b // nb,),
        in_specs=[
            pl.BlockSpec((nb, s, d), lambda i, sk: (i, 0, 0)),
            pl.BlockSpec((nb, 1, 6 * d), lambda i, sk: (i, 0, 0)),
            pl.BlockSpec((nb, GLA_HEADS, GLA_DK, GLA_DV), lambda i, sk: (i, 0, 0, 0)),
            pl.BlockSpec((nb, WINDOW, LANE), lambda i, sk: (i, 0, 0)),
            pl.BlockSpec((nb, WINDOW, LANE), lambda i, sk: (i, 0, 0)),
            pl.BlockSpec((d, PROJ_W), const2),
            pl.BlockSpec((LANE, GLA_KW), const2),
            pl.BlockSpec((1, GLA_KW), const2),
            pl.BlockSpec((1, GLA_DV), const2),
            pl.BlockSpec((d, d), const2),
            pl.BlockSpec((1, d), const2),
            pl.BlockSpec((1, d), const2),
            pl.BlockSpec((rows, rows), const2),
            pl.BlockSpec((SWA_KV_HEADS, SWA_GROUP * s, nkeys), const3),
            pl.BlockSpec((GLA_HEADS * s, GLA_KW), const2),
            pl.BlockSpec((GLA_HEADS * s, GLA_VW), const2),
            pl.BlockSpec((GLA_VW, GLA_KW), const2),
            pl.BlockSpec((s, GLA_HEADS * s), const2),
        ],
        out_specs=[
            pl.BlockSpec((nb, s, d), lambda i, sk: (i, 0, 0)),
            pl.BlockSpec((nb, GLA_HEADS, GLA_DK, GLA_DV), lambda i, sk: (i, 0, 0, 0)),
            pl.BlockSpec((nb, WINDOW, LANE), lambda i, sk: (i, 0, 0)),
            pl.BlockSpec((nb, WINDOW, LANE), lambda i, sk: (i, 0, 0)),
        ],
        scratch_shapes=[
            pltpu.VMEM((rows, PROJ_W), F32),
            pltpu.VMEM((rows, GLA_KW), BF16),
            pltpu.VMEM((rows, GLA_KW), BF16),
            pltpu.VMEM((rows, GLA_KW), BF16),
            pltpu.VMEM((rows, GLA_KW), BF16),
            pltpu.VMEM((rows, GLA_KW), BF16),
            pltpu.VMEM((rows, GLA_KW), F32),
            pltpu.VMEM((rows, GLA_VW), BF16),
            pltpu.VMEM((nkeys, LANE), BF16),
            pltpu.VMEM((nkeys, LANE), BF16),
            pltpu.VMEM((rows, d), BF16),
            pltpu.VMEM((GLA_VW, GLA_KW), F32),
            pltpu.VMEM((rows, d), F32),
        ],
    )
    return pl.pallas_call(
        functools.partial(_mixer_sample_kernel, nb=nb, s=s),
        out_shape=[
            jax.ShapeDtypeStruct((b, s, d), F32),
            jax.ShapeDtypeStruct((b, GLA_HEADS, GLA_DK, GLA_DV), F32),
            jax.ShapeDtypeStruct((b, WINDOW, LANE), F32),
            jax.ShapeDtypeStruct((b, WINDOW, LANE), F32),
        ],
        grid_spec=grid_spec,
        compiler_params=pltpu.CompilerParams(dimension_semantics=("arbitrary",),
                                             vmem_limit_bytes=VMEM_LIMIT),
        name="mixer_sample",
    )(sinks, x, mod, s0, kc, vc, win, a2w, a2b, gnw, wo, lng, lnb, *consts)


MOE_BLK = 128


def _route_t(logits_t):
    t = logits_t.shape[1]
    row = lax.broadcasted_iota(jnp.int32, (EPG, t), 0).astype(F32)
    big = 99.0
    gl = jnp.where(row < N_GROUPS, logits_t[0:EPG, :], -jnp.inf)
    gmax = jnp.max(gl, axis=0, keepdims=True)
    grp = jnp.min(jnp.where(gl == gmax, row, big), axis=0, keepdims=True)
    p_grp = 1.0 / jnp.sum(jnp.exp(gl - gmax), axis=0, keepdims=True)
    el = jnp.zeros((EPG, t), F32)
    for g in range(N_GROUPS):
        el = el + jnp.where(grp == float(g), logits_t[R_EXP0 + EPG * g:R_EXP0 + EPG * (g + 1), :], 0.0)
    v1 = jnp.max(el, axis=0, keepdims=True)
    i1 = jnp.min(jnp.where(el == v1, row, big), axis=0, keepdims=True)
    el2 = jnp.where(row == i1, -jnp.inf, el)
    v2 = jnp.max(el2, axis=0, keepdims=True)
    i2 = jnp.min(jnp.where(el2 == v2, row, big), axis=0, keepdims=True)
    e2 = jnp.exp(v2 - v1)
    w1 = p_grp / (1.0 + e2)
    w2 = p_grp * e2 / (1.0 + e2)
    cw = jnp.where(row == i1, w1, 0.0) + jnp.where(row == i2, w2, 0.0)
    return grp, cw


MOE_PC = 256


def _moe_sort_kernel(x1_ref, mod_ref, wrt_ref, brt_ref, xs_ref, cws_ref, pos_ref, cnt_ref, *, nb, r):
    tm = nb * r
    tmp = tm + N_GROUPS * MOE_BLK
    d = D_MODEL
    mod = mod_ref[...]
    t3 = x1_ref[...] * (1.0 + mod[:, :, 4 * d:5 * d]) + mod[:, :, 3 * d:4 * d]
    t = t3.reshape(tm, d).astype(BF16)
    grp, cw = _route_t(_mm_nt(wrt_ref[...], t) + brt_ref[...])
    row = lax.broadcasted_iota(jnp.int32, (EPG, tm), 0).astype(F32)
    onehot_g = jnp.where(row == grp, 1.0, 0.0)
    nch = tm // LANE
    strict = (lax.broadcasted_iota(jnp.int32, (LANE, LANE), 0)
              < lax.broadcasted_iota(jnp.int32, (LANE, LANE), 1)).astype(F32).astype(BF16)
    stacked = jnp.concatenate([onehot_g[:, c * LANE:(c + 1) * LANE] for c in range(nch)], axis=0)
    pref = _mm(stacked.astype(BF16), strict)
    tot = jnp.sum(stacked, axis=1, keepdims=True)
    cnt = jnp.zeros((EPG, 1), F32)
    ranks = []
    for c in range(nch):
        ranks.append(pref[c * EPG:(c + 1) * EPG, :] + cnt)
        cnt = cnt + tot[c * EPG:(c + 1) * EPG, :]
    rank = jnp.concatenate(ranks, axis=1)
    padded = jnp.floor((cnt + (MOE_BLK - 1)) * (1.0 / MOE_BLK)) * MOE_BLK
    rowc = lax.broadcasted_iota(jnp.int32, (EPG, 1), 0)
    off = jnp.zeros((EPG, 1), F32)
    for gg in range(N_GROUPS - 1):
        off = off + jnp.where(rowc > gg, padded[gg:gg + 1, :], 0.0)
    pos = jnp.sum(onehot_g * (off + rank), axis=0, keepdims=True)
    pos_ref[0] = jnp.broadcast_to(pos, (EPG, tm))
    cnt_ref[0] = jnp.broadcast_to(cnt, (EPG, LANE))
    cw_hi, cw_lo = _split_bf16(jnp.concatenate([cw, jnp.zeros((LANE - EPG, tm), F32)], axis=0).T)
    t_aug = jnp.concatenate([t, cw_hi, cw_lo], axis=1)
    for c in range(tmp // MOE_PC):
        slot = (lax.broadcasted_iota(jnp.int32, (MOE_PC, tm), 0) + c * MOE_PC).astype(F32)
        perm = jnp.where(slot == pos, 1.0, 0.0).astype(BF16)
        moved = _mm(perm, t_aug)
        xs_ref[c * MOE_PC:(c + 1) * MOE_PC, :] = moved[:, 0:d].astype(BF16)
        cws_ref[c * MOE_PC:(c + 1) * MOE_PC, :] = moved[:, d:d + LANE] + moved[:, d + LANE:d + 2 * LANE]


MOE_BIG = 2 * MOE_BLK


def _moe_expert_kernel(boff_ref, nblk_ref, nused_ref, xs_ref, cws_ref, wg_ref, wu_ref, wd_ref, ys_hbm,
                       ybuf, ytail, zbuf, sem, *, ntiles, tmp):
    q = pl.program_id(0)
    grp = q // ntiles
    tile = q - grp * ntiles
    off = boff_ref[q]
    n = nblk_ref[q]
    n2 = lax.shift_right_logical(n, 1)
    tail = jnp.bitwise_and(n, 1)
    base = tile * tmp
    wd = wd_ref[0].reshape(EPG * EXPERT_FF, D_MODEL)

    def experts(rows):
        xb = xs_ref[rows, :]
        cwb = cws_ref[rows, :]
        hs = []
        for e in range(EPG):
            gg_ = _mm(xb, wg_ref[0, e])
            uu = _mm(xb, wu_ref[0, e])
            hs.append((gg_ * _sigmoid(gg_) * uu * cwb[:, e:e + 1]).astype(BF16))
        return _mm(jnp.concatenate(hs, axis=1), wd).astype(BF16)

    def big_copy(slot, blk):
        return pltpu.make_async_copy(
            ybuf.at[slot], ys_hbm.at[pl.ds(pl.multiple_of(base + blk * MOE_BLK, MOE_BLK), MOE_BIG), :], sem.at[slot])

    def tail_copy(blk):
        return pltpu.make_async_copy(
            ytail, ys_hbm.at[pl.ds(pl.multiple_of(base + blk * MOE_BLK, MOE_BLK), MOE_BLK), :], sem.at[2])

    def zero_copy(blk):
        return pltpu.make_async_copy(
            zbuf, ys_hbm.at[pl.ds(pl.multiple_of(base + blk * MOE_BLK, MOE_BLK), MOE_BLK), :], sem.at[3])

    def body(k, carry):
        slot = jnp.bitwise_and(k, 1)
        blk = off + 2 * k
        y = experts(pl.ds(pl.multiple_of(blk * MOE_BLK, MOE_BLK), MOE_BIG))

        @pl.when(k >= 2)
        def _():
            big_copy(slot, blk - 4).wait()

        ybuf[slot] = y
        big_copy(slot, blk).start()
        return carry

    lax.fori_loop(0, n2, body, 0)

    @pl.when(tail == 1)
    def _():
        blk = off + 2 * n2
        ytail[...] = experts(pl.ds(pl.multiple_of(blk * MOE_BLK, MOE_BLK), MOE_BLK))
        tail_copy(blk).start()

    @pl.when(n2 >= 2)
    def _():
        big_copy(jnp.bitwise_and(n2, 1), off + 2 * (n2 - 2)).wait()

    @pl.when(n2 >= 1)
    def _():
        big_copy(jnp.bitwise_and(n2 - 1, 1), off + 2 * (n2 - 1)).wait()

    @pl.when(tail == 1)
    def _():
        tail_copy(off + 2 * n2).wait()

    @pl.when(grp == N_GROUPS - 1)
    def _():
        zbuf[...] = jnp.zeros_like(zbuf)

        def zfill(blk, carry):
            zero_copy(blk).start()
            zero_copy(blk).wait()
            return carry

        lax.fori_loop(nused_ref[tile], tmp // MOE_BLK, zfill, 0)


def _moe_unsort_kernel(ys_ref, pos_ref, x1_ref, mod_ref, lng_ref, lnb_ref, out_ref, *, nb, r):
    tm = nb * r
    tmp = tm + N_GROUPS * MOE_BLK
    d = D_MODEL
    ysb = ys_ref[...]
    posc = jnp.broadcast_to(pos_ref[0][0:1, :], (LANE, tm)).T
    lng = lng_ref[...]
    lnb = lnb_ref[...]
    for c in range(tm // MOE_PC):
        slot = lax.broadcasted_iota(jnp.int32, (MOE_PC, tmp), 1).astype(F32)
        unperm = jnp.where(slot == posc[c * MOE_PC:(c + 1) * MOE_PC, 0:1], 1.0, 0.0).astype(BF16)
        y = _mm(unperm, ysb)
        if nb == 1:
            x1c = x1_ref[0, c * MOE_PC:(c + 1) * MOE_PC, :]
            gt2 = mod_ref[0][:, 5 * d:6 * d]
            out_ref[0, c * MOE_PC:(c + 1) * MOE_PC, :] = _layer_norm(DEEPNORM_ALPHA * x1c + gt2 * y, lng, lnb)
        else:
            cb = MOE_PC // r
            x1c = x1_ref[c * cb:(c + 1) * cb]
            gt2 = mod_ref[c * cb:(c + 1) * cb][:, :, 5 * d:6 * d]
            yy = DEEPNORM_ALPHA * x1c + gt2 * y.reshape(cb, r, d)
            out_ref[c * cb:(c + 1) * cb] = _layer_norm(yy, lng, lnb)


def _moe(x1, mod, wrt, brt, wg, wu, wd, lng, lnb, *, nb, r):
    b, l, d = x1.shape
    tpb = l // r
    ntiles = (b // nb) * tpb
    tm = nb * r
    tmp = tm + N_GROUPS * MOE_BLK
    bpt = tmp // MOE_BLK
    xmap = lambda i: (i // tpb, i % tpb, 0)
    mmap = lambda i: (i // tpb, 0, 0)
    const2 = lambda i: (0, 0)
    params = pltpu.CompilerParams(dimension_semantics=("arbitrary",), vmem_limit_bytes=VMEM_LIMIT)

    xs, cws, pos, cnt = pl.pallas_call(
        functools.partial(_moe_sort_kernel, nb=nb, r=r),
        out_shape=[jax.ShapeDtypeStruct((ntiles * tmp, d), BF16),
                   jax.ShapeDtypeStruct((ntiles * tmp, LANE), F32),
                   jax.ShapeDtypeStruct((ntiles, EPG, tm), F32),
                   jax.ShapeDtypeStruct((ntiles, EPG, LANE), F32)],
        grid=(ntiles,),
        in_specs=[
            pl.BlockSpec((nb, r, d), xmap),
            pl.BlockSpec((nb, 1, 6 * d), mmap),
            pl.BlockSpec((LANE, d), const2),
            pl.BlockSpec((LANE, 1), const2),
        ],
        out_specs=[pl.BlockSpec((tmp, d), lambda i: (i, 0)),
                   pl.BlockSpec((tmp, LANE), lambda i: (i, 0)),
                   pl.BlockSpec((1, EPG, tm), lambda i: (i, 0, 0)),
                   pl.BlockSpec((1, EPG, LANE), lambda i: (i, 0, 0))],
        compiler_params=params,
        name="moe_sort",
    )(x1, mod, wrt, brt)

    nblk = ((cnt[:, :N_GROUPS, 0].astype(jnp.int32) + (MOE_BLK - 1)) // MOE_BLK)
    boff = jnp.cumsum(nblk, axis=1) - nblk
    nused = jnp.sum(nblk, axis=1).astype(jnp.int32)
    nblk_q = nblk.T.reshape(-1).astype(jnp.int32)
    boff_q = boff.T.reshape(-1).astype(jnp.int32)

    tmap = lambda q, bo, nk, nu: (q % ntiles, 0)
    wmap = lambda q, bo, nk, nu: (q // ntiles, 0, 0, 0)
    ys = pl.pallas_call(
        functools.partial(_moe_expert_kernel, ntiles=ntiles, tmp=tmp),
        out_shape=jax.ShapeDtypeStruct((ntiles * tmp, d), BF16),
        grid_spec=pltpu.PrefetchScalarGridSpec(
            num_scalar_prefetch=3,
            grid=(N_GROUPS * ntiles,),
            in_specs=[
                pl.BlockSpec((tmp, d), tmap),
                pl.BlockSpec((tmp, LANE), tmap),
                pl.BlockSpec((1, EPG, d, EXPERT_FF), wmap),
                pl.BlockSpec((1, EPG, d, EXPERT_FF), wmap),
                pl.BlockSpec((1, EPG, EXPERT_FF, d), wmap),
            ],
            out_specs=pl.BlockSpec(memory_space=pl.ANY),
            scratch_shapes=[
                pltpu.VMEM((2, MOE_BIG, d), BF16),
                pltpu.VMEM((MOE_BLK, d), BF16),
                pltpu.VMEM((MOE_BLK, d), BF16),
                pltpu.SemaphoreType.DMA((4,)),
            ],
        ),
        compiler_params=params,
        name="moe_experts",
    )(boff_q, nblk_q, nused, xs, cws, wg, wu, wd)

    return pl.pallas_call(
        functools.partial(_moe_unsort_kernel, nb=nb, r=r),
        out_shape=jax.ShapeDtypeStruct((b, l, d), F32),
        grid=(ntiles,),
        in_specs=[
            pl.BlockSpec((tmp, d), lambda i: (i, 0)),
            pl.BlockSpec((1, EPG, tm), lambda i: (i, 0, 0)),
            pl.BlockSpec((nb, r, d), xmap),
            pl.BlockSpec((nb, 1, 6 * d), mmap),
            pl.BlockSpec((1, d), const2),
            pl.BlockSpec((1, d), const2),
        ],
        out_specs=pl.BlockSpec((nb, r, d), xmap),
        compiler_params=params,
        name="moe_unsort",
    )(ys, pos, x1, mod, lng, lnb)


def kernel(x_prompt, x_sample, c_prompt, c_sample, state_gla, cache_swa_k, cache_swa_v, ada_w, ada_b, w_in,
           gla_a2_w, gla_a2_b, gla_norm_w, swa_sinks, w_o, ln1_g, ln1_b, router_g_w, router_g_b, router_e_w,
           router_e_b, moe_w_gate, moe_w_up, moe_w_down, ln2_g, ln2_b):
    assert ada_w.shape[0] == 1
    bp = x_prompt.shape[0]
    bs, ss, d = x_sample.shape
    lc = cache_swa_k.shape[2]

    w = w_in[0]
    zpad = jnp.zeros((d, LANE - GLA_RANK), F32)
    w_sq = w[:, 1552:2064].reshape(d, SWA_KV_HEADS, SWA_GROUP, SWA_DH).transpose(0, 2, 1, 3).reshape(d, SWA_W)
    win = jnp.concatenate([w[:, 0:1536], w_sq, w[:, 2064:2320], w[:, 1536:1552], zpad], axis=1).astype(BF16)
    a2w = jnp.concatenate([gla_a2_w[0], jnp.zeros((LANE - GLA_RANK, GLA_KW), F32)], axis=0).astype(BF16)
    a2b = gla_a2_b[0].reshape(1, GLA_KW)
    gnw = gla_norm_w[0].reshape(1, GLA_DV)
    wo_swa = w_o[0][GLA_VW:].reshape(SWA_KV_HEADS, SWA_GROUP, SWA_DH, d).transpose(1, 0, 2, 3).reshape(SWA_W, d)
    wo = jnp.concatenate([w_o[0][:GLA_VW], wo_swa], axis=0).astype(BF16)
    sinks = swa_sinks[0]
    wrt = jnp.concatenate([router_g_w[0], jnp.zeros((d, R_EXP0 - N_GROUPS), F32),
                           jnp.transpose(router_e_w[0], (1, 0, 2)).reshape(d, N_GROUPS * EPG),
                           jnp.zeros((d, LANE - R_EXP0 - N_GROUPS * EPG), F32)], axis=1).T.astype(BF16)
    brt = jnp.concatenate([router_g_b[0], jnp.zeros((R_EXP0 - N_GROUPS,), F32), router_e_b[0].reshape(-1),
                           jnp.zeros((LANE - R_EXP0 - N_GROUPS * EPG,), F32)]).reshape(LANE, 1)
    wg = moe_w_gate[0].astype(BF16)
    wu = moe_w_up[0].astype(BF16)
    wd = moe_w_down[0].astype(BF16)
    lng1, lnb1 = ln1_g[0].reshape(1, d), ln1_b[0].reshape(1, d)
    lng2, lnb2 = ln2_g[0].reshape(1, d), ln2_b[0].reshape(1, d)

    mod = _adaln(jnp.concatenate([c_prompt, c_sample], axis=0), ada_w[0], ada_b[0].reshape(1, 6 * d))
    mod = mod.reshape(bp + bs, 1, 6 * d)
    mod_p, mod_s = mod[:bp], mod[bp:]

    x1p, s_p, k_p, v_p = _mixer_prompt(x_prompt, mod_p, sinks, win, a2w, a2b, gnw, wo, lng1, lnb1)
    x1s, s_s, k_s, v_s = _mixer_sample(
        x_sample, mod_s, state_gla[0], cache_swa_k[0].reshape(bs, lc, LANE), cache_swa_v[0].reshape(bs, lc, LANE),
        sinks, win, a2w, a2b, gnw, wo, lng1, lnb1)

    yp = _moe(x1p, mod_p, wrt, brt, wg, wu, wd, lng2, lnb2, nb=1, r=1024)
    ys = _moe(x1s, mod_s, wrt, brt, wg, wu, wd, lng2, lnb2, nb=bs, r=ss)

    kv_shape_p = (1, bp, WINDOW, SWA_KV_HEADS, SWA_DH)
    kv_shape_s = (1, bs, lc, SWA_KV_HEADS, SWA_DH)
    return (yp, ys, s_p[None], k_p.reshape(kv_shape_p), v_p.reshape(kv_shape_p),
            s_s[None], k_s.reshape(kv_shape_s), v_s.reshape(kv_shape_s))
```

```python
import functools

import jax
import jax.numpy as jnp
import numpy as np
from jax import lax
from jax.experimental import pallas as pl
from jax.experimental.pallas import tpu as pltpu

F32 = jnp.float32
BF16 = jnp.bfloat16

D_MODEL = 1024
CHUNK = 64
GLA_HEADS = 4
GLA_DK = 64
GLA_DV = 128
GLA_KW = GLA_HEADS * GLA_DK
GLA_VW = GLA_HEADS * GLA_DV
GLA_RANK = 16
GLA_TAU = 16.0
SWA_Q_HEADS = 8
SWA_KV_HEADS = 2
SWA_GROUP = 4
SWA_DH = 64
SWA_W = SWA_Q_HEADS * SWA_DH
WINDOW = 128
N_GROUPS = 4
EPG = 8
EXPERT_FF = 256
DEEPNORM_ALPHA = 2.0 ** 0.25
LN_EPS = 1e-5
NEG_INF = -1e30

C_GQ, C_GK, C_GV, C_GR, C_SQ, C_SK, C_SV, C_GA = 0, 256, 512, 1024, 1536, 2048, 2176, 2304
PROJ_W = 2432
LANE = 128
R_EXP0 = 8

VMEM_LIMIT = 56 * 1024 * 1024


def _mm(a, b):
    return jnp.dot(a, b, preferred_element_type=F32)


def _mm_nt(a, b):
    return lax.dot_general(a, b, (((1,), (1,)), ((), ())), preferred_element_type=F32)


def _mm_tn(a, b):
    return lax.dot_general(a, b, (((0,), (0,)), ((), ())), preferred_element_type=F32)


def _split_bf16(a):
    hi = a.astype(BF16)
    lo = (a - hi.astype(F32)).astype(BF16)
    return hi, lo


def _sigmoid(x):
    return 1.0 / (1.0 + jnp.exp(-x))


def _layer_norm(y, g, b):
    mu = jnp.mean(y, axis=-1, keepdims=True)
    d = y - mu
    var = jnp.mean(d * d, axis=-1, keepdims=True)
    return d * lax.rsqrt(var + LN_EPS) * g + b


def _adaln_kernel(c_ref, w_ref, b_ref, o_ref):
    c = c_ref[...]
    a = c * _sigmoid(c)
    a_hi, a_lo = _split_bf16(a)
    w_hi, w_lo = _split_bf16(w_ref[...])
    o_ref[...] = _mm(a_hi, w_hi) + (_mm(a_hi, w_lo) + _mm(a_lo, w_hi)) + b_ref[...]


def _adaln(c_all, ada_w, ada_b):
    n = c_all.shape[0]
    bn = 1024
    return pl.pallas_call(
        _adaln_kernel,
        out_shape=jax.ShapeDtypeStruct((n, 6 * D_MODEL), F32),
        grid=(6 * D_MODEL // bn,),
        in_specs=[pl.BlockSpec((n, D_MODEL), lambda j: (0, 0)),
                  pl.BlockSpec((D_MODEL, bn), lambda j: (0, j)),
                  pl.BlockSpec((1, bn), lambda j: (0, j))],
        out_specs=pl.BlockSpec((n, bn), lambda j: (0, j)),
        compiler_params=pltpu.CompilerParams(dimension_semantics=("arbitrary",), vmem_limit_bytes=VMEM_LIMIT),
        name="adaln",
    )(c_all, ada_w, ada_b)


def _mixer_consts(nrows, chunk, qb, kw):
    r = np.arange(nrows)
    tri = ((r[:, None] // chunk == r[None, :] // chunk) & (r[:, None] >= r[None, :])).astype(np.float32)
    hs = np.arange(GLA_HEADS * chunk)
    mkk = (hs[:, None] // chunk == np.arange(GLA_KW)[None, :] // GLA_DK).astype(np.float32)
    mv = (hs[:, None] // chunk == np.arange(GLA_VW)[None, :] // GLA_DV).astype(np.float32)
    ms = (np.arange(GLA_VW)[:, None] // GLA_DV == np.arange(GLA_KW)[None, :] // GLA_DK).astype(np.float32)
    caus = (np.arange(chunk)[:, None] >= (hs[None, :] % chunk)).astype(np.float32)
    t = np.arange(qb)
    kj = np.arange(kw)
    cs = (t // chunk) * chunk
    kpos = kj[None, :] - WINDOW
    vis = (kpos >= cs[:, None] - WINDOW) & (kpos < cs[:, None] + chunk)
    dist = np.abs(t[:, None] + WINDOW - kj[None, :]).astype(np.float32)
    bias = np.zeros((SWA_KV_HEADS, SWA_GROUP * qb, kw), np.float32)
    for hk in range(SWA_KV_HEADS):
        for g in range(SWA_GROUP):
            slope = np.float32(2.0 ** (-(hk * SWA_GROUP + g + 1)))
            bias[hk, g * qb:(g + 1) * qb] = np.where(vis, -slope * dist, np.float32(2.0 * NEG_INF))
    return (jnp.asarray(tri, BF16), jnp.asarray(bias), jnp.asarray(mkk, BF16), jnp.asarray(mv, BF16),
            jnp.asarray(ms), jnp.asarray(caus))


def _in_proj(h, win_ref, a2w_ref, a2b_ref, proj_ref):
    proj_ref[...] = _mm(h, win_ref[...])
    ga = proj_ref[:, C_GA:C_GA + LANE].astype(BF16)
    z = _mm(ga, a2w_ref[...]) + a2b_ref[...]
    lsig = -(jnp.maximum(-z, 0.0) + jnp.log(1.0 + jnp.exp(-jnp.abs(z))))
    return lsig * (1.0 / GLA_TAU)


def _gla_prep(loga, tri_ref, proj_ref, chunk, qe_ref, qn_ref, ke_ref, kn_ref, kw_ref, eb_ref, vb_ref):
    nrows = loga.shape[0]
    la_hi, la_lo = _split_bf16(loga)
    tri = tri_ref[...]
    b = _mm(tri, la_hi) + _mm(tri, la_lo)
    b_end = jnp.concatenate(
        [jnp.broadcast_to(b[c * chunk + chunk - 1:(c + 1) * chunk, :], (chunk, GLA_KW))
         for c in range(nrows // chunk)], axis=0)
    eb = jnp.exp(b)
    ebn = jnp.exp(-b)
    wk = jnp.exp(b_end - b)
    q = proj_ref[:, C_GQ:C_GQ + GLA_KW] * (GLA_DK ** -0.5)
    k = proj_ref[:, C_GK:C_GK + GLA_KW]
    eb_ref[...] = eb
    qe_ref[...] = (q * eb).astype(BF16)
    qn_ref[...] = (q * ebn).astype(BF16)
    ke_ref[...] = (k * eb).astype(BF16)
    kn_ref[...] = (k * ebn).astype(BF16)
    kw_ref[...] = (k * wk).astype(BF16)
    vb_ref[...] = proj_ref[:, C_GV:C_GV + GLA_VW].astype(BF16)


def _gla_chunk(r0, chunk, qe_ref, qn_ref, ke_ref, kn_ref, kw_ref, eb_ref, vb_ref, sbt_ref, mkk, mv, ms, caus):
    rows = slice(r0, r0 + chunk)
    qe = qe_ref[rows, :]
    qn = qn_ref[rows, :]
    zero = jnp.zeros((), BF16)
    kn4 = jnp.where(mkk != 0, jnp.concatenate([kn_ref[rows, :]] * GLA_HEADS, axis=0), zero)
    ke4 = jnp.where(mkk != 0, jnp.concatenate([ke_ref[rows, :]] * GLA_HEADS, axis=0), zero)
    a_lo = _mm_nt(qe, kn4)
    a_up = _mm_nt(qn, ke4)
    a = jnp.where(caus != 0.0, a_lo, a_up).astype(BF16)
    v = vb_ref[rows, :]
    v4 = jnp.where(mv != 0, jnp.concatenate([v] * GLA_HEADS, axis=0), zero)
    sbt = sbt_ref[...]
    o = _mm(a, v4) + _mm_nt(qe, sbt.astype(BF16))
    ut = _mm_tn(v, kw_ref[rows, :])
    g_end = eb_ref[r0 + chunk - 1:r0 + chunk, :]
    sbt_ref[...] = g_end * sbt + jnp.where(ms != 0.0, ut, 0.0)
    return o


def _gla_post(o, gr, gnw):
    res = []
    for h in range(GLA_HEADS):
        oh = o[:, h * GLA_DV:(h + 1) * GLA_DV]
        ms_ = jnp.mean(oh * oh, axis=-1, keepdims=True)
        og = oh * lax.rsqrt(ms_ + LN_EPS) * gnw
        r = gr[:, h * GLA_DV:(h + 1) * GLA_DV]
        res.append((og * (r * _sigmoid(r))).astype(BF16))
    return res


def _swa_problem(qs, kwin, vwin, bias, sinks_ref, hk, extra_valid, qb):
    s = _mm_nt(qs, kwin)
    ok = bias > -1e29
    if extra_valid is not None:
        ok = ok & extra_valid
    s = jnp.where(ok, s + bias, NEG_INF)
    ps, denoms = [], []
    for g in range(SWA_GROUP):
        sg = s[g * qb:(g + 1) * qb, :]
        sink = sinks_ref[hk * SWA_GROUP + g]
        m = jnp.maximum(jnp.max(sg, axis=-1, keepdims=True), sink)
        pg = jnp.exp(sg - m)
        denoms.append(jnp.sum(pg, axis=-1, keepdims=True) + jnp.exp(sink - m))
        ps.append(pg.astype(BF16))
    o = _mm(jnp.concatenate(ps, axis=0), vwin)
    return o / jnp.concatenate(denoms, axis=0)


def _swa_block(sq, kwin, vwin, bias_ref, sinks_ref, extra_valid, qb):
    low = lax.broadcasted_iota(jnp.int32, (qb, LANE), 1) < SWA_DH
    outs = []
    for hk in range(SWA_KV_HEADS):
        keep = low if hk == 0 else jnp.logical_not(low)
        qs = jnp.concatenate([jnp.where(keep, sq[:, g * LANE:(g + 1) * LANE] * (SWA_DH ** -0.5), 0.0)
                              for g in range(SWA_GROUP)], axis=0).astype(BF16)
        outs.append(_swa_problem(qs, kwin, vwin, bias_ref[hk], sinks_ref, hk, extra_valid, qb))
    return [jnp.where(low, outs[0][g * qb:(g + 1) * qb, :], outs[1][g * qb:(g + 1) * qb, :]).astype(BF16)
            for g in range(SWA_GROUP)]


def _out_proj_ln(x, gt1, mixed_ref, wo_ref, ln_g, ln_b):
    mix = _mm(mixed_ref[...], wo_ref[...])
    return _layer_norm(DEEPNORM_ALPHA * x + gt1 * mix, ln_g, ln_b)


P_QB = 128
P_KW = WINDOW + P_QB


def _mixer_prompt_kernel(sinks_ref, x_ref, mod_ref, win_ref, a2w_ref, a2b_ref, gnw_ref, wo_ref, lng_ref, lnb_ref,
                         tri_ref, bias_ref, mkk_ref, mv_ref, ms_ref, caus_ref,
                         x1_ref, s_out_ref, k_out_ref, v_out_ref,
                         proj_ref, qe_ref, qn_ref, ke_ref, kn_ref, kw_ref, eb_ref, vb_ref, kbuf, vbuf, mixed_ref,
                         sbt_ref, *, tl):
    j = pl.program_id(1)
    nj = pl.num_programs(1)
    d = D_MODEL

    @pl.when(j == 0)
    def _():
        sbt_ref[...] = jnp.zeros_like(sbt_ref)
        kbuf[0:WINDOW, :] = jnp.zeros((WINDOW, LANE), BF16)
        vbuf[0:WINDOW, :] = jnp.zeros((WINDOW, LANE), BF16)

    x = x_ref[0]
    mod = mod_ref[0]
    h = (x * (1.0 + mod[:, d:2 * d]) + mod[:, 0:d]).astype(BF16)
    loga = _in_proj(h, win_ref, a2w_ref, a2b_ref, proj_ref)
    _gla_prep(loga, tri_ref, proj_ref, CHUNK, qe_ref, qn_ref, ke_ref, kn_ref, kw_ref, eb_ref, vb_ref)
    kbuf[WINDOW:WINDOW + tl, :] = proj_ref[:, C_SK:C_SK + LANE].astype(BF16)
    vbuf[WINDOW:WINDOW + tl, :] = proj_ref[:, C_SV:C_SV + LANE].astype(BF16)

    mkk = mkk_ref[...]
    mv = mv_ref[...]
    ms = ms_ref[...]
    caus = caus_ref[...]
    gnw = gnw_ref[...]
    kj = lax.broadcasted_iota(jnp.int32, (SWA_GROUP * P_QB, P_KW), 1)
    first_valid = kj >= jnp.where(j > 0, 0, WINDOW)

    for p in range(tl // P_QB):
        q0 = p * P_QB
        blocks = _swa_block(proj_ref[q0:q0 + P_QB, C_SQ:C_SQ + SWA_W], kbuf[q0:q0 + P_KW, :], vbuf[q0:q0 + P_KW, :],
                            bias_ref, sinks_ref, first_valid if p == 0 else None, P_QB)
        for i, blk in enumerate(blocks):
            mixed_ref[q0:q0 + P_QB, GLA_VW + i * LANE:GLA_VW + (i + 1) * LANE] = blk
        for c in range(P_QB // CHUNK):
            r0 = q0 + c * CHUNK
            o = _gla_chunk(r0, CHUNK, qe_ref, qn_ref, ke_ref, kn_ref, kw_ref, eb_ref, vb_ref, sbt_ref,
                           mkk, mv, ms, caus)
            og = _gla_post(o, proj_ref[r0:r0 + CHUNK, C_GR:C_GR + GLA_VW], gnw)
            for hh in range(GLA_HEADS):
                mixed_ref[r0:r0 + CHUNK, hh * GLA_DV:(hh + 1) * GLA_DV] = og[hh]

    x1_ref[0] = _out_proj_ln(x, mod[:, 2 * d:3 * d], mixed_ref, wo_ref, lng_ref[...], lnb_ref[...])

    kbuf[0:WINDOW, :] = kbuf[tl:tl + WINDOW, :]
    vbuf[0:WINDOW, :] = vbuf[tl:tl + WINDOW, :]

    @pl.when(j == nj - 1)
    def _():
        for hh in range(GLA_HEADS):
            s_out_ref[0, hh] = sbt_ref[hh * GLA_DV:(hh + 1) * GLA_DV, hh * GLA_DK:(hh + 1) * GLA_DK].T
        k_out_ref[0] = proj_ref[tl - WINDOW:tl, C_SK:C_SK + LANE]
        v_out_ref[0] = proj_ref[tl - WINDOW:tl, C_SV:C_SV + LANE]


def _mixer_prompt(x, mod, sinks, win, a2w, a2b, gnw, wo, lng, lnb, *, tl=512):
    b, l, d = x.shape
    nj = l // tl
    consts = _mixer_consts(tl, CHUNK, P_QB, P_KW)
    const2 = lambda i, j, s: (0, 0)
    const3 = lambda i, j, s: (0, 0, 0)
    grid_spec = pltpu.PrefetchScalarGridSpec(
        num_scalar_prefetch=1,
        grid=(b, nj),
        in_specs=[
            pl.BlockSpec((1, tl, d), lambda i, j, s: (i, j, 0)),
            pl.BlockSpec((1, 1, 6 * d), lambda i, j, s: (i, 0, 0)),
            pl.BlockSpec((d, PROJ_W), const2),
            pl.BlockSpec((LANE, GLA_KW), const2),
            pl.BlockSpec((1, GLA_KW), const2),
            pl.BlockSpec((1, GLA_DV), const2),
            pl.BlockSpec((d, d), const2),
            pl.BlockSpec((1, d), const2),
            pl.BlockSpec((1, d), const2),
            pl.BlockSpec((tl, tl), const2),
            pl.BlockSpec((SWA_KV_HEADS, SWA_GROUP * P_QB, P_KW), const3),
            pl.BlockSpec((GLA_HEADS * CHUNK, GLA_KW), const2),
            pl.BlockSpec((GLA_HEADS * CHUNK, GLA_VW), const2),
            pl.BlockSpec((GLA_VW, GLA_KW), const2),
            pl.BlockSpec((CHUNK, GLA_HEADS * CHUNK), const2),
        ],
        out_specs=[
            pl.BlockSpec((1, tl, d), lambda i, j, s: (i, j, 0)),
            pl.BlockSpec((1, GLA_HEADS, GLA_DK, GLA_DV), lambda i, j, s: (i, 0, 0, 0)),
            pl.BlockSpec((1, WINDOW, LANE), lambda i, j, s: (i, 0, 0)),
            pl.BlockSpec((1, WINDOW, LANE), lambda i, j, s: (i, 0, 0)),
        ],
        scratch_shapes=[
            pltpu.VMEM((tl, PROJ_W), F32),
            pltpu.VMEM((tl, GLA_KW), BF16),
            pltpu.VMEM((tl, GLA_KW), BF16),
            pltpu.VMEM((tl, GLA_KW), BF16),
            pltpu.VMEM((tl, GLA_KW), BF16),
            pltpu.VMEM((tl, GLA_KW), BF16),
            pltpu.VMEM((tl, GLA_KW), F32),
            pltpu.VMEM((tl, GLA_VW), BF16),
            pltpu.VMEM((WINDOW + tl, LANE), BF16),
            pltpu.VMEM((WINDOW + tl, LANE), BF16),
            pltpu.VMEM((tl, d), BF16),
            pltpu.VMEM((GLA_VW, GLA_KW), F32),
        ],
    )
    return pl.pallas_call(
        functools.partial(_mixer_prompt_kernel, tl=tl),
        out_shape=[
            jax.ShapeDtypeStruct((b, l, d), F32),
            jax.ShapeDtypeStruct((b, GLA_HEADS, GLA_DK, GLA_DV), F32),
            jax.ShapeDtypeStruct((b, WINDOW, LANE), F32),
            jax.ShapeDtypeStruct((b, WINDOW, LANE), F32),
        ],
        grid_spec=grid_spec,
        compiler_params=pltpu.CompilerParams(dimension_semantics=("arbitrary", "arbitrary"),
                                             vmem_limit_bytes=VMEM_LIMIT),
        name="mixer_prompt",
    )(sinks, x, mod, win, a2w, a2b, gnw, wo, lng, lnb, *consts)


def _mixer_sample_kernel(sinks_ref, x_ref, mod_ref, s0_ref, kc_ref, vc_ref, win_ref, a2w_ref, a2b_ref, gnw_ref,
                         wo_ref, lng_ref, lnb_ref, tri_ref, bias_ref, mkk_ref, mv_ref, ms_ref, caus_ref,
                         x1_ref, s_out_ref, k_out_ref, v_out_ref,
                         proj_ref, qe_ref, qn_ref, ke_ref, kn_ref, kw_ref, eb_ref, vb_ref, kbuf, vbuf, mixed_ref,
                         sbt_ref, xm_ref, *, nb, s):
    nkeys = WINDOW + s
    d = D_MODEL
    for bb in range(nb):
        m = mod_ref[bb]
        xm_ref[bb * s:(bb + 1) * s, :] = x_ref[bb] * (1.0 + m[:, d:2 * d]) + m[:, 0:d]
    loga = _in_proj(xm_ref[...].astype(BF16), win_ref, a2w_ref, a2b_ref, proj_ref)
    _gla_prep(loga, tri_ref, proj_ref, s, qe_ref, qn_ref, ke_ref, kn_ref, kw_ref, eb_ref, vb_ref)

    mkk = mkk_ref[...]
    mv = mv_ref[...]
    ms = ms_ref[...]
    caus = caus_ref[...]
    gnw = gnw_ref[...]

    for bb in range(nb):
        r0 = bb * s
        rows = slice(r0, r0 + s)
        sbt_ref[...] = jnp.zeros_like(sbt_ref)
        for hh in range(GLA_HEADS):
            sbt_ref[hh * GLA_DV:(hh + 1) * GLA_DV, hh * GLA_DK:(hh + 1) * GLA_DK] = s0_ref[bb, hh].T
        kbuf[0:WINDOW, :] = kc_ref[bb].astype(BF16)
        vbuf[0:WINDOW, :] = vc_ref[bb].astype(BF16)
        kbuf[WINDOW:nkeys, :] = proj_ref[rows, C_SK:C_SK + LANE].astype(BF16)
        vbuf[WINDOW:nkeys, :] = proj_ref[rows, C_SV:C_SV + LANE].astype(BF16)
        blocks = _swa_block(proj_ref[rows, C_SQ:C_SQ + SWA_W], kbuf[...], vbuf[...], bias_ref, sinks_ref, None, s)
        for i, blk in enumerate(blocks):
            mixed_ref[rows, GLA_VW + i * LANE:GLA_VW + (i + 1) * LANE] = blk
        o = _gla_chunk(r0, s, qe_ref, qn_ref, ke_ref, kn_ref, kw_ref, eb_ref, vb_ref, sbt_ref, mkk, mv, ms, caus)
        og = _gla_post(o, proj_ref[rows, C_GR:C_GR + GLA_VW], gnw)
        for hh in range(GLA_HEADS):
            mixed_ref[rows, hh * GLA_DV:(hh + 1) * GLA_DV] = og[hh]
            s_out_ref[bb, hh] = sbt_ref[hh * GLA_DV:(hh + 1) * GLA_DV, hh * GLA_DK:(hh + 1) * GLA_DK].T
        k_out_ref[bb, 0:WINDOW - s, :] = kc_ref[bb, s:WINDOW, :]
        v_out_ref[bb, 0:WINDOW - s, :] = vc_ref[bb, s:WINDOW, :]
        k_out_ref[bb, WINDOW - s:WINDOW, :] = proj_ref[rows, C_SK:C_SK + LANE]
        v_out_ref[bb, WINDOW - s:WINDOW, :] = proj_ref[rows, C_SV:C_SV + LANE]

    mix = _mm(mixed_ref[...], wo_ref[...])
    lng = lng_ref[...]
    lnb = lnb_ref[...]
    for bb in range(nb):
        m = mod_ref[bb]
        y = DEEPNORM_ALPHA * x_ref[bb] + m[:, 2 * d:3 * d] * mix[bb * s:(bb + 1) * s, :]
        x1_ref[bb] = _layer_norm(y, lng, lnb)


def _mixer_sample(x, mod, s0, kc, vc, sinks, win, a2w, a2b, gnw, wo, lng, lnb, *, nb=8):
    b, s, d = x.shape
    assert kc.shape[1] == WINDOW and s <= WINDOW
    rows = nb * s
    nkeys = WINDOW + s
    consts = _mixer_consts(rows, s, s, nkeys)
    const2 = lambda i, sk: (0, 0)
    const3 = lambda i, sk: (0, 0, 0)
    grid_spec = pltpu.PrefetchScalarGridSpec(
        num_scalar_prefetch=1,
        grid=(b // nb,),
        in_specs=[
            pl.BlockSpec((nb, s, d), lambda i, sk: (i, 0, 0)),
            pl.BlockSpec((nb, 1, 6 * d), lambda i, sk: (i, 0, 0)),
            pl.BlockSpec((nb, GLA_HEADS, GLA_DK, GLA_DV), lambda i, sk: (i, 0, 0, 0)),
            pl.BlockSpec((nb, WINDOW, LANE), lambda i, sk: (i, 0, 0)),
            pl.BlockSpec((nb, WINDOW, LANE), lambda i, sk: (i, 0, 0)),
            pl.BlockSpec((d, PROJ_W), const2),
            pl.BlockSpec((LANE, GLA_KW), const2),
            pl.BlockSpec((1, GLA_KW), const2),
            pl.BlockSpec((1, GLA_DV), const2),
            pl.BlockSpec((d, d), const2),
            pl.BlockSpec((1, d), const2),
            pl.BlockSpec((1, d), const2),
            pl.BlockSpec((rows, rows), const2),
            pl.BlockSpec((SWA_KV_HEADS, SWA_GROUP * s, nkeys), const3),
            pl.BlockSpec((GLA_HEADS * s, GLA_KW), const2),
            pl.BlockSpec((GLA_HEADS * s, GLA_VW), const2),
            pl.BlockSpec((GLA_VW, GLA_KW), const2),
            pl.BlockSpec((s, GLA_HEADS * s), const2),
        ],
        out_specs=[
            pl.BlockSpec((nb, s, d), lambda i, sk: (i, 0, 0)),
            pl.BlockSpec((nb, GLA_HEADS, GLA_DK, GLA_DV), lambda i, sk: (i, 0, 0, 0)),
            pl.BlockSpec((nb, WINDOW, LANE), lambda i, sk: (i, 0, 0)),
            pl.BlockSpec((nb, WINDOW, LANE), lambda i, sk: (i, 0, 0)),
        ],
        scratch_shapes=[
            pltpu.VMEM((rows, PROJ_W), F32),
            pltpu.VMEM((rows, GLA_KW), BF16),
            pltpu.VMEM((rows, GLA_KW), BF16),
            pltpu.VMEM((rows, GLA_KW), BF16),
            pltpu.VMEM((rows, GLA_KW), BF16),
            pltpu.VMEM((rows, GLA_KW), BF16),
            pltpu.VMEM((rows, GLA_KW), F32),
            pltpu.VMEM((rows, GLA_VW), BF16),
            pltpu.VMEM((nkeys, LANE), BF16),
            pltpu.VMEM((nkeys, LANE), BF16),
            pltpu.VMEM((rows, d), BF16),
            pltpu.VMEM((GLA_VW, GLA_KW), F32),
            pltpu.VMEM((rows, d), F32),
        ],
    )
    return pl.pallas_call(
        functools.partial(_mixer_sample_kernel, nb=nb, s=s),
        out_shape=[
            jax.ShapeDtypeStruct((b, s, d), F32),
            jax.ShapeDtypeStruct((b, GLA_HEADS, GLA_DK, GLA_DV), F32),
            jax.ShapeDtypeStruct((b, WINDOW, LANE), F32),
            jax.ShapeDtypeStruct((b, WINDOW, LANE), F32),
        ],
        grid_spec=grid_spec,
        compiler_params=pltpu.CompilerParams(dimension_semantics=("arbitrary",),
                                             vmem_limit_bytes=VMEM_LIMIT),
        name="mixer_sample",
    )(sinks, x, mod, s0, kc, vc, win, a2w, a2b, gnw, wo, lng, lnb, *consts)


MOE_BLK = 128


def _route_t(logits_t):
    t = logits_t.shape[1]
    row = lax.broadcasted_iota(jnp.int32, (EPG, t), 0).astype(F32)
    big = 99.0
    gl = jnp.where(row < N_GROUPS, logits_t[0:EPG, :], -jnp.inf)
    gmax = jnp.max(gl, axis=0, keepdims=True)
    grp = jnp.min(jnp.where(gl == gmax, row, big), axis=0, keepdims=True)
    p_grp = 1.0 / jnp.sum(jnp.exp(gl - gmax), axis=0, keepdims=True)
    el = jnp.zeros((EPG, t), F32)
    for g in range(N_GROUPS):
        el = el + jnp.where(grp == float(g), logits_t[R_EXP0 + EPG * g:R_EXP0 + EPG * (g + 1), :], 0.0)
    v1 = jnp.max(el, axis=0, keepdims=True)
    i1 = jnp.min(jnp.where(el == v1, row, big), axis=0, keepdims=True)
    el2 = jnp.where(row == i1, -jnp.inf, el)
    v2 = jnp.max(el2, axis=0, keepdims=True)
    i2 = jnp.min(jnp.where(el2 == v2, row, big), axis=0, keepdims=True)
    e2 = jnp.exp(v2 - v1)
    w1 = p_grp / (1.0 + e2)
    w2 = p_grp * e2 / (1.0 + e2)
    cw = jnp.where(row == i1, w1, 0.0) + jnp.where(row == i2, w2, 0.0)
    return grp, cw


MOE_PC = 256


def _moe_sort_kernel(x1_ref, mod_ref, wrt_ref, brt_ref, xs_ref, cws_ref, pos_ref, cnt_ref, *, nb, r):
    tm = nb * r
    tmp = tm + N_GROUPS * MOE_BLK
    d = D_MODEL
    mod = mod_ref[...]
    t3 = x1_ref[...] * (1.0 + mod[:, :, 4 * d:5 * d]) + mod[:, :, 3 * d:4 * d]
    t = t3.reshape(tm, d).astype(BF16)
    grp, cw = _route_t(_mm_nt(wrt_ref[...], t) + brt_ref[...])
    row = lax.broadcasted_iota(jnp.int32, (EPG, tm), 0).astype(F32)
    onehot_g = jnp.where(row == grp, 1.0, 0.0)
    nch = tm // LANE
    strict = (lax.broadcasted_iota(jnp.int32, (LANE, LANE), 0)
              < lax.broadcasted_iota(jnp.int32, (LANE, LANE), 1)).astype(F32).astype(BF16)
    stacked = jnp.concatenate([onehot_g[:, c * LANE:(c + 1) * LANE] for c in range(nch)], axis=0)
    pref = _mm(stacked.astype(BF16), strict)
    tot = jnp.sum(stacked, axis=1, keepdims=True)
    cnt = jnp.zeros((EPG, 1), F32)
    ranks = []
    for c in range(nch):
        ranks.append(pref[c * EPG:(c + 1) * EPG, :] + cnt)
        cnt = cnt + tot[c * EPG:(c + 1) * EPG, :]
    rank = jnp.concatenate(ranks, axis=1)
    padded = jnp.floor((cnt + (MOE_BLK - 1)) * (1.0 / MOE_BLK)) * MOE_BLK
    rowc = lax.broadcasted_iota(jnp.int32, (EPG, 1), 0)
    off = jnp.zeros((EPG, 1), F32)
    for gg in range(N_GROUPS - 1):
        off = off + jnp.where(rowc > gg, padded[gg:gg + 1, :], 0.0)
    pos = jnp.sum(onehot_g * (off + rank), axis=0, keepdims=True)
    pos_ref[0] = jnp.broadcast_to(pos, (EPG, tm))
    cnt_ref[0] = jnp.broadcast_to(cnt, (EPG, LANE))
    cw_hi, cw_lo = _split_bf16(jnp.concatenate([cw, jnp.zeros((LANE - EPG, tm), F32)], axis=0).T)
    t_aug = jnp.concatenate([t, cw_hi, cw_lo], axis=1)
    for c in range(tmp // MOE_PC):
        slot = (lax.broadcasted_iota(jnp.int32, (MOE_PC, tm), 0) + c * MOE_PC).astype(F32)
        perm = jnp.where(slot == pos, 1.0, 0.0).astype(BF16)
        moved = _mm(perm, t_aug)
        xs_ref[c * MOE_PC:(c + 1) * MOE_PC, :] = moved[:, 0:d].astype(BF16)
        cws_ref[c * MOE_PC:(c + 1) * MOE_PC, :] = moved[:, d:d + LANE] + moved[:, d + LANE:d + 2 * LANE]


MOE_BIG = 2 * MOE_BLK


def _moe_expert_kernel(boff_ref, nblk_ref, nused_ref, xs_ref, cws_ref, wg_ref, wu_ref, wd_ref, ys_hbm,
                       ybuf, ytail, zbuf, sem, st_ref, *, ntiles, tmp):
    q = pl.program_id(0)
    grp = q // ntiles
    tile = q - grp * ntiles
    off = boff_ref[q]
    n = nblk_ref[q]
    n2 = lax.shift_right_logical(n, 1)
    tail = jnp.bitwise_and(n, 1)
    base = tile * tmp
    wd = wd_ref[0].reshape(EPG * EXPERT_FF, D_MODEL)

    def experts(rows):
        xb = xs_ref[rows, :]
        cwb = cws_ref[rows, :]
        hs = []
        for e in range(EPG):
            gg_ = _mm(xb, wg_ref[0, e])
            uu = _mm(xb, wu_ref[0, e])
            hs.append((gg_ * _sigmoid(gg_) * uu * cwb[:, e:e + 1]).astype(BF16))
        return _mm(jnp.concatenate(hs, axis=1), wd).astype(BF16)

    def big_copy(slot, blk):
        return pltpu.make_async_copy(
            ybuf.at[slot], ys_hbm.at[pl.ds(pl.multiple_of(base + blk * MOE_BLK, MOE_BLK), MOE_BIG), :], sem.at[slot])

    def tail_copy(blk):
        return pltpu.make_async_copy(
            ytail, ys_hbm.at[pl.ds(pl.multiple_of(base + blk * MOE_BLK, MOE_BLK), MOE_BLK), :], sem.at[2])

    def zero_copy(blk):
        return pltpu.make_async_copy(
            zbuf, ys_hbm.at[pl.ds(pl.multiple_of(base + blk * MOE_BLK, MOE_BLK), MOE_BLK), :], sem.at[3])

    @pl.when(q == 0)
    def _():
        st_ref[0] = 0
        st_ref[1] = 0
        st_ref[2] = 0
        zbuf[...] = jnp.zeros_like(zbuf)

    def body(k, carry):
        c = st_ref[0]
        slot = jnp.bitwise_and(c, 1)
        blk = off + 2 * k
        y = experts(pl.ds(pl.multiple_of(blk * MOE_BLK, MOE_BLK), MOE_BIG))

        @pl.when(c >= 2)
        def _():
            big_copy(slot, 0).wait()

        ybuf[slot] = y
        big_copy(slot, blk).start()
        st_ref[0] = c + 1
        return carry

    lax.fori_loop(0, n2, body, 0)

    @pl.when(tail == 1)
    def _():
        blk = off + 2 * n2
        y = experts(pl.ds(pl.multiple_of(blk * MOE_BLK, MOE_BLK), MOE_BLK))

        @pl.when(st_ref[1] == 1)
        def _():
            tail_copy(0).wait()

        ytail[...] = y
        tail_copy(blk).start()
        st_ref[1] = 1

    @pl.when(grp == N_GROUPS - 1)
    def _():
        def zfill(blk, carry):
            zero_copy(blk).start()
            return carry

        nz = tmp // MOE_BLK - nused_ref[tile]
        lax.fori_loop(nused_ref[tile], tmp // MOE_BLK, zfill, 0)
        st_ref[2] = st_ref[2] + nz

    @pl.when(q == pl.num_programs(0) - 1)
    def _():
        c = st_ref[0]

        @pl.when(c >= 2)
        def _():
            big_copy(jnp.bitwise_and(c, 1), 0).wait()

        @pl.when(c >= 1)
        def _():
            big_copy(jnp.bitwise_and(c - 1, 1), 0).wait()

        @pl.when(st_ref[1] == 1)
        def _():
            tail_copy(0).wait()

        def zwait(i, carry):
            zero_copy(0).wait()
            return carry

        lax.fori_loop(0, st_ref[2], zwait, 0)


def _moe_unsort_kernel(ys_ref, pos_ref, x1_ref, mod_ref, lng_ref, lnb_ref, out_ref, *, nb, r):
    tm = nb * r
    tmp = tm + N_GROUPS * MOE_BLK
    d = D_MODEL
    ysb = ys_ref[...]
    posc = jnp.broadcast_to(pos_ref[0][0:1, :], (LANE, tm)).T
    lng = lng_ref[...]
    lnb = lnb_ref[...]
    for c in range(tm // MOE_PC):
        slot = lax.broadcasted_iota(jnp.int32, (MOE_PC, tmp), 1).astype(F32)
        unperm = jnp.where(slot == posc[c * MOE_PC:(c + 1) * MOE_PC, 0:1], 1.0, 0.0).astype(BF16)
        y = _mm(unperm, ysb)
        if nb == 1:
            x1c = x1_ref[0, c * MOE_PC:(c + 1) * MOE_PC, :]
            gt2 = mod_ref[0][:, 5 * d:6 * d]
            out_ref[0, c * MOE_PC:(c + 1) * MOE_PC, :] = _layer_norm(DEEPNORM_ALPHA * x1c + gt2 * y, lng, lnb)
        else:
            cb = MOE_PC // r
            x1c = x1_ref[c * cb:(c + 1) * cb]
            gt2 = mod_ref[c * cb:(c + 1) * cb][:, :, 5 * d:6 * d]
            yy = DEEPNORM_ALPHA * x1c + gt2 * y.reshape(cb, r, d)
            out_ref[c * cb:(c + 1) * cb] = _layer_norm(yy, lng, lnb)


def _moe(x1, mod, wrt, brt, wg, wu, wd, lng, lnb, *, nb, r):
    b, l, d = x1.shape
    tpb = l // r
    ntiles = (b // nb) * tpb
    tm = nb * r
    tmp = tm + N_GROUPS * MOE_BLK
    bpt = tmp // MOE_BLK
    xmap = lambda i: (i // tpb, i % tpb, 0)
    mmap = lambda i: (i // tpb, 0, 0)
    const2 = lambda i: (0, 0)
    params = pltpu.CompilerParams(dimension_semantics=("arbitrary",), vmem_limit_bytes=VMEM_LIMIT)

    xs, cws, pos, cnt = pl.pallas_call(
        functools.partial(_moe_sort_kernel, nb=nb, r=r),
        out_shape=[jax.ShapeDtypeStruct((ntiles * tmp, d), BF16),
                   jax.ShapeDtypeStruct((ntiles * tmp, LANE), F32),
                   jax.ShapeDtypeStruct((ntiles, EPG, tm), F32),
                   jax.ShapeDtypeStruct((ntiles, EPG, LANE), F32)],
        grid=(ntiles,),
        in_specs=[
            pl.BlockSpec((nb, r, d), xmap),
            pl.BlockSpec((nb, 1, 6 * d), mmap),
            pl.BlockSpec((LANE, d), const2),
            pl.BlockSpec((LANE, 1), const2),
        ],
        out_specs=[pl.BlockSpec((tmp, d), lambda i: (i, 0)),
                   pl.BlockSpec((tmp, LANE), lambda i: (i, 0)),
                   pl.BlockSpec((1, EPG, tm), lambda i: (i, 0, 0)),
                   pl.BlockSpec((1, EPG, LANE), lambda i: (i, 0, 0))],
        compiler_params=params,
        name="moe_sort",
    )(x1, mod, wrt, brt)

    nblk = ((cnt[:, :N_GROUPS, 0].astype(jnp.int32) + (MOE_BLK - 1)) // MOE_BLK)
    boff = jnp.cumsum(nblk, axis=1) - nblk
    nused = jnp.sum(nblk, axis=1).astype(jnp.int32)
    nblk_q = nblk.T.reshape(-1).astype(jnp.int32)
    boff_q = boff.T.reshape(-1).astype(jnp.int32)

    tmap = lambda q, bo, nk, nu: (q % ntiles, 0)
    wmap = lambda q, bo, nk, nu: (q // ntiles, 0, 0, 0)
    ys = pl.pallas_call(
        functools.partial(_moe_expert_kernel, ntiles=ntiles, tmp=tmp),
        out_shape=jax.ShapeDtypeStruct((ntiles * tmp, d), BF16),
        grid_spec=pltpu.PrefetchScalarGridSpec(
            num_scalar_prefetch=3,
            grid=(N_GROUPS * ntiles,),
            in_specs=[
                pl.BlockSpec((tmp, d), tmap),
                pl.BlockSpec((tmp, LANE), tmap),
                pl.BlockSpec((1, EPG, d, EXPERT_FF), wmap),
                pl.BlockSpec((1, EPG, d, EXPERT_FF), wmap),
                pl.BlockSpec((1, EPG, EXPERT_FF, d), wmap),
            ],
            out_specs=pl.BlockSpec(memory_space=pl.ANY),
            scratch_shapes=[
                pltpu.VMEM((2, MOE_BIG, d), BF16),
                pltpu.VMEM((MOE_BLK, d), BF16),
                pltpu.VMEM((MOE_BLK, d), BF16),
                pltpu.SemaphoreType.DMA((4,)),
                pltpu.SMEM((4,), jnp.int32),
            ],
        ),
        compiler_params=params,
        name="moe_experts",
    )(boff_q, nblk_q, nused, xs, cws, wg, wu, wd)

    return pl.pallas_call(
        functools.partial(_moe_unsort_kernel, nb=nb, r=r),
        out_shape=jax.ShapeDtypeStruct((b, l, d), F32),
        grid=(ntiles,),
        in_specs=[
            pl.BlockSpec((tmp, d), lambda i: (i, 0)),
            pl.BlockSpec((1, EPG, tm), lambda i: (i, 0, 0)),
            pl.BlockSpec((nb, r, d), xmap),
            pl.BlockSpec((nb, 1, 6 * d), mmap),
            pl.BlockSpec((1, d), const2),
            pl.BlockSpec((1, d), const2),
        ],
        out_specs=pl.BlockSpec((nb, r, d), xmap),
        compiler_params=params,
        name="moe_unsort",
    )(ys, pos, x1, mod, lng, lnb)


def kernel(x_prompt, x_sample, c_prompt, c_sample, state_gla, cache_swa_k, cache_swa_v, ada_w, ada_b, w_in,
           gla_a2_w, gla_a2_b, gla_norm_w, swa_sinks, w_o, ln1_g, ln1_b, router_g_w, router_g_b, router_e_w,
           router_e_b, moe_w_gate, moe_w_up, moe_w_down, ln2_g, ln2_b):
    assert ada_w.shape[0] == 1
    bp = x_prompt.shape[0]
    bs, ss, d = x_sample.shape
    lc = cache_swa_k.shape[2]

    w = w_in[0]
    zpad = jnp.zeros((d, LANE - GLA_RANK), F32)
    w_sq = w[:, 1552:2064].reshape(d, SWA_KV_HEADS, SWA_GROUP, SWA_DH).transpose(0, 2, 1, 3).reshape(d, SWA_W)
    win = jnp.concatenate([w[:, 0:1536], w_sq, w[:, 2064:2320], w[:, 1536:1552], zpad], axis=1).astype(BF16)
    a2w = jnp.concatenate([gla_a2_w[0], jnp.zeros((LANE - GLA_RANK, GLA_KW), F32)], axis=0).astype(BF16)
    a2b = gla_a2_b[0].reshape(1, GLA_KW)
    gnw = gla_norm_w[0].reshape(1, GLA_DV)
    wo_swa = w_o[0][GLA_VW:].reshape(SWA_KV_HEADS, SWA_GROUP, SWA_DH, d).transpose(1, 0, 2, 3).reshape(SWA_W, d)
    wo = jnp.concatenate([w_o[0][:GLA_VW], wo_swa], axis=0).astype(BF16)
    sinks = swa_sinks[0]
    wrt = jnp.concatenate([router_g_w[0], jnp.zeros((d, R_EXP0 - N_GROUPS), F32),
                           jnp.transpose(router_e_w[0], (1, 0, 2)).reshape(d, N_GROUPS * EPG),
                           jnp.zeros((d, LANE - R_EXP0 - N_GROUPS * EPG), F32)], axis=1).T.astype(BF16)
    brt = jnp.concatenate([router_g_b[0], jnp.zeros((R_EXP0 - N_GROUPS,), F32), router_e_b[0].reshape(-1),
                           jnp.zeros((LANE - R_EXP0 - N_GROUPS * EPG,), F32)]).reshape(LANE, 1)
    wg = moe_w_gate[0].astype(BF16)
    wu = moe_w_up[0].astype(BF16)
    wd = moe_w_down[0].astype(BF16)
    lng1, lnb1 = ln1_g[0].reshape(1, d), ln1_b[0].reshape(1, d)
    lng2, lnb2 = ln2_g[0].reshape(1, d), ln2_b[0].reshape(1, d)

    mod = _adaln(jnp.concatenate([c_prompt, c_sample], axis=0), ada_w[0], ada_b[0].reshape(1, 6 * d))
    mod = mod.reshape(bp + bs, 1, 6 * d)
    mod_p, mod_s = mod[:bp], mod[bp:]

    x1p, s_p, k_p, v_p = _mixer_prompt(x_prompt, mod_p, sinks, win, a2w, a2b, gnw, wo, lng1, lnb1)
    x1s, s_s, k_s, v_s = _mixer_sample(
        x_sample, mod_s, state_gla[0], cache_swa_k[0].reshape(bs, lc, LANE), cache_swa_v[0].reshape(bs, lc, LANE),
        sinks, win, a2w, a2b, gnw, wo, lng1, lnb1)

    yp = _moe(x1p, mod_p, wrt, brt, wg, wu, wd, lng2, lnb2, nb=1, r=1024)
    ys = _moe(x1s, mod_s, wrt, brt, wg, wu, wd, lng2, lnb2, nb=bs, r=ss)

    kv_shape_p = (1, bp, WINDOW, SWA_KV_HEADS, SWA_DH)
    kv_shape_s = (1, bs, lc, SWA_KV_HEADS, SWA_DH)
    return (yp, ys, s_p[None], k_p.reshape(kv_shape_p), v_p.reshape(kv_shape_p),
            s_s[None], k_s.reshape(kv_shape_s), v_s.reshape(kv_shape_s))
```

```python
import functools

import jax
import jax.numpy as jnp
import numpy as np
from jax import lax
from jax.experimental import pallas as pl
from jax.experimental.pallas import tpu as pltpu

F32 = jnp.float32
BF16 = jnp.bfloat16

D_MODEL = 1024
CHUNK = 64
GLA_HEADS = 4
GLA_DK = 64
GLA_DV = 128
GLA_KW = GLA_HEADS * GLA_DK
GLA_VW = GLA_HEADS * GLA_DV
GLA_RANK = 16
GLA_TAU = 16.0
SWA_Q_HEADS = 8
SWA_KV_HEADS = 2
SWA_GROUP = 4
SWA_DH = 64
SWA_W = SWA_Q_HEADS * SWA_DH
WINDOW = 128
N_GROUPS = 4
EPG = 8
EXPERT_FF = 256
DEEPNORM_ALPHA = 2.0 ** 0.25
LN_EPS = 1e-5
NEG_INF = -1e30

C_GQ, C_GK, C_GV, C_GR, C_SQ, C_SK, C_SV, C_GA = 0, 256, 512, 1024, 1536, 2048, 2176, 2304
PROJ_W = 2432
LANE = 128
R_EXP0 = 8

VMEM_LIMIT = 56 * 1024 * 1024


def _mm(a, b):
    return jnp.dot(a, b, preferred_element_type=F32)


def _mm_nt(a, b):
    return lax.dot_general(a, b, (((1,), (1,)), ((), ())), preferred_element_type=F32)


def _mm_tn(a, b):
    return lax.dot_general(a, b, (((0,), (0,)), ((), ())), preferred_element_type=F32)


def _split_bf16(a):
    hi = a.astype(BF16)
    lo = (a - hi.astype(F32)).astype(BF16)
    return hi, lo


def _sigmoid(x):
    return 1.0 / (1.0 + jnp.exp(-x))


def _layer_norm(y, g, b):
    mu = jnp.mean(y, axis=-1, keepdims=True)
    d = y - mu
    var = jnp.mean(d * d, axis=-1, keepdims=True)
    return d * lax.rsqrt(var + LN_EPS) * g + b


def _adaln_kernel(c_ref, w_ref, b_ref, o_ref):
    c = c_ref[...]
    a = c * _sigmoid(c)
    a_hi, a_lo = _split_bf16(a)
    w_hi, w_lo = _split_bf16(w_ref[...])
    o_ref[...] = _mm(a_hi, w_hi) + (_mm(a_hi, w_lo) + _mm(a_lo, w_hi)) + b_ref[...]


def _adaln(c_all, ada_w, ada_b):
    n = c_all.shape[0]
    bn = 1024
    return pl.pallas_call(
        _adaln_kernel,
        out_shape=jax.ShapeDtypeStruct((n, 6 * D_MODEL), F32),
        grid=(6 * D_MODEL // bn,),
        in_specs=[pl.BlockSpec((n, D_MODEL), lambda j: (0, 0)),
                  pl.BlockSpec((D_MODEL, bn), lambda j: (0, j)),
                  pl.BlockSpec((1, bn), lambda j: (0, j))],
        out_specs=pl.BlockSpec((n, bn), lambda j: (0, j)),
        compiler_params=pltpu.CompilerParams(dimension_semantics=("arbitrary",), vmem_limit_bytes=VMEM_LIMIT),
        name="adaln",
    )(c_all, ada_w, ada_b)


def _mixer_consts(nrows, chunk, qb, kw):
    r = np.arange(nrows)
    tri = ((r[:, None] // chunk == r[None, :] // chunk) & (r[:, None] >= r[None, :])).astype(np.float32)
    hs = np.arange(GLA_HEADS * chunk)
    mkk = (hs[:, None] // chunk == np.arange(GLA_KW)[None, :] // GLA_DK).astype(np.float32)
    mv = (hs[:, None] // chunk == np.arange(GLA_VW)[None, :] // GLA_DV).astype(np.float32)
    ms = (np.arange(GLA_VW)[:, None] // GLA_DV == np.arange(GLA_KW)[None, :] // GLA_DK).astype(np.float32)
    caus = (np.arange(chunk)[:, None] >= (hs[None, :] % chunk)).astype(np.float32)
    t = np.arange(qb)
    kj = np.arange(kw)
    cs = (t // chunk) * chunk
    kpos = kj[None, :] - WINDOW
    vis = (kpos >= cs[:, None] - WINDOW) & (kpos < cs[:, None] + chunk)
    dist = np.abs(t[:, None] + WINDOW - kj[None, :]).astype(np.float32)
    bias = np.zeros((SWA_KV_HEADS, SWA_GROUP * qb, kw), np.float32)
    for hk in range(SWA_KV_HEADS):
        for g in range(SWA_GROUP):
            slope = np.float32(2.0 ** (-(hk * SWA_GROUP + g + 1)))
            bias[hk, g * qb:(g + 1) * qb] = np.where(vis, -slope * dist, np.float32(2.0 * NEG_INF))
    return (jnp.asarray(tri, BF16), jnp.asarray(bias), jnp.asarray(mkk, BF16), jnp.asarray(mv, BF16),
            jnp.asarray(ms), jnp.asarray(caus))


def _modulate(x, mod):
    return (x * (1.0 + mod[:, D_MODEL:2 * D_MODEL]) + mod[:, 0:D_MODEL]).astype(BF16)


def _log_gates(proj_ref, a2w_ref, a2b_ref):
    ga = proj_ref[:, C_GA:C_GA + LANE].astype(BF16)
    z = _mm(ga, a2w_ref[...]) + a2b_ref[...]
    lsig = -(jnp.maximum(-z, 0.0) + jnp.log(1.0 + jnp.exp(-jnp.abs(z))))
    return lsig * (1.0 / GLA_TAU)


def _gla_prep(loga, tri_ref, proj_ref, chunk, qe_ref, qn_ref, ke_ref, kn_ref, kw_ref, eb_ref, vb_ref):
    nrows = loga.shape[0]
    la_hi, la_lo = _split_bf16(loga)
    tri = tri_ref[...]
    b = _mm(tri, la_hi) + _mm(tri, la_lo)
    b_end = jnp.concatenate(
        [jnp.broadcast_to(b[c * chunk + chunk - 1:(c + 1) * chunk, :], (chunk, GLA_KW))
         for c in range(nrows // chunk)], axis=0)
    eb = jnp.exp(b)
    ebn = jnp.exp(-b)
    wk = jnp.exp(b_end - b)
    q = proj_ref[:, C_GQ:C_GQ + GLA_KW] * (GLA_DK ** -0.5)
    k = proj_ref[:, C_GK:C_GK + GLA_KW]
    eb_ref[...] = eb
    qe_ref[...] = (q * eb).astype(BF16)
    qn_ref[...] = (q * ebn).astype(BF16)
    ke_ref[...] = (k * eb).astype(BF16)
    kn_ref[...] = (k * ebn).astype(BF16)
    kw_ref[...] = (k * wk).astype(BF16)
    vb_ref[...] = proj_ref[:, C_GV:C_GV + GLA_VW].astype(BF16)


def _gla_chunk(r0, chunk, qe_ref, qn_ref, ke_ref, kn_ref, kw_ref, eb_ref, vb_ref, sbt_ref, mkk, mv, ms, caus):
    rows = slice(r0, r0 + chunk)
    qe = qe_ref[rows, :]
    qn = qn_ref[rows, :]
    zero = jnp.zeros((), BF16)
    kn4 = jnp.where(mkk != 0, jnp.concatenate([kn_ref[rows, :]] * GLA_HEADS, axis=0), zero)
    ke4 = jnp.where(mkk != 0, jnp.concatenate([ke_ref[rows, :]] * GLA_HEADS, axis=0), zero)
    a_lo = _mm_nt(qe, kn4)
    a_up = _mm_nt(qn, ke4)
    a = jnp.where(caus != 0.0, a_lo, a_up).astype(BF16)
    v = vb_ref[rows, :]
    v4 = jnp.where(mv != 0, jnp.concatenate([v] * GLA_HEADS, axis=0), zero)
    sbt = sbt_ref[...]
    o = _mm(a, v4) + _mm_nt(qe, sbt.astype(BF16))
    ut = _mm_tn(v, kw_ref[rows, :])
    g_end = eb_ref[r0 + chunk - 1:r0 + chunk, :]
    sbt_ref[...] = g_end * sbt + jnp.where(ms != 0.0, ut, 0.0)
    return o


def _gla_post(o, gr, gnw):
    res = []
    for h in range(GLA_HEADS):
        oh = o[:, h * GLA_DV:(h + 1) * GLA_DV]
        ms_ = jnp.mean(oh * oh, axis=-1, keepdims=True)
        og = oh * lax.rsqrt(ms_ + LN_EPS) * gnw
        r = gr[:, h * GLA_DV:(h + 1) * GLA_DV]
        res.append((og * (r * _sigmoid(r))).astype(BF16))
    return res


def _swa_scores(sq, kwin, qb):
    low = lax.broadcasted_iota(jnp.int32, (qb, LANE), 1) < SWA_DH
    res = []
    for hk in range(SWA_KV_HEADS):
        keep = low if hk == 0 else jnp.logical_not(low)
        qs = jnp.concatenate([jnp.where(keep, sq[:, g * LANE:(g + 1) * LANE] * (SWA_DH ** -0.5), 0.0)
                              for g in range(SWA_GROUP)], axis=0).astype(BF16)
        res.append(_mm_nt(qs, kwin))
    return res


def _swa_finish(s, vwin, bias, sinks_ref, hk, extra_valid, qb):
    ok = bias > -1e29
    if extra_valid is not None:
        ok = ok & extra_valid
    s = jnp.where(ok, s + bias, NEG_INF)
    ps, denoms = [], []
    for g in range(SWA_GROUP):
        sg = s[g * qb:(g + 1) * qb, :]
        sink = sinks_ref[hk * SWA_GROUP + g]
        m = jnp.maximum(jnp.max(sg, axis=-1, keepdims=True), sink)
        pg = jnp.exp(sg - m)
        denoms.append(jnp.sum(pg, axis=-1, keepdims=True) + jnp.exp(sink - m))
        ps.append(pg.astype(BF16))
    o = _mm(jnp.concatenate(ps, axis=0), vwin)
    return o / jnp.concatenate(denoms, axis=0)


def _swa_merge(o0, o1, qb):
    low = lax.broadcasted_iota(jnp.int32, (qb, LANE), 1) < SWA_DH
    return [jnp.where(low, o0[g * qb:(g + 1) * qb, :], o1[g * qb:(g + 1) * qb, :]).astype(BF16)
            for g in range(SWA_GROUP)]


def _out_proj_ln(x, gt1, mixed_ref, wo_ref, ln_g, ln_b):
    mix = _mm(mixed_ref[...], wo_ref[...])
    return _layer_norm(DEEPNORM_ALPHA * x + gt1 * mix, ln_g, ln_b)


P_QB = 128
P_KW = WINDOW + P_QB


def _mixer_prompt_kernel(sinks_ref, x_ref, mod_ref, win_ref, a2w_ref, a2b_ref, gnw_ref, wo_ref, lng_ref, lnb_ref,
                         tri_ref, bias_ref, mkk_ref, mv_ref, ms_ref, caus_ref,
                         x1_ref, s_out_ref, k_out_ref, v_out_ref,
                         proj_ref, qe_ref, qn_ref, ke_ref, kn_ref, kw_ref, eb_ref, vb_ref, kbuf, vbuf, mixed_ref,
                         sbt_ref, *, tl):
    j = pl.program_id(1)
    nj = pl.num_programs(1)
    d = D_MODEL

    @pl.when(j == 0)
    def _():
        sbt_ref[...] = jnp.zeros_like(sbt_ref)
        kbuf[0:WINDOW, :] = jnp.zeros((WINDOW, LANE), BF16)
        vbuf[0:WINDOW, :] = jnp.zeros((WINDOW, LANE), BF16)

    x = x_ref[0]
    mod = mod_ref[0]
    proj_ref[...] = _mm(_modulate(x, mod), win_ref[...])
    loga = _log_gates(proj_ref, a2w_ref, a2b_ref)
    _gla_prep(loga, tri_ref, proj_ref, CHUNK, qe_ref, qn_ref, ke_ref, kn_ref, kw_ref, eb_ref, vb_ref)
    kbuf[WINDOW:WINDOW + tl, :] = proj_ref[:, C_SK:C_SK + LANE].astype(BF16)
    vbuf[WINDOW:WINDOW + tl, :] = proj_ref[:, C_SV:C_SV + LANE].astype(BF16)

    mkk = mkk_ref[...]
    mv = mv_ref[...]
    ms = ms_ref[...]
    caus = caus_ref[...]
    gnw = gnw_ref[...]
    kj = lax.broadcasted_iota(jnp.int32, (SWA_GROUP * P_QB, P_KW), 1)
    first_valid = kj >= jnp.where(j > 0, 0, WINDOW)

    def scores(p):
        q0 = p * P_QB
        return _swa_scores(proj_ref[q0:q0 + P_QB, C_SQ:C_SQ + SWA_W], kbuf[q0:q0 + P_KW, :], P_QB)

    def gla(r0):
        o = _gla_chunk(r0, CHUNK, qe_ref, qn_ref, ke_ref, kn_ref, kw_ref, eb_ref, vb_ref, sbt_ref, mkk, mv, ms, caus)
        og = _gla_post(o, proj_ref[r0:r0 + CHUNK, C_GR:C_GR + GLA_VW], gnw)
        for hh in range(GLA_HEADS):
            mixed_ref[r0:r0 + CHUNK, hh * GLA_DV:(hh + 1) * GLA_DV] = og[hh]

    for p in range(tl // P_QB):
        q0 = p * P_QB
        s = scores(p)
        vwin = vbuf[q0:q0 + P_KW, :]
        extra = first_valid if p == 0 else None
        o0 = _swa_finish(s[0], vwin, bias_ref[0], sinks_ref, 0, extra, P_QB)
        o1 = _swa_finish(s[1], vwin, bias_ref[1], sinks_ref, 1, extra, P_QB)
        for i, blk in enumerate(_swa_merge(o0, o1, P_QB)):
            mixed_ref[q0:q0 + P_QB, GLA_VW + i * LANE:GLA_VW + (i + 1) * LANE] = blk
        gla(q0)
        gla(q0 + CHUNK)

    x1_ref[0] = _out_proj_ln(x, mod[:, 2 * d:3 * d], mixed_ref, wo_ref, lng_ref[...], lnb_ref[...])

    kbuf[0:WINDOW, :] = kbuf[tl:tl + WINDOW, :]
    vbuf[0:WINDOW, :] = vbuf[tl:tl + WINDOW, :]

    @pl.when(j == nj - 1)
    def _():
        for hh in range(GLA_HEADS):
            s_out_ref[0, hh] = sbt_ref[hh * GLA_DV:(hh + 1) * GLA_DV, hh * GLA_DK:(hh + 1) * GLA_DK].T
        k_out_ref[0] = proj_ref[tl - WINDOW:tl, C_SK:C_SK + LANE]
        v_out_ref[0] = proj_ref[tl - WINDOW:tl, C_SV:C_SV + LANE]


def _mixer_prompt(x, mod, sinks, win, a2w, a2b, gnw, wo, lng, lnb, *, tl=512):
    b, l, d = x.shape
    nj = l // tl
    consts = _mixer_consts(tl, CHUNK, P_QB, P_KW)
    const2 = lambda i, j, s: (0, 0)
    const3 = lambda i, j, s: (0, 0, 0)
    grid_spec = pltpu.PrefetchScalarGridSpec(
        num_scalar_prefetch=1,
        grid=(b, nj),
        in_specs=[
            pl.BlockSpec((1, tl, d), lambda i, j, s: (i, j, 0)),
            pl.BlockSpec((1, 1, 6 * d), lambda i, j, s: (i, 0, 0)),
            pl.BlockSpec((d, PROJ_W), const2),
            pl.BlockSpec((LANE, GLA_KW), const2),
            pl.BlockSpec((1, GLA_KW), const2),
            pl.BlockSpec((1, GLA_DV), const2),
            pl.BlockSpec((d, d), const2),
            pl.BlockSpec((1, d), const2),
            pl.BlockSpec((1, d), const2),
            pl.BlockSpec((tl, tl), const2),
            pl.BlockSpec((SWA_KV_HEADS, SWA_GROUP * P_QB, P_KW), const3),
            pl.BlockSpec((GLA_HEADS * CHUNK, GLA_KW), const2),
            pl.BlockSpec((GLA_HEADS * CHUNK, GLA_VW), const2),
            pl.BlockSpec((GLA_VW, GLA_KW), const2),
            pl.BlockSpec((CHUNK, GLA_HEADS * CHUNK), const2),
        ],
        out_specs=[
            pl.BlockSpec((1, tl, d), lambda i, j, s: (i, j, 0)),
            pl.BlockSpec((1, GLA_HEADS, GLA_DK, GLA_DV), lambda i, j, s: (i, 0, 0, 0)),
            pl.BlockSpec((1, WINDOW, LANE), lambda i, j, s: (i, 0, 0)),
            pl.BlockSpec((1, WINDOW, LANE), lambda i, j, s: (i, 0, 0)),
        ],
        scratch_shapes=[
            pltpu.VMEM((tl, PROJ_W), F32),
            pltpu.VMEM((tl, GLA_KW), BF16),
            pltpu.VMEM((tl, GLA_KW), BF16),
            pltpu.VMEM((tl, GLA_KW), BF16),
            pltpu.VMEM((tl, GLA_KW), BF16),
            pltpu.VMEM((tl, GLA_KW), BF16),
            pltpu.VMEM((tl, GLA_KW), F32),
            pltpu.VMEM((tl, GLA_VW), BF16),
            pltpu.VMEM((WINDOW + tl, LANE), BF16),
            pltpu.VMEM((WINDOW + tl, LANE), BF16),
            pltpu.VMEM((tl, d), BF16),
            pltpu.VMEM((GLA_VW, GLA_KW), F32),
        ],
    )
    return pl.pallas_call(
        functools.partial(_mixer_prompt_kernel, tl=tl),
        out_shape=[
            jax.ShapeDtypeStruct((b, l, d), F32),
            jax.ShapeDtypeStruct((b, GLA_HEADS, GLA_DK, GLA_DV), F32),
            jax.ShapeDtypeStruct((b, WINDOW, LANE), F32),
            jax.ShapeDtypeStruct((b, WINDOW, LANE), F32),
        ],
        grid_spec=grid_spec,
        compiler_params=pltpu.CompilerParams(dimension_semantics=("arbitrary", "arbitrary"),
                                             vmem_limit_bytes=VMEM_LIMIT),
        name="mixer_prompt",
    )(sinks, x, mod, win, a2w, a2b, gnw, wo, lng, lnb, *consts)


def _mixer_sample_kernel(sinks_ref, x_ref, mod_ref, s0_ref, kc_ref, vc_ref, win_ref, a2w_ref, a2b_ref, gnw_ref,
                         wo_ref, lng_ref, lnb_ref, tri_ref, bias_ref, mkk_ref, mv_ref, ms_ref, caus_ref,
                         x1_ref, s_out_ref, k_out_ref, v_out_ref,
                         proj_ref, qe_ref, qn_ref, ke_ref, kn_ref, kw_ref, eb_ref, vb_ref, kbuf, vbuf, mixed_ref,
                         sbt_ref, xm_ref, *, nb, s):
    nkeys = WINDOW + s
    d = D_MODEL
    for bb in range(nb):
        m = mod_ref[bb]
        xm_ref[bb * s:(bb + 1) * s, :] = x_ref[bb] * (1.0 + m[:, d:2 * d]) + m[:, 0:d]
    proj_ref[...] = _mm(xm_ref[...].astype(BF16), win_ref[...])
    loga = _log_gates(proj_ref, a2w_ref, a2b_ref)
    _gla_prep(loga, tri_ref, proj_ref, s, qe_ref, qn_ref, ke_ref, kn_ref, kw_ref, eb_ref, vb_ref)

    mkk = mkk_ref[...]
    mv = mv_ref[...]
    ms = ms_ref[...]
    caus = caus_ref[...]
    gnw = gnw_ref[...]

    for bb in range(nb):
        r0 = bb * s
        rows = slice(r0, r0 + s)
        sbt_ref[...] = jnp.zeros_like(sbt_ref)
        for hh in range(GLA_HEADS):
            sbt_ref[hh * GLA_DV:(hh + 1) * GLA_DV, hh * GLA_DK:(hh + 1) * GLA_DK] = s0_ref[bb, hh].T
        kbuf[0:WINDOW, :] = kc_ref[bb].astype(BF16)
        vbuf[0:WINDOW, :] = vc_ref[bb].astype(BF16)
        kbuf[WINDOW:nkeys, :] = proj_ref[rows, C_SK:C_SK + LANE].astype(BF16)
        vbuf[WINDOW:nkeys, :] = proj_ref[rows, C_SV:C_SV + LANE].astype(BF16)
        sc = _swa_scores(proj_ref[rows, C_SQ:C_SQ + SWA_W], kbuf[...], s)
        vwin = vbuf[...]
        blocks = _swa_merge(_swa_finish(sc[0], vwin, bias_ref[0], sinks_ref, 0, None, s),
                            _swa_finish(sc[1], vwin, bias_ref[1], sinks_ref, 1, None, s), s)
        for i, blk in enumerate(blocks):
            mixed_ref[rows, GLA_VW + i * LANE:GLA_VW + (i + 1) * LANE] = blk
        o = _gla_chunk(r0, s, qe_ref, qn_ref, ke_ref, kn_ref, kw_ref, eb_ref, vb_ref, sbt_ref, mkk, mv, ms, caus)
        og = _gla_post(o, proj_ref[rows, C_GR:C_GR + GLA_VW], gnw)
        for hh in range(GLA_HEADS):
            mixed_ref[rows, hh * GLA_DV:(hh + 1) * GLA_DV] = og[hh]
            s_out_ref[bb, hh] = sbt_ref[hh * GLA_DV:(hh + 1) * GLA_DV, hh * GLA_DK:(hh + 1) * GLA_DK].T
        k_out_ref[bb, 0:WINDOW - s, :] = kc_ref[bb, s:WINDOW, :]
        v_out_ref[bb, 0:WINDOW - s, :] = vc_ref[bb, s:WINDOW, :]
        k_out_ref[bb, WINDOW - s:WINDOW, :] = proj_ref[rows, C_SK:C_SK + LANE]
        v_out_ref[bb, WINDOW - s:WINDOW, :] = proj_ref[rows, C_SV:C_SV + LANE]

    mix = _mm(mixed_ref[...], wo_ref[...])
    lng = lng_ref[...]
    lnb = lnb_ref[...]
    for bb in range(nb):
        m = mod_ref[bb]
        y = DEEPNORM_ALPHA * x_ref[bb] + m[:, 2 * d:3 * d] * mix[bb * s:(bb + 1) * s, :]
        x1_ref[bb] = _layer_norm(y, lng, lnb)


def _mixer_sample(x, mod, s0, kc, vc, sinks, win, a2w, a2b, gnw, wo, lng, lnb, *, nb=8):
    b, s, d = x.shape
    assert kc.shape[1] == WINDOW and s <= WINDOW
    rows = nb * s
    nkeys = WINDOW + s
    consts = _mixer_consts(rows, s, s, nkeys)
    const2 = lambda i, sk: (0, 0)
    const3 = lambda i, sk: (0, 0, 0)
    grid_spec = pltpu.PrefetchScalarGridSpec(
        num_scalar_prefetch=1,
        grid=(b // nb,),
        in_specs=[
            pl.BlockSpec((nb, s, d), lambda i, sk: (i, 0, 0)),
            pl.BlockSpec((nb, 1, 6 * d), lambda i, sk: (i, 0, 0)),
            pl.BlockSpec((nb, GLA_HEADS, GLA_DK, GLA_DV), lambda i, sk: (i, 0, 0, 0)),
            pl.BlockSpec((nb, WINDOW, LANE), lambda i, sk: (i, 0, 0)),
            pl.BlockSpec((nb, WINDOW, LANE), lambda i, sk: (i, 0, 0)),
            pl.BlockSpec((d, PROJ_W), const2),
            pl.BlockSpec((LANE, GLA_KW), const2),
            pl.BlockSpec((1, GLA_KW), const2),
            pl.BlockSpec((1, GLA_DV), const2),
            pl.BlockSpec((d, d), const2),
            pl.BlockSpec((1, d), const2),
            pl.BlockSpec((1, d), const2),
            pl.BlockSpec((rows, rows), const2),
            pl.BlockSpec((SWA_KV_HEADS, SWA_GROUP * s, nkeys), const3),
            pl.BlockSpec((GLA_HEADS * s, GLA_KW), const2),
            pl.BlockSpec((GLA_HEADS * s, GLA_VW), const2),
            pl.BlockSpec((GLA_VW, GLA_KW), const2),
            pl.BlockSpec((s, GLA_HEADS * s), const2),
        ],
        out_specs=[
            pl.BlockSpec((nb, s, d), lambda i, sk: (i, 0, 0)),
            pl.BlockSpec((nb, GLA_HEADS, GLA_DK, GLA_DV), lambda i, sk: (i, 0, 0, 0)),
            pl.BlockSpec((nb, WINDOW, LANE), lambda i, sk: (i, 0, 0)),
            pl.BlockSpec((nb, WINDOW, LANE), lambda i, sk: (i, 0, 0)),
        ],
        scratch_shapes=[
            pltpu.VMEM((rows, PROJ_W), F32),
            pltpu.VMEM((rows, GLA_KW), BF16),
            pltpu.VMEM((rows, GLA_KW), BF16),
            pltpu.VMEM((rows, GLA_KW), BF16),
            pltpu.VMEM((rows, GLA_KW), BF16),
            pltpu.VMEM((rows, GLA_KW), BF16),
            pltpu.VMEM((rows, GLA_KW), F32),
            pltpu.VMEM((rows, GLA_VW), BF16),
            pltpu.VMEM((nkeys, LANE), BF16),
            pltpu.VMEM((nkeys, LANE), BF16),
            pltpu.VMEM((rows, d), BF16),
            pltpu.VMEM((GLA_VW, GLA_KW), F32),
            pltpu.VMEM((rows, d), F32),
        ],
    )
    return pl.pallas_call(
        functools.partial(_mixer_sample_kernel, nb=nb, s=s),
        out_shape=[
            jax.ShapeDtypeStruct((b, s, d), F32),
            jax.ShapeDtypeStruct((b, GLA_HEADS, GLA_DK, GLA_DV), F32),
            jax.ShapeDtypeStruct((b, WINDOW, LANE), F32),
            jax.ShapeDtypeStruct((b, WINDOW, LANE), F32),
        ],
        grid_spec=grid_spec,
        compiler_params=pltpu.CompilerParams(dimension_semantics=("arbitrary",),
                                             vmem_limit_bytes=VMEM_LIMIT),
        name="mixer_sample",
    )(sinks, x, mod, s0, kc, vc, win, a2w, a2b, gnw, wo, lng, lnb, *consts)


MOE_BLK = 64


def _route_t(logits_t):
    t = logits_t.shape[1]
    row = lax.broadcasted_iota(jnp.int32, (EPG, t), 0).astype(F32)
    big = 99.0
    gl = jnp.where(row < N_GROUPS, logits_t[0:EPG, :], -jnp.inf)
    gmax = jnp.max(gl, axis=0, keepdims=True)
    grp = jnp.min(jnp.where(gl == gmax, row, big), axis=0, keepdims=True)
    p_grp = 1.0 / jnp.sum(jnp.exp(gl - gmax), axis=0, keepdims=True)
    el = jnp.zeros((EPG, t), F32)
    for g in range(N_GROUPS):
        el = el + jnp.where(grp == float(g), logits_t[R_EXP0 + EPG * g:R_EXP0 + EPG * (g + 1), :], 0.0)
    v1 = jnp.max(el, axis=0, keepdims=True)
    i1 = jnp.min(jnp.where(el == v1, row, big), axis=0, keepdims=True)
    el2 = jnp.where(row == i1, -jnp.inf, el)
    v2 = jnp.max(el2, axis=0, keepdims=True)
    i2 = jnp.min(jnp.where(el2 == v2, row, big), axis=0, keepdims=True)
    e2 = jnp.exp(v2 - v1)
    w1 = p_grp / (1.0 + e2)
    w2 = p_grp * e2 / (1.0 + e2)
    cw = jnp.where(row == i1, w1, 0.0) + jnp.where(row == i2, w2, 0.0)
    return grp, cw


MOE_PC = 256


def _moe_sort_kernel(x1_ref, mod_ref, wrt_ref, brt_ref, xs_ref, cws_ref, pos_ref, cnt_ref, *, nb, r):
    tm = nb * r
    tmp = tm + N_GROUPS * MOE_BLK
    d = D_MODEL
    mod = mod_ref[...]
    t3 = x1_ref[...] * (1.0 + mod[:, :, 4 * d:5 * d]) + mod[:, :, 3 * d:4 * d]
    t = t3.reshape(tm, d).astype(BF16)
    grp, cw = _route_t(_mm_nt(wrt_ref[...], t) + brt_ref[...])
    row = lax.broadcasted_iota(jnp.int32, (EPG, tm), 0).astype(F32)
    onehot_g = jnp.where(row == grp, 1.0, 0.0)
    nch = tm // LANE
    strict = (lax.broadcasted_iota(jnp.int32, (LANE, LANE), 0)
              < lax.broadcasted_iota(jnp.int32, (LANE, LANE), 1)).astype(F32).astype(BF16)
    stacked = jnp.concatenate([onehot_g[:, c * LANE:(c + 1) * LANE] for c in range(nch)], axis=0)
    pref = _mm(stacked.astype(BF16), strict)
    tot = jnp.sum(stacked, axis=1, keepdims=True)
    cnt = jnp.zeros((EPG, 1), F32)
    ranks = []
    for c in range(nch):
        ranks.append(pref[c * EPG:(c + 1) * EPG, :] + cnt)
        cnt = cnt + tot[c * EPG:(c + 1) * EPG, :]
    rank = jnp.concatenate(ranks, axis=1)
    padded = jnp.floor((cnt + (MOE_BLK - 1)) * (1.0 / MOE_BLK)) * MOE_BLK
    rowc = lax.broadcasted_iota(jnp.int32, (EPG, 1), 0)
    off = jnp.zeros((EPG, 1), F32)
    for gg in range(N_GROUPS - 1):
        off = off + jnp.where(rowc > gg, padded[gg:gg + 1, :], 0.0)
    pos = jnp.sum(onehot_g * (off + rank), axis=0, keepdims=True)
    pos_ref[0] = jnp.broadcast_to(pos, (EPG, tm))
    cnt_ref[0] = jnp.broadcast_to(cnt, (EPG, LANE))
    cw_hi, cw_lo = _split_bf16(jnp.concatenate([cw, jnp.zeros((LANE - EPG, tm), F32)], axis=0).T)
    t_aug = jnp.concatenate([t, cw_hi, cw_lo], axis=1)
    for c in range(tmp // MOE_PC):
        slot = (lax.broadcasted_iota(jnp.int32, (MOE_PC, tm), 0) + c * MOE_PC).astype(F32)
        perm = jnp.where(slot == pos, 1.0, 0.0).astype(BF16)
        moved = _mm(perm, t_aug)
        xs_ref[c * MOE_PC:(c + 1) * MOE_PC, :] = moved[:, 0:d].astype(BF16)
        cws_ref[c * MOE_PC:(c + 1) * MOE_PC, :] = moved[:, d:d + LANE] + moved[:, d + LANE:d + 2 * LANE]


MOE_BIG_UNITS = 4
MOE_BIG = MOE_BIG_UNITS * MOE_BLK
MOE_TAIL_UNITS = (2, 1)


def _moe_expert_kernel(boff_ref, nblk_ref, nused_ref, xs_ref, cws_ref, wg_ref, wu_ref, wd_ref, ys_hbm,
                       ybuf, ytail2, ytail1, zbuf, sem, st_ref, *, ntiles, tmp):
    q = pl.program_id(0)
    grp = q // ntiles
    tile = q - grp * ntiles
    off = boff_ref[q]
    n = nblk_ref[q]
    nbig = lax.shift_right_logical(n, 2)
    base = tile * tmp
    ytails = (ytail2, ytail1)
    wd = wd_ref[0].reshape(EPG * EXPERT_FF, D_MODEL)

    def experts(rows):
        xb = xs_ref[rows, :]
        cwb = cws_ref[rows, :]
        hs = []
        for e in range(EPG):
            gg_ = _mm(xb, wg_ref[0, e])
            uu = _mm(xb, wu_ref[0, e])
            hs.append((gg_ * _sigmoid(gg_) * uu * cwb[:, e:e + 1]).astype(BF16))
        return _mm(jnp.concatenate(hs, axis=1), wd).astype(BF16)

    def big_copy(slot, blk):
        return pltpu.make_async_copy(
            ybuf.at[slot], ys_hbm.at[pl.ds(pl.multiple_of(base + blk * MOE_BLK, MOE_BLK), MOE_BIG), :], sem.at[slot])

    def tail_copy(i, blk):
        rows = MOE_TAIL_UNITS[i] * MOE_BLK
        return pltpu.make_async_copy(
            ytails[i], ys_hbm.at[pl.ds(pl.multiple_of(base + blk * MOE_BLK, MOE_BLK), rows), :], sem.at[2 + i])

    def zero_copy(blk):
        return pltpu.make_async_copy(
            zbuf, ys_hbm.at[pl.ds(pl.multiple_of(base + blk * MOE_BLK, MOE_BLK), MOE_BLK), :], sem.at[4])

    @pl.when(q == 0)
    def _():
        for i in range(4):
            st_ref[i] = 0
        zbuf[...] = jnp.zeros_like(zbuf)

    def body(k, carry):
        c = st_ref[0]
        slot = jnp.bitwise_and(c, 1)
        blk = off + MOE_BIG_UNITS * k
        y = experts(pl.ds(pl.multiple_of(blk * MOE_BLK, MOE_BLK), MOE_BIG))

        @pl.when(c >= 2)
        def _():
            big_copy(slot, 0).wait()

        ybuf[slot] = y
        big_copy(slot, blk).start()
        st_ref[0] = c + 1
        return carry

    lax.fori_loop(0, nbig, body, 0)

    done = MOE_BIG_UNITS * nbig
    for i, units in enumerate(MOE_TAIL_UNITS):
        has = jnp.bitwise_and(n, units) != 0

        @pl.when(has)
        def _(i=i, units=units, blk=off + done):
            y = experts(pl.ds(pl.multiple_of(blk * MOE_BLK, MOE_BLK), units * MOE_BLK))

            @pl.when(st_ref[1 + i] == 1)
            def _():
                tail_copy(i, 0).wait()

            ytails[i][...] = y
            tail_copy(i, blk).start()
            st_ref[1 + i] = 1

        done = done + jnp.where(has, units, 0)

    @pl.when(grp == N_GROUPS - 1)
    def _():
        def zfill(blk, carry):
            zero_copy(blk).start()
            return carry

        nz = tmp // MOE_BLK - nused_ref[tile]
        lax.fori_loop(nused_ref[tile], tmp // MOE_BLK, zfill, 0)
        st_ref[3] = st_ref[3] + nz

    @pl.when(q == pl.num_programs(0) - 1)
    def _():
        c = st_ref[0]

        @pl.when(c >= 2)
        def _():
            big_copy(jnp.bitwise_and(c, 1), 0).wait()

        @pl.when(c >= 1)
        def _():
            big_copy(jnp.bitwise_and(c - 1, 1), 0).wait()

        for i in range(len(MOE_TAIL_UNITS)):
            @pl.when(st_ref[1 + i] == 1)
            def _(i=i):
                tail_copy(i, 0).wait()

        def zwait(i, carry):
            zero_copy(0).wait()
            return carry

        lax.fori_loop(0, st_ref[3], zwait, 0)


def _moe_unsort_kernel(ys_ref, pos_ref, x1_ref, mod_ref, lng_ref, lnb_ref, out_ref, *, nb, r):
    tm = nb * r
    tmp = tm + N_GROUPS * MOE_BLK
    d = D_MODEL
    ysb = ys_ref[...]
    posc = jnp.broadcast_to(pos_ref[0][0:1, :], (LANE, tm)).T
    lng = lng_ref[...]
    lnb = lnb_ref[...]
    for c in range(tm // MOE_PC):
        slot = lax.broadcasted_iota(jnp.int32, (MOE_PC, tmp), 1).astype(F32)
        unperm = jnp.where(slot == posc[c * MOE_PC:(c + 1) * MOE_PC, 0:1], 1.0, 0.0).astype(BF16)
        y = _mm(unperm, ysb)
        if nb == 1:
            x1c = x1_ref[0, c * MOE_PC:(c + 1) * MOE_PC, :]
            gt2 = mod_ref[0][:, 5 * d:6 * d]
            out_ref[0, c * MOE_PC:(c + 1) * MOE_PC, :] = _layer_norm(DEEPNORM_ALPHA * x1c + gt2 * y, lng, lnb)
        else:
            cb = MOE_PC // r
            x1c = x1_ref[c * cb:(c + 1) * cb]
            gt2 = mod_ref[c * cb:(c + 1) * cb][:, :, 5 * d:6 * d]
            yy = DEEPNORM_ALPHA * x1c + gt2 * y.reshape(cb, r, d)
            out_ref[c * cb:(c + 1) * cb] = _layer_norm(yy, lng, lnb)


def _moe(x1, mod, wrt, brt, wg, wu, wd, lng, lnb, *, nb, r):
    b, l, d = x1.shape
    tpb = l // r
    ntiles = (b // nb) * tpb
    tm = nb * r
    tmp = tm + N_GROUPS * MOE_BLK
    bpt = tmp // MOE_BLK
    xmap = lambda i: (i // tpb, i % tpb, 0)
    mmap = lambda i: (i // tpb, 0, 0)
    const2 = lambda i: (0, 0)
    params = pltpu.CompilerParams(dimension_semantics=("arbitrary",), vmem_limit_bytes=VMEM_LIMIT)

    xs, cws, pos, cnt = pl.pallas_call(
        functools.partial(_moe_sort_kernel, nb=nb, r=r),
        out_shape=[jax.ShapeDtypeStruct((ntiles * tmp, d), BF16),
                   jax.ShapeDtypeStruct((ntiles * tmp, LANE), F32),
                   jax.ShapeDtypeStruct((ntiles, EPG, tm), F32),
                   jax.ShapeDtypeStruct((ntiles, EPG, LANE), F32)],
        grid=(ntiles,),
        in_specs=[
            pl.BlockSpec((nb, r, d), xmap),
            pl.BlockSpec((nb, 1, 6 * d), mmap),
            pl.BlockSpec((LANE, d), const2),
            pl.BlockSpec((LANE, 1), const2),
        ],
        out_specs=[pl.BlockSpec((tmp, d), lambda i: (i, 0)),
                   pl.BlockSpec((tmp, LANE), lambda i: (i, 0)),
                   pl.BlockSpec((1, EPG, tm), lambda i: (i, 0, 0)),
                   pl.BlockSpec((1, EPG, LANE), lambda i: (i, 0, 0))],
        compiler_params=params,
        name="moe_sort",
    )(x1, mod, wrt, brt)

    nblk = ((cnt[:, :N_GROUPS, 0].astype(jnp.int32) + (MOE_BLK - 1)) // MOE_BLK)
    boff = jnp.cumsum(nblk, axis=1) - nblk
    nused = jnp.sum(nblk, axis=1).astype(jnp.int32)
    nblk_q = nblk.T.reshape(-1).astype(jnp.int32)
    boff_q = boff.T.reshape(-1).astype(jnp.int32)

    tmap = lambda q, bo, nk, nu: (q % ntiles, 0)
    wmap = lambda q, bo, nk, nu: (q // ntiles, 0, 0, 0)
    ys = pl.pallas_call(
        functools.partial(_moe_expert_kernel, ntiles=ntiles, tmp=tmp),
        out_shape=jax.ShapeDtypeStruct((ntiles * tmp, d), BF16),
        grid_spec=pltpu.PrefetchScalarGridSpec(
            num_scalar_prefetch=3,
            grid=(N_GROUPS * ntiles,),
            in_specs=[
                pl.BlockSpec((tmp, d), tmap),
                pl.BlockSpec((tmp, LANE), tmap),
                pl.BlockSpec((1, EPG, d, EXPERT_FF), wmap),
                pl.BlockSpec((1, EPG, d, EXPERT_FF), wmap),
                pl.BlockSpec((1, EPG, EXPERT_FF, d), wmap),
            ],
            out_specs=pl.BlockSpec(memory_space=pl.ANY),
            scratch_shapes=[
                pltpu.VMEM((2, MOE_BIG, d), BF16),
                pltpu.VMEM((MOE_TAIL_UNITS[0] * MOE_BLK, d), BF16),
                pltpu.VMEM((MOE_TAIL_UNITS[1] * MOE_BLK, d), BF16),
                pltpu.VMEM((MOE_BLK, d), BF16),
                pltpu.SemaphoreType.DMA((5,)),
                pltpu.SMEM((4,), jnp.int32),
            ],
        ),
        compiler_params=params,
        name="moe_experts",
    )(boff_q, nblk_q, nused, xs, cws, wg, wu, wd)

    return pl.pallas_call(
        functools.partial(_moe_unsort_kernel, nb=nb, r=r),
        out_shape=jax.ShapeDtypeStruct((b, l, d), F32),
        grid=(ntiles,),
        in_specs=[
            pl.BlockSpec((tmp, d), lambda i: (i, 0)),
            pl.BlockSpec((1, EPG, tm), lambda i: (i, 0, 0)),
            pl.BlockSpec((nb, r, d), xmap),
            pl.BlockSpec((nb, 1, 6 * d), mmap),
            pl.BlockSpec((1, d), const2),
            pl.BlockSpec((1, d), const2),
        ],
        out_specs=pl.BlockSpec((nb, r, d), xmap),
        compiler_params=params,
        name="moe_unsort",
    )(ys, pos, x1, mod, lng, lnb)


def kernel(x_prompt, x_sample, c_prompt, c_sample, state_gla, cache_swa_k, cache_swa_v, ada_w, ada_b, w_in,
           gla_a2_w, gla_a2_b, gla_norm_w, swa_sinks, w_o, ln1_g, ln1_b, router_g_w, router_g_b, router_e_w,
           router_e_b, moe_w_gate, moe_w_up, moe_w_down, ln2_g, ln2_b):
    assert ada_w.shape[0] == 1
    bp = x_prompt.shape[0]
    bs, ss, d = x_sample.shape
    lc = cache_swa_k.shape[2]

    w = w_in[0]
    zpad = jnp.zeros((d, LANE - GLA_RANK), F32)
    w_sq = w[:, 1552:2064].reshape(d, SWA_KV_HEADS, SWA_GROUP, SWA_DH).transpose(0, 2, 1, 3).reshape(d, SWA_W)
    win = jnp.concatenate([w[:, 0:1536], w_sq, w[:, 2064:2320], w[:, 1536:1552], zpad], axis=1).astype(BF16)
    a2w = jnp.concatenate([gla_a2_w[0], jnp.zeros((LANE - GLA_RANK, GLA_KW), F32)], axis=0).astype(BF16)
    a2b = gla_a2_b[0].reshape(1, GLA_KW)
    gnw = gla_norm_w[0].reshape(1, GLA_DV)
    wo_swa = w_o[0][GLA_VW:].reshape(SWA_KV_HEADS, SWA_GROUP, SWA_DH, d).transpose(1, 0, 2, 3).reshape(SWA_W, d)
    wo = jnp.concatenate([w_o[0][:GLA_VW], wo_swa], axis=0).astype(BF16)
    sinks = swa_sinks[0]
    wrt = jnp.concatenate([router_g_w[0], jnp.zeros((d, R_EXP0 - N_GROUPS), F32),
                           jnp.transpose(router_e_w[0], (1, 0, 2)).reshape(d, N_GROUPS * EPG),
                           jnp.zeros((d, LANE - R_EXP0 - N_GROUPS * EPG), F32)], axis=1).T.astype(BF16)
    brt = jnp.concatenate([router_g_b[0], jnp.zeros((R_EXP0 - N_GROUPS,), F32), router_e_b[0].reshape(-1),
                           jnp.zeros((LANE - R_EXP0 - N_GROUPS * EPG,), F32)]).reshape(LANE, 1)
    wg = moe_w_gate[0].astype(BF16)
    wu = moe_w_up[0].astype(BF16)
    wd = moe_w_down[0].astype(BF16)
    lng1, lnb1 = ln1_g[0].reshape(1, d), ln1_b[0].reshape(1, d)
    lng2, lnb2 = ln2_g[0].reshape(1, d), ln2_b[0].reshape(1, d)

    mod = _adaln(jnp.concatenate([c_prompt, c_sample], axis=0), ada_w[0], ada_b[0].reshape(1, 6 * d))
    mod = mod.reshape(bp + bs, 1, 6 * d)
    mod_p, mod_s = mod[:bp], mod[bp:]

    x1p, s_p, k_p, v_p = _mixer_prompt(x_prompt, mod_p, sinks, win, a2w, a2b, gnw, wo, lng1, lnb1)
    x1s, s_s, k_s, v_s = _mixer_sample(
        x_sample, mod_s, state_gla[0], cache_swa_k[0].reshape(bs, lc, LANE), cache_swa_v[0].reshape(bs, lc, LANE),
        sinks, win, a2w, a2b, gnw, wo, lng1, lnb1)

    yp = _moe(x1p, mod_p, wrt, brt, wg, wu, wd, lng2, lnb2, nb=1, r=1024)
    ys = _moe(x1s, mod_s, wrt, brt, wg, wu, wd, lng2, lnb2, nb=bs, r=ss)

    kv_shape_p = (1, bp, WINDOW, SWA_KV_HEADS, SWA_DH)
    kv_shape_s = (1, bs, lc, SWA_KV_HEADS, SWA_DH)
    return (yp, ys, s_p[None], k_p.reshape(kv_shape_p), v_p.reshape(kv_shape_p),
            s_s[None], k_s.reshape(kv_shape_s), v_s.reshape(kv_shape_s))
```

```python
import functools

import jax
import jax.numpy as jnp
import numpy as np
from jax import lax
from jax.experimental import pallas as pl
from jax.experimental.pallas import tpu as pltpu

F32 = jnp.float32
BF16 = jnp.bfloat16

D_MODEL = 1024
CHUNK = 64
GLA_HEADS = 4
GLA_DK = 64
GLA_DV = 128
GLA_KW = GLA_HEADS * GLA_DK
GLA_VW = GLA_HEADS * GLA_DV
GLA_RANK = 16
GLA_TAU = 16.0
SWA_Q_HEADS = 8
SWA_KV_HEADS = 2
SWA_GROUP = 4
SWA_DH = 64
SWA_W = SWA_Q_HEADS * SWA_DH
WINDOW = 128
N_GROUPS = 4
EPG = 8
EXPERT_FF = 256
DEEPNORM_ALPHA = 2.0 ** 0.25
LN_EPS = 1e-5
NEG_INF = -1e30

C_GQ, C_GK, C_GV, C_GR, C_SQ, C_SK, C_SV, C_GA = 0, 256, 512, 1024, 1536, 2048, 2176, 2304
PROJ_W = 2432
LANE = 128
R_EXP0 = 8

VMEM_LIMIT = 56 * 1024 * 1024


def _mm(a, b):
    return jnp.dot(a, b, preferred_element_type=F32)


def _mm_nt(a, b):
    return lax.dot_general(a, b, (((1,), (1,)), ((), ())), preferred_element_type=F32)


def _mm_tn(a, b):
    return lax.dot_general(a, b, (((0,), (0,)), ((), ())), preferred_element_type=F32)


def _split_bf16(a):
    hi = a.astype(BF16)
    lo = (a - hi.astype(F32)).astype(BF16)
    return hi, lo


def _sigmoid(x):
    return 1.0 / (1.0 + jnp.exp(-x))


def _layer_norm(y, g, b):
    mu = jnp.mean(y, axis=-1, keepdims=True)
    d = y - mu
    var = jnp.mean(d * d, axis=-1, keepdims=True)
    return d * lax.rsqrt(var + LN_EPS) * g + b


def _adaln_kernel(c_ref, w_ref, b_ref, o_ref):
    c = c_ref[...]
    a = c * _sigmoid(c)
    a_hi, a_lo = _split_bf16(a)
    w_hi, w_lo = _split_bf16(w_ref[...])
    o_ref[...] = _mm(a_hi, w_hi) + (_mm(a_hi, w_lo) + _mm(a_lo, w_hi)) + b_ref[...]


def _adaln(c_all, ada_w, ada_b):
    n = c_all.shape[0]
    bn = 1024
    return pl.pallas_call(
        _adaln_kernel,
        out_shape=jax.ShapeDtypeStruct((n, 6 * D_MODEL), F32),
        grid=(6 * D_MODEL // bn,),
        in_specs=[pl.BlockSpec((n, D_MODEL), lambda j: (0, 0)),
                  pl.BlockSpec((D_MODEL, bn), lambda j: (0, j)),
                  pl.BlockSpec((1, bn), lambda j: (0, j))],
        out_specs=pl.BlockSpec((n, bn), lambda j: (0, j)),
        compiler_params=pltpu.CompilerParams(dimension_semantics=("arbitrary",), vmem_limit_bytes=VMEM_LIMIT),
        name="adaln",
    )(c_all, ada_w, ada_b)


def _mixer_consts(nrows, chunk, qb, kw):
    r = np.arange(nrows)
    tri = ((r[:, None] // chunk == r[None, :] // chunk) & (r[:, None] >= r[None, :])).astype(np.float32)
    hs = np.arange(GLA_HEADS * chunk)
    mkk = (hs[:, None] // chunk == np.arange(GLA_KW)[None, :] // GLA_DK).astype(np.float32)
    mv = (hs[:, None] // chunk == np.arange(GLA_VW)[None, :] // GLA_DV).astype(np.float32)
    ms = (np.arange(GLA_VW)[:, None] // GLA_DV == np.arange(GLA_KW)[None, :] // GLA_DK).astype(np.float32)
    caus = (np.arange(chunk)[:, None] >= (hs[None, :] % chunk)).astype(np.float32)
    t = np.arange(qb)
    kj = np.arange(kw)
    cs = (t // chunk) * chunk
    kpos = kj[None, :] - WINDOW
    vis = (kpos >= cs[:, None] - WINDOW) & (kpos < cs[:, None] + chunk)
    dist = np.abs(t[:, None] + WINDOW - kj[None, :]).astype(np.float32)
    bias = np.zeros((SWA_KV_HEADS, SWA_GROUP * qb, kw), np.float32)
    for hk in range(SWA_KV_HEADS):
        for g in range(SWA_GROUP):
            slope = np.float32(2.0 ** (-(hk * SWA_GROUP + g + 1)))
            bias[hk, g * qb:(g + 1) * qb] = np.where(vis, -slope * dist, np.float32(2.0 * NEG_INF))
    return (jnp.asarray(tri, BF16), jnp.asarray(bias), jnp.asarray(mkk, BF16), jnp.asarray(mv, BF16),
            jnp.asarray(ms), jnp.asarray(caus))


def _modulate(x, mod):
    return (x * (1.0 + mod[:, D_MODEL:2 * D_MODEL]) + mod[:, 0:D_MODEL]).astype(BF16)


def _log_gates(proj_ref, a2w_ref, a2b_ref):
    ga = proj_ref[:, C_GA:C_GA + LANE].astype(BF16)
    z = _mm(ga, a2w_ref[...]) + a2b_ref[...]
    lsig = -(jnp.maximum(-z, 0.0) + jnp.log(1.0 + jnp.exp(-jnp.abs(z))))
    return lsig * (1.0 / GLA_TAU)


def _gla_prep(loga, tri_ref, proj_ref, chunk, qe_ref, qn_ref, ke_ref, kn_ref, kw_ref, eb_ref, vb_ref):
    nrows = loga.shape[0]
    la_hi, la_lo = _split_bf16(loga)
    tri = tri_ref[...]
    b = _mm(tri, la_hi) + _mm(tri, la_lo)
    b_end = jnp.concatenate(
        [jnp.broadcast_to(b[c * chunk + chunk - 1:(c + 1) * chunk, :], (chunk, GLA_KW))
         for c in range(nrows // chunk)], axis=0)
    eb = jnp.exp(b)
    ebn = jnp.exp(-b)
    wk = jnp.exp(b_end - b)
    q = proj_ref[:, C_GQ:C_GQ + GLA_KW] * (GLA_DK ** -0.5)
    k = proj_ref[:, C_GK:C_GK + GLA_KW]
    eb_ref[...] = eb
    qe_ref[...] = (q * eb).astype(BF16)
    qn_ref[...] = (q * ebn).astype(BF16)
    ke_ref[...] = (k * eb).astype(BF16)
    kn_ref[...] = (k * ebn).astype(BF16)
    kw_ref[...] = (k * wk).astype(BF16)
    vb_ref[...] = proj_ref[:, C_GV:C_GV + GLA_VW].astype(BF16)


def _gla_chunk(r0, chunk, qe_ref, qn_ref, ke_ref, kn_ref, kw_ref, eb_ref, vb_ref, sbt_ref, mkk, mv, ms, caus):
    rows = slice(r0, r0 + chunk)
    qe = qe_ref[rows, :]
    qn = qn_ref[rows, :]
    zero = jnp.zeros((), BF16)
    kn4 = jnp.where(mkk != 0, jnp.concatenate([kn_ref[rows, :]] * GLA_HEADS, axis=0), zero)
    ke4 = jnp.where(mkk != 0, jnp.concatenate([ke_ref[rows, :]] * GLA_HEADS, axis=0), zero)
    a_lo = _mm_nt(qe, kn4)
    a_up = _mm_nt(qn, ke4)
    a = jnp.where(caus != 0.0, a_lo, a_up).astype(BF16)
    v = vb_ref[rows, :]
    v4 = jnp.where(mv != 0, jnp.concatenate([v] * GLA_HEADS, axis=0), zero)
    sbt = sbt_ref[...]
    o = _mm(a, v4) + _mm_nt(qe, sbt.astype(BF16))
    ut = _mm_tn(v, kw_ref[rows, :])
    g_end = eb_ref[r0 + chunk - 1:r0 + chunk, :]
    sbt_ref[...] = g_end * sbt + jnp.where(ms != 0.0, ut, 0.0)
    return o


def _gla_post(o, gr, gnw):
    res = []
    for h in range(GLA_HEADS):
        oh = o[:, h * GLA_DV:(h + 1) * GLA_DV]
        ms_ = jnp.mean(oh * oh, axis=-1, keepdims=True)
        og = oh * lax.rsqrt(ms_ + LN_EPS) * gnw
        r = gr[:, h * GLA_DV:(h + 1) * GLA_DV]
        res.append((og * (r * _sigmoid(r))).astype(BF16))
    return res


def _swa_scores(sq, kwin, qb):
    low = lax.broadcasted_iota(jnp.int32, (qb, LANE), 1) < SWA_DH
    res = []
    for hk in range(SWA_KV_HEADS):
        keep = low if hk == 0 else jnp.logical_not(low)
        qs = jnp.concatenate([jnp.where(keep, sq[:, g * LANE:(g + 1) * LANE] * (SWA_DH ** -0.5), 0.0)
                              for g in range(SWA_GROUP)], axis=0).astype(BF16)
        res.append(_mm_nt(qs, kwin))
    return res


def _swa_finish(s, vwin, bias, sinks_ref, hk, extra_valid, qb):
    ok = bias > -1e29
    if extra_valid is not None:
        ok = ok & extra_valid
    s = jnp.where(ok, s + bias, NEG_INF)
    ps, denoms = [], []
    for g in range(SWA_GROUP):
        sg = s[g * qb:(g + 1) * qb, :]
        sink = sinks_ref[hk * SWA_GROUP + g]
        m = jnp.maximum(jnp.max(sg, axis=-1, keepdims=True), sink)
        pg = jnp.exp(sg - m)
        denoms.append(jnp.sum(pg, axis=-1, keepdims=True) + jnp.exp(sink - m))
        ps.append(pg.astype(BF16))
    o = _mm(jnp.concatenate(ps, axis=0), vwin)
    return o / jnp.concatenate(denoms, axis=0)


def _swa_merge(o0, o1, qb):
    low = lax.broadcasted_iota(jnp.int32, (qb, LANE), 1) < SWA_DH
    return [jnp.where(low, o0[g * qb:(g + 1) * qb, :], o1[g * qb:(g + 1) * qb, :]).astype(BF16)
            for g in range(SWA_GROUP)]


def _out_proj_ln(x, gt1, mixed_ref, wo_ref, ln_g, ln_b):
    mix = _mm(mixed_ref[...], wo_ref[...])
    return _layer_norm(DEEPNORM_ALPHA * x + gt1 * mix, ln_g, ln_b)


P_QB = 128
P_KW = WINDOW + P_QB


def _mixer_prompt_kernel(sinks_ref, x_ref, mod_ref, win_ref, a2w_ref, a2b_ref, gnw_ref, wo_ref, lng_ref, lnb_ref,
                         tri_ref, bias_ref, mkk_ref, mv_ref, ms_ref, caus_ref,
                         x1_ref, s_out_ref, k_out_ref, v_out_ref,
                         proj_ref, qe_ref, qn_ref, ke_ref, kn_ref, kw_ref, eb_ref, vb_ref, kbuf, vbuf, mixed_ref,
                         sbt_ref, *, tl):
    j = pl.program_id(1)
    nj = pl.num_programs(1)
    d = D_MODEL

    @pl.when(j == 0)
    def _():
        sbt_ref[...] = jnp.zeros_like(sbt_ref)
        kbuf[0:WINDOW, :] = jnp.zeros((WINDOW, LANE), BF16)
        vbuf[0:WINDOW, :] = jnp.zeros((WINDOW, LANE), BF16)

    x = x_ref[0]
    mod = mod_ref[0]
    proj_ref[...] = _mm(_modulate(x, mod), win_ref[...])
    loga = _log_gates(proj_ref, a2w_ref, a2b_ref)
    _gla_prep(loga, tri_ref, proj_ref, CHUNK, qe_ref, qn_ref, ke_ref, kn_ref, kw_ref, eb_ref, vb_ref)
    kbuf[WINDOW:WINDOW + tl, :] = proj_ref[:, C_SK:C_SK + LANE].astype(BF16)
    vbuf[WINDOW:WINDOW + tl, :] = proj_ref[:, C_SV:C_SV + LANE].astype(BF16)

    mkk = mkk_ref[...]
    mv = mv_ref[...]
    ms = ms_ref[...]
    caus = caus_ref[...]
    gnw = gnw_ref[...]
    kj = lax.broadcasted_iota(jnp.int32, (SWA_GROUP * P_QB, P_KW), 1)
    first_valid = kj >= jnp.where(j > 0, 0, WINDOW)

    def scores(p):
        q0 = p * P_QB
        return _swa_scores(proj_ref[q0:q0 + P_QB, C_SQ:C_SQ + SWA_W], kbuf[q0:q0 + P_KW, :], P_QB)

    def gla(r0):
        o = _gla_chunk(r0, CHUNK, qe_ref, qn_ref, ke_ref, kn_ref, kw_ref, eb_ref, vb_ref, sbt_ref, mkk, mv, ms, caus)
        og = _gla_post(o, proj_ref[r0:r0 + CHUNK, C_GR:C_GR + GLA_VW], gnw)
        for hh in range(GLA_HEADS):
            mixed_ref[r0:r0 + CHUNK, hh * GLA_DV:(hh + 1) * GLA_DV] = og[hh]

    for p in range(tl // P_QB):
        q0 = p * P_QB
        s = scores(p)
        vwin = vbuf[q0:q0 + P_KW, :]
        extra = first_valid if p == 0 else None
        o0 = _swa_finish(s[0], vwin, bias_ref[0], sinks_ref, 0, extra, P_QB)
        o1 = _swa_finish(s[1], vwin, bias_ref[1], sinks_ref, 1, extra, P_QB)
        for i, blk in enumerate(_swa_merge(o0, o1, P_QB)):
            mixed_ref[q0:q0 + P_QB, GLA_VW + i * LANE:GLA_VW + (i + 1) * LANE] = blk
        gla(q0)
        gla(q0 + CHUNK)

    x1_ref[0] = _out_proj_ln(x, mod[:, 2 * d:3 * d], mixed_ref, wo_ref, lng_ref[...], lnb_ref[...])

    kbuf[0:WINDOW, :] = kbuf[tl:tl + WINDOW, :]
    vbuf[0:WINDOW, :] = vbuf[tl:tl + WINDOW, :]

    @pl.when(j == nj - 1)
    def _():
        for hh in range(GLA_HEADS):
            s_out_ref[0, hh] = sbt_ref[hh * GLA_DV:(hh + 1) * GLA_DV, hh * GLA_DK:(hh + 1) * GLA_DK].T
        k_out_ref[0] = proj_ref[tl - WINDOW:tl, C_SK:C_SK + LANE]
        v_out_ref[0] = proj_ref[tl - WINDOW:tl, C_SV:C_SV + LANE]


def _mixer_prompt(x, mod, sinks, win, a2w, a2b, gnw, wo, lng, lnb, *, tl=512):
    b, l, d = x.shape
    nj = l // tl
    consts = _mixer_consts(tl, CHUNK, P_QB, P_KW)
    const2 = lambda i, j, s: (0, 0)
    const3 = lambda i, j, s: (0, 0, 0)
    grid_spec = pltpu.PrefetchScalarGridSpec(
        num_scalar_prefetch=1,
        grid=(b, nj),
        in_specs=[
            pl.BlockSpec((1, tl, d), lambda i, j, s: (i, j, 0)),
            pl.BlockSpec((1, 1, 6 * d), lambda i, j, s: (i, 0, 0)),
            pl.BlockSpec((d, PROJ_W), const2),
            pl.BlockSpec((LANE, GLA_KW), const2),
            pl.BlockSpec((1, GLA_KW), const2),
            pl.BlockSpec((1, GLA_DV), const2),
            pl.BlockSpec((d, d), const2),
            pl.BlockSpec((1, d), const2),
            pl.BlockSpec((1, d), const2),
            pl.BlockSpec((tl, tl), const2),
            pl.BlockSpec((SWA_KV_HEADS, SWA_GROUP * P_QB, P_KW), const3),
            pl.BlockSpec((GLA_HEADS * CHUNK, GLA_KW), const2),
            pl.BlockSpec((GLA_HEADS * CHUNK, GLA_VW), const2),
            pl.BlockSpec((GLA_VW, GLA_KW), const2),
            pl.BlockSpec((CHUNK, GLA_HEADS * CHUNK), const2),
        ],
        out_specs=[
            pl.BlockSpec((1, tl, d), lambda i, j, s: (i, j, 0)),
            pl.BlockSpec((1, GLA_HEADS, GLA_DK, GLA_DV), lambda i, j, s: (i, 0, 0, 0)),
            pl.BlockSpec((1, WINDOW, LANE), lambda i, j, s: (i, 0, 0)),
            pl.BlockSpec((1, WINDOW, LANE), lambda i, j, s: (i, 0, 0)),
        ],
        scratch_shapes=[
            pltpu.VMEM((tl, PROJ_W), F32),
            pltpu.VMEM((tl, GLA_KW), BF16),
            pltpu.VMEM((tl, GLA_KW), BF16),
            pltpu.VMEM((tl, GLA_KW), BF16),
            pltpu.VMEM((tl, GLA_KW), BF16),
            pltpu.VMEM((tl, GLA_KW), BF16),
            pltpu.VMEM((tl, GLA_KW), F32),
            pltpu.VMEM((tl, GLA_VW), BF16),
            pltpu.VMEM((WINDOW + tl, LANE), BF16),
            pltpu.VMEM((WINDOW + tl, LANE), BF16),
            pltpu.VMEM((tl, d), BF16),
            pltpu.VMEM((GLA_VW, GLA_KW), F32),
        ],
    )
    return pl.pallas_call(
        functools.partial(_mixer_prompt_kernel, tl=tl),
        out_shape=[
            jax.ShapeDtypeStruct((b, l, d), F32),
            jax.ShapeDtypeStruct((b, GLA_HEADS, GLA_DK, GLA_DV), F32),
            jax.ShapeDtypeStruct((b, WINDOW, LANE), F32),
            jax.ShapeDtypeStruct((b, WINDOW, LANE), F32),
        ],
        grid_spec=grid_spec,
        compiler_params=pltpu.CompilerParams(dimension_semantics=("arbitrary", "arbitrary"),
                                             vmem_limit_bytes=VMEM_LIMIT),
        name="mixer_prompt",
    )(sinks, x, mod, win, a2w, a2b, gnw, wo, lng, lnb, *consts)


def _mixer_sample_kernel(sinks_ref, x_ref, mod_ref, s0_ref, kc_ref, vc_ref, win_ref, a2w_ref, a2b_ref, gnw_ref,
                         wo_ref, lng_ref, lnb_ref, tri_ref, bias_ref, mkk_ref, mv_ref, ms_ref, caus_ref,
                         x1_ref, s_out_ref, k_out_ref, v_out_ref,
                         proj_ref, qe_ref, qn_ref, ke_ref, kn_ref, kw_ref, eb_ref, vb_ref, kbuf, vbuf, mixed_ref,
                         sbt_ref, xm_ref, *, nb, s):
    nkeys = WINDOW + s
    d = D_MODEL
    for bb in range(nb):
        m = mod_ref[bb]
        xm_ref[bb * s:(bb + 1) * s, :] = x_ref[bb] * (1.0 + m[:, d:2 * d]) + m[:, 0:d]
    proj_ref[...] = _mm(xm_ref[...].astype(BF16), win_ref[...])
    loga = _log_gates(proj_ref, a2w_ref, a2b_ref)
    _gla_prep(loga, tri_ref, proj_ref, s, qe_ref, qn_ref, ke_ref, kn_ref, kw_ref, eb_ref, vb_ref)

    mkk = mkk_ref[...]
    mv = mv_ref[...]
    ms = ms_ref[...]
    caus = caus_ref[...]
    gnw = gnw_ref[...]

    for bb in range(nb):
        r0 = bb * s
        rows = slice(r0, r0 + s)
        sbt_ref[...] = jnp.zeros_like(sbt_ref)
        for hh in range(GLA_HEADS):
            sbt_ref[hh * GLA_DV:(hh + 1) * GLA_DV, hh * GLA_DK:(hh + 1) * GLA_DK] = s0_ref[bb, hh].T
        kbuf[0:WINDOW, :] = kc_ref[bb].astype(BF16)
        vbuf[0:WINDOW, :] = vc_ref[bb].astype(BF16)
        kbuf[WINDOW:nkeys, :] = proj_ref[rows, C_SK:C_SK + LANE].astype(BF16)
        vbuf[WINDOW:nkeys, :] = proj_ref[rows, C_SV:C_SV + LANE].astype(BF16)
        sc = _swa_scores(proj_ref[rows, C_SQ:C_SQ + SWA_W], kbuf[...], s)
        vwin = vbuf[...]
        blocks = _swa_merge(_swa_finish(sc[0], vwin, bias_ref[0], sinks_ref, 0, None, s),
                            _swa_finish(sc[1], vwin, bias_ref[1], sinks_ref, 1, None, s), s)
        for i, blk in enumerate(blocks):
            mixed_ref[rows, GLA_VW + i * LANE:GLA_VW + (i + 1) * LANE] = blk
        o = _gla_chunk(r0, s, qe_ref, qn_ref, ke_ref, kn_ref, kw_ref, eb_ref, vb_ref, sbt_ref, mkk, mv, ms, caus)
        og = _gla_post(o, proj_ref[rows, C_GR:C_GR + GLA_VW], gnw)
        for hh in range(GLA_HEADS):
            mixed_ref[rows, hh * GLA_DV:(hh + 1) * GLA_DV] = og[hh]
            s_out_ref[bb, hh] = sbt_ref[hh * GLA_DV:(hh + 1) * GLA_DV, hh * GLA_DK:(hh + 1) * GLA_DK].T
        k_out_ref[bb, 0:WINDOW - s, :] = kc_ref[bb, s:WINDOW, :]
        v_out_ref[bb, 0:WINDOW - s, :] = vc_ref[bb, s:WINDOW, :]
        k_out_ref[bb, WINDOW - s:WINDOW, :] = proj_ref[rows, C_SK:C_SK + LANE]
        v_out_ref[bb, WINDOW - s:WINDOW, :] = proj_ref[rows, C_SV:C_SV + LANE]

    mix = _mm(mixed_ref[...], wo_ref[...])
    lng = lng_ref[...]
    lnb = lnb_ref[...]
    for bb in range(nb):
        m = mod_ref[bb]
        y = DEEPNORM_ALPHA * x_ref[bb] + m[:, 2 * d:3 * d] * mix[bb * s:(bb + 1) * s, :]
        x1_ref[bb] = _layer_norm(y, lng, lnb)


def _mixer_sample(x, mod, s0, kc, vc, sinks, win, a2w, a2b, gnw, wo, lng, lnb, *, nb=8):
    b, s, d = x.shape
    assert kc.shape[1] == WINDOW and s <= WINDOW
    rows = nb * s
    nkeys = WINDOW + s
    consts = _mixer_consts(rows, s, s, nkeys)
    const2 = lambda i, sk: (0, 0)
    const3 = lambda i, sk: (0, 0, 0)
    grid_spec = pltpu.PrefetchScalarGridSpec(
        num_scalar_prefetch=1,
        grid=(b // nb,),
        in_specs=[
            pl.BlockSpec((nb, s, d), lambda i, sk: (i, 0, 0)),
            pl.BlockSpec((nb, 1, 6 * d), lambda i, sk: (i, 0, 0)),
            pl.BlockSpec((nb, GLA_HEADS, GLA_DK, GLA_DV), lambda i, sk: (i, 0, 0, 0)),
            pl.BlockSpec((nb, WINDOW, LANE), lambda i, sk: (i, 0, 0)),
            pl.BlockSpec((nb, WINDOW, LANE), lambda i, sk: (i, 0, 0)),
            pl.BlockSpec((d, PROJ_W), const2),
            pl.BlockSpec((LANE, GLA_KW), const2),
            pl.BlockSpec((1, GLA_KW), const2),
            pl.BlockSpec((1, GLA_DV), const2),
            pl.BlockSpec((d, d), const2),
            pl.BlockSpec((1, d), const2),
            pl.BlockSpec((1, d), const2),
            pl.BlockSpec((rows, rows), const2),
            pl.BlockSpec((SWA_KV_HEADS, SWA_GROUP * s, nkeys), const3),
            pl.BlockSpec((GLA_HEADS * s, GLA_KW), const2),
            pl.BlockSpec((GLA_HEADS * s, GLA_VW), const2),
            pl.BlockSpec((GLA_VW, GLA_KW), const2),
            pl.BlockSpec((s, GLA_HEADS * s), const2),
        ],
        out_specs=[
            pl.BlockSpec((nb, s, d), lambda i, sk: (i, 0, 0)),
            pl.BlockSpec((nb, GLA_HEADS, GLA_DK, GLA_DV), lambda i, sk: (i, 0, 0, 0)),
            pl.BlockSpec((nb, WINDOW, LANE), lambda i, sk: (i, 0, 0)),
            pl.BlockSpec((nb, WINDOW, LANE), lambda i, sk: (i, 0, 0)),
        ],
        scratch_shapes=[
            pltpu.VMEM((rows, PROJ_W), F32),
            pltpu.VMEM((rows, GLA_KW), BF16),
            pltpu.VMEM((rows, GLA_KW), BF16),
            pltpu.VMEM((rows, GLA_KW), BF16),
            pltpu.VMEM((rows, GLA_KW), BF16),
            pltpu.VMEM((rows, GLA_KW), BF16),
            pltpu.VMEM((rows, GLA_KW), F32),
            pltpu.VMEM((rows, GLA_VW), BF16),
            pltpu.VMEM((nkeys, LANE), BF16),
            pltpu.VMEM((nkeys, LANE), BF16),
            pltpu.VMEM((rows, d), BF16),
            pltpu.VMEM((GLA_VW, GLA_KW), F32),
            pltpu.VMEM((rows, d), F32),
        ],
    )
    return pl.pallas_call(
        functools.partial(_mixer_sample_kernel, nb=nb, s=s),
        out_shape=[
            jax.ShapeDtypeStruct((b, s, d), F32),
            jax.ShapeDtypeStruct((b, GLA_HEADS, GLA_DK, GLA_DV), F32),
            jax.ShapeDtypeStruct((b, WINDOW, LANE), F32),
            jax.ShapeDtypeStruct((b, WINDOW, LANE), F32),
        ],
        grid_spec=grid_spec,
        compiler_params=pltpu.CompilerParams(dimension_semantics=("arbitrary",),
                                             vmem_limit_bytes=VMEM_LIMIT),
        name="mixer_sample",
    )(sinks, x, mod, s0, kc, vc, win, a2w, a2b, gnw, wo, lng, lnb, *consts)


MOE_BLK = 64


def _route_t(logits_t):
    t = logits_t.shape[1]
    row = lax.broadcasted_iota(jnp.int32, (EPG, t), 0).astype(F32)
    big = 99.0
    gl = jnp.where(row < N_GROUPS, logits_t[0:EPG, :], -jnp.inf)
    gmax = jnp.max(gl, axis=0, keepdims=True)
    grp = jnp.min(jnp.where(gl == gmax, row, big), axis=0, keepdims=True)
    p_grp = 1.0 / jnp.sum(jnp.exp(gl - gmax), axis=0, keepdims=True)
    el = jnp.zeros((EPG, t), F32)
    for g in range(N_GROUPS):
        el = el + jnp.where(grp == float(g), logits_t[R_EXP0 + EPG * g:R_EXP0 + EPG * (g + 1), :], 0.0)
    v1 = jnp.max(el, axis=0, keepdims=True)
    i1 = jnp.min(jnp.where(el == v1, row, big), axis=0, keepdims=True)
    el2 = jnp.where(row == i1, -jnp.inf, el)
    v2 = jnp.max(el2, axis=0, keepdims=True)
    i2 = jnp.min(jnp.where(el2 == v2, row, big), axis=0, keepdims=True)
    e2 = jnp.exp(v2 - v1)
    w1 = p_grp / (1.0 + e2)
    w2 = p_grp * e2 / (1.0 + e2)
    cw = jnp.where(row == i1, w1, 0.0) + jnp.where(row == i2, w2, 0.0)
    return grp, cw


MOE_PC = 256


def _moe_sort_kernel(x1_ref, mod_ref, wrt_ref, brt_ref, xs_ref, cws_ref, pos_ref, cnt_ref, *, nb, r):
    tm = nb * r
    tmp = tm + N_GROUPS * MOE_BLK
    d = D_MODEL
    mod = mod_ref[...]
    t3 = x1_ref[...] * (1.0 + mod[:, :, 4 * d:5 * d]) + mod[:, :, 3 * d:4 * d]
    t = t3.reshape(tm, d).astype(BF16)
    grp, cw = _route_t(_mm_nt(wrt_ref[...], t) + brt_ref[...])
    row = lax.broadcasted_iota(jnp.int32, (EPG, tm), 0).astype(F32)
    onehot_g = jnp.where(row == grp, 1.0, 0.0)
    nch = tm // LANE
    strict = (lax.broadcasted_iota(jnp.int32, (LANE, LANE), 0)
              < lax.broadcasted_iota(jnp.int32, (LANE, LANE), 1)).astype(F32).astype(BF16)
    stacked = jnp.concatenate([onehot_g[:, c * LANE:(c + 1) * LANE] for c in range(nch)], axis=0)
    pref = _mm(stacked.astype(BF16), strict)
    tot = jnp.sum(stacked, axis=1, keepdims=True)
    cnt = jnp.zeros((EPG, 1), F32)
    ranks = []
    for c in range(nch):
        ranks.append(pref[c * EPG:(c + 1) * EPG, :] + cnt)
        cnt = cnt + tot[c * EPG:(c + 1) * EPG, :]
    rank = jnp.concatenate(ranks, axis=1)
    padded = jnp.floor((cnt + (MOE_BLK - 1)) * (1.0 / MOE_BLK)) * MOE_BLK
    rowc = lax.broadcasted_iota(jnp.int32, (EPG, 1), 0)
    off = jnp.zeros((EPG, 1), F32)
    for gg in range(N_GROUPS - 1):
        off = off + jnp.where(rowc > gg, padded[gg:gg + 1, :], 0.0)
    pos = jnp.sum(onehot_g * (off + rank), axis=0, keepdims=True)
    pos_ref[0] = jnp.broadcast_to(pos, (EPG, tm))
    cnt_ref[0] = jnp.broadcast_to(cnt, (EPG, LANE))
    cw_hi, cw_lo = _split_bf16(jnp.concatenate([cw, jnp.zeros((LANE - EPG, tm), F32)], axis=0).T)
    t_aug = jnp.concatenate([t, cw_hi, cw_lo], axis=1)
    for c in range(tmp // MOE_PC):
        slot = (lax.broadcasted_iota(jnp.int32, (MOE_PC, tm), 0) + c * MOE_PC).astype(F32)
        perm = jnp.where(slot == pos, 1.0, 0.0).astype(BF16)
        moved = _mm(perm, t_aug)
        xs_ref[c * MOE_PC:(c + 1) * MOE_PC, :] = moved[:, 0:d].astype(BF16)
        cws_ref[c * MOE_PC:(c + 1) * MOE_PC, :] = moved[:, d:d + LANE] + moved[:, d + LANE:d + 2 * LANE]


MOE_BIG_UNITS = 4
MOE_BIG = MOE_BIG_UNITS * MOE_BLK
MOE_LAST_MAX = 2 * MOE_BIG_UNITS - 1


def _moe_expert_kernel(boff_ref, nblk_ref, nused_ref, xs_ref, cws_ref, wg_ref, wu_ref, wd_ref, ys_hbm,
                       ybuf, ylast, zbuf, sem, st_ref, *, ntiles, tmp):
    q = pl.program_id(0)
    grp = q // ntiles
    tile = q - grp * ntiles
    off = boff_ref[q]
    n = nblk_ref[q]
    nmain = jnp.maximum(lax.shift_right_logical(n, 2) - 1, 0)
    nlast = n - MOE_BIG_UNITS * nmain
    base = tile * tmp
    wd = wd_ref[0].reshape(EPG * EXPERT_FF, D_MODEL)

    def experts(rows):
        xb = xs_ref[rows, :]
        cwb = cws_ref[rows, :]
        hs = []
        for e in range(EPG):
            gg_ = _mm(xb, wg_ref[0, e])
            uu = _mm(xb, wu_ref[0, e])
            hs.append((gg_ * _sigmoid(gg_) * uu * cwb[:, e:e + 1]).astype(BF16))
        return _mm(jnp.concatenate(hs, axis=1), wd).astype(BF16)

    def big_copy(slot, blk):
        return pltpu.make_async_copy(
            ybuf.at[slot], ys_hbm.at[pl.ds(pl.multiple_of(base + blk * MOE_BLK, MOE_BLK), MOE_BIG), :], sem.at[slot])

    def last_copy(units, blk):
        rows = units * MOE_BLK
        return pltpu.make_async_copy(
            ylast.at[0:rows], ys_hbm.at[pl.ds(pl.multiple_of(base + blk * MOE_BLK, MOE_BLK), rows), :], sem.at[2])

    def wait_last():
        for u in range(1, MOE_LAST_MAX + 1):
            @pl.when(st_ref[1] == u)
            def _(u=u):
                last_copy(u, 0).wait()

    def zero_copy(blk):
        return pltpu.make_async_copy(
            zbuf, ys_hbm.at[pl.ds(pl.multiple_of(base + blk * MOE_BLK, MOE_BLK), MOE_BLK), :], sem.at[3])

    @pl.when(q == 0)
    def _():
        for i in range(3):
            st_ref[i] = 0
        zbuf[...] = jnp.zeros_like(zbuf)

    def body(k, carry):
        c = st_ref[0]
        slot = jnp.bitwise_and(c, 1)
        blk = off + MOE_BIG_UNITS * k
        y = experts(pl.ds(pl.multiple_of(blk * MOE_BLK, MOE_BLK), MOE_BIG))

        @pl.when(c >= 2)
        def _():
            big_copy(slot, 0).wait()

        ybuf[slot] = y
        big_copy(slot, blk).start()
        st_ref[0] = c + 1
        return carry

    lax.fori_loop(0, nmain, body, 0)

    @pl.when(nlast > 0)
    def _():
        wait_last()

    lblk = off + MOE_BIG_UNITS * nmain
    for u in range(1, MOE_LAST_MAX + 1):
        @pl.when(nlast == u)
        def _(u=u):
            ylast[0:u * MOE_BLK, :] = experts(pl.ds(pl.multiple_of(lblk * MOE_BLK, MOE_BLK), u * MOE_BLK))
            last_copy(u, lblk).start()
            st_ref[1] = u

    @pl.when(grp == N_GROUPS - 1)
    def _():
        def zfill(blk, carry):
            zero_copy(blk).start()
            return carry

        nz = tmp // MOE_BLK - nused_ref[tile]
        lax.fori_loop(nused_ref[tile], tmp // MOE_BLK, zfill, 0)
        st_ref[2] = st_ref[2] + nz

    @pl.when(q == pl.num_programs(0) - 1)
    def _():
        c = st_ref[0]

        @pl.when(c >= 2)
        def _():
            big_copy(jnp.bitwise_and(c, 1), 0).wait()

        @pl.when(c >= 1)
        def _():
            big_copy(jnp.bitwise_and(c - 1, 1), 0).wait()

        wait_last()

        def zwait(i, carry):
            zero_copy(0).wait()
            return carry

        lax.fori_loop(0, st_ref[2], zwait, 0)


def _moe_unsort_kernel(ys_ref, pos_ref, x1_ref, mod_ref, lng_ref, lnb_ref, out_ref, *, nb, r):
    tm = nb * r
    tmp = tm + N_GROUPS * MOE_BLK
    d = D_MODEL
    ysb = ys_ref[...]
    posc = jnp.broadcast_to(pos_ref[0][0:1, :], (LANE, tm)).T
    lng = lng_ref[...]
    lnb = lnb_ref[...]
    for c in range(tm // MOE_PC):
        slot = lax.broadcasted_iota(jnp.int32, (MOE_PC, tmp), 1).astype(F32)
        unperm = jnp.where(slot == posc[c * MOE_PC:(c + 1) * MOE_PC, 0:1], 1.0, 0.0).astype(BF16)
        y = _mm(unperm, ysb)
        if nb == 1:
            x1c = x1_ref[0, c * MOE_PC:(c + 1) * MOE_PC, :]
            gt2 = mod_ref[0][:, 5 * d:6 * d]
            out_ref[0, c * MOE_PC:(c + 1) * MOE_PC, :] = _layer_norm(DEEPNORM_ALPHA * x1c + gt2 * y, lng, lnb)
        else:
            cb = MOE_PC // r
            x1c = x1_ref[c * cb:(c + 1) * cb]
            gt2 = mod_ref[c * cb:(c + 1) * cb][:, :, 5 * d:6 * d]
            yy = DEEPNORM_ALPHA * x1c + gt2 * y.reshape(cb, r, d)
            out_ref[c * cb:(c + 1) * cb] = _layer_norm(yy, lng, lnb)


def _moe(x1, mod, wrt, brt, wg, wu, wd, lng, lnb, *, nb, r):
    b, l, d = x1.shape
    tpb = l // r
    ntiles = (b // nb) * tpb
    tm = nb * r
    tmp = tm + N_GROUPS * MOE_BLK
    bpt = tmp // MOE_BLK
    xmap = lambda i: (i // tpb, i % tpb, 0)
    mmap = lambda i: (i // tpb, 0, 0)
    const2 = lambda i: (0, 0)
    params = pltpu.CompilerParams(dimension_semantics=("arbitrary",), vmem_limit_bytes=VMEM_LIMIT)

    xs, cws, pos, cnt = pl.pallas_call(
        functools.partial(_moe_sort_kernel, nb=nb, r=r),
        out_shape=[jax.ShapeDtypeStruct((ntiles * tmp, d), BF16),
                   jax.ShapeDtypeStruct((ntiles * tmp, LANE), F32),
                   jax.ShapeDtypeStruct((ntiles, EPG, tm), F32),
                   jax.ShapeDtypeStruct((ntiles, EPG, LANE), F32)],
        grid=(ntiles,),
        in_specs=[
            pl.BlockSpec((nb, r, d), xmap),
            pl.BlockSpec((nb, 1, 6 * d), mmap),
            pl.BlockSpec((LANE, d), const2),
            pl.BlockSpec((LANE, 1), const2),
        ],
        out_specs=[pl.BlockSpec((tmp, d), lambda i: (i, 0)),
                   pl.BlockSpec((tmp, LANE), lambda i: (i, 0)),
                   pl.BlockSpec((1, EPG, tm), lambda i: (i, 0, 0)),
                   pl.BlockSpec((1, EPG, LANE), lambda i: (i, 0, 0))],
        compiler_params=params,
        name="moe_sort",
    )(x1, mod, wrt, brt)

    nblk = ((cnt[:, :N_GROUPS, 0].astype(jnp.int32) + (MOE_BLK - 1)) // MOE_BLK)
    boff = jnp.cumsum(nblk, axis=1) - nblk
    nused = jnp.sum(nblk, axis=1).astype(jnp.int32)
    nblk_q = nblk.T.reshape(-1).astype(jnp.int32)
    boff_q = boff.T.reshape(-1).astype(jnp.int32)

    tmap = lambda q, bo, nk, nu: (q % ntiles, 0)
    wmap = lambda q, bo, nk, nu: (q // ntiles, 0, 0, 0)
    ys = pl.pallas_call(
        functools.partial(_moe_expert_kernel, ntiles=ntiles, tmp=tmp),
        out_shape=jax.ShapeDtypeStruct((ntiles * tmp, d), BF16),
        grid_spec=pltpu.PrefetchScalarGridSpec(
            num_scalar_prefetch=3,
            grid=(N_GROUPS * ntiles,),
            in_specs=[
                pl.BlockSpec((tmp, d), tmap),
                pl.BlockSpec((tmp, LANE), tmap),
                pl.BlockSpec((1, EPG, d, EXPERT_FF), wmap),
                pl.BlockSpec((1, EPG, d, EXPERT_FF), wmap),
                pl.BlockSpec((1, EPG, EXPERT_FF, d), wmap),
            ],
            out_specs=pl.BlockSpec(memory_space=pl.ANY),
            scratch_shapes=[
                pltpu.VMEM((2, MOE_BIG, d), BF16),
                pltpu.VMEM((MOE_LAST_MAX * MOE_BLK, d), BF16),
                pltpu.VMEM((MOE_BLK, d), BF16),
                pltpu.SemaphoreType.DMA((4,)),
                pltpu.SMEM((4,), jnp.int32),
            ],
        ),
        compiler_params=params,
        name="moe_experts",
    )(boff_q, nblk_q, nused, xs, cws, wg, wu, wd)

    return pl.pallas_call(
        functools.partial(_moe_unsort_kernel, nb=nb, r=r),
        out_shape=jax.ShapeDtypeStruct((b, l, d), F32),
        grid=(ntiles,),
        in_specs=[
            pl.BlockSpec((tmp, d), lambda i: (i, 0)),
            pl.BlockSpec((1, EPG, tm), lambda i: (i, 0, 0)),
            pl.BlockSpec((nb, r, d), xmap),
            pl.BlockSpec((nb, 1, 6 * d), mmap),
            pl.BlockSpec((1, d), const2),
            pl.BlockSpec((1, d), const2),
        ],
        out_specs=pl.BlockSpec((nb, r, d), xmap),
        compiler_params=params,
        name="moe_unsort",
    )(ys, pos, x1, mod, lng, lnb)


def kernel(x_prompt, x_sample, c_prompt, c_sample, state_gla, cache_swa_k, cache_swa_v, ada_w, ada_b, w_in,
           gla_a2_w, gla_a2_b, gla_norm_w, swa_sinks, w_o, ln1_g, ln1_b, router_g_w, router_g_b, router_e_w,
           router_e_b, moe_w_gate, moe_w_up, moe_w_down, ln2_g, ln2_b):
    assert ada_w.shape[0] == 1
    bp = x_prompt.shape[0]
    bs, ss, d = x_sample.shape
    lc = cache_swa_k.shape[2]

    w = w_in[0]
    zpad = jnp.zeros((d, LANE - GLA_RANK), F32)
    w_sq = w[:, 1552:2064].reshape(d, SWA_KV_HEADS, SWA_GROUP, SWA_DH).transpose(0, 2, 1, 3).reshape(d, SWA_W)
    win = jnp.concatenate([w[:, 0:1536], w_sq, w[:, 2064:2320], w[:, 1536:1552], zpad], axis=1).astype(BF16)
    a2w = jnp.concatenate([gla_a2_w[0], jnp.zeros((LANE - GLA_RANK, GLA_KW), F32)], axis=0).astype(BF16)
    a2b = gla_a2_b[0].reshape(1, GLA_KW)
    gnw = gla_norm_w[0].reshape(1, GLA_DV)
    wo_swa = w_o[0][GLA_VW:].reshape(SWA_KV_HEADS, SWA_GROUP, SWA_DH, d).transpose(1, 0, 2, 3).reshape(SWA_W, d)
    wo = jnp.concatenate([w_o[0][:GLA_VW], wo_swa], axis=0).astype(BF16)
    sinks = swa_sinks[0]
    wrt = jnp.concatenate([router_g_w[0], jnp.zeros((d, R_EXP0 - N_GROUPS), F32),
                           jnp.transpose(router_e_w[0], (1, 0, 2)).reshape(d, N_GROUPS * EPG),
                           jnp.zeros((d, LANE - R_EXP0 - N_GROUPS * EPG), F32)], axis=1).T.astype(BF16)
    brt = jnp.concatenate([router_g_b[0], jnp.zeros((R_EXP0 - N_GROUPS,), F32), router_e_b[0].reshape(-1),
                           jnp.zeros((LANE - R_EXP0 - N_GROUPS * EPG,), F32)]).reshape(LANE, 1)
    wg = moe_w_gate[0].astype(BF16)
    wu = moe_w_up[0].astype(BF16)
    wd = moe_w_down[0].astype(BF16)
    lng1, lnb1 = ln1_g[0].reshape(1, d), ln1_b[0].reshape(1, d)
    lng2, lnb2 = ln2_g[0].reshape(1, d), ln2_b[0].reshape(1, d)

    mod = _adaln(jnp.concatenate([c_prompt, c_sample], axis=0), ada_w[0], ada_b[0].reshape(1, 6 * d))
    mod = mod.reshape(bp + bs, 1, 6 * d)
    mod_p, mod_s = mod[:bp], mod[bp:]

    x1p, s_p, k_p, v_p = _mixer_prompt(x_prompt, mod_p, sinks, win, a2w, a2b, gnw, wo, lng1, lnb1)
    x1s, s_s, k_s, v_s = _mixer_sample(
        x_sample, mod_s, state_gla[0], cache_swa_k[0].reshape(bs, lc, LANE), cache_swa_v[0].reshape(bs, lc, LANE),
        sinks, win, a2w, a2b, gnw, wo, lng1, lnb1)

    yp = _moe(x1p, mod_p, wrt, brt, wg, wu, wd, lng2, lnb2, nb=1, r=1024)
    ys = _moe(x1s, mod_s, wrt, brt, wg, wu, wd, lng2, lnb2, nb=bs, r=ss)

    kv_shape_p = (1, bp, WINDOW, SWA_KV_HEADS, SWA_DH)
    kv_shape_s = (1, bs, lc, SWA_KV_HEADS, SWA_DH)
    return (yp, ys, s_p[None], k_p.reshape(kv_shape_p), v_p.reshape(kv_shape_p),
            s_s[None], k_s.reshape(kv_shape_s), v_s.reshape(kv_shape_s))
```

```python
import functools

import jax
import jax.numpy as jnp
import numpy as np
from jax import lax
from jax.experimental import pallas as pl
from jax.experimental.pallas import tpu as pltpu

F32 = jnp.float32
BF16 = jnp.bfloat16

D_MODEL = 1024
CHUNK = 64
GLA_HEADS = 4
GLA_DK = 64
GLA_DV = 128
GLA_KW = GLA_HEADS * GLA_DK
GLA_VW = GLA_HEADS * GLA_DV
GLA_RANK = 16
GLA_TAU = 16.0
SWA_Q_HEADS = 8
SWA_KV_HEADS = 2
SWA_GROUP = 4
SWA_DH = 64
SWA_W = SWA_Q_HEADS * SWA_DH
WINDOW = 128
N_GROUPS = 4
EPG = 8
EXPERT_FF = 256
DEEPNORM_ALPHA = 2.0 ** 0.25
LN_EPS = 1e-5
NEG_INF = -1e30

C_GQ, C_GK, C_GV, C_GR, C_SQ, C_SK, C_SV, C_GA = 0, 256, 512, 1024, 1536, 2048, 2176, 2304
PROJ_W = 2432
LANE = 128
R_EXP0 = 8

VMEM_LIMIT = 56 * 1024 * 1024


def _mm(a, b):
    return jnp.dot(a, b, preferred_element_type=F32)


def _mm_nt(a, b):
    return lax.dot_general(a, b, (((1,), (1,)), ((), ())), preferred_element_type=F32)


def _mm_tn(a, b):
    return lax.dot_general(a, b, (((0,), (0,)), ((), ())), preferred_element_type=F32)


def _split_bf16(a):
    hi = a.astype(BF16)
    lo = (a - hi.astype(F32)).astype(BF16)
    return hi, lo


def _sigmoid(x):
    return 1.0 / (1.0 + jnp.exp(-x))


def _layer_norm(y, g, b):
    mu = jnp.mean(y, axis=-1, keepdims=True)
    d = y - mu
    var = jnp.mean(d * d, axis=-1, keepdims=True)
    return d * lax.rsqrt(var + LN_EPS) * g + b


def _adaln_kernel(c_ref, w_ref, b_ref, o_ref):
    c = c_ref[...]
    a = c * _sigmoid(c)
    a_hi, a_lo = _split_bf16(a)
    w_hi, w_lo = _split_bf16(w_ref[...])
    o_ref[...] = _mm(a_hi, w_hi) + (_mm(a_hi, w_lo) + _mm(a_lo, w_hi)) + b_ref[...]


def _adaln(c_all, ada_w, ada_b):
    n = c_all.shape[0]
    bn = 1024
    return pl.pallas_call(
        _adaln_kernel,
        out_shape=jax.ShapeDtypeStruct((n, 6 * D_MODEL), F32),
        grid=(6 * D_MODEL // bn,),
        in_specs=[pl.BlockSpec((n, D_MODEL), lambda j: (0, 0)),
                  pl.BlockSpec((D_MODEL, bn), lambda j: (0, j)),
                  pl.BlockSpec((1, bn), lambda j: (0, j))],
        out_specs=pl.BlockSpec((n, bn), lambda j: (0, j)),
        compiler_params=pltpu.CompilerParams(dimension_semantics=("arbitrary",), vmem_limit_bytes=VMEM_LIMIT),
        name="adaln",
    )(c_all, ada_w, ada_b)


def _mixer_consts(nrows, chunk, qb, kw):
    r = np.arange(nrows)
    tri = ((r[:, None] // chunk == r[None, :] // chunk) & (r[:, None] >= r[None, :])).astype(np.float32)
    hs = np.arange(GLA_HEADS * chunk)
    mkk = (hs[:, None] // chunk == np.arange(GLA_KW)[None, :] // GLA_DK).astype(np.float32)
    mv = (hs[:, None] // chunk == np.arange(GLA_VW)[None, :] // GLA_DV).astype(np.float32)
    ms = (np.arange(GLA_VW)[:, None] // GLA_DV == np.arange(GLA_KW)[None, :] // GLA_DK).astype(np.float32)
    caus = (np.arange(chunk)[:, None] >= (hs[None, :] % chunk)).astype(np.float32)
    t = np.arange(qb)
    kj = np.arange(kw)
    cs = (t // chunk) * chunk
    kpos = kj[None, :] - WINDOW
    vis = (kpos >= cs[:, None] - WINDOW) & (kpos < cs[:, None] + chunk)
    dist = np.abs(t[:, None] + WINDOW - kj[None, :]).astype(np.float32)
    bias = np.zeros((SWA_KV_HEADS, SWA_GROUP * qb, kw), np.float32)
    for hk in range(SWA_KV_HEADS):
        for g in range(SWA_GROUP):
            slope = np.float32(2.0 ** (-(hk * SWA_GROUP + g + 1)))
            bias[hk, g * qb:(g + 1) * qb] = np.where(vis, -slope * dist, np.float32(2.0 * NEG_INF))
    return (jnp.asarray(tri, BF16), jnp.asarray(bias), jnp.asarray(mkk, BF16), jnp.asarray(mv, BF16),
            jnp.asarray(ms), jnp.asarray(caus))


def _modulate(x, mod):
    return (x * (1.0 + mod[:, D_MODEL:2 * D_MODEL]) + mod[:, 0:D_MODEL]).astype(BF16)


def _log_gates(proj_ref, a2w_ref, a2b_ref):
    ga = proj_ref[:, C_GA:C_GA + LANE].astype(BF16)
    z = _mm(ga, a2w_ref[...]) + a2b_ref[...]
    lsig = -(jnp.maximum(-z, 0.0) + jnp.log(1.0 + jnp.exp(-jnp.abs(z))))
    return lsig * (1.0 / GLA_TAU)


def _gla_prep(loga, tri_ref, proj_ref, chunk, qe_ref, qn_ref, ke_ref, kn_ref, kw_ref, eb_ref, vb_ref):
    nrows = loga.shape[0]
    la_hi, la_lo = _split_bf16(loga)
    tri = tri_ref[...]
    b = _mm(tri, la_hi) + _mm(tri, la_lo)
    b_end = jnp.concatenate(
        [jnp.broadcast_to(b[c * chunk + chunk - 1:(c + 1) * chunk, :], (chunk, GLA_KW))
         for c in range(nrows // chunk)], axis=0)
    eb = jnp.exp(b)
    ebn = jnp.exp(-b)
    wk = jnp.exp(b_end - b)
    q = proj_ref[:, C_GQ:C_GQ + GLA_KW] * (GLA_DK ** -0.5)
    k = proj_ref[:, C_GK:C_GK + GLA_KW]
    eb_ref[...] = eb
    qe_ref[...] = (q * eb).astype(BF16)
    qn_ref[...] = (q * ebn).astype(BF16)
    ke_ref[...] = (k * eb).astype(BF16)
    kn_ref[...] = (k * ebn).astype(BF16)
    kw_ref[...] = (k * wk).astype(BF16)
    vb_ref[...] = proj_ref[:, C_GV:C_GV + GLA_VW].astype(BF16)


def _gla_chunk(r0, chunk, qe_ref, qn_ref, ke_ref, kn_ref, kw_ref, eb_ref, vb_ref, sbt_ref, mkk, mv, ms, caus):
    rows = slice(r0, r0 + chunk)
    qe = qe_ref[rows, :]
    qn = qn_ref[rows, :]
    zero = jnp.zeros((), BF16)
    kn4 = jnp.where(mkk != 0, jnp.concatenate([kn_ref[rows, :]] * GLA_HEADS, axis=0), zero)
    ke4 = jnp.where(mkk != 0, jnp.concatenate([ke_ref[rows, :]] * GLA_HEADS, axis=0), zero)
    a_lo = _mm_nt(qe, kn4)
    a_up = _mm_nt(qn, ke4)
    a = jnp.where(caus != 0.0, a_lo, a_up).astype(BF16)
    v = vb_ref[rows, :]
    v4 = jnp.where(mv != 0, jnp.concatenate([v] * GLA_HEADS, axis=0), zero)
    sbt = sbt_ref[...]
    o = _mm(a, v4) + _mm_nt(qe, sbt.astype(BF16))
    ut = _mm_tn(v, kw_ref[rows, :])
    g_end = eb_ref[r0 + chunk - 1:r0 + chunk, :]
    sbt_ref[...] = g_end * sbt + jnp.where(ms != 0.0, ut, 0.0)
    return o


def _gla_post(o, gr, gnw):
    res = []
    for h in range(GLA_HEADS):
        oh = o[:, h * GLA_DV:(h + 1) * GLA_DV]
        ms_ = jnp.mean(oh * oh, axis=-1, keepdims=True)
        og = oh * lax.rsqrt(ms_ + LN_EPS) * gnw
        r = gr[:, h * GLA_DV:(h + 1) * GLA_DV]
        res.append((og * (r * _sigmoid(r))).astype(BF16))
    return res


def _swa_scores(sq, kwin, qb):
    low = lax.broadcasted_iota(jnp.int32, (qb, LANE), 1) < SWA_DH
    res = []
    for hk in range(SWA_KV_HEADS):
        keep = low if hk == 0 else jnp.logical_not(low)
        qs = jnp.concatenate([jnp.where(keep, sq[:, g * LANE:(g + 1) * LANE] * (SWA_DH ** -0.5), 0.0)
                              for g in range(SWA_GROUP)], axis=0).astype(BF16)
        res.append(_mm_nt(qs, kwin))
    return res


def _swa_finish(s, vwin, bias, sinks_ref, hk, extra_valid, qb):
    ok = bias > -1e29
    if extra_valid is not None:
        ok = ok & extra_valid
    s = jnp.where(ok, s + bias, NEG_INF)
    ps, denoms = [], []
    for g in range(SWA_GROUP):
        sg = s[g * qb:(g + 1) * qb, :]
        sink = sinks_ref[hk * SWA_GROUP + g]
        m = jnp.maximum(jnp.max(sg, axis=-1, keepdims=True), sink)
        pg = jnp.exp(sg - m)
        denoms.append(jnp.sum(pg, axis=-1, keepdims=True) + jnp.exp(sink - m))
        ps.append(pg.astype(BF16))
    o = _mm(jnp.concatenate(ps, axis=0), vwin)
    return o / jnp.concatenate(denoms, axis=0)


def _swa_merge(o0, o1, qb):
    low = lax.broadcasted_iota(jnp.int32, (qb, LANE), 1) < SWA_DH
    return [jnp.where(low, o0[g * qb:(g + 1) * qb, :], o1[g * qb:(g + 1) * qb, :]).astype(BF16)
            for g in range(SWA_GROUP)]


def _out_proj_ln(x, gt1, mixed_ref, wo_ref, ln_g, ln_b):
    mix = _mm(mixed_ref[...], wo_ref[...])
    return _layer_norm(DEEPNORM_ALPHA * x + gt1 * mix, ln_g, ln_b)


P_QB = 128
P_KW = WINDOW + P_QB


def _mixer_prompt_kernel(sinks_ref, x_ref, mod_ref, win_ref, a2w_ref, a2b_ref, gnw_ref, wo_ref, lng_ref, lnb_ref,
                         tri_ref, bias_ref, mkk_ref, mv_ref, ms_ref, caus_ref,
                         x1_ref, s_out_ref, k_out_ref, v_out_ref,
                         proj_ref, qe_ref, qn_ref, ke_ref, kn_ref, kw_ref, eb_ref, vb_ref, kbuf, vbuf, mixed_ref,
                         sbt_ref, *, tl):
    j = pl.program_id(1)
    nj = pl.num_programs(1)
    d = D_MODEL

    @pl.when(j == 0)
    def _():
        sbt_ref[...] = jnp.zeros_like(sbt_ref)
        kbuf[0:WINDOW, :] = jnp.zeros((WINDOW, LANE), BF16)
        vbuf[0:WINDOW, :] = jnp.zeros((WINDOW, LANE), BF16)

    x = x_ref[0]
    mod = mod_ref[0]
    proj_ref[...] = _mm(_modulate(x, mod), win_ref[...])
    loga = _log_gates(proj_ref, a2w_ref, a2b_ref)
    _gla_prep(loga, tri_ref, proj_ref, CHUNK, qe_ref, qn_ref, ke_ref, kn_ref, kw_ref, eb_ref, vb_ref)
    kbuf[WINDOW:WINDOW + tl, :] = proj_ref[:, C_SK:C_SK + LANE].astype(BF16)
    vbuf[WINDOW:WINDOW + tl, :] = proj_ref[:, C_SV:C_SV + LANE].astype(BF16)

    mkk = mkk_ref[...]
    mv = mv_ref[...]
    ms = ms_ref[...]
    caus = caus_ref[...]
    gnw = gnw_ref[...]
    kj = lax.broadcasted_iota(jnp.int32, (SWA_GROUP * P_QB, P_KW), 1)
    first_valid = kj >= jnp.where(j > 0, 0, WINDOW)

    def scores(p):
        q0 = p * P_QB
        return _swa_scores(proj_ref[q0:q0 + P_QB, C_SQ:C_SQ + SWA_W], kbuf[q0:q0 + P_KW, :], P_QB)

    def gla(r0):
        o = _gla_chunk(r0, CHUNK, qe_ref, qn_ref, ke_ref, kn_ref, kw_ref, eb_ref, vb_ref, sbt_ref, mkk, mv, ms, caus)
        og = _gla_post(o, proj_ref[r0:r0 + CHUNK, C_GR:C_GR + GLA_VW], gnw)
        for hh in range(GLA_HEADS):
            mixed_ref[r0:r0 + CHUNK, hh * GLA_DV:(hh + 1) * GLA_DV] = og[hh]

    for p in range(tl // P_QB):
        q0 = p * P_QB
        s = scores(p)
        vwin = vbuf[q0:q0 + P_KW, :]
        extra = first_valid if p == 0 else None
        o0 = _swa_finish(s[0], vwin, bias_ref[0], sinks_ref, 0, extra, P_QB)
        o1 = _swa_finish(s[1], vwin, bias_ref[1], sinks_ref, 1, extra, P_QB)
        for i, blk in enumerate(_swa_merge(o0, o1, P_QB)):
            mixed_ref[q0:q0 + P_QB, GLA_VW + i * LANE:GLA_VW + (i + 1) * LANE] = blk
        gla(q0)
        gla(q0 + CHUNK)

    x1_ref[0] = _out_proj_ln(x, mod[:, 2 * d:3 * d], mixed_ref, wo_ref, lng_ref[...], lnb_ref[...])

    kbuf[0:WINDOW, :] = kbuf[tl:tl + WINDOW, :]
    vbuf[0:WINDOW, :] = vbuf[tl:tl + WINDOW, :]

    @pl.when(j == nj - 1)
    def _():
        for hh in range(GLA_HEADS):
            s_out_ref[0, hh] = sbt_ref[hh * GLA_DV:(hh + 1) * GLA_DV, hh * GLA_DK:(hh + 1) * GLA_DK].T
        k_out_ref[0] = proj_ref[tl - WINDOW:tl, C_SK:C_SK + LANE]
        v_out_ref[0] = proj_ref[tl - WINDOW:tl, C_SV:C_SV + LANE]


def _mixer_prompt(x, mod, sinks, win, a2w, a2b, gnw, wo, lng, lnb, *, tl=512):
    b, l, d = x.shape
    nj = l // tl
    consts = _mixer_consts(tl, CHUNK, P_QB, P_KW)
    const2 = lambda i, j, s: (0, 0)
    const3 = lambda i, j, s: (0, 0, 0)
    grid_spec = pltpu.PrefetchScalarGridSpec(
        num_scalar_prefetch=1,
        grid=(b, nj),
        in_specs=[
            pl.BlockSpec((1, tl, d), lambda i, j, s: (i, j, 0)),
            pl.BlockSpec((1, 1, 6 * d), lambda i, j, s: (i, 0, 0)),
            pl.BlockSpec((d, PROJ_W), const2),
            pl.BlockSpec((LANE, GLA_KW), const2),
            pl.BlockSpec((1, GLA_KW), const2),
            pl.BlockSpec((1, GLA_DV), const2),
            pl.BlockSpec((d, d), const2),
            pl.BlockSpec((1, d), const2),
            pl.BlockSpec((1, d), const2),
            pl.BlockSpec((tl, tl), const2),
            pl.BlockSpec((SWA_KV_HEADS, SWA_GROUP * P_QB, P_KW), const3),
            pl.BlockSpec((GLA_HEADS * CHUNK, GLA_KW), const2),
            pl.BlockSpec((GLA_HEADS * CHUNK, GLA_VW), const2),
            pl.BlockSpec((GLA_VW, GLA_KW), const2),
            pl.BlockSpec((CHUNK, GLA_HEADS * CHUNK), const2),
        ],
        out_specs=[
            pl.BlockSpec((1, tl, d), lambda i, j, s: (i, j, 0)),
            pl.BlockSpec((1, GLA_HEADS, GLA_DK, GLA_DV), lambda i, j, s: (i, 0, 0, 0)),
            pl.BlockSpec((1, WINDOW, LANE), lambda i, j, s: (i, 0, 0)),
            pl.BlockSpec((1, WINDOW, LANE), lambda i, j, s: (i, 0, 0)),
        ],
        scratch_shapes=[
            pltpu.VMEM((tl, PROJ_W), F32),
            pltpu.VMEM((tl, GLA_KW), BF16),
            pltpu.VMEM((tl, GLA_KW), BF16),
            pltpu.VMEM((tl, GLA_KW), BF16),
            pltpu.VMEM((tl, GLA_KW), BF16),
            pltpu.VMEM((tl, GLA_KW), BF16),
            pltpu.VMEM((tl, GLA_KW), F32),
            pltpu.VMEM((tl, GLA_VW), BF16),
            pltpu.VMEM((WINDOW + tl, LANE), BF16),
            pltpu.VMEM((WINDOW + tl, LANE), BF16),
            pltpu.VMEM((tl, d), BF16),
            pltpu.VMEM((GLA_VW, GLA_KW), F32),
        ],
    )
    return pl.pallas_call(
        functools.partial(_mixer_prompt_kernel, tl=tl),
        out_shape=[
            jax.ShapeDtypeStruct((b, l, d), F32),
            jax.ShapeDtypeStruct((b, GLA_HEADS, GLA_DK, GLA_DV), F32),
            jax.ShapeDtypeStruct((b, WINDOW, LANE), F32),
            jax.ShapeDtypeStruct((b, WINDOW, LANE), F32),
        ],
        grid_spec=grid_spec,
        compiler_params=pltpu.CompilerParams(dimension_semantics=("arbitrary", "arbitrary"),
                                             vmem_limit_bytes=VMEM_LIMIT),
        name="mixer_prompt",
    )(sinks, x, mod, win, a2w, a2b, gnw, wo, lng, lnb, *consts)


def _mixer_sample_kernel(sinks_ref, x_ref, mod_ref, s0_ref, kc_ref, vc_ref, win_ref, a2w_ref, a2b_ref, gnw_ref,
                         wo_ref, lng_ref, lnb_ref, tri_ref, bias_ref, mkk_ref, mv_ref, ms_ref, caus_ref,
                         x1_ref, s_out_ref, k_out_ref, v_out_ref,
                         proj_ref, qe_ref, qn_ref, ke_ref, kn_ref, kw_ref, eb_ref, vb_ref, kbuf, vbuf, mixed_ref,
                         sbt_ref, xm_ref, *, nb, s):
    nkeys = WINDOW + s
    d = D_MODEL
    for bb in range(nb):
        m = mod_ref[bb]
        xm_ref[bb * s:(bb + 1) * s, :] = x_ref[bb] * (1.0 + m[:, d:2 * d]) + m[:, 0:d]
    proj_ref[...] = _mm(xm_ref[...].astype(BF16), win_ref[...])
    loga = _log_gates(proj_ref, a2w_ref, a2b_ref)
    _gla_prep(loga, tri_ref, proj_ref, s, qe_ref, qn_ref, ke_ref, kn_ref, kw_ref, eb_ref, vb_ref)

    mkk = mkk_ref[...]
    mv = mv_ref[...]
    ms = ms_ref[...]
    caus = caus_ref[...]
    gnw = gnw_ref[...]

    for bb in range(nb):
        r0 = bb * s
        rows = slice(r0, r0 + s)
        sbt_ref[...] = jnp.zeros_like(sbt_ref)
        for hh in range(GLA_HEADS):
            sbt_ref[hh * GLA_DV:(hh + 1) * GLA_DV, hh * GLA_DK:(hh + 1) * GLA_DK] = s0_ref[bb, hh].T
        kbuf[0:WINDOW, :] = kc_ref[bb].astype(BF16)
        vbuf[0:WINDOW, :] = vc_ref[bb].astype(BF16)
        kbuf[WINDOW:nkeys, :] = proj_ref[rows, C_SK:C_SK + LANE].astype(BF16)
        vbuf[WINDOW:nkeys, :] = proj_ref[rows, C_SV:C_SV + LANE].astype(BF16)
        sc = _swa_scores(proj_ref[rows, C_SQ:C_SQ + SWA_W], kbuf[...], s)
        vwin = vbuf[...]
        blocks = _swa_merge(_swa_finish(sc[0], vwin, bias_ref[0], sinks_ref, 0, None, s),
                            _swa_finish(sc[1], vwin, bias_ref[1], sinks_ref, 1, None, s), s)
        for i, blk in enumerate(blocks):
            mixed_ref[rows, GLA_VW + i * LANE:GLA_VW + (i + 1) * LANE] = blk
        o = _gla_chunk(r0, s, qe_ref, qn_ref, ke_ref, kn_ref, kw_ref, eb_ref, vb_ref, sbt_ref, mkk, mv, ms, caus)
        og = _gla_post(o, proj_ref[rows, C_GR:C_GR + GLA_VW], gnw)
        for hh in range(GLA_HEADS):
            mixed_ref[rows, hh * GLA_DV:(hh + 1) * GLA_DV] = og[hh]
            s_out_ref[bb, hh] = sbt_ref[hh * GLA_DV:(hh + 1) * GLA_DV, hh * GLA_DK:(hh + 1) * GLA_DK].T
        k_out_ref[bb, 0:WINDOW - s, :] = kc_ref[bb, s:WINDOW, :]
        v_out_ref[bb, 0:WINDOW - s, :] = vc_ref[bb, s:WINDOW, :]
        k_out_ref[bb, WINDOW - s:WINDOW, :] = proj_ref[rows, C_SK:C_SK + LANE]
        v_out_ref[bb, WINDOW - s:WINDOW, :] = proj_ref[rows, C_SV:C_SV + LANE]

    mix = _mm(mixed_ref[...], wo_ref[...])
    lng = lng_ref[...]
    lnb = lnb_ref[...]
    for bb in range(nb):
        m = mod_ref[bb]
        y = DEEPNORM_ALPHA * x_ref[bb] + m[:, 2 * d:3 * d] * mix[bb * s:(bb + 1) * s, :]
        x1_ref[bb] = _layer_norm(y, lng, lnb)


def _mixer_sample(x, mod, s0, kc, vc, sinks, win, a2w, a2b, gnw, wo, lng, lnb, *, nb=8):
    b, s, d = x.shape
    assert kc.shape[1] == WINDOW and s <= WINDOW
    rows = nb * s
    nkeys = WINDOW + s
    consts = _mixer_consts(rows, s, s, nkeys)
    const2 = lambda i, sk: (0, 0)
    const3 = lambda i, sk: (0, 0, 0)
    grid_spec = pltpu.PrefetchScalarGridSpec(
        num_scalar_prefetch=1,
        grid=(b // nb,),
        in_specs=[
            pl.BlockSpec((nb, s, d), lambda i, sk: (i, 0, 0)),
            pl.BlockSpec((nb, 1, 6 * d), lambda i, sk: (i, 0, 0)),
            pl.BlockSpec((nb, GLA_HEADS, GLA_DK, GLA_DV), lambda i, sk: (i, 0, 0, 0)),
            pl.BlockSpec((nb, WINDOW, LANE), lambda i, sk: (i, 0, 0)),
            pl.BlockSpec((nb, WINDOW, LANE), lambda i, sk: (i, 0, 0)),
            pl.BlockSpec((d, PROJ_W), const2),
            pl.BlockSpec((LANE, GLA_KW), const2),
            pl.BlockSpec((1, GLA_KW), const2),
            pl.BlockSpec((1, GLA_DV), const2),
            pl.BlockSpec((d, d), const2),
            pl.BlockSpec((1, d), const2),
            pl.BlockSpec((1, d), const2),
            pl.BlockSpec((rows, rows), const2),
            pl.BlockSpec((SWA_KV_HEADS, SWA_GROUP * s, nkeys), const3),
            pl.BlockSpec((GLA_HEADS * s, GLA_KW), const2),
            pl.BlockSpec((GLA_HEADS * s, GLA_VW), const2),
            pl.BlockSpec((GLA_VW, GLA_KW), const2),
            pl.BlockSpec((s, GLA_HEADS * s), const2),
        ],
        out_specs=[
            pl.BlockSpec((nb, s, d), lambda i, sk: (i, 0, 0)),
            pl.BlockSpec((nb, GLA_HEADS, GLA_DK, GLA_DV), lambda i, sk: (i, 0, 0, 0)),
            pl.BlockSpec((nb, WINDOW, LANE), lambda i, sk: (i, 0, 0)),
            pl.BlockSpec((nb, WINDOW, LANE), lambda i, sk: (i, 0, 0)),
        ],
        scratch_shapes=[
            pltpu.VMEM((rows, PROJ_W), F32),
            pltpu.VMEM((rows, GLA_KW), BF16),
            pltpu.VMEM((rows, GLA_KW), BF16),
            pltpu.VMEM((rows, GLA_KW), BF16),
            pltpu.VMEM((rows, GLA_KW), BF16),
            pltpu.VMEM((rows, GLA_KW), BF16),
            pltpu.VMEM((rows, GLA_KW), F32),
            pltpu.VMEM((rows, GLA_VW), BF16),
            pltpu.VMEM((nkeys, LANE), BF16),
            pltpu.VMEM((nkeys, LANE), BF16),
            pltpu.VMEM((rows, d), BF16),
            pltpu.VMEM((GLA_VW, GLA_KW), F32),
            pltpu.VMEM((rows, d), F32),
        ],
    )
    return pl.pallas_call(
        functools.partial(_mixer_sample_kernel, nb=nb, s=s),
        out_shape=[
            jax.ShapeDtypeStruct((b, s, d), F32),
            jax.ShapeDtypeStruct((b, GLA_HEADS, GLA_DK, GLA_DV), F32),
            jax.ShapeDtypeStruct((b, WINDOW, LANE), F32),
            jax.ShapeDtypeStruct((b, WINDOW, LANE), F32),
        ],
        grid_spec=grid_spec,
        compiler_params=pltpu.CompilerParams(dimension_semantics=("arbitrary",),
                                             vmem_limit_bytes=VMEM_LIMIT),
        name="mixer_sample",
    )(sinks, x, mod, s0, kc, vc, win, a2w, a2b, gnw, wo, lng, lnb, *consts)


MOE_BLK = 64


def _route_t(logits_t):
    t = logits_t.shape[1]
    row = lax.broadcasted_iota(jnp.int32, (EPG, t), 0).astype(F32)
    big = 99.0
    gl = jnp.where(row < N_GROUPS, logits_t[0:EPG, :], -jnp.inf)
    gmax = jnp.max(gl, axis=0, keepdims=True)
    grp = jnp.min(jnp.where(gl == gmax, row, big), axis=0, keepdims=True)
    p_grp = 1.0 / jnp.sum(jnp.exp(gl - gmax), axis=0, keepdims=True)
    el = jnp.zeros((EPG, t), F32)
    for g in range(N_GROUPS):
        el = el + jnp.where(grp == float(g), logits_t[R_EXP0 + EPG * g:R_EXP0 + EPG * (g + 1), :], 0.0)
    v1 = jnp.max(el, axis=0, keepdims=True)
    i1 = jnp.min(jnp.where(el == v1, row, big), axis=0, keepdims=True)
    el2 = jnp.where(row == i1, -jnp.inf, el)
    v2 = jnp.max(el2, axis=0, keepdims=True)
    i2 = jnp.min(jnp.where(el2 == v2, row, big), axis=0, keepdims=True)
    e2 = jnp.exp(v2 - v1)
    w1 = p_grp / (1.0 + e2)
    w2 = p_grp * e2 / (1.0 + e2)
    cw = jnp.where(row == i1, w1, 0.0) + jnp.where(row == i2, w2, 0.0)
    return grp, cw


MOE_PC = 256


def _moe_sort_kernel(x1_ref, mod_ref, wrt_ref, brt_ref, *refs, nb, r, nreal):
    outs = refs[-4:]

    @pl.when(pl.program_id(0) < nreal)
    def _():
        _moe_sort_tile(x1_ref, mod_ref, wrt_ref, brt_ref, *outs, nb=nb, r=r)

    @pl.when(pl.program_id(0) >= nreal)
    def _():
        for o in outs:
            o[...] = jnp.zeros_like(o)


def _moe_sort_tile(x1_ref, mod_ref, wrt_ref, brt_ref, xs_ref, cws_ref, pos_ref, cnt_ref, *, nb, r):
    tm = nb * r
    tmp = tm + N_GROUPS * MOE_BLK
    d = D_MODEL
    mod = mod_ref[...]
    t3 = x1_ref[...] * (1.0 + mod[:, :, 4 * d:5 * d]) + mod[:, :, 3 * d:4 * d]
    t = t3.reshape(tm, d).astype(BF16)
    grp, cw = _route_t(_mm_nt(wrt_ref[...], t) + brt_ref[...])
    row = lax.broadcasted_iota(jnp.int32, (EPG, tm), 0).astype(F32)
    onehot_g = jnp.where(row == grp, 1.0, 0.0)
    nch = tm // LANE
    strict = (lax.broadcasted_iota(jnp.int32, (LANE, LANE), 0)
              < lax.broadcasted_iota(jnp.int32, (LANE, LANE), 1)).astype(F32).astype(BF16)
    stacked = jnp.concatenate([onehot_g[:, c * LANE:(c + 1) * LANE] for c in range(nch)], axis=0)
    pref = _mm(stacked.astype(BF16), strict)
    tot = jnp.sum(stacked, axis=1, keepdims=True)
    cnt = jnp.zeros((EPG, 1), F32)
    ranks = []
    for c in range(nch):
        ranks.append(pref[c * EPG:(c + 1) * EPG, :] + cnt)
        cnt = cnt + tot[c * EPG:(c + 1) * EPG, :]
    rank = jnp.concatenate(ranks, axis=1)
    padded = jnp.floor((cnt + (MOE_BLK - 1)) * (1.0 / MOE_BLK)) * MOE_BLK
    rowc = lax.broadcasted_iota(jnp.int32, (EPG, 1), 0)
    off = jnp.zeros((EPG, 1), F32)
    for gg in range(N_GROUPS - 1):
        off = off + jnp.where(rowc > gg, padded[gg:gg + 1, :], 0.0)
    pos = jnp.sum(onehot_g * (off + rank), axis=0, keepdims=True)
    pos_ref[0] = jnp.broadcast_to(pos, (EPG, tm))
    cnt_ref[0] = jnp.broadcast_to(cnt, (EPG, LANE))
    cw_hi, cw_lo = _split_bf16(jnp.concatenate([cw, jnp.zeros((LANE - EPG, tm), F32)], axis=0).T)
    t_aug = jnp.concatenate([t, cw_hi, cw_lo], axis=1)
    for c in range(tmp // MOE_PC):
        slot = (lax.broadcasted_iota(jnp.int32, (MOE_PC, tm), 0) + c * MOE_PC).astype(F32)
        perm = jnp.where(slot == pos, 1.0, 0.0).astype(BF16)
        moved = _mm(perm, t_aug)
        xs_ref[c * MOE_PC:(c + 1) * MOE_PC, :] = moved[:, 0:d].astype(BF16)
        cws_ref[c * MOE_PC:(c + 1) * MOE_PC, :] = moved[:, d:d + LANE] + moved[:, d + LANE:d + 2 * LANE]


MOE_BIG_UNITS = 4
MOE_BIG = MOE_BIG_UNITS * MOE_BLK
MOE_LAST_MAX = 2 * MOE_BIG_UNITS - 1


def _moe_expert_kernel(boff_ref, nblk_ref, nused_ref, xs_ref, cws_ref, wg_ref, wu_ref, wd_ref, ys_hbm,
                       ybuf, ylast, zbuf, sem, st_ref, *, ntiles, tmp):
    q = pl.program_id(0)
    grp = q // ntiles
    tile = q - grp * ntiles
    off = boff_ref[q]
    n = nblk_ref[q]
    nmain = jnp.maximum(lax.shift_right_logical(n, 2) - 1, 0)
    nlast = n - MOE_BIG_UNITS * nmain
    base = tile * tmp
    wd = wd_ref[0].reshape(EPG * EXPERT_FF, D_MODEL)

    def experts(rows):
        xb = xs_ref[rows, :]
        cwb = cws_ref[rows, :]
        hs = []
        for e in range(EPG):
            gg_ = _mm(xb, wg_ref[0, e])
            uu = _mm(xb, wu_ref[0, e])
            hs.append((gg_ * _sigmoid(gg_) * uu * cwb[:, e:e + 1]).astype(BF16))
        return _mm(jnp.concatenate(hs, axis=1), wd).astype(BF16)

    def big_copy(slot, blk):
        return pltpu.make_async_copy(
            ybuf.at[slot], ys_hbm.at[pl.ds(pl.multiple_of(base + blk * MOE_BLK, MOE_BLK), MOE_BIG), :], sem.at[slot])

    def last_copy(units, blk):
        rows = units * MOE_BLK
        return pltpu.make_async_copy(
            ylast.at[0:rows], ys_hbm.at[pl.ds(pl.multiple_of(base + blk * MOE_BLK, MOE_BLK), rows), :], sem.at[2])

    def wait_last():
        for u in range(1, MOE_LAST_MAX + 1):
            @pl.when(st_ref[1] == u)
            def _(u=u):
                last_copy(u, 0).wait()

    def zero_copy(blk):
        return pltpu.make_async_copy(
            zbuf, ys_hbm.at[pl.ds(pl.multiple_of(base + blk * MOE_BLK, MOE_BLK), MOE_BLK), :], sem.at[3])

    @pl.when(q == 0)
    def _():
        for i in range(3):
            st_ref[i] = 0
        zbuf[...] = jnp.zeros_like(zbuf)

    def body(k, carry):
        c = st_ref[0]
        slot = jnp.bitwise_and(c, 1)
        blk = off + MOE_BIG_UNITS * k
        y = experts(pl.ds(pl.multiple_of(blk * MOE_BLK, MOE_BLK), MOE_BIG))

        @pl.when(c >= 2)
        def _():
            big_copy(slot, 0).wait()

        ybuf[slot] = y
        big_copy(slot, blk).start()
        st_ref[0] = c + 1
        return carry

    lax.fori_loop(0, nmain, body, 0)

    @pl.when(nlast > 0)
    def _():
        wait_last()

    lblk = off + MOE_BIG_UNITS * nmain
    for u in range(1, MOE_LAST_MAX + 1):
        @pl.when(nlast == u)
        def _(u=u):
            ylast[0:u * MOE_BLK, :] = experts(pl.ds(pl.multiple_of(lblk * MOE_BLK, MOE_BLK), u * MOE_BLK))
            last_copy(u, lblk).start()
            st_ref[1] = u

    @pl.when(grp == N_GROUPS - 1)
    def _():
        def zfill(blk, carry):
            zero_copy(blk).start()
            return carry

        nz = tmp // MOE_BLK - nused_ref[tile]
        lax.fori_loop(nused_ref[tile], tmp // MOE_BLK, zfill, 0)
        st_ref[2] = st_ref[2] + nz

    @pl.when(q == pl.num_programs(0) - 1)
    def _():
        c = st_ref[0]

        @pl.when(c >= 2)
        def _():
            big_copy(jnp.bitwise_and(c, 1), 0).wait()

        @pl.when(c >= 1)
        def _():
            big_copy(jnp.bitwise_and(c - 1, 1), 0).wait()

        wait_last()

        def zwait(i, carry):
            zero_copy(0).wait()
            return carry

        lax.fori_loop(0, st_ref[2], zwait, 0)


def _moe_unsort_kernel(ys_ref, pos_ref, x1_ref, mod_ref, lng_ref, lnb_ref, out_ref, *, nb, r):
    tm = nb * r
    tmp = tm + N_GROUPS * MOE_BLK
    d = D_MODEL
    ysb = ys_ref[...]
    posc = jnp.broadcast_to(pos_ref[0][0:1, :], (LANE, tm)).T
    lng = lng_ref[...]
    lnb = lnb_ref[...]
    for c in range(tm // MOE_PC):
        slot = lax.broadcasted_iota(jnp.int32, (MOE_PC, tmp), 1).astype(F32)
        unperm = jnp.where(slot == posc[c * MOE_PC:(c + 1) * MOE_PC, 0:1], 1.0, 0.0).astype(BF16)
        y = _mm(unperm, ysb)
        if nb == 1:
            x1c = x1_ref[0, c * MOE_PC:(c + 1) * MOE_PC, :]
            gt2 = mod_ref[0][:, 5 * d:6 * d]
            out_ref[0, c * MOE_PC:(c + 1) * MOE_PC, :] = _layer_norm(DEEPNORM_ALPHA * x1c + gt2 * y, lng, lnb)
        else:
            cb = MOE_PC // r
            x1c = x1_ref[c * cb:(c + 1) * cb]
            gt2 = mod_ref[c * cb:(c + 1) * cb][:, :, 5 * d:6 * d]
            yy = DEEPNORM_ALPHA * x1c + gt2 * y.reshape(cb, r, d)
            out_ref[c * cb:(c + 1) * cb] = _layer_norm(yy, lng, lnb)


def _moe(streams, wrt, brt, wg, wu, wd, lng, lnb):
    d = D_MODEL
    tm = streams[0][2] * streams[0][3]
    tmp = tm + N_GROUPS * MOE_BLK
    const2 = lambda i: (0, 0)
    params = pltpu.CompilerParams(dimension_semantics=("arbitrary",), vmem_limit_bytes=VMEM_LIMIT)
    geo = []
    ntiles = 0
    for x1, _, nb, r in streams:
        assert nb * r == tm
        tpb = x1.shape[1] // r
        n = (x1.shape[0] // nb) * tpb
        geo.append((tpb, n, ntiles))
        ntiles += n

    sorted_shapes = [jax.ShapeDtypeStruct((ntiles * tmp, d), BF16),
                     jax.ShapeDtypeStruct((ntiles * tmp, LANE), F32),
                     jax.ShapeDtypeStruct((ntiles, EPG, tm), F32),
                     jax.ShapeDtypeStruct((ntiles, EPG, LANE), F32)]
    bufs = []
    for (x1, mod, nb, r), (tpb, n, t0) in zip(streams, geo):
        steps = ntiles if not bufs else n
        xmap = lambda i, tpb=tpb, n=n: (jnp.minimum(i, n - 1) // tpb, jnp.minimum(i, n - 1) % tpb, 0)
        mmap = lambda i, tpb=tpb, n=n: (jnp.minimum(i, n - 1) // tpb, 0, 0)
        bufs = pl.pallas_call(
            functools.partial(_moe_sort_kernel, nb=nb, r=r, nreal=n),
            out_shape=sorted_shapes,
            grid=(steps,),
            in_specs=[
                pl.BlockSpec((nb, r, d), xmap),
                pl.BlockSpec((nb, 1, 6 * d), mmap),
                pl.BlockSpec((LANE, d), const2),
                pl.BlockSpec((LANE, 1), const2),
            ] + [pl.BlockSpec(memory_space=pl.ANY)] * len(bufs),
            out_specs=[pl.BlockSpec((tmp, d), lambda i, t0=t0: (i + t0, 0)),
                       pl.BlockSpec((tmp, LANE), lambda i, t0=t0: (i + t0, 0)),
                       pl.BlockSpec((1, EPG, tm), lambda i, t0=t0: (i + t0, 0, 0)),
                       pl.BlockSpec((1, EPG, LANE), lambda i, t0=t0: (i + t0, 0, 0))],
            input_output_aliases={4 + k: k for k in range(len(bufs))},
            compiler_params=params,
            name="moe_sort",
        )(x1, mod, wrt, brt, *bufs)
    xs, cws, pos, cnt = bufs

    nblk = ((cnt[:, :N_GROUPS, 0].astype(jnp.int32) + (MOE_BLK - 1)) // MOE_BLK)
    boff = jnp.cumsum(nblk, axis=1) - nblk
    nused = jnp.sum(nblk, axis=1).astype(jnp.int32)
    nblk_q = nblk.T.reshape(-1).astype(jnp.int32)
    boff_q = boff.T.reshape(-1).astype(jnp.int32)

    tmap = lambda q, bo, nk, nu: (q % ntiles, 0)
    wmap = lambda q, bo, nk, nu: (q // ntiles, 0, 0, 0)
    ys = pl.pallas_call(
        functools.partial(_moe_expert_kernel, ntiles=ntiles, tmp=tmp),
        out_shape=jax.ShapeDtypeStruct((ntiles * tmp, d), BF16),
        grid_spec=pltpu.PrefetchScalarGridSpec(
            num_scalar_prefetch=3,
            grid=(N_GROUPS * ntiles,),
            in_specs=[
                pl.BlockSpec((tmp, d), tmap),
                pl.BlockSpec((tmp, LANE), tmap),
                pl.BlockSpec((1, EPG, d, EXPERT_FF), wmap),
                pl.BlockSpec((1, EPG, d, EXPERT_FF), wmap),
                pl.BlockSpec((1, EPG, EXPERT_FF, d), wmap),
            ],
            out_specs=pl.BlockSpec(memory_space=pl.ANY),
            scratch_shapes=[
                pltpu.VMEM((2, MOE_BIG, d), BF16),
                pltpu.VMEM((MOE_LAST_MAX * MOE_BLK, d), BF16),
                pltpu.VMEM((MOE_BLK, d), BF16),
                pltpu.SemaphoreType.DMA((4,)),
                pltpu.SMEM((4,), jnp.int32),
            ],
        ),
        compiler_params=params,
        name="moe_experts",
    )(boff_q, nblk_q, nused, xs, cws, wg, wu, wd)

    outs = []
    for (x1, mod, nb, r), (tpb, n, t0) in zip(streams, geo):
        xmap = lambda i, tpb=tpb: (i // tpb, i % tpb, 0)
        mmap = lambda i, tpb=tpb: (i // tpb, 0, 0)
        outs.append(pl.pallas_call(
            functools.partial(_moe_unsort_kernel, nb=nb, r=r),
            out_shape=jax.ShapeDtypeStruct(x1.shape, F32),
            grid=(n,),
            in_specs=[
                pl.BlockSpec((tmp, d), lambda i, t0=t0: (i + t0, 0)),
                pl.BlockSpec((1, EPG, tm), lambda i, t0=t0: (i + t0, 0, 0)),
                pl.BlockSpec((nb, r, d), xmap),
                pl.BlockSpec((nb, 1, 6 * d), mmap),
                pl.BlockSpec((1, d), const2),
                pl.BlockSpec((1, d), const2),
            ],
            out_specs=pl.BlockSpec((nb, r, d), xmap),
            compiler_params=params,
            name="moe_unsort",
        )(ys, pos, x1, mod, lng, lnb))
    return outs


def kernel(x_prompt, x_sample, c_prompt, c_sample, state_gla, cache_swa_k, cache_swa_v, ada_w, ada_b, w_in,
           gla_a2_w, gla_a2_b, gla_norm_w, swa_sinks, w_o, ln1_g, ln1_b, router_g_w, router_g_b, router_e_w,
           router_e_b, moe_w_gate, moe_w_up, moe_w_down, ln2_g, ln2_b):
    assert ada_w.shape[0] == 1
    bp = x_prompt.shape[0]
    bs, ss, d = x_sample.shape
    lc = cache_swa_k.shape[2]

    w = w_in[0]
    zpad = jnp.zeros((d, LANE - GLA_RANK), F32)
    w_sq = w[:, 1552:2064].reshape(d, SWA_KV_HEADS, SWA_GROUP, SWA_DH).transpose(0, 2, 1, 3).reshape(d, SWA_W)
    win = jnp.concatenate([w[:, 0:1536], w_sq, w[:, 2064:2320], w[:, 1536:1552], zpad], axis=1).astype(BF16)
    a2w = jnp.concatenate([gla_a2_w[0], jnp.zeros((LANE - GLA_RANK, GLA_KW), F32)], axis=0).astype(BF16)
    a2b = gla_a2_b[0].reshape(1, GLA_KW)
    gnw = gla_norm_w[0].reshape(1, GLA_DV)
    wo_swa = w_o[0][GLA_VW:].reshape(SWA_KV_HEADS, SWA_GROUP, SWA_DH, d).transpose(1, 0, 2, 3).reshape(SWA_W, d)
    wo = jnp.concatenate([w_o[0][:GLA_VW], wo_swa], axis=0).astype(BF16)
    sinks = swa_sinks[0]
    wrt = jnp.concatenate([router_g_w[0], jnp.zeros((d, R_EXP0 - N_GROUPS), F32),
                           jnp.transpose(router_e_w[0], (1, 0, 2)).reshape(d, N_GROUPS * EPG),
                           jnp.zeros((d, LANE - R_EXP0 - N_GROUPS * EPG), F32)], axis=1).T.astype(BF16)
    brt = jnp.concatenate([router_g_b[0], jnp.zeros((R_EXP0 - N_GROUPS,), F32), router_e_b[0].reshape(-1),
                           jnp.zeros((LANE - R_EXP0 - N_GROUPS * EPG,), F32)]).reshape(LANE, 1)
    wg = moe_w_gate[0].astype(BF16)
    wu = moe_w_up[0].astype(BF16)
    wd = moe_w_down[0].astype(BF16)
    lng1, lnb1 = ln1_g[0].reshape(1, d), ln1_b[0].reshape(1, d)
    lng2, lnb2 = ln2_g[0].reshape(1, d), ln2_b[0].reshape(1, d)

    mod = _adaln(jnp.concatenate([c_prompt, c_sample], axis=0), ada_w[0], ada_b[0].reshape(1, 6 * d))
    mod = mod.reshape(bp + bs, 1, 6 * d)
    mod_p, mod_s = mod[:bp], mod[bp:]

    x1p, s_p, k_p, v_p = _mixer_prompt(x_prompt, mod_p, sinks, win, a2w, a2b, gnw, wo, lng1, lnb1)
    x1s, s_s, k_s, v_s = _mixer_sample(
        x_sample, mod_s, state_gla[0], cache_swa_k[0].reshape(bs, lc, LANE), cache_swa_v[0].reshape(bs, lc, LANE),
        sinks, win, a2w, a2b, gnw, wo, lng1, lnb1)

    yp, ys = _moe([(x1p, mod_p, 1, bs * ss), (x1s, mod_s, bs, ss)], wrt, brt, wg, wu, wd, lng2, lnb2)

    kv_shape_p = (1, bp, WINDOW, SWA_KV_HEADS, SWA_DH)
    kv_shape_s = (1, bs, lc, SWA_KV_HEADS, SWA_DH)
    return (yp, ys, s_p[None], k_p.reshape(kv_shape_p), v_p.reshape(kv_shape_p),
            s_s[None], k_s.reshape(kv_shape_s), v_s.reshape(kv_shape_s))
```

```python
import functools

import jax
import jax.numpy as jnp
import numpy as np
from jax import lax
from jax.experimental import pallas as pl
from jax.experimental.pallas import tpu as pltpu

F32 = jnp.float32
BF16 = jnp.bfloat16

D_MODEL = 1024
CHUNK = 64
GLA_HEADS = 4
GLA_DK = 64
GLA_DV = 128
GLA_KW = GLA_HEADS * GLA_DK
GLA_VW = GLA_HEADS * GLA_DV
GLA_RANK = 16
GLA_TAU = 16.0
SWA_Q_HEADS = 8
SWA_KV_HEADS = 2
SWA_GROUP = 4
SWA_DH = 64
SWA_W = SWA_Q_HEADS * SWA_DH
WINDOW = 128
N_GROUPS = 4
EPG = 8
EXPERT_FF = 256
DEEPNORM_ALPHA = 2.0 ** 0.25
LN_EPS = 1e-5
NEG_INF = -1e30

C_GQ, C_GK, C_GV, C_GR, C_SQ, C_SK, C_SV, C_GA = 0, 256, 512, 1024, 1536, 2048, 2176, 2304
PROJ_W = 2432
LANE = 128
R_EXP0 = 8

VMEM_LIMIT = 56 * 1024 * 1024


def _mm(a, b):
    return jnp.dot(a, b, preferred_element_type=F32)


def _mm_nt(a, b):
    return lax.dot_general(a, b, (((1,), (1,)), ((), ())), preferred_element_type=F32)


def _mm_tn(a, b):
    return lax.dot_general(a, b, (((0,), (0,)), ((), ())), preferred_element_type=F32)


def _split_bf16(a):
    hi = a.astype(BF16)
    lo = (a - hi.astype(F32)).astype(BF16)
    return hi, lo


def _sigmoid(x):
    return 1.0 / (1.0 + jnp.exp(-x))


def _layer_norm(y, g, b):
    mu = jnp.mean(y, axis=-1, keepdims=True)
    d = y - mu
    var = jnp.mean(d * d, axis=-1, keepdims=True)
    return d * lax.rsqrt(var + LN_EPS) * g + b


def _adaln_kernel(c_ref, w_ref, b_ref, o_ref):
    c = c_ref[...]
    a = c * _sigmoid(c)
    a_hi, a_lo = _split_bf16(a)
    w_hi, w_lo = _split_bf16(w_ref[...])
    o_ref[...] = _mm(a_hi, w_hi) + (_mm(a_hi, w_lo) + _mm(a_lo, w_hi)) + b_ref[...]


def _adaln(c_all, ada_w, ada_b):
    n = c_all.shape[0]
    bn = 1024
    return pl.pallas_call(
        _adaln_kernel,
        out_shape=jax.ShapeDtypeStruct((n, 6 * D_MODEL), F32),
        grid=(6 * D_MODEL // bn,),
        in_specs=[pl.BlockSpec((n, D_MODEL), lambda j: (0, 0)),
                  pl.BlockSpec((D_MODEL, bn), lambda j: (0, j)),
                  pl.BlockSpec((1, bn), lambda j: (0, j))],
        out_specs=pl.BlockSpec((n, bn), lambda j: (0, j)),
        compiler_params=pltpu.CompilerParams(dimension_semantics=("arbitrary",), vmem_limit_bytes=VMEM_LIMIT),
        name="adaln",
    )(c_all, ada_w, ada_b)


def _mixer_consts(nrows, chunk, qb, kw):
    r = np.arange(nrows)
    tri = ((r[:, None] // chunk == r[None, :] // chunk) & (r[:, None] >= r[None, :])).astype(np.float32)
    hs = np.arange(GLA_HEADS * chunk)
    mkk = (hs[:, None] // chunk == np.arange(GLA_KW)[None, :] // GLA_DK).astype(np.float32)
    mv = (hs[:, None] // chunk == np.arange(GLA_VW)[None, :] // GLA_DV).astype(np.float32)
    ms = (np.arange(GLA_VW)[:, None] // GLA_DV == np.arange(GLA_KW)[None, :] // GLA_DK).astype(np.float32)
    caus = (np.arange(chunk)[:, None] >= (hs[None, :] % chunk)).astype(np.float32)
    t = np.arange(qb)
    kj = np.arange(kw)
    cs = (t // chunk) * chunk
    kpos = kj[None, :] - WINDOW
    vis = (kpos >= cs[:, None] - WINDOW) & (kpos < cs[:, None] + chunk)
    dist = np.abs(t[:, None] + WINDOW - kj[None, :]).astype(np.float32)
    bias = np.zeros((SWA_KV_HEADS, SWA_GROUP * qb, kw), np.float32)
    for hk in range(SWA_KV_HEADS):
        for g in range(SWA_GROUP):
            slope = np.float32(2.0 ** (-(hk * SWA_GROUP + g + 1)))
            bias[hk, g * qb:(g + 1) * qb] = np.where(vis, -slope * dist, np.float32(2.0 * NEG_INF))
    return (jnp.asarray(tri, BF16), jnp.asarray(bias), jnp.asarray(mkk, BF16), jnp.asarray(mv, BF16),
            jnp.asarray(ms), jnp.asarray(caus))


def _modulate(x, mod):
    return (x * (1.0 + mod[:, D_MODEL:2 * D_MODEL]) + mod[:, 0:D_MODEL]).astype(BF16)


def _log_gates(proj_ref, a2w_ref, a2b_ref):
    ga = proj_ref[:, C_GA:C_GA + LANE].astype(BF16)
    z = _mm(ga, a2w_ref[...]) + a2b_ref[...]
    lsig = -(jnp.maximum(-z, 0.0) + jnp.log(1.0 + jnp.exp(-jnp.abs(z))))
    return lsig * (1.0 / GLA_TAU)


def _gla_prep(loga, tri_ref, proj_ref, chunk, qe_ref, qn_ref, ke_ref, kn_ref, kw_ref, eb_ref, vb_ref):
    nrows = loga.shape[0]
    la_hi, la_lo = _split_bf16(loga)
    tri = tri_ref[...]
    b = _mm(tri, la_hi) + _mm(tri, la_lo)
    b_end = jnp.concatenate(
        [jnp.broadcast_to(b[c * chunk + chunk - 1:(c + 1) * chunk, :], (chunk, GLA_KW))
         for c in range(nrows // chunk)], axis=0)
    eb = jnp.exp(b)
    ebn = jnp.exp(-b)
    wk = jnp.exp(b_end - b)
    q = proj_ref[:, C_GQ:C_GQ + GLA_KW] * (GLA_DK ** -0.5)
    k = proj_ref[:, C_GK:C_GK + GLA_KW]
    eb_ref[...] = eb
    qe_ref[...] = (q * eb).astype(BF16)
    qn_ref[...] = (q * ebn).astype(BF16)
    ke_ref[...] = (k * eb).astype(BF16)
    kn_ref[...] = (k * ebn).astype(BF16)
    kw_ref[...] = (k * wk).astype(BF16)
    vb_ref[...] = proj_ref[:, C_GV:C_GV + GLA_VW].astype(BF16)


def _gla_chunk(r0, chunk, qe_ref, qn_ref, ke_ref, kn_ref, kw_ref, eb_ref, vb_ref, sbt_ref, mkk, mv, ms, caus):
    rows = slice(r0, r0 + chunk)
    qe = qe_ref[rows, :]
    qn = qn_ref[rows, :]
    zero = jnp.zeros((), BF16)
    kn4 = jnp.where(mkk != 0, jnp.concatenate([kn_ref[rows, :]] * GLA_HEADS, axis=0), zero)
    ke4 = jnp.where(mkk != 0, jnp.concatenate([ke_ref[rows, :]] * GLA_HEADS, axis=0), zero)
    a_lo = _mm_nt(qe, kn4)
    a_up = _mm_nt(qn, ke4)
    a = jnp.where(caus != 0.0, a_lo, a_up).astype(BF16)
    v = vb_ref[rows, :]
    v4 = jnp.where(mv != 0, jnp.concatenate([v] * GLA_HEADS, axis=0), zero)
    sbt = sbt_ref[...]
    o = _mm(a, v4) + _mm_nt(qe, sbt.astype(BF16))
    ut = _mm_tn(v, kw_ref[rows, :])
    g_end = eb_ref[r0 + chunk - 1:r0 + chunk, :]
    sbt_ref[...] = g_end * sbt + jnp.where(ms != 0.0, ut, 0.0)
    return o


def _gla_post(o, gr, gnw):
    res = []
    for h in range(GLA_HEADS):
        oh = o[:, h * GLA_DV:(h + 1) * GLA_DV]
        ms_ = jnp.mean(oh * oh, axis=-1, keepdims=True)
        og = oh * lax.rsqrt(ms_ + LN_EPS) * gnw
        r = gr[:, h * GLA_DV:(h + 1) * GLA_DV]
        res.append((og * (r * _sigmoid(r))).astype(BF16))
    return res


def _swa_scores(sq, kwin, qb):
    low = lax.broadcasted_iota(jnp.int32, (qb, LANE), 1) < SWA_DH
    res = []
    for hk in range(SWA_KV_HEADS):
        keep = low if hk == 0 else jnp.logical_not(low)
        qs = jnp.concatenate([jnp.where(keep, sq[:, g * LANE:(g + 1) * LANE] * (SWA_DH ** -0.5), 0.0)
                              for g in range(SWA_GROUP)], axis=0).astype(BF16)
        res.append(_mm_nt(qs, kwin))
    return res


def _swa_finish(s, vwin, bias, sinks_ref, hk, extra_valid, qb):
    ok = bias > -1e29
    if extra_valid is not None:
        ok = ok & extra_valid
    s = jnp.where(ok, s + bias, NEG_INF)
    ps, denoms = [], []
    for g in range(SWA_GROUP):
        sg = s[g * qb:(g + 1) * qb, :]
        sink = sinks_ref[hk * SWA_GROUP + g]
        m = jnp.maximum(jnp.max(sg, axis=-1, keepdims=True), sink)
        pg = jnp.exp(sg - m)
        denoms.append(jnp.sum(pg, axis=-1, keepdims=True) + jnp.exp(sink - m))
        ps.append(pg.astype(BF16))
    o = _mm(jnp.concatenate(ps, axis=0), vwin)
    return o / jnp.concatenate(denoms, axis=0)


def _swa_merge(o0, o1, qb):
    low = lax.broadcasted_iota(jnp.int32, (qb, LANE), 1) < SWA_DH
    return [jnp.where(low, o0[g * qb:(g + 1) * qb, :], o1[g * qb:(g + 1) * qb, :]).astype(BF16)
            for g in range(SWA_GROUP)]


def _out_proj_ln(x, gt1, mixed_ref, wo_ref, ln_g, ln_b):
    mix = _mm(mixed_ref[...], wo_ref[...])
    return _layer_norm(DEEPNORM_ALPHA * x + gt1 * mix, ln_g, ln_b)


P_QB = 128
P_KW = WINDOW + P_QB


def _mixer_prompt_kernel(sinks_ref, x_ref, mod_ref, win_ref, a2w_ref, a2b_ref, gnw_ref, wo_ref, lng_ref, lnb_ref,
                         tri_ref, bias_ref, mkk_ref, mv_ref, ms_ref, caus_ref,
                         x1_ref, s_out_ref, k_out_ref, v_out_ref,
                         proj_ref, qe_ref, qn_ref, ke_ref, kn_ref, kw_ref, eb_ref, vb_ref, kbuf, vbuf, mixed_ref,
                         sbt_ref, *, tl):
    j = pl.program_id(1)
    nj = pl.num_programs(1)
    d = D_MODEL

    @pl.when(j == 0)
    def _():
        sbt_ref[...] = jnp.zeros_like(sbt_ref)
        kbuf[0:WINDOW, :] = jnp.zeros((WINDOW, LANE), BF16)
        vbuf[0:WINDOW, :] = jnp.zeros((WINDOW, LANE), BF16)

    x = x_ref[0]
    mod = mod_ref[0]
    proj_ref[...] = _mm(_modulate(x, mod), win_ref[...])
    loga = _log_gates(proj_ref, a2w_ref, a2b_ref)
    _gla_prep(loga, tri_ref, proj_ref, CHUNK, qe_ref, qn_ref, ke_ref, kn_ref, kw_ref, eb_ref, vb_ref)
    kbuf[WINDOW:WINDOW + tl, :] = proj_ref[:, C_SK:C_SK + LANE].astype(BF16)
    vbuf[WINDOW:WINDOW + tl, :] = proj_ref[:, C_SV:C_SV + LANE].astype(BF16)

    mkk = mkk_ref[...]
    mv = mv_ref[...]
    ms = ms_ref[...]
    caus = caus_ref[...]
    gnw = gnw_ref[...]
    kj = lax.broadcasted_iota(jnp.int32, (SWA_GROUP * P_QB, P_KW), 1)
    first_valid = kj >= jnp.where(j > 0, 0, WINDOW)

    def scores(p):
        q0 = p * P_QB
        return _swa_scores(proj_ref[q0:q0 + P_QB, C_SQ:C_SQ + SWA_W], kbuf[q0:q0 + P_KW, :], P_QB)

    def gla(r0):
        o = _gla_chunk(r0, CHUNK, qe_ref, qn_ref, ke_ref, kn_ref, kw_ref, eb_ref, vb_ref, sbt_ref, mkk, mv, ms, caus)
        og = _gla_post(o, proj_ref[r0:r0 + CHUNK, C_GR:C_GR + GLA_VW], gnw)
        for hh in range(GLA_HEADS):
            mixed_ref[r0:r0 + CHUNK, hh * GLA_DV:(hh + 1) * GLA_DV] = og[hh]

    for p in range(tl // P_QB):
        q0 = p * P_QB
        s = scores(p)
        vwin = vbuf[q0:q0 + P_KW, :]
        extra = first_valid if p == 0 else None
        o0 = _swa_finish(s[0], vwin, bias_ref[0], sinks_ref, 0, extra, P_QB)
        o1 = _swa_finish(s[1], vwin, bias_ref[1], sinks_ref, 1, extra, P_QB)
        for i, blk in enumerate(_swa_merge(o0, o1, P_QB)):
            mixed_ref[q0:q0 + P_QB, GLA_VW + i * LANE:GLA_VW + (i + 1) * LANE] = blk
        gla(q0)
        gla(q0 + CHUNK)

    x1_ref[0] = _out_proj_ln(x, mod[:, 2 * d:3 * d], mixed_ref, wo_ref, lng_ref[...], lnb_ref[...])

    kbuf[0:WINDOW, :] = kbuf[tl:tl + WINDOW, :]
    vbuf[0:WINDOW, :] = vbuf[tl:tl + WINDOW, :]

    @pl.when(j == nj - 1)
    def _():
        for hh in range(GLA_HEADS):
            s_out_ref[0, hh] = sbt_ref[hh * GLA_DV:(hh + 1) * GLA_DV, hh * GLA_DK:(hh + 1) * GLA_DK].T
        k_out_ref[0] = proj_ref[tl - WINDOW:tl, C_SK:C_SK + LANE]
        v_out_ref[0] = proj_ref[tl - WINDOW:tl, C_SV:C_SV + LANE]


def _mixer_prompt(x, mod, sinks, win, a2w, a2b, gnw, wo, lng, lnb, *, tl=512):
    b, l, d = x.shape
    nj = l // tl
    consts = _mixer_consts(tl, CHUNK, P_QB, P_KW)
    const2 = lambda i, j, s: (0, 0)
    const3 = lambda i, j, s: (0, 0, 0)
    grid_spec = pltpu.PrefetchScalarGridSpec(
        num_scalar_prefetch=1,
        grid=(b, nj),
        in_specs=[
            pl.BlockSpec((1, tl, d), lambda i, j, s: (i, j, 0)),
            pl.BlockSpec((1, 1, 6 * d), lambda i, j, s: (i, 0, 0)),
            pl.BlockSpec((d, PROJ_W), const2),
            pl.BlockSpec((LANE, GLA_KW), const2),
            pl.BlockSpec((1, GLA_KW), const2),
            pl.BlockSpec((1, GLA_DV), const2),
            pl.BlockSpec((d, d), const2),
            pl.BlockSpec((1, d), const2),
            pl.BlockSpec((1, d), const2),
            pl.BlockSpec((tl, tl), const2),
            pl.BlockSpec((SWA_KV_HEADS, SWA_GROUP * P_QB, P_KW), const3),
            pl.BlockSpec((GLA_HEADS * CHUNK, GLA_KW), const2),
            pl.BlockSpec((GLA_HEADS * CHUNK, GLA_VW), const2),
            pl.BlockSpec((GLA_VW, GLA_KW), const2),
            pl.BlockSpec((CHUNK, GLA_HEADS * CHUNK), const2),
        ],
        out_specs=[
            pl.BlockSpec((1, tl, d), lambda i, j, s: (i, j, 0)),
            pl.BlockSpec((1, GLA_HEADS, GLA_DK, GLA_DV), lambda i, j, s: (i, 0, 0, 0)),
            pl.BlockSpec((1, WINDOW, LANE), lambda i, j, s: (i, 0, 0)),
            pl.BlockSpec((1, WINDOW, LANE), lambda i, j, s: (i, 0, 0)),
        ],
        scratch_shapes=[
            pltpu.VMEM((tl, PROJ_W), F32),
            pltpu.VMEM((tl, GLA_KW), BF16),
            pltpu.VMEM((tl, GLA_KW), BF16),
            pltpu.VMEM((tl, GLA_KW), BF16),
            pltpu.VMEM((tl, GLA_KW), BF16),
            pltpu.VMEM((tl, GLA_KW), BF16),
            pltpu.VMEM((tl, GLA_KW), F32),
            pltpu.VMEM((tl, GLA_VW), BF16),
            pltpu.VMEM((WINDOW + tl, LANE), BF16),
            pltpu.VMEM((WINDOW + tl, LANE), BF16),
            pltpu.VMEM((tl, d), BF16),
            pltpu.VMEM((GLA_VW, GLA_KW), F32),
        ],
    )
    return pl.pallas_call(
        functools.partial(_mixer_prompt_kernel, tl=tl),
        out_shape=[
            jax.ShapeDtypeStruct((b, l, d), F32),
            jax.ShapeDtypeStruct((b, GLA_HEADS, GLA_DK, GLA_DV), F32),
            jax.ShapeDtypeStruct((b, WINDOW, LANE), F32),
            jax.ShapeDtypeStruct((b, WINDOW, LANE), F32),
        ],
        grid_spec=grid_spec,
        compiler_params=pltpu.CompilerParams(dimension_semantics=("arbitrary", "arbitrary"),
                                             vmem_limit_bytes=VMEM_LIMIT),
        name="mixer_prompt",
    )(sinks, x, mod, win, a2w, a2b, gnw, wo, lng, lnb, *consts)


def _mixer_sample_kernel(sinks_ref, x_ref, mod_ref, s0_ref, kc_ref, vc_ref, win_ref, a2w_ref, a2b_ref, gnw_ref,
                         wo_ref, lng_ref, lnb_ref, tri_ref, bias_ref, mkk_ref, mv_ref, ms_ref, caus_ref,
                         x1_ref, s_out_ref, k_out_ref, v_out_ref,
                         proj_ref, qe_ref, qn_ref, ke_ref, kn_ref, kw_ref, eb_ref, vb_ref, kbuf, vbuf, mixed_ref,
                         sbt_ref, xm_ref, *, nb, s):
    nkeys = WINDOW + s
    d = D_MODEL
    for bb in range(nb):
        m = mod_ref[bb]
        xm_ref[bb * s:(bb + 1) * s, :] = x_ref[bb] * (1.0 + m[:, d:2 * d]) + m[:, 0:d]
    proj_ref[...] = _mm(xm_ref[...].astype(BF16), win_ref[...])
    loga = _log_gates(proj_ref, a2w_ref, a2b_ref)
    _gla_prep(loga, tri_ref, proj_ref, s, qe_ref, qn_ref, ke_ref, kn_ref, kw_ref, eb_ref, vb_ref)

    mkk = mkk_ref[...]
    mv = mv_ref[...]
    ms = ms_ref[...]
    caus = caus_ref[...]
    gnw = gnw_ref[...]

    for bb in range(nb):
        r0 = bb * s
        rows = slice(r0, r0 + s)
        sbt_ref[...] = jnp.zeros_like(sbt_ref)
        for hh in range(GLA_HEADS):
            sbt_ref[hh * GLA_DV:(hh + 1) * GLA_DV, hh * GLA_DK:(hh + 1) * GLA_DK] = s0_ref[bb, hh].T
        kbuf[0:WINDOW, :] = kc_ref[bb].astype(BF16)
        vbuf[0:WINDOW, :] = vc_ref[bb].astype(BF16)
        kbuf[WINDOW:nkeys, :] = proj_ref[rows, C_SK:C_SK + LANE].astype(BF16)
        vbuf[WINDOW:nkeys, :] = proj_ref[rows, C_SV:C_SV + LANE].astype(BF16)
        sc = _swa_scores(proj_ref[rows, C_SQ:C_SQ + SWA_W], kbuf[...], s)
        vwin = vbuf[...]
        blocks = _swa_merge(_swa_finish(sc[0], vwin, bias_ref[0], sinks_ref, 0, None, s),
                            _swa_finish(sc[1], vwin, bias_ref[1], sinks_ref, 1, None, s), s)
        for i, blk in enumerate(blocks):
            mixed_ref[rows, GLA_VW + i * LANE:GLA_VW + (i + 1) * LANE] = blk
        o = _gla_chunk(r0, s, qe_ref, qn_ref, ke_ref, kn_ref, kw_ref, eb_ref, vb_ref, sbt_ref, mkk, mv, ms, caus)
        og = _gla_post(o, proj_ref[rows, C_GR:C_GR + GLA_VW], gnw)
        for hh in range(GLA_HEADS):
            mixed_ref[rows, hh * GLA_DV:(hh + 1) * GLA_DV] = og[hh]
            s_out_ref[bb, hh] = sbt_ref[hh * GLA_DV:(hh + 1) * GLA_DV, hh * GLA_DK:(hh + 1) * GLA_DK].T
        k_out_ref[bb, 0:WINDOW - s, :] = kc_ref[bb, s:WINDOW, :]
        v_out_ref[bb, 0:WINDOW - s, :] = vc_ref[bb, s:WINDOW, :]
        k_out_ref[bb, WINDOW - s:WINDOW, :] = proj_ref[rows, C_SK:C_SK + LANE]
        v_out_ref[bb, WINDOW - s:WINDOW, :] = proj_ref[rows, C_SV:C_SV + LANE]

    mix = _mm(mixed_ref[...], wo_ref[...])
    lng = lng_ref[...]
    lnb = lnb_ref[...]
    for bb in range(nb):
        m = mod_ref[bb]
        y = DEEPNORM_ALPHA * x_ref[bb] + m[:, 2 * d:3 * d] * mix[bb * s:(bb + 1) * s, :]
        x1_ref[bb] = _layer_norm(y, lng, lnb)


def _mixer_sample(x, mod, s0, kc, vc, sinks, win, a2w, a2b, gnw, wo, lng, lnb, *, nb=8):
    b, s, d = x.shape
    assert kc.shape[1] == WINDOW and s <= WINDOW
    rows = nb * s
    nkeys = WINDOW + s
    consts = _mixer_consts(rows, s, s, nkeys)
    const2 = lambda i, sk: (0, 0)
    const3 = lambda i, sk: (0, 0, 0)
    grid_spec = pltpu.PrefetchScalarGridSpec(
        num_scalar_prefetch=1,
        grid=(b // nb,),
        in_specs=[
            pl.BlockSpec((nb, s, d), lambda i, sk: (i, 0, 0)),
            pl.BlockSpec((nb, 1, 6 * d), lambda i, sk: (i, 0, 0)),
            pl.BlockSpec((nb, GLA_HEADS, GLA_DK, GLA_DV), lambda i, sk: (i, 0, 0, 0)),
            pl.BlockSpec((nb, WINDOW, LANE), lambda i, sk: (i, 0, 0)),
            pl.BlockSpec((nb, WINDOW, LANE), lambda i, sk: (i, 0, 0)),
            pl.BlockSpec((d, PROJ_W), const2),
            pl.BlockSpec((LANE, GLA_KW), const2),
            pl.BlockSpec((1, GLA_KW), const2),
            pl.BlockSpec((1, GLA_DV), const2),
            pl.BlockSpec((d, d), const2),
            pl.BlockSpec((1, d), const2),
            pl.BlockSpec((1, d), const2),
            pl.BlockSpec((rows, rows), const2),
            pl.BlockSpec((SWA_KV_HEADS, SWA_GROUP * s, nkeys), const3),
            pl.BlockSpec((GLA_HEADS * s, GLA_KW), const2),
            pl.BlockSpec((GLA_HEADS * s, GLA_VW), const2),
            pl.BlockSpec((GLA_VW, GLA_KW), const2),
            pl.BlockSpec((s, GLA_HEADS * s), const2),
        ],
        out_specs=[
            pl.BlockSpec((nb, s, d), lambda i, sk: (i, 0, 0)),
            pl.BlockSpec((nb, GLA_HEADS, GLA_DK, GLA_DV), lambda i, sk: (i, 0, 0, 0)),
            pl.BlockSpec((nb, WINDOW, LANE), lambda i, sk: (i, 0, 0)),
            pl.BlockSpec((nb, WINDOW, LANE), lambda i, sk: (i, 0, 0)),
        ],
        scratch_shapes=[
            pltpu.VMEM((rows, PROJ_W), F32),
            pltpu.VMEM((rows, GLA_KW), BF16),
            pltpu.VMEM((rows, GLA_KW), BF16),
            pltpu.VMEM((rows, GLA_KW), BF16),
            pltpu.VMEM((rows, GLA_KW), BF16),
            pltpu.VMEM((rows, GLA_KW), BF16),
            pltpu.VMEM((rows, GLA_KW), F32),
            pltpu.VMEM((rows, GLA_VW), BF16),
            pltpu.VMEM((nkeys, LANE), BF16),
            pltpu.VMEM((nkeys, LANE), BF16),
            pltpu.VMEM((rows, d), BF16),
            pltpu.VMEM((GLA_VW, GLA_KW), F32),
            pltpu.VMEM((rows, d), F32),
        ],
    )
    return pl.pallas_call(
        functools.partial(_mixer_sample_kernel, nb=nb, s=s),
        out_shape=[
            jax.ShapeDtypeStruct((b, s, d), F32),
            jax.ShapeDtypeStruct((b, GLA_HEADS, GLA_DK, GLA_DV), F32),
            jax.ShapeDtypeStruct((b, WINDOW, LANE), F32),
            jax.ShapeDtypeStruct((b, WINDOW, LANE), F32),
        ],
        grid_spec=grid_spec,
        compiler_params=pltpu.CompilerParams(dimension_semantics=("arbitrary",),
                                             vmem_limit_bytes=VMEM_LIMIT),
        name="mixer_sample",
    )(sinks, x, mod, s0, kc, vc, win, a2w, a2b, gnw, wo, lng, lnb, *consts)


MOE_BLK = 64


def _route_t(logits_t):
    t = logits_t.shape[1]
    row = lax.broadcasted_iota(jnp.int32, (EPG, t), 0).astype(F32)
    big = 99.0
    gl = jnp.where(row < N_GROUPS, logits_t[0:EPG, :], -jnp.inf)
    gmax = jnp.max(gl, axis=0, keepdims=True)
    grp = jnp.min(jnp.where(gl == gmax, row, big), axis=0, keepdims=True)
    p_grp = 1.0 / jnp.sum(jnp.exp(gl - gmax), axis=0, keepdims=True)
    el = jnp.zeros((EPG, t), F32)
    for g in range(N_GROUPS):
        el = el + jnp.where(grp == float(g), logits_t[R_EXP0 + EPG * g:R_EXP0 + EPG * (g + 1), :], 0.0)
    v1 = jnp.max(el, axis=0, keepdims=True)
    i1 = jnp.min(jnp.where(el == v1, row, big), axis=0, keepdims=True)
    el2 = jnp.where(row == i1, -jnp.inf, el)
    v2 = jnp.max(el2, axis=0, keepdims=True)
    i2 = jnp.min(jnp.where(el2 == v2, row, big), axis=0, keepdims=True)
    e2 = jnp.exp(v2 - v1)
    w1 = p_grp / (1.0 + e2)
    w2 = p_grp * e2 / (1.0 + e2)
    cw = jnp.where(row == i1, w1, 0.0) + jnp.where(row == i2, w2, 0.0)
    return grp, cw


MOE_PC = 256


def _moe_sort_kernel(x1_ref, mod_ref, wrt_ref, brt_ref, *refs, nb, r, nreal):
    outs = refs[-4:]

    @pl.when(pl.program_id(0) < nreal)
    def _():
        _moe_sort_tile(x1_ref, mod_ref, wrt_ref, brt_ref, *outs, nb=nb, r=r)

    @pl.when(pl.program_id(0) >= nreal)
    def _():
        for o in outs:
            o[...] = jnp.zeros_like(o)


def _moe_sort_tile(x1_ref, mod_ref, wrt_ref, brt_ref, xs_ref, cws_ref, pos_ref, cnt_ref, *, nb, r):
    tm = nb * r
    tmp = tm + N_GROUPS * MOE_BLK
    d = D_MODEL
    mod = mod_ref[...]
    t3 = x1_ref[...] * (1.0 + mod[:, :, 4 * d:5 * d]) + mod[:, :, 3 * d:4 * d]
    t = t3.reshape(tm, d).astype(BF16)
    grp, cw = _route_t(_mm_nt(wrt_ref[...], t) + brt_ref[...])
    row = lax.broadcasted_iota(jnp.int32, (EPG, tm), 0).astype(F32)
    onehot_g = jnp.where(row == grp, 1.0, 0.0)
    nch = tm // LANE
    strict = (lax.broadcasted_iota(jnp.int32, (LANE, LANE), 0)
              < lax.broadcasted_iota(jnp.int32, (LANE, LANE), 1)).astype(F32).astype(BF16)
    stacked = jnp.concatenate([onehot_g[:, c * LANE:(c + 1) * LANE] for c in range(nch)], axis=0)
    pref = _mm(stacked.astype(BF16), strict)
    tot = jnp.sum(stacked, axis=1, keepdims=True)
    cnt = jnp.zeros((EPG, 1), F32)
    ranks = []
    for c in range(nch):
        ranks.append(pref[c * EPG:(c + 1) * EPG, :] + cnt)
        cnt = cnt + tot[c * EPG:(c + 1) * EPG, :]
    rank = jnp.concatenate(ranks, axis=1)
    padded = jnp.floor((cnt + (MOE_BLK - 1)) * (1.0 / MOE_BLK)) * MOE_BLK
    rowc = lax.broadcasted_iota(jnp.int32, (EPG, 1), 0)
    off = jnp.zeros((EPG, 1), F32)
    for gg in range(N_GROUPS - 1):
        off = off + jnp.where(rowc > gg, padded[gg:gg + 1, :], 0.0)
    pos = jnp.sum(onehot_g * (off + rank), axis=0, keepdims=True)
    pos_ref[0] = jnp.broadcast_to(pos, (EPG, tm))
    cnt_ref[0] = jnp.broadcast_to(cnt, (EPG, LANE))
    cw_hi, cw_lo = _split_bf16(jnp.concatenate([cw, jnp.zeros((LANE - EPG, tm), F32)], axis=0).T)
    t_aug = jnp.concatenate([t, cw_hi, cw_lo], axis=1)
    for c in range(tmp // MOE_PC):
        slot = (lax.broadcasted_iota(jnp.int32, (MOE_PC, tm), 0) + c * MOE_PC).astype(F32)
        perm = jnp.where(slot == pos, 1.0, 0.0).astype(BF16)
        moved = _mm(perm, t_aug)
        xs_ref[c * MOE_PC:(c + 1) * MOE_PC, :] = moved[:, 0:d].astype(BF16)
        cws_ref[c * MOE_PC:(c + 1) * MOE_PC, :] = moved[:, d:d + LANE] + moved[:, d + LANE:d + 2 * LANE]


MOE_BIG_UNITS = 4
MOE_BIG = MOE_BIG_UNITS * MOE_BLK
MOE_LAST_MAX = 2 * MOE_BIG_UNITS - 1


def _moe_expert_kernel(tile_ref, boff_ref, nblk_ref, nused_ref, xs_ref, cws_ref, wg_ref, wu_ref, wd_ref, ys_hbm,
                       ybuf, ylast, zbuf, sem, st_ref, *, ntiles, tmp):
    q = pl.program_id(0)
    grp = q // ntiles
    tile = tile_ref[q]
    off = boff_ref[q]
    n = nblk_ref[q]
    nmain = jnp.maximum(lax.shift_right_logical(n, 2) - 1, 0)
    nlast = n - MOE_BIG_UNITS * nmain
    base = tile * tmp
    wd = wd_ref[0].reshape(EPG * EXPERT_FF, D_MODEL)

    def experts(rows):
        xb = xs_ref[rows, :]
        cwb = cws_ref[rows, :]
        hs = []
        for e in range(EPG):
            gg_ = _mm(xb, wg_ref[0, e])
            uu = _mm(xb, wu_ref[0, e])
            hs.append((gg_ * _sigmoid(gg_) * uu * cwb[:, e:e + 1]).astype(BF16))
        return _mm(jnp.concatenate(hs, axis=1), wd).astype(BF16)

    def big_copy(slot, blk):
        return pltpu.make_async_copy(
            ybuf.at[slot], ys_hbm.at[pl.ds(pl.multiple_of(base + blk * MOE_BLK, MOE_BLK), MOE_BIG), :], sem.at[slot])

    def last_copy(units, blk):
        rows = units * MOE_BLK
        return pltpu.make_async_copy(
            ylast.at[0:rows], ys_hbm.at[pl.ds(pl.multiple_of(base + blk * MOE_BLK, MOE_BLK), rows), :], sem.at[2])

    def wait_last():
        for u in range(1, MOE_LAST_MAX + 1):
            @pl.when(st_ref[1] == u)
            def _(u=u):
                last_copy(u, 0).wait()

    def zero_copy(blk):
        return pltpu.make_async_copy(
            zbuf, ys_hbm.at[pl.ds(pl.multiple_of(base + blk * MOE_BLK, MOE_BLK), MOE_BLK), :], sem.at[3])

    @pl.when(q == 0)
    def _():
        for i in range(3):
            st_ref[i] = 0
        zbuf[...] = jnp.zeros_like(zbuf)

    def body(k, carry):
        c = st_ref[0]
        slot = jnp.bitwise_and(c, 1)
        blk = off + MOE_BIG_UNITS * k
        y = experts(pl.ds(pl.multiple_of(blk * MOE_BLK, MOE_BLK), MOE_BIG))

        @pl.when(c >= 2)
        def _():
            big_copy(slot, 0).wait()

        ybuf[slot] = y
        big_copy(slot, blk).start()
        st_ref[0] = c + 1
        return carry

    lax.fori_loop(0, nmain, body, 0)

    @pl.when(nlast > 0)
    def _():
        wait_last()

    lblk = off + MOE_BIG_UNITS * nmain
    for u in range(1, MOE_LAST_MAX + 1):
        @pl.when(nlast == u)
        def _(u=u):
            ylast[0:u * MOE_BLK, :] = experts(pl.ds(pl.multiple_of(lblk * MOE_BLK, MOE_BLK), u * MOE_BLK))
            last_copy(u, lblk).start()
            st_ref[1] = u

    @pl.when(grp == N_GROUPS - 1)
    def _():
        def zfill(blk, carry):
            zero_copy(blk).start()
            return carry

        nz = tmp // MOE_BLK - nused_ref[tile]
        lax.fori_loop(nused_ref[tile], tmp // MOE_BLK, zfill, 0)
        st_ref[2] = st_ref[2] + nz

    @pl.when(q == pl.num_programs(0) - 1)
    def _():
        c = st_ref[0]

        @pl.when(c >= 2)
        def _():
            big_copy(jnp.bitwise_and(c, 1), 0).wait()

        @pl.when(c >= 1)
        def _():
            big_copy(jnp.bitwise_and(c - 1, 1), 0).wait()

        wait_last()

        def zwait(i, carry):
            zero_copy(0).wait()
            return carry

        lax.fori_loop(0, st_ref[2], zwait, 0)


def _moe_unsort_kernel(ys_ref, pos_ref, x1_ref, mod_ref, lng_ref, lnb_ref, out_ref, *, nb, r):
    tm = nb * r
    tmp = tm + N_GROUPS * MOE_BLK
    d = D_MODEL
    ysb = ys_ref[...]
    posc = jnp.broadcast_to(pos_ref[0][0:1, :], (LANE, tm)).T
    lng = lng_ref[...]
    lnb = lnb_ref[...]
    for c in range(tm // MOE_PC):
        slot = lax.broadcasted_iota(jnp.int32, (MOE_PC, tmp), 1).astype(F32)
        unperm = jnp.where(slot == posc[c * MOE_PC:(c + 1) * MOE_PC, 0:1], 1.0, 0.0).astype(BF16)
        y = _mm(unperm, ysb)
        if nb == 1:
            x1c = x1_ref[0, c * MOE_PC:(c + 1) * MOE_PC, :]
            gt2 = mod_ref[0][:, 5 * d:6 * d]
            out_ref[0, c * MOE_PC:(c + 1) * MOE_PC, :] = _layer_norm(DEEPNORM_ALPHA * x1c + gt2 * y, lng, lnb)
        else:
            cb = MOE_PC // r
            x1c = x1_ref[c * cb:(c + 1) * cb]
            gt2 = mod_ref[c * cb:(c + 1) * cb][:, :, 5 * d:6 * d]
            yy = DEEPNORM_ALPHA * x1c + gt2 * y.reshape(cb, r, d)
            out_ref[c * cb:(c + 1) * cb] = _layer_norm(yy, lng, lnb)


def _moe(streams, wrt, brt, wg, wu, wd, lng, lnb):
    d = D_MODEL
    tm = streams[0][2] * streams[0][3]
    tmp = tm + N_GROUPS * MOE_BLK
    const2 = lambda i: (0, 0)
    params = pltpu.CompilerParams(dimension_semantics=("arbitrary",), vmem_limit_bytes=VMEM_LIMIT)
    geo = []
    ntiles = 0
    for x1, _, nb, r in streams:
        assert nb * r == tm
        tpb = x1.shape[1] // r
        n = (x1.shape[0] // nb) * tpb
        geo.append((tpb, n, ntiles))
        ntiles += n

    sorted_shapes = [jax.ShapeDtypeStruct((ntiles * tmp, d), BF16),
                     jax.ShapeDtypeStruct((ntiles * tmp, LANE), F32),
                     jax.ShapeDtypeStruct((ntiles, EPG, tm), F32),
                     jax.ShapeDtypeStruct((ntiles, EPG, LANE), F32)]
    bufs = []
    for (x1, mod, nb, r), (tpb, n, t0) in zip(streams, geo):
        steps = ntiles if not bufs else n
        xmap = lambda i, tpb=tpb, n=n: (jnp.minimum(i, n - 1) // tpb, jnp.minimum(i, n - 1) % tpb, 0)
        mmap = lambda i, tpb=tpb, n=n: (jnp.minimum(i, n - 1) // tpb, 0, 0)
        bufs = pl.pallas_call(
            functools.partial(_moe_sort_kernel, nb=nb, r=r, nreal=n),
            out_shape=sorted_shapes,
            grid=(steps,),
            in_specs=[
                pl.BlockSpec((nb, r, d), xmap),
                pl.BlockSpec((nb, 1, 6 * d), mmap),
                pl.BlockSpec((LANE, d), const2),
                pl.BlockSpec((LANE, 1), const2),
            ] + [pl.BlockSpec(memory_space=pl.ANY)] * len(bufs),
            out_specs=[pl.BlockSpec((tmp, d), lambda i, t0=t0: (i + t0, 0)),
                       pl.BlockSpec((tmp, LANE), lambda i, t0=t0: (i + t0, 0)),
                       pl.BlockSpec((1, EPG, tm), lambda i, t0=t0: (i + t0, 0, 0)),
                       pl.BlockSpec((1, EPG, LANE), lambda i, t0=t0: (i + t0, 0, 0))],
            input_output_aliases={4 + k: k for k in range(len(bufs))},
            compiler_params=params,
            name="moe_sort",
        )(x1, mod, wrt, brt, *bufs)
    xs, cws, pos, cnt = bufs

    nblk = ((cnt[:, :N_GROUPS, 0].astype(jnp.int32) + (MOE_BLK - 1)) // MOE_BLK)
    boff = jnp.cumsum(nblk, axis=1) - nblk
    nused = jnp.sum(nblk, axis=1).astype(jnp.int32)
    tix = jnp.arange(ntiles, dtype=jnp.int32)
    n_g = nblk.T
    before = (n_g[:, None, :] < n_g[:, :, None]) | ((n_g[:, None, :] == n_g[:, :, None]) & (tix[None, None, :]
                                                                                            < tix[None, :, None]))
    rank = jnp.sum(before.astype(jnp.int32), axis=2)
    hit = (rank[:, None, :] == tix[None, :, None]).astype(jnp.int32)
    tile_q = jnp.sum(hit * tix[None, None, :], axis=2).reshape(-1).astype(jnp.int32)
    nblk_q = jnp.sum(hit * n_g[:, None, :], axis=2).reshape(-1).astype(jnp.int32)
    boff_q = jnp.sum(hit * boff.T[:, None, :], axis=2).reshape(-1).astype(jnp.int32)

    tmap = lambda q, tq, bo, nk, nu: (tq[q], 0)
    wmap = lambda q, tq, bo, nk, nu: (q // ntiles, 0, 0, 0)
    ys = pl.pallas_call(
        functools.partial(_moe_expert_kernel, ntiles=ntiles, tmp=tmp),
        out_shape=jax.ShapeDtypeStruct((ntiles * tmp, d), BF16),
        grid_spec=pltpu.PrefetchScalarGridSpec(
            num_scalar_prefetch=4,
            grid=(N_GROUPS * ntiles,),
            in_specs=[
                pl.BlockSpec((tmp, d), tmap),
                pl.BlockSpec((tmp, LANE), tmap),
                pl.BlockSpec((1, EPG, d, EXPERT_FF), wmap),
                pl.BlockSpec((1, EPG, d, EXPERT_FF), wmap),
                pl.BlockSpec((1, EPG, EXPERT_FF, d), wmap),
            ],
            out_specs=pl.BlockSpec(memory_space=pl.ANY),
            scratch_shapes=[
                pltpu.VMEM((2, MOE_BIG, d), BF16),
                pltpu.VMEM((MOE_LAST_MAX * MOE_BLK, d), BF16),
                pltpu.VMEM((MOE_BLK, d), BF16),
                pltpu.SemaphoreType.DMA((4,)),
                pltpu.SMEM((4,), jnp.int32),
            ],
        ),
        compiler_params=params,
        name="moe_experts",
    )(tile_q, boff_q, nblk_q, nused, xs, cws, wg, wu, wd)

    outs = []
    for (x1, mod, nb, r), (tpb, n, t0) in zip(streams, geo):
        xmap = lambda i, tpb=tpb: (i // tpb, i % tpb, 0)
        mmap = lambda i, tpb=tpb: (i // tpb, 0, 0)
        outs.append(pl.pallas_call(
            functools.partial(_moe_unsort_kernel, nb=nb, r=r),
            out_shape=jax.ShapeDtypeStruct(x1.shape, F32),
            grid=(n,),
            in_specs=[
                pl.BlockSpec((tmp, d), lambda i, t0=t0: (i + t0, 0)),
                pl.BlockSpec((1, EPG, tm), lambda i, t0=t0: (i + t0, 0, 0)),
                pl.BlockSpec((nb, r, d), xmap),
                pl.BlockSpec((nb, 1, 6 * d), mmap),
                pl.BlockSpec((1, d), const2),
                pl.BlockSpec((1, d), const2),
            ],
            out_specs=pl.BlockSpec((nb, r, d), xmap),
            compiler_params=params,
            name="moe_unsort",
        )(ys, pos, x1, mod, lng, lnb))
    return outs


def kernel(x_prompt, x_sample, c_prompt, c_sample, state_gla, cache_swa_k, cache_swa_v, ada_w, ada_b, w_in,
           gla_a2_w, gla_a2_b, gla_norm_w, swa_sinks, w_o, ln1_g, ln1_b, router_g_w, router_g_b, router_e_w,
           router_e_b, moe_w_gate, moe_w_up, moe_w_down, ln2_g, ln2_b):
    assert ada_w.shape[0] == 1
    bp = x_prompt.shape[0]
    bs, ss, d = x_sample.shape
    lc = cache_swa_k.shape[2]

    w = w_in[0]
    zpad = jnp.zeros((d, LANE - GLA_RANK), F32)
    w_sq = w[:, 1552:2064].reshape(d, SWA_KV_HEADS, SWA_GROUP, SWA_DH).transpose(0, 2, 1, 3).reshape(d, SWA_W)
    win = jnp.concatenate([w[:, 0:1536], w_sq, w[:, 2064:2320], w[:, 1536:1552], zpad], axis=1).astype(BF16)
    a2w = jnp.concatenate([gla_a2_w[0], jnp.zeros((LANE - GLA_RANK, GLA_KW), F32)], axis=0).astype(BF16)
    a2b = gla_a2_b[0].reshape(1, GLA_KW)
    gnw = gla_norm_w[0].reshape(1, GLA_DV)
    wo_swa = w_o[0][GLA_VW:].reshape(SWA_KV_HEADS, SWA_GROUP, SWA_DH, d).transpose(1, 0, 2, 3).reshape(SWA_W, d)
    wo = jnp.concatenate([w_o[0][:GLA_VW], wo_swa], axis=0).astype(BF16)
    sinks = swa_sinks[0]
    wrt = jnp.concatenate([router_g_w[0], jnp.zeros((d, R_EXP0 - N_GROUPS), F32),
                           jnp.transpose(router_e_w[0], (1, 0, 2)).reshape(d, N_GROUPS * EPG),
                           jnp.zeros((d, LANE - R_EXP0 - N_GROUPS * EPG), F32)], axis=1).T.astype(BF16)
    brt = jnp.concatenate([router_g_b[0], jnp.zeros((R_EXP0 - N_GROUPS,), F32), router_e_b[0].reshape(-1),
                           jnp.zeros((LANE - R_EXP0 - N_GROUPS * EPG,), F32)]).reshape(LANE, 1)
    wg = moe_w_gate[0].astype(BF16)
    wu = moe_w_up[0].astype(BF16)
    wd = moe_w_down[0].astype(BF16)
    lng1, lnb1 = ln1_g[0].reshape(1, d), ln1_b[0].reshape(1, d)
    lng2, lnb2 = ln2_g[0].reshape(1, d), ln2_b[0].reshape(1, d)

    mod = _adaln(jnp.concatenate([c_prompt, c_sample], axis=0), ada_w[0], ada_b[0].reshape(1, 6 * d))
    mod = mod.reshape(bp + bs, 1, 6 * d)
    mod_p, mod_s = mod[:bp], mod[bp:]

    x1p, s_p, k_p, v_p = _mixer_prompt(x_prompt, mod_p, sinks, win, a2w, a2b, gnw, wo, lng1, lnb1)
    x1s, s_s, k_s, v_s = _mixer_sample(
        x_sample, mod_s, state_gla[0], cache_swa_k[0].reshape(bs, lc, LANE), cache_swa_v[0].reshape(bs, lc, LANE),
        sinks, win, a2w, a2b, gnw, wo, lng1, lnb1)

    yp, ys = _moe([(x1p, mod_p, 1, bs * ss), (x1s, mod_s, bs, ss)], wrt, brt, wg, wu, wd, lng2, lnb2)

    kv_shape_p = (1, bp, WINDOW, SWA_KV_HEADS, SWA_DH)
    kv_shape_s = (1, bs, lc, SWA_KV_HEADS, SWA_DH)
    return (yp, ys, s_p[None], k_p.reshape(kv_shape_p), v_p.reshape(kv_shape_p),
            s_s[None], k_s.reshape(kv_shape_s), v_s.reshape(kv_shape_s))
```

```python
import functools

import jax
import jax.numpy as jnp
import numpy as np
from jax import lax
from jax.experimental import pallas as pl
from jax.experimental.pallas import tpu as pltpu

F32 = jnp.float32
BF16 = jnp.bfloat16

D_MODEL = 1024
CHUNK = 64
GLA_HEADS = 4
GLA_DK = 64
GLA_DV = 128
GLA_KW = GLA_HEADS * GLA_DK
GLA_VW = GLA_HEADS * GLA_DV
GLA_RANK = 16
GLA_TAU = 16.0
SWA_Q_HEADS = 8
SWA_KV_HEADS = 2
SWA_GROUP = 4
SWA_DH = 64
SWA_W = SWA_Q_HEADS * SWA_DH
WINDOW = 128
N_GROUPS = 4
EPG = 8
EXPERT_FF = 256
DEEPNORM_ALPHA = 2.0 ** 0.25
LN_EPS = 1e-5
NEG_INF = -1e30

C_GQ, C_GK, C_GV, C_GR, C_SQ, C_SK, C_SV, C_GA = 0, 256, 512, 1024, 1536, 2048, 2176, 2304
PROJ_W = 2432
LANE = 128
R_EXP0 = 8

V7X_VMEM_BYTES = 64 * 1024 * 1024
VMEM_LIMIT = V7X_VMEM_BYTES * 7 // 8
MIXER_TILE = 512
SAMPLE_SEQS_PER_STEP = 8
MOE_TILE = 1024


def _mm(a, b):
    return jnp.dot(a, b, preferred_element_type=F32)


def _mm_nt(a, b):
    return lax.dot_general(a, b, (((1,), (1,)), ((), ())), preferred_element_type=F32)


def _mm_tn(a, b):
    return lax.dot_general(a, b, (((0,), (0,)), ((), ())), preferred_element_type=F32)


def _split_bf16(a):
    hi = a.astype(BF16)
    lo = (a - hi.astype(F32)).astype(BF16)
    return hi, lo


def _sigmoid(x):
    return 1.0 / (1.0 + jnp.exp(-x))


def _layer_norm(y, g, b):
    mu = jnp.mean(y, axis=-1, keepdims=True)
    d = y - mu
    var = jnp.mean(d * d, axis=-1, keepdims=True)
    return d * lax.rsqrt(var + LN_EPS) * g + b


def _adaln_kernel(c_ref, w_ref, b_ref, o_ref):
    c = c_ref[...]
    a = c * _sigmoid(c)
    a_hi, a_lo = _split_bf16(a)
    w_hi, w_lo = _split_bf16(w_ref[...])
    o_ref[...] = _mm(a_hi, w_hi) + (_mm(a_hi, w_lo) + _mm(a_lo, w_hi)) + b_ref[...]


def _adaln(c_all, ada_w, ada_b):
    n = c_all.shape[0]
    bn = 1024
    return pl.pallas_call(
        _adaln_kernel,
        out_shape=jax.ShapeDtypeStruct((n, 6 * D_MODEL), F32),
        grid=(6 * D_MODEL // bn,),
        in_specs=[pl.BlockSpec((n, D_MODEL), lambda j: (0, 0)),
                  pl.BlockSpec((D_MODEL, bn), lambda j: (0, j)),
                  pl.BlockSpec((1, bn), lambda j: (0, j))],
        out_specs=pl.BlockSpec((n, bn), lambda j: (0, j)),
        compiler_params=pltpu.CompilerParams(dimension_semantics=("arbitrary",), vmem_limit_bytes=VMEM_LIMIT),
        name="adaln",
    )(c_all, ada_w, ada_b)


def _mixer_consts(nrows, chunk, qb, kw):
    r = np.arange(nrows)
    tri = ((r[:, None] // chunk == r[None, :] // chunk) & (r[:, None] >= r[None, :])).astype(np.float32)
    hs = np.arange(GLA_HEADS * chunk)
    mkk = (hs[:, None] // chunk == np.arange(GLA_KW)[None, :] // GLA_DK).astype(np.float32)
    mv = (hs[:, None] // chunk == np.arange(GLA_VW)[None, :] // GLA_DV).astype(np.float32)
    ms = (np.arange(GLA_VW)[:, None] // GLA_DV == np.arange(GLA_KW)[None, :] // GLA_DK).astype(np.float32)
    caus = (np.arange(chunk)[:, None] >= (hs[None, :] % chunk)).astype(np.float32)
    t = np.arange(qb)
    kj = np.arange(kw)
    cs = (t // chunk) * chunk
    kpos = kj[None, :] - WINDOW
    vis = (kpos >= cs[:, None] - WINDOW) & (kpos < cs[:, None] + chunk)
    dist = np.abs(t[:, None] + WINDOW - kj[None, :]).astype(np.float32)
    bias = np.zeros((SWA_KV_HEADS, SWA_GROUP * qb, kw), np.float32)
    for hk in range(SWA_KV_HEADS):
        for g in range(SWA_GROUP):
            slope = np.float32(2.0 ** (-(hk * SWA_GROUP + g + 1)))
            bias[hk, g * qb:(g + 1) * qb] = np.where(vis, -slope * dist, np.float32(2.0 * NEG_INF))
    return (jnp.asarray(tri, BF16), jnp.asarray(bias), jnp.asarray(mkk, BF16), jnp.asarray(mv, BF16),
            jnp.asarray(ms), jnp.asarray(caus))


def _modulate(x, mod):
    return (x * (1.0 + mod[:, D_MODEL:2 * D_MODEL]) + mod[:, 0:D_MODEL]).astype(BF16)


def _log_gates(proj_ref, a2w_ref, a2b_ref):
    ga = proj_ref[:, C_GA:C_GA + LANE].astype(BF16)
    z = _mm(ga, a2w_ref[...]) + a2b_ref[...]
    lsig = -(jnp.maximum(-z, 0.0) + jnp.log(1.0 + jnp.exp(-jnp.abs(z))))
    return lsig * (1.0 / GLA_TAU)


def _gla_prep(loga, tri_ref, proj_ref, chunk, qe_ref, qn_ref, ke_ref, kn_ref, kw_ref, eb_ref, vb_ref):
    nrows = loga.shape[0]
    la_hi, la_lo = _split_bf16(loga)
    tri = tri_ref[...]
    b = _mm(tri, la_hi) + _mm(tri, la_lo)
    b_end = jnp.concatenate(
        [jnp.broadcast_to(b[c * chunk + chunk - 1:(c + 1) * chunk, :], (chunk, GLA_KW))
         for c in range(nrows // chunk)], axis=0)
    eb = jnp.exp(b)
    ebn = jnp.exp(-b)
    wk = jnp.exp(b_end - b)
    q = proj_ref[:, C_GQ:C_GQ + GLA_KW] * (GLA_DK ** -0.5)
    k = proj_ref[:, C_GK:C_GK + GLA_KW]
    eb_ref[...] = eb
    qe_ref[...] = (q * eb).astype(BF16)
    qn_ref[...] = (q * ebn).astype(BF16)
    ke_ref[...] = (k * eb).astype(BF16)
    kn_ref[...] = (k * ebn).astype(BF16)
    kw_ref[...] = (k * wk).astype(BF16)
    vb_ref[...] = proj_ref[:, C_GV:C_GV + GLA_VW].astype(BF16)


def _gla_chunk(r0, chunk, qe_ref, qn_ref, ke_ref, kn_ref, kw_ref, eb_ref, vb_ref, sbt_ref, mkk, mv, ms, caus):
    rows = slice(r0, r0 + chunk)
    qe = qe_ref[rows, :]
    qn = qn_ref[rows, :]
    zero = jnp.zeros((), BF16)
    kn4 = jnp.where(mkk != 0, jnp.concatenate([kn_ref[rows, :]] * GLA_HEADS, axis=0), zero)
    ke4 = jnp.where(mkk != 0, jnp.concatenate([ke_ref[rows, :]] * GLA_HEADS, axis=0), zero)
    a_lo = _mm_nt(qe, kn4)
    a_up = _mm_nt(qn, ke4)
    a = jnp.where(caus != 0.0, a_lo, a_up).astype(BF16)
    v = vb_ref[rows, :]
    v4 = jnp.where(mv != 0, jnp.concatenate([v] * GLA_HEADS, axis=0), zero)
    sbt = sbt_ref[...]
    o = _mm(a, v4) + _mm_nt(qe, sbt.astype(BF16))
    ut = _mm_tn(v, kw_ref[rows, :])
    g_end = eb_ref[r0 + chunk - 1:r0 + chunk, :]
    sbt_ref[...] = g_end * sbt + jnp.where(ms != 0.0, ut, 0.0)
    return o


def _gla_post(o, gr, gnw):
    res = []
    for h in range(GLA_HEADS):
        oh = o[:, h * GLA_DV:(h + 1) * GLA_DV]
        ms_ = jnp.mean(oh * oh, axis=-1, keepdims=True)
        og = oh * lax.rsqrt(ms_ + LN_EPS) * gnw
        r = gr[:, h * GLA_DV:(h + 1) * GLA_DV]
        res.append((og * (r * _sigmoid(r))).astype(BF16))
    return res


def _swa_scores(sq, kwin, qb):
    low = lax.broadcasted_iota(jnp.int32, (qb, LANE), 1) < SWA_DH
    res = []
    for hk in range(SWA_KV_HEADS):
        keep = low if hk == 0 else jnp.logical_not(low)
        qs = jnp.concatenate([jnp.where(keep, sq[:, g * LANE:(g + 1) * LANE] * (SWA_DH ** -0.5), 0.0)
                              for g in range(SWA_GROUP)], axis=0).astype(BF16)
        res.append(_mm_nt(qs, kwin))
    return res


def _swa_finish(s, vwin, bias, sinks_ref, hk, extra_valid, qb):
    ok = bias > -1e29
    if extra_valid is not None:
        ok = ok & extra_valid
    s = jnp.where(ok, s + bias, NEG_INF)
    ps, denoms = [], []
    for g in range(SWA_GROUP):
        sg = s[g * qb:(g + 1) * qb, :]
        sink = sinks_ref[hk * SWA_GROUP + g]
        m = jnp.maximum(jnp.max(sg, axis=-1, keepdims=True), sink)
        pg = jnp.exp(sg - m)
        denoms.append(jnp.sum(pg, axis=-1, keepdims=True) + jnp.exp(sink - m))
        ps.append(pg.astype(BF16))
    o = _mm(jnp.concatenate(ps, axis=0), vwin)
    return o / jnp.concatenate(denoms, axis=0)


def _swa_merge(o0, o1, qb):
    low = lax.broadcasted_iota(jnp.int32, (qb, LANE), 1) < SWA_DH
    return [jnp.where(low, o0[g * qb:(g + 1) * qb, :], o1[g * qb:(g + 1) * qb, :]).astype(BF16)
            for g in range(SWA_GROUP)]


def _out_proj_ln(x, gt1, mixed_ref, wo_ref, ln_g, ln_b):
    mix = _mm(mixed_ref[...], wo_ref[...])
    return _layer_norm(DEEPNORM_ALPHA * x + gt1 * mix, ln_g, ln_b)


P_QB = 128
P_KW = WINDOW + P_QB


def _mixer_prompt_kernel(sinks_ref, x_ref, mod_ref, win_ref, a2w_ref, a2b_ref, gnw_ref, wo_ref, lng_ref, lnb_ref,
                         tri_ref, bias_ref, mkk_ref, mv_ref, ms_ref, caus_ref,
                         x1_ref, s_out_ref, k_out_ref, v_out_ref,
                         proj_ref, qe_ref, qn_ref, ke_ref, kn_ref, kw_ref, eb_ref, vb_ref, kbuf, vbuf, mixed_ref,
                         sbt_ref, *, tl):
    j = pl.program_id(1)
    nj = pl.num_programs(1)
    d = D_MODEL

    @pl.when(j == 0)
    def _():
        sbt_ref[...] = jnp.zeros_like(sbt_ref)
        kbuf[0:WINDOW, :] = jnp.zeros((WINDOW, LANE), BF16)
        vbuf[0:WINDOW, :] = jnp.zeros((WINDOW, LANE), BF16)

    x = x_ref[0]
    mod = mod_ref[0]
    proj_ref[...] = _mm(_modulate(x, mod), win_ref[...])
    loga = _log_gates(proj_ref, a2w_ref, a2b_ref)
    _gla_prep(loga, tri_ref, proj_ref, CHUNK, qe_ref, qn_ref, ke_ref, kn_ref, kw_ref, eb_ref, vb_ref)
    kbuf[WINDOW:WINDOW + tl, :] = proj_ref[:, C_SK:C_SK + LANE].astype(BF16)
    vbuf[WINDOW:WINDOW + tl, :] = proj_ref[:, C_SV:C_SV + LANE].astype(BF16)

    mkk = mkk_ref[...]
    mv = mv_ref[...]
    ms = ms_ref[...]
    caus = caus_ref[...]
    gnw = gnw_ref[...]
    kj = lax.broadcasted_iota(jnp.int32, (SWA_GROUP * P_QB, P_KW), 1)
    first_valid = kj >= jnp.where(j > 0, 0, WINDOW)

    def scores(p):
        q0 = p * P_QB
        return _swa_scores(proj_ref[q0:q0 + P_QB, C_SQ:C_SQ + SWA_W], kbuf[q0:q0 + P_KW, :], P_QB)

    def gla(r0):
        o = _gla_chunk(r0, CHUNK, qe_ref, qn_ref, ke_ref, kn_ref, kw_ref, eb_ref, vb_ref, sbt_ref, mkk, mv, ms, caus)
        og = _gla_post(o, proj_ref[r0:r0 + CHUNK, C_GR:C_GR + GLA_VW], gnw)
        for hh in range(GLA_HEADS):
            mixed_ref[r0:r0 + CHUNK, hh * GLA_DV:(hh + 1) * GLA_DV] = og[hh]

    for p in range(tl // P_QB):
        q0 = p * P_QB
        s = scores(p)
        vwin = vbuf[q0:q0 + P_KW, :]
        extra = first_valid if p == 0 else None
        o0 = _swa_finish(s[0], vwin, bias_ref[0], sinks_ref, 0, extra, P_QB)
        o1 = _swa_finish(s[1], vwin, bias_ref[1], sinks_ref, 1, extra, P_QB)
        for i, blk in enumerate(_swa_merge(o0, o1, P_QB)):
            mixed_ref[q0:q0 + P_QB, GLA_VW + i * LANE:GLA_VW + (i + 1) * LANE] = blk
        gla(q0)
        gla(q0 + CHUNK)

    x1_ref[0] = _out_proj_ln(x, mod[:, 2 * d:3 * d], mixed_ref, wo_ref, lng_ref[...], lnb_ref[...])

    kbuf[0:WINDOW, :] = kbuf[tl:tl + WINDOW, :]
    vbuf[0:WINDOW, :] = vbuf[tl:tl + WINDOW, :]

    @pl.when(j == nj - 1)
    def _():
        for hh in range(GLA_HEADS):
            s_out_ref[0, hh] = sbt_ref[hh * GLA_DV:(hh + 1) * GLA_DV, hh * GLA_DK:(hh + 1) * GLA_DK].T
        k_out_ref[0] = proj_ref[tl - WINDOW:tl, C_SK:C_SK + LANE]
        v_out_ref[0] = proj_ref[tl - WINDOW:tl, C_SV:C_SV + LANE]


def _mixer_prompt(x, mod, sinks, win, a2w, a2b, gnw, wo, lng, lnb, *, tl=MIXER_TILE):
    b, l, d = x.shape
    nj = l // tl
    consts = _mixer_consts(tl, CHUNK, P_QB, P_KW)
    const2 = lambda i, j, s: (0, 0)
    const3 = lambda i, j, s: (0, 0, 0)
    grid_spec = pltpu.PrefetchScalarGridSpec(
        num_scalar_prefetch=1,
        grid=(b, nj),
        in_specs=[
            pl.BlockSpec((1, tl, d), lambda i, j, s: (i, j, 0)),
            pl.BlockSpec((1, 1, 6 * d), lambda i, j, s: (i, 0, 0)),
            pl.BlockSpec((d, PROJ_W), const2),
            pl.BlockSpec((LANE, GLA_KW), const2),
            pl.BlockSpec((1, GLA_KW), const2),
            pl.BlockSpec((1, GLA_DV), const2),
            pl.BlockSpec((d, d), const2),
            pl.BlockSpec((1, d), const2),
            pl.BlockSpec((1, d), const2),
            pl.BlockSpec((tl, tl), const2),
            pl.BlockSpec((SWA_KV_HEADS, SWA_GROUP * P_QB, P_KW), const3),
            pl.BlockSpec((GLA_HEADS * CHUNK, GLA_KW), const2),
            pl.BlockSpec((GLA_HEADS * CHUNK, GLA_VW), const2),
            pl.BlockSpec((GLA_VW, GLA_KW), const2),
            pl.BlockSpec((CHUNK, GLA_HEADS * CHUNK), const2),
        ],
        out_specs=[
            pl.BlockSpec((1, tl, d), lambda i, j, s: (i, j, 0)),
            pl.BlockSpec((1, GLA_HEADS, GLA_DK, GLA_DV), lambda i, j, s: (i, 0, 0, 0)),
            pl.BlockSpec((1, WINDOW, LANE), lambda i, j, s: (i, 0, 0)),
            pl.BlockSpec((1, WINDOW, LANE), lambda i, j, s: (i, 0, 0)),
        ],
        scratch_shapes=[
            pltpu.VMEM((tl, PROJ_W), F32),
            pltpu.VMEM((tl, GLA_KW), BF16),
            pltpu.VMEM((tl, GLA_KW), BF16),
            pltpu.VMEM((tl, GLA_KW), BF16),
            pltpu.VMEM((tl, GLA_KW), BF16),
            pltpu.VMEM((tl, GLA_KW), BF16),
            pltpu.VMEM((tl, GLA_KW), F32),
            pltpu.VMEM((tl, GLA_VW), BF16),
            pltpu.VMEM((WINDOW + tl, LANE), BF16),
            pltpu.VMEM((WINDOW + tl, LANE), BF16),
            pltpu.VMEM((tl, d), BF16),
            pltpu.VMEM((GLA_VW, GLA_KW), F32),
        ],
    )
    return pl.pallas_call(
        functools.partial(_mixer_prompt_kernel, tl=tl),
        out_shape=[
            jax.ShapeDtypeStruct((b, l, d), F32),
            jax.ShapeDtypeStruct((b, GLA_HEADS, GLA_DK, GLA_DV), F32),
            jax.ShapeDtypeStruct((b, WINDOW, LANE), F32),
            jax.ShapeDtypeStruct((b, WINDOW, LANE), F32),
        ],
        grid_spec=grid_spec,
        compiler_params=pltpu.CompilerParams(dimension_semantics=("arbitrary", "arbitrary"),
                                             vmem_limit_bytes=VMEM_LIMIT),
        name="mixer_prompt",
    )(sinks, x, mod, win, a2w, a2b, gnw, wo, lng, lnb, *consts)


def _mixer_sample_kernel(sinks_ref, x_ref, mod_ref, s0_ref, kc_ref, vc_ref, win_ref, a2w_ref, a2b_ref, gnw_ref,
                         wo_ref, lng_ref, lnb_ref, tri_ref, bias_ref, mkk_ref, mv_ref, ms_ref, caus_ref,
                         x1_ref, s_out_ref, k_out_ref, v_out_ref,
                         proj_ref, qe_ref, qn_ref, ke_ref, kn_ref, kw_ref, eb_ref, vb_ref, kbuf, vbuf, mixed_ref,
                         sbt_ref, xm_ref, *, nb, s):
    nkeys = WINDOW + s
    d = D_MODEL
    for bb in range(nb):
        m = mod_ref[bb]
        xm_ref[bb * s:(bb + 1) * s, :] = x_ref[bb] * (1.0 + m[:, d:2 * d]) + m[:, 0:d]
    proj_ref[...] = _mm(xm_ref[...].astype(BF16), win_ref[...])
    loga = _log_gates(proj_ref, a2w_ref, a2b_ref)
    _gla_prep(loga, tri_ref, proj_ref, s, qe_ref, qn_ref, ke_ref, kn_ref, kw_ref, eb_ref, vb_ref)

    mkk = mkk_ref[...]
    mv = mv_ref[...]
    ms = ms_ref[...]
    caus = caus_ref[...]
    gnw = gnw_ref[...]

    for bb in range(nb):
        r0 = bb * s
        rows = slice(r0, r0 + s)
        sbt_ref[...] = jnp.zeros_like(sbt_ref)
        for hh in range(GLA_HEADS):
            sbt_ref[hh * GLA_DV:(hh + 1) * GLA_DV, hh * GLA_DK:(hh + 1) * GLA_DK] = s0_ref[bb, hh].T
        kbuf[0:WINDOW, :] = kc_ref[bb].astype(BF16)
        vbuf[0:WINDOW, :] = vc_ref[bb].astype(BF16)
        kbuf[WINDOW:nkeys, :] = proj_ref[rows, C_SK:C_SK + LANE].astype(BF16)
        vbuf[WINDOW:nkeys, :] = proj_ref[rows, C_SV:C_SV + LANE].astype(BF16)
        sc = _swa_scores(proj_ref[rows, C_SQ:C_SQ + SWA_W], kbuf[...], s)
        vwin = vbuf[...]
        blocks = _swa_merge(_swa_finish(sc[0], vwin, bias_ref[0], sinks_ref, 0, None, s),
                            _swa_finish(sc[1], vwin, bias_ref[1], sinks_ref, 1, None, s), s)
        for i, blk in enumerate(blocks):
            mixed_ref[rows, GLA_VW + i * LANE:GLA_VW + (i + 1) * LANE] = blk
        o = _gla_chunk(r0, s, qe_ref, qn_ref, ke_ref, kn_ref, kw_ref, eb_ref, vb_ref, sbt_ref, mkk, mv, ms, caus)
        og = _gla_post(o, proj_ref[rows, C_GR:C_GR + GLA_VW], gnw)
        for hh in range(GLA_HEADS):
            mixed_ref[rows, hh * GLA_DV:(hh + 1) * GLA_DV] = og[hh]
            s_out_ref[bb, hh] = sbt_ref[hh * GLA_DV:(hh + 1) * GLA_DV, hh * GLA_DK:(hh + 1) * GLA_DK].T
        k_out_ref[bb, 0:WINDOW - s, :] = kc_ref[bb, s:WINDOW, :]
        v_out_ref[bb, 0:WINDOW - s, :] = vc_ref[bb, s:WINDOW, :]
        k_out_ref[bb, WINDOW - s:WINDOW, :] = proj_ref[rows, C_SK:C_SK + LANE]
        v_out_ref[bb, WINDOW - s:WINDOW, :] = proj_ref[rows, C_SV:C_SV + LANE]

    mix = _mm(mixed_ref[...], wo_ref[...])
    lng = lng_ref[...]
    lnb = lnb_ref[...]
    for bb in range(nb):
        m = mod_ref[bb]
        y = DEEPNORM_ALPHA * x_ref[bb] + m[:, 2 * d:3 * d] * mix[bb * s:(bb + 1) * s, :]
        x1_ref[bb] = _layer_norm(y, lng, lnb)


def _mixer_sample(x, mod, s0, kc, vc, sinks, win, a2w, a2b, gnw, wo, lng, lnb, *, nb=SAMPLE_SEQS_PER_STEP):
    b, s, d = x.shape
    assert kc.shape[1] == WINDOW and s <= WINDOW
    rows = nb * s
    nkeys = WINDOW + s
    consts = _mixer_consts(rows, s, s, nkeys)
    const2 = lambda i, sk: (0, 0)
    const3 = lambda i, sk: (0, 0, 0)
    grid_spec = pltpu.PrefetchScalarGridSpec(
        num_scalar_prefetch=1,
        grid=(b // nb,),
        in_specs=[
            pl.BlockSpec((nb, s, d), lambda i, sk: (i, 0, 0)),
            pl.BlockSpec((nb, 1, 6 * d), lambda i, sk: (i, 0, 0)),
            pl.BlockSpec((nb, GLA_HEADS, GLA_DK, GLA_DV), lambda i, sk: (i, 0, 0, 0)),
            pl.BlockSpec((nb, WINDOW, LANE), lambda i, sk: (i, 0, 0)),
            pl.BlockSpec((nb, WINDOW, LANE), lambda i, sk: (i, 0, 0)),
            pl.BlockSpec((d, PROJ_W), const2),
            pl.BlockSpec((LANE, GLA_KW), const2),
            pl.BlockSpec((1, GLA_KW), const2),
            pl.BlockSpec((1, GLA_DV), const2),
            pl.BlockSpec((d, d), const2),
            pl.BlockSpec((1, d), const2),
            pl.BlockSpec((1, d), const2),
            pl.BlockSpec((rows, rows), const2),
            pl.BlockSpec((SWA_KV_HEADS, SWA_GROUP * s, nkeys), const3),
            pl.BlockSpec((GLA_HEADS * s, GLA_KW), const2),
            pl.BlockSpec((GLA_HEADS * s, GLA_VW), const2),
            pl.BlockSpec((GLA_VW, GLA_KW), const2),
            pl.BlockSpec((s, GLA_HEADS * s), const2),
        ],
        out_specs=[
            pl.BlockSpec((nb, s, d), lambda i, sk: (i, 0, 0)),
            pl.BlockSpec((nb, GLA_HEADS, GLA_DK, GLA_DV), lambda i, sk: (i, 0, 0, 0)),
            pl.BlockSpec((nb, WINDOW, LANE), lambda i, sk: (i, 0, 0)),
            pl.BlockSpec((nb, WINDOW, LANE), lambda i, sk: (i, 0, 0)),
        ],
        scratch_shapes=[
            pltpu.VMEM((rows, PROJ_W), F32),
            pltpu.VMEM((rows, GLA_KW), BF16),
            pltpu.VMEM((rows, GLA_KW), BF16),
            pltpu.VMEM((rows, GLA_KW), BF16),
            pltpu.VMEM((rows, GLA_KW), BF16),
            pltpu.VMEM((rows, GLA_KW), BF16),
            pltpu.VMEM((rows, GLA_KW), F32),
            pltpu.VMEM((rows, GLA_VW), BF16),
            pltpu.VMEM((nkeys, LANE), BF16),
            pltpu.VMEM((nkeys, LANE), BF16),
            pltpu.VMEM((rows, d), BF16),
            pltpu.VMEM((GLA_VW, GLA_KW), F32),
            pltpu.VMEM((rows, d), F32),
        ],
    )
    return pl.pallas_call(
        functools.partial(_mixer_sample_kernel, nb=nb, s=s),
        out_shape=[
            jax.ShapeDtypeStruct((b, s, d), F32),
            jax.ShapeDtypeStruct((b, GLA_HEADS, GLA_DK, GLA_DV), F32),
            jax.ShapeDtypeStruct((b, WINDOW, LANE), F32),
            jax.ShapeDtypeStruct((b, WINDOW, LANE), F32),
        ],
        grid_spec=grid_spec,
        compiler_params=pltpu.CompilerParams(dimension_semantics=("arbitrary",),
                                             vmem_limit_bytes=VMEM_LIMIT),
        name="mixer_sample",
    )(sinks, x, mod, s0, kc, vc, win, a2w, a2b, gnw, wo, lng, lnb, *consts)


MOE_BLK = 64


def _route_t(logits_t):
    t = logits_t.shape[1]
    row = lax.broadcasted_iota(jnp.int32, (EPG, t), 0).astype(F32)
    big = 99.0
    gl = jnp.where(row < N_GROUPS, logits_t[0:EPG, :], -jnp.inf)
    gmax = jnp.max(gl, axis=0, keepdims=True)
    grp = jnp.min(jnp.where(gl == gmax, row, big), axis=0, keepdims=True)
    p_grp = 1.0 / jnp.sum(jnp.exp(gl - gmax), axis=0, keepdims=True)
    el = jnp.zeros((EPG, t), F32)
    for g in range(N_GROUPS):
        el = el + jnp.where(grp == float(g), logits_t[R_EXP0 + EPG * g:R_EXP0 + EPG * (g + 1), :], 0.0)
    v1 = jnp.max(el, axis=0, keepdims=True)
    i1 = jnp.min(jnp.where(el == v1, row, big), axis=0, keepdims=True)
    el2 = jnp.where(row == i1, -jnp.inf, el)
    v2 = jnp.max(el2, axis=0, keepdims=True)
    i2 = jnp.min(jnp.where(el2 == v2, row, big), axis=0, keepdims=True)
    e2 = jnp.exp(v2 - v1)
    w1 = p_grp / (1.0 + e2)
    w2 = p_grp * e2 / (1.0 + e2)
    cw = jnp.where(row == i1, w1, 0.0) + jnp.where(row == i2, w2, 0.0)
    return grp, cw


MOE_PC = 256


def _moe_sort_kernel(x1_ref, mod_ref, wrt_ref, brt_ref, *refs, nb, r, nreal):
    outs = refs[-4:]

    @pl.when(pl.program_id(0) < nreal)
    def _():
        _moe_sort_tile(x1_ref, mod_ref, wrt_ref, brt_ref, *outs, nb=nb, r=r)

    @pl.when(pl.program_id(0) >= nreal)
    def _():
        for o in outs:
            o[...] = jnp.zeros_like(o)


def _moe_sort_tile(x1_ref, mod_ref, wrt_ref, brt_ref, xs_ref, cws_ref, pos_ref, cnt_ref, *, nb, r):
    tm = nb * r
    tmp = tm + N_GROUPS * MOE_BLK
    d = D_MODEL
    mod = mod_ref[...]
    t3 = x1_ref[...] * (1.0 + mod[:, :, 4 * d:5 * d]) + mod[:, :, 3 * d:4 * d]
    t = t3.reshape(tm, d).astype(BF16)
    grp, cw = _route_t(_mm_nt(wrt_ref[...], t) + brt_ref[...])
    row = lax.broadcasted_iota(jnp.int32, (EPG, tm), 0).astype(F32)
    onehot_g = jnp.where(row == grp, 1.0, 0.0)
    nch = tm // LANE
    strict = (lax.broadcasted_iota(jnp.int32, (LANE, LANE), 0)
              < lax.broadcasted_iota(jnp.int32, (LANE, LANE), 1)).astype(F32).astype(BF16)
    stacked = jnp.concatenate([onehot_g[:, c * LANE:(c + 1) * LANE] for c in range(nch)], axis=0)
    pref = _mm(stacked.astype(BF16), strict)
    tot = jnp.sum(stacked, axis=1, keepdims=True)
    cnt = jnp.zeros((EPG, 1), F32)
    ranks = []
    for c in range(nch):
        ranks.append(pref[c * EPG:(c + 1) * EPG, :] + cnt)
        cnt = cnt + tot[c * EPG:(c + 1) * EPG, :]
    rank = jnp.concatenate(ranks, axis=1)
    padded = jnp.floor((cnt + (MOE_BLK - 1)) * (1.0 / MOE_BLK)) * MOE_BLK
    rowc = lax.broadcasted_iota(jnp.int32, (EPG, 1), 0)
    off = jnp.zeros((EPG, 1), F32)
    for gg in range(N_GROUPS - 1):
        off = off + jnp.where(rowc > gg, padded[gg:gg + 1, :], 0.0)
    pos = jnp.sum(onehot_g * (off + rank), axis=0, keepdims=True)
    pos_ref[0] = jnp.broadcast_to(pos, (EPG, tm))
    cnt_ref[0] = jnp.broadcast_to(cnt, (EPG, LANE))
    cw_hi, cw_lo = _split_bf16(jnp.concatenate([cw, jnp.zeros((LANE - EPG, tm), F32)], axis=0).T)
    t_aug = jnp.concatenate([t, cw_hi, cw_lo], axis=1)
    pc = MOE_PC
    for c in range(tmp // pc):
        slot = (lax.broadcasted_iota(jnp.int32, (pc, tm), 0) + c * pc).astype(F32)
        perm = jnp.where(slot == pos, 1.0, 0.0).astype(BF16)
        moved = _mm(perm, t_aug)
        xs_ref[c * pc:(c + 1) * pc, :] = moved[:, 0:d].astype(BF16)
        cws_ref[c * pc:(c + 1) * pc, :] = moved[:, d:d + LANE] + moved[:, d + LANE:d + 2 * LANE]


MOE_BIG_UNITS = 4
MOE_BIG = MOE_BIG_UNITS * MOE_BLK
MOE_LAST_MAX = 2 * MOE_BIG_UNITS - 1


def _moe_expert_kernel(boff_ref, nblk_ref, nused_ref, xs_ref, cws_ref, wg_ref, wu_ref, wd_ref, ys_hbm,
                       ybuf, ylast, zbuf, sem, st_ref, *, ntiles, tmp):
    q = pl.program_id(0)
    grp = q // ntiles
    tile = q - grp * ntiles
    off = boff_ref[q]
    n = nblk_ref[q]
    nmain = jnp.maximum(lax.shift_right_logical(n, 2) - 1, 0)
    nlast = n - MOE_BIG_UNITS * nmain
    base = tile * tmp
    wd = wd_ref[0].reshape(EPG * EXPERT_FF, D_MODEL)

    def experts(rows):
        xb = xs_ref[rows, :]
        cwb = cws_ref[rows, :]
        hs = []
        for e in range(EPG):
            gg_ = _mm(xb, wg_ref[0, e])
            uu = _mm(xb, wu_ref[0, e])
            hs.append((gg_ * _sigmoid(gg_) * uu * cwb[:, e:e + 1]).astype(BF16))
        return _mm(jnp.concatenate(hs, axis=1), wd).astype(BF16)

    def big_copy(slot, blk):
        return pltpu.make_async_copy(
            ybuf.at[slot], ys_hbm.at[pl.ds(pl.multiple_of(base + blk * MOE_BLK, MOE_BLK), MOE_BIG), :], sem.at[slot])

    def last_copy(units, blk):
        rows = units * MOE_BLK
        return pltpu.make_async_copy(
            ylast.at[0:rows], ys_hbm.at[pl.ds(pl.multiple_of(base + blk * MOE_BLK, MOE_BLK), rows), :], sem.at[2])

    def wait_last():
        for u in range(1, MOE_LAST_MAX + 1):
            @pl.when(st_ref[1] == u)
            def _(u=u):
                last_copy(u, 0).wait()

    def zero_copy(blk):
        return pltpu.make_async_copy(
            zbuf, ys_hbm.at[pl.ds(pl.multiple_of(base + blk * MOE_BLK, MOE_BLK), MOE_BLK), :], sem.at[3])

    @pl.when(q == 0)
    def _():
        for i in range(3):
            st_ref[i] = 0
        zbuf[...] = jnp.zeros_like(zbuf)

    def body(k, carry):
        c = st_ref[0]
        slot = jnp.bitwise_and(c, 1)
        blk = off + MOE_BIG_UNITS * k
        y = experts(pl.ds(pl.multiple_of(blk * MOE_BLK, MOE_BLK), MOE_BIG))

        @pl.when(c >= 2)
        def _():
            big_copy(slot, 0).wait()

        ybuf[slot] = y
        big_copy(slot, blk).start()
        st_ref[0] = c + 1
        return carry

    lax.fori_loop(0, nmain, body, 0)

    @pl.when(nlast > 0)
    def _():
        wait_last()

    lblk = off + MOE_BIG_UNITS * nmain
    for u in range(1, MOE_LAST_MAX + 1):
        @pl.when(nlast == u)
        def _(u=u):
            ylast[0:u * MOE_BLK, :] = experts(pl.ds(pl.multiple_of(lblk * MOE_BLK, MOE_BLK), u * MOE_BLK))
            last_copy(u, lblk).start()
            st_ref[1] = u

    @pl.when(grp == N_GROUPS - 1)
    def _():
        def zfill(blk, carry):
            zero_copy(blk).start()
            return carry

        nz = tmp // MOE_BLK - nused_ref[tile]
        lax.fori_loop(nused_ref[tile], tmp // MOE_BLK, zfill, 0)
        st_ref[2] = st_ref[2] + nz

    @pl.when(q == pl.num_programs(0) - 1)
    def _():
        c = st_ref[0]

        @pl.when(c >= 2)
        def _():
            big_copy(jnp.bitwise_and(c, 1), 0).wait()

        @pl.when(c >= 1)
        def _():
            big_copy(jnp.bitwise_and(c - 1, 1), 0).wait()

        wait_last()

        def zwait(i, carry):
            zero_copy(0).wait()
            return carry

        lax.fori_loop(0, st_ref[2], zwait, 0)


def _moe_unsort_kernel(ys_ref, pos_ref, x1_ref, mod_ref, lng_ref, lnb_ref, out_ref, *, nb, r):
    tm = nb * r
    tmp = tm + N_GROUPS * MOE_BLK
    d = D_MODEL
    ysb = ys_ref[...]
    posc = jnp.broadcast_to(pos_ref[0][0:1, :], (LANE, tm)).T
    lng = lng_ref[...]
    lnb = lnb_ref[...]
    pc = MOE_PC
    for c in range(tm // pc):
        slot = lax.broadcasted_iota(jnp.int32, (pc, tmp), 1).astype(F32)
        unperm = jnp.where(slot == posc[c * pc:(c + 1) * pc, 0:1], 1.0, 0.0).astype(BF16)
        y = _mm(unperm, ysb)
        if nb == 1:
            x1c = x1_ref[0, c * pc:(c + 1) * pc, :]
            gt2 = mod_ref[0][:, 5 * d:6 * d]
            out_ref[0, c * pc:(c + 1) * pc, :] = _layer_norm(DEEPNORM_ALPHA * x1c + gt2 * y, lng, lnb)
        else:
            cb = pc // r
            x1c = x1_ref[c * cb:(c + 1) * cb]
            gt2 = mod_ref[c * cb:(c + 1) * cb][:, :, 5 * d:6 * d]
            yy = DEEPNORM_ALPHA * x1c + gt2 * y.reshape(cb, r, d)
            out_ref[c * cb:(c + 1) * cb] = _layer_norm(yy, lng, lnb)


def _moe(streams, wrt, brt, wg, wu, wd, lng, lnb):
    d = D_MODEL
    tm = streams[0][2] * streams[0][3]
    tmp = tm + N_GROUPS * MOE_BLK
    const2 = lambda i: (0, 0)
    params = pltpu.CompilerParams(dimension_semantics=("arbitrary",), vmem_limit_bytes=VMEM_LIMIT)
    geo = []
    ntiles = 0
    for x1, _, nb, r in streams:
        assert nb * r == tm
        tpb = x1.shape[1] // r
        n = (x1.shape[0] // nb) * tpb
        geo.append((tpb, n, ntiles))
        ntiles += n

    sorted_shapes = [jax.ShapeDtypeStruct((ntiles * tmp, d), BF16),
                     jax.ShapeDtypeStruct((ntiles * tmp, LANE), F32),
                     jax.ShapeDtypeStruct((ntiles, EPG, tm), F32),
                     jax.ShapeDtypeStruct((ntiles, EPG, LANE), F32)]
    bufs = []
    for (x1, mod, nb, r), (tpb, n, t0) in zip(streams, geo):
        steps = ntiles if not bufs else n
        xmap = lambda i, tpb=tpb, n=n: (jnp.minimum(i, n - 1) // tpb, jnp.minimum(i, n - 1) % tpb, 0)
        mmap = lambda i, tpb=tpb, n=n: (jnp.minimum(i, n - 1) // tpb, 0, 0)
        bufs = pl.pallas_call(
            functools.partial(_moe_sort_kernel, nb=nb, r=r, nreal=n),
            out_shape=sorted_shapes,
            grid=(steps,),
            in_specs=[
                pl.BlockSpec((nb, r, d), xmap),
                pl.BlockSpec((nb, 1, 6 * d), mmap),
                pl.BlockSpec((LANE, d), const2),
                pl.BlockSpec((LANE, 1), const2),
            ] + [pl.BlockSpec(memory_space=pl.ANY)] * len(bufs),
            out_specs=[pl.BlockSpec((tmp, d), lambda i, t0=t0: (i + t0, 0)),
                       pl.BlockSpec((tmp, LANE), lambda i, t0=t0: (i + t0, 0)),
                       pl.BlockSpec((1, EPG, tm), lambda i, t0=t0: (i + t0, 0, 0)),
                       pl.BlockSpec((1, EPG, LANE), lambda i, t0=t0: (i + t0, 0, 0))],
            input_output_aliases={4 + k: k for k in range(len(bufs))},
            compiler_params=params,
            name="moe_sort",
        )(x1, mod, wrt, brt, *bufs)
    xs, cws, pos, cnt = bufs

    nblk = ((cnt[:, :N_GROUPS, 0].astype(jnp.int32) + (MOE_BLK - 1)) // MOE_BLK)
    boff = jnp.cumsum(nblk, axis=1) - nblk
    nused = jnp.sum(nblk, axis=1).astype(jnp.int32)
    nblk_q = nblk.T.reshape(-1).astype(jnp.int32)
    boff_q = boff.T.reshape(-1).astype(jnp.int32)

    tmap = lambda q, bo, nk, nu: (q % ntiles, 0)
    wmap = lambda q, bo, nk, nu: (q // ntiles, 0, 0, 0)
    ys = pl.pallas_call(
        functools.partial(_moe_expert_kernel, ntiles=ntiles, tmp=tmp),
        out_shape=jax.ShapeDtypeStruct((ntiles * tmp, d), BF16),
        grid_spec=pltpu.PrefetchScalarGridSpec(
            num_scalar_prefetch=3,
            grid=(N_GROUPS * ntiles,),
            in_specs=[
                pl.BlockSpec((tmp, d), tmap),
                pl.BlockSpec((tmp, LANE), tmap),
                pl.BlockSpec((1, EPG, d, EXPERT_FF), wmap),
                pl.BlockSpec((1, EPG, d, EXPERT_FF), wmap),
                pl.BlockSpec((1, EPG, EXPERT_FF, d), wmap),
            ],
            out_specs=pl.BlockSpec(memory_space=pl.ANY),
            scratch_shapes=[
                pltpu.VMEM((2, MOE_BIG, d), BF16),
                pltpu.VMEM((MOE_LAST_MAX * MOE_BLK, d), BF16),
                pltpu.VMEM((MOE_BLK, d), BF16),
                pltpu.SemaphoreType.DMA((4,)),
                pltpu.SMEM((4,), jnp.int32),
            ],
        ),
        compiler_params=params,
        name="moe_experts",
    )(boff_q, nblk_q, nused, xs, cws, wg, wu, wd)

    outs = []
    for (x1, mod, nb, r), (tpb, n, t0) in zip(streams, geo):
        xmap = lambda i, tpb=tpb: (i // tpb, i % tpb, 0)
        mmap = lambda i, tpb=tpb: (i // tpb, 0, 0)
        outs.append(pl.pallas_call(
            functools.partial(_moe_unsort_kernel, nb=nb, r=r),
            out_shape=jax.ShapeDtypeStruct(x1.shape, F32),
            grid=(n,),
            in_specs=[
                pl.BlockSpec((tmp, d), lambda i, t0=t0: (i + t0, 0)),
                pl.BlockSpec((1, EPG, tm), lambda i, t0=t0: (i + t0, 0, 0)),
                pl.BlockSpec((nb, r, d), xmap),
                pl.BlockSpec((nb, 1, 6 * d), mmap),
                pl.BlockSpec((1, d), const2),
                pl.BlockSpec((1, d), const2),
            ],
            out_specs=pl.BlockSpec((nb, r, d), xmap),
            compiler_params=params,
            name="moe_unsort",
        )(ys, pos, x1, mod, lng, lnb))
    return outs


def kernel(x_prompt, x_sample, c_prompt, c_sample, state_gla, cache_swa_k, cache_swa_v, ada_w, ada_b, w_in,
           gla_a2_w, gla_a2_b, gla_norm_w, swa_sinks, w_o, ln1_g, ln1_b, router_g_w, router_g_b, router_e_w,
           router_e_b, moe_w_gate, moe_w_up, moe_w_down, ln2_g, ln2_b):
    assert ada_w.shape[0] == 1
    bp = x_prompt.shape[0]
    bs, ss, d = x_sample.shape
    lc = cache_swa_k.shape[2]

    w = w_in[0]
    zpad = jnp.zeros((d, LANE - GLA_RANK), F32)
    w_sq = w[:, 1552:2064].reshape(d, SWA_KV_HEADS, SWA_GROUP, SWA_DH).transpose(0, 2, 1, 3).reshape(d, SWA_W)
    win = jnp.concatenate([w[:, 0:1536], w_sq, w[:, 2064:2320], w[:, 1536:1552], zpad], axis=1).astype(BF16)
    a2w = jnp.concatenate([gla_a2_w[0], jnp.zeros((LANE - GLA_RANK, GLA_KW), F32)], axis=0).astype(BF16)
    a2b = gla_a2_b[0].reshape(1, GLA_KW)
    gnw = gla_norm_w[0].reshape(1, GLA_DV)
    wo_swa = w_o[0][GLA_VW:].reshape(SWA_KV_HEADS, SWA_GROUP, SWA_DH, d).transpose(1, 0, 2, 3).reshape(SWA_W, d)
    wo = jnp.concatenate([w_o[0][:GLA_VW], wo_swa], axis=0).astype(BF16)
    sinks = swa_sinks[0]
    wrt = jnp.concatenate([router_g_w[0], jnp.zeros((d, R_EXP0 - N_GROUPS), F32),
                           jnp.transpose(router_e_w[0], (1, 0, 2)).reshape(d, N_GROUPS * EPG),
                           jnp.zeros((d, LANE - R_EXP0 - N_GROUPS * EPG), F32)], axis=1).T.astype(BF16)
    brt = jnp.concatenate([router_g_b[0], jnp.zeros((R_EXP0 - N_GROUPS,), F32), router_e_b[0].reshape(-1),
                           jnp.zeros((LANE - R_EXP0 - N_GROUPS * EPG,), F32)]).reshape(LANE, 1)
    wg = moe_w_gate[0].astype(BF16)
    wu = moe_w_up[0].astype(BF16)
    wd = moe_w_down[0].astype(BF16)
    lng1, lnb1 = ln1_g[0].reshape(1, d), ln1_b[0].reshape(1, d)
    lng2, lnb2 = ln2_g[0].reshape(1, d), ln2_b[0].reshape(1, d)

    mod = _adaln(jnp.concatenate([c_prompt, c_sample], axis=0), ada_w[0], ada_b[0].reshape(1, 6 * d))
    mod = mod.reshape(bp + bs, 1, 6 * d)
    mod_p, mod_s = mod[:bp], mod[bp:]

    x1p, s_p, k_p, v_p = _mixer_prompt(x_prompt, mod_p, sinks, win, a2w, a2b, gnw, wo, lng1, lnb1)
    x1s, s_s, k_s, v_s = _mixer_sample(
        x_sample, mod_s, state_gla[0], cache_swa_k[0].reshape(bs, lc, LANE), cache_swa_v[0].reshape(bs, lc, LANE),
        sinks, win, a2w, a2b, gnw, wo, lng1, lnb1)

    assert MOE_TILE % ss == 0 and bs % (MOE_TILE // ss) == 0 and x_prompt.shape[1] % MOE_TILE == 0
    yp, ys = _moe([(x1p, mod_p, 1, MOE_TILE), (x1s, mod_s, MOE_TILE // ss, ss)], wrt, brt, wg, wu, wd, lng2, lnb2)

    kv_shape_p = (1, bp, WINDOW, SWA_KV_HEADS, SWA_DH)
    kv_shape_s = (1, bs, lc, SWA_KV_HEADS, SWA_DH)
    return (yp, ys, s_p[None], k_p.reshape(kv_shape_p), v_p.reshape(kv_shape_p),
            s_s[None], k_s.reshape(kv_shape_s), v_s.reshape(kv_shape_s))
```

```python
import functools

import jax
import jax.numpy as jnp
import numpy as np
from jax import lax
from jax.experimental import pallas as pl
from jax.experimental.pallas import tpu as pltpu

F32 = jnp.float32
BF16 = jnp.bfloat16

D_MODEL = 1024
CHUNK = 64
GLA_HEADS = 4
GLA_DK = 64
GLA_DV = 128
GLA_KW = GLA_HEADS * GLA_DK
GLA_VW = GLA_HEADS * GLA_DV
GLA_RANK = 16
GLA_TAU = 16.0
SWA_Q_HEADS = 8
SWA_KV_HEADS = 2
SWA_GROUP = 4
SWA_DH = 64
SWA_W = SWA_Q_HEADS * SWA_DH
WINDOW = 128
N_GROUPS = 4
EPG = 8
EXPERT_FF = 256
DEEPNORM_ALPHA = 2.0 ** 0.25
LN_EPS = 1e-5
NEG_INF = -1e30

C_GQ, C_GK, C_GV, C_GR, C_SQ, C_SK, C_SV, C_GA = 0, 256, 512, 1024, 1536, 2048, 2176, 2304
PROJ_W = 2432
LANE = 128
R_EXP0 = 8

V7X_VMEM_BYTES = 64 * 1024 * 1024
VMEM_LIMIT = V7X_VMEM_BYTES * 7 // 8
MIXER_TILE = 512
SAMPLE_SEQS_PER_STEP = 8
MOE_TILE = 1024


def _mm(a, b):
    return jnp.dot(a, b, preferred_element_type=F32)


def _mm_nt(a, b):
    return lax.dot_general(a, b, (((1,), (1,)), ((), ())), preferred_element_type=F32)


def _mm_tn(a, b):
    return lax.dot_general(a, b, (((0,), (0,)), ((), ())), preferred_element_type=F32)


def _split_bf16(a):
    hi = a.astype(BF16)
    lo = (a - hi.astype(F32)).astype(BF16)
    return hi, lo


def _sigmoid(x):
    return 1.0 / (1.0 + jnp.exp(-x))


def _layer_norm(y, g, b):
    mu = jnp.mean(y, axis=-1, keepdims=True)
    d = y - mu
    var = jnp.mean(d * d, axis=-1, keepdims=True)
    return d * lax.rsqrt(var + LN_EPS) * g + b


def _adaln_kernel(c_ref, w_ref, b_ref, o_ref):
    c = c_ref[...]
    a = c * _sigmoid(c)
    a_hi, a_lo = _split_bf16(a)
    w_hi, w_lo = _split_bf16(w_ref[...])
    o_ref[...] = _mm(a_hi, w_hi) + (_mm(a_hi, w_lo) + _mm(a_lo, w_hi)) + b_ref[...]


def _adaln(c_all, ada_w, ada_b):
    n = c_all.shape[0]
    bn = 1024
    return pl.pallas_call(
        _adaln_kernel,
        out_shape=jax.ShapeDtypeStruct((n, 6 * D_MODEL), F32),
        grid=(6 * D_MODEL // bn,),
        in_specs=[pl.BlockSpec((n, D_MODEL), lambda j: (0, 0)),
                  pl.BlockSpec((D_MODEL, bn), lambda j: (0, j)),
                  pl.BlockSpec((1, bn), lambda j: (0, j))],
        out_specs=pl.BlockSpec((n, bn), lambda j: (0, j)),
        compiler_params=pltpu.CompilerParams(dimension_semantics=("arbitrary",), vmem_limit_bytes=VMEM_LIMIT),
        name="adaln",
    )(c_all, ada_w, ada_b)


def _mixer_consts(nrows, chunk, qb, kw):
    r = np.arange(nrows)
    tri = ((r[:, None] // chunk == r[None, :] // chunk) & (r[:, None] >= r[None, :])).astype(np.float32)
    hs = np.arange(GLA_HEADS * chunk)
    mkk = (hs[:, None] // chunk == np.arange(GLA_KW)[None, :] // GLA_DK).astype(np.float32)
    mv = (hs[:, None] // chunk == np.arange(GLA_VW)[None, :] // GLA_DV).astype(np.float32)
    ms = (np.arange(GLA_VW)[:, None] // GLA_DV == np.arange(GLA_KW)[None, :] // GLA_DK).astype(np.float32)
    caus = (np.arange(chunk)[:, None] >= (hs[None, :] % chunk)).astype(np.float32)
    t = np.arange(qb)
    kj = np.arange(kw)
    cs = (t // chunk) * chunk
    kpos = kj[None, :] - WINDOW
    vis = (kpos >= cs[:, None] - WINDOW) & (kpos < cs[:, None] + chunk)
    dist = np.abs(t[:, None] + WINDOW - kj[None, :]).astype(np.float32)
    bias = np.zeros((SWA_KV_HEADS, SWA_GROUP * qb, kw), np.float32)
    for hk in range(SWA_KV_HEADS):
        for g in range(SWA_GROUP):
            slope = np.float32(2.0 ** (-(hk * SWA_GROUP + g + 1)))
            bias[hk, g * qb:(g + 1) * qb] = np.where(vis, -slope * dist, np.float32(2.0 * NEG_INF))
    return (jnp.asarray(tri, BF16), jnp.asarray(bias), jnp.asarray(mkk, BF16), jnp.asarray(mv, BF16),
            jnp.asarray(ms), jnp.asarray(caus))


def _modulate(x, mod):
    return (x * (1.0 + mod[:, D_MODEL:2 * D_MODEL]) + mod[:, 0:D_MODEL]).astype(BF16)


def _log_gates(proj_ref, a2w_ref, a2b_ref):
    ga = proj_ref[:, C_GA:C_GA + LANE].astype(BF16)
    z = _mm(ga, a2w_ref[...]) + a2b_ref[...]
    lsig = -(jnp.maximum(-z, 0.0) + jnp.log(1.0 + jnp.exp(-jnp.abs(z))))
    return lsig * (1.0 / GLA_TAU)


def _gla_prep(loga, tri_ref, proj_ref, chunk, qe_ref, qn_ref, ke_ref, kn_ref, kw_ref, eb_ref, vb_ref):
    nrows = loga.shape[0]
    la_hi, la_lo = _split_bf16(loga)
    tri = tri_ref[...]
    b = _mm(tri, la_hi) + _mm(tri, la_lo)
    b_end = jnp.concatenate(
        [jnp.broadcast_to(b[c * chunk + chunk - 1:(c + 1) * chunk, :], (chunk, GLA_KW))
         for c in range(nrows // chunk)], axis=0)
    eb = jnp.exp(b)
    ebn = jnp.exp(-b)
    wk = jnp.exp(b_end - b)
    q = proj_ref[:, C_GQ:C_GQ + GLA_KW] * (GLA_DK ** -0.5)
    k = proj_ref[:, C_GK:C_GK + GLA_KW]
    eb_ref[...] = eb
    qe_ref[...] = (q * eb).astype(BF16)
    qn_ref[...] = (q * ebn).astype(BF16)
    ke_ref[...] = (k * eb).astype(BF16)
    kn_ref[...] = (k * ebn).astype(BF16)
    kw_ref[...] = (k * wk).astype(BF16)
    vb_ref[...] = proj_ref[:, C_GV:C_GV + GLA_VW].astype(BF16)


def _gla_chunk(r0, chunk, qe_ref, qn_ref, ke_ref, kn_ref, kw_ref, eb_ref, vb_ref, sbt_ref, mkk, mv, ms, caus):
    rows = slice(r0, r0 + chunk)
    qe = qe_ref[rows, :]
    qn = qn_ref[rows, :]
    zero = jnp.zeros((), BF16)
    kn4 = jnp.where(mkk != 0, jnp.concatenate([kn_ref[rows, :]] * GLA_HEADS, axis=0), zero)
    ke4 = jnp.where(mkk != 0, jnp.concatenate([ke_ref[rows, :]] * GLA_HEADS, axis=0), zero)
    a_lo = _mm_nt(qe, kn4)
    a_up = _mm_nt(qn, ke4)
    a = jnp.where(caus != 0.0, a_lo, a_up).astype(BF16)
    v = vb_ref[rows, :]
    v4 = jnp.where(mv != 0, jnp.concatenate([v] * GLA_HEADS, axis=0), zero)
    sbt = sbt_ref[...]
    o = _mm(a, v4) + _mm_nt(qe, sbt.astype(BF16))
    ut = _mm_tn(v, kw_ref[rows, :])
    g_end = eb_ref[r0 + chunk - 1:r0 + chunk, :]
    sbt_ref[...] = g_end * sbt + jnp.where(ms != 0.0, ut, 0.0)
    return o


def _gla_post(o, gr, gnw):
    res = []
    for h in range(GLA_HEADS):
        oh = o[:, h * GLA_DV:(h + 1) * GLA_DV]
        ms_ = jnp.mean(oh * oh, axis=-1, keepdims=True)
        og = oh * lax.rsqrt(ms_ + LN_EPS) * gnw
        r = gr[:, h * GLA_DV:(h + 1) * GLA_DV]
        res.append((og * (r * _sigmoid(r))).astype(BF16))
    return res


def _swa_scores(sq, kwin, qb):
    low = lax.broadcasted_iota(jnp.int32, (qb, LANE), 1) < SWA_DH
    res = []
    for hk in range(SWA_KV_HEADS):
        keep = low if hk == 0 else jnp.logical_not(low)
        qs = jnp.concatenate([jnp.where(keep, sq[:, g * LANE:(g + 1) * LANE] * (SWA_DH ** -0.5), 0.0)
                              for g in range(SWA_GROUP)], axis=0).astype(BF16)
        res.append(_mm_nt(qs, kwin))
    return res


def _swa_finish(s, vwin, bias, sinks_ref, hk, extra_valid, qb):
    ok = bias > -1e29
    if extra_valid is not None:
        ok = ok & extra_valid
    s = jnp.where(ok, s + bias, NEG_INF)
    ps, denoms = [], []
    for g in range(SWA_GROUP):
        sg = s[g * qb:(g + 1) * qb, :]
        sink = sinks_ref[hk * SWA_GROUP + g]
        m = jnp.maximum(jnp.max(sg, axis=-1, keepdims=True), sink)
        pg = jnp.exp(sg - m)
        denoms.append(jnp.sum(pg, axis=-1, keepdims=True) + jnp.exp(sink - m))
        ps.append(pg.astype(BF16))
    o = _mm(jnp.concatenate(ps, axis=0), vwin)
    return o / jnp.concatenate(denoms, axis=0)


def _swa_merge(o0, o1, qb):
    low = lax.broadcasted_iota(jnp.int32, (qb, LANE), 1) < SWA_DH
    return [jnp.where(low, o0[g * qb:(g + 1) * qb, :], o1[g * qb:(g + 1) * qb, :]).astype(BF16)
            for g in range(SWA_GROUP)]


def _out_proj_ln(x, gt1, mixed_ref, wo_ref, ln_g, ln_b):
    mix = _mm(mixed_ref[...], wo_ref[...])
    return _layer_norm(DEEPNORM_ALPHA * x + gt1 * mix, ln_g, ln_b)


P_QB = 128
P_KW = WINDOW + P_QB


def _mixer_prompt_kernel(sinks_ref, x_ref, mod_ref, win_ref, a2w_ref, a2b_ref, gnw_ref, wo_ref, lng_ref, lnb_ref,
                         tri_ref, bias_ref, mkk_ref, mv_ref, ms_ref, caus_ref, cast_in0, cast_in1, cast_in2,
                         x1_ref, s_out_ref, k_out_ref, v_out_ref, cast_out0, cast_out1, cast_out2,
                         proj_ref, qe_ref, qn_ref, ke_ref, kn_ref, kw_ref, eb_ref, vb_ref, kbuf, vbuf, mixed_ref,
                         sbt_ref, *, tl):
    j = pl.program_id(1)
    nj = pl.num_programs(1)
    d = D_MODEL

    for src, dst in ((cast_in0, cast_out0), (cast_in1, cast_out1), (cast_in2, cast_out2)):
        dst[...] = src[...].astype(BF16)

    @pl.when(j == 0)
    def _():
        sbt_ref[...] = jnp.zeros_like(sbt_ref)
        kbuf[0:WINDOW, :] = jnp.zeros((WINDOW, LANE), BF16)
        vbuf[0:WINDOW, :] = jnp.zeros((WINDOW, LANE), BF16)

    x = x_ref[0]
    mod = mod_ref[0]
    proj_ref[...] = _mm(_modulate(x, mod), win_ref[...])
    loga = _log_gates(proj_ref, a2w_ref, a2b_ref)
    _gla_prep(loga, tri_ref, proj_ref, CHUNK, qe_ref, qn_ref, ke_ref, kn_ref, kw_ref, eb_ref, vb_ref)
    kbuf[WINDOW:WINDOW + tl, :] = proj_ref[:, C_SK:C_SK + LANE].astype(BF16)
    vbuf[WINDOW:WINDOW + tl, :] = proj_ref[:, C_SV:C_SV + LANE].astype(BF16)

    mkk = mkk_ref[...]
    mv = mv_ref[...]
    ms = ms_ref[...]
    caus = caus_ref[...]
    gnw = gnw_ref[...]
    kj = lax.broadcasted_iota(jnp.int32, (SWA_GROUP * P_QB, P_KW), 1)
    first_valid = kj >= jnp.where(j > 0, 0, WINDOW)

    def scores(p):
        q0 = p * P_QB
        return _swa_scores(proj_ref[q0:q0 + P_QB, C_SQ:C_SQ + SWA_W], kbuf[q0:q0 + P_KW, :], P_QB)

    def gla(r0):
        o = _gla_chunk(r0, CHUNK, qe_ref, qn_ref, ke_ref, kn_ref, kw_ref, eb_ref, vb_ref, sbt_ref, mkk, mv, ms, caus)
        og = _gla_post(o, proj_ref[r0:r0 + CHUNK, C_GR:C_GR + GLA_VW], gnw)
        for hh in range(GLA_HEADS):
            mixed_ref[r0:r0 + CHUNK, hh * GLA_DV:(hh + 1) * GLA_DV] = og[hh]

    for p in range(tl // P_QB):
        q0 = p * P_QB
        s = scores(p)
        vwin = vbuf[q0:q0 + P_KW, :]
        extra = first_valid if p == 0 else None
        o0 = _swa_finish(s[0], vwin, bias_ref[0], sinks_ref, 0, extra, P_QB)
        o1 = _swa_finish(s[1], vwin, bias_ref[1], sinks_ref, 1, extra, P_QB)
        for i, blk in enumerate(_swa_merge(o0, o1, P_QB)):
            mixed_ref[q0:q0 + P_QB, GLA_VW + i * LANE:GLA_VW + (i + 1) * LANE] = blk
        gla(q0)
        gla(q0 + CHUNK)

    x1_ref[0] = _out_proj_ln(x, mod[:, 2 * d:3 * d], mixed_ref, wo_ref, lng_ref[...], lnb_ref[...])

    kbuf[0:WINDOW, :] = kbuf[tl:tl + WINDOW, :]
    vbuf[0:WINDOW, :] = vbuf[tl:tl + WINDOW, :]

    @pl.when(j == nj - 1)
    def _():
        for hh in range(GLA_HEADS):
            s_out_ref[0, hh] = sbt_ref[hh * GLA_DV:(hh + 1) * GLA_DV, hh * GLA_DK:(hh + 1) * GLA_DK].T
        k_out_ref[0] = proj_ref[tl - WINDOW:tl, C_SK:C_SK + LANE]
        v_out_ref[0] = proj_ref[tl - WINDOW:tl, C_SV:C_SV + LANE]


def _mixer_prompt(x, mod, sinks, win, a2w, a2b, gnw, wo, lng, lnb, to_cast, *, tl=MIXER_TILE):
    b, l, d = x.shape
    nj = l // tl
    consts = _mixer_consts(tl, CHUNK, P_QB, P_KW)
    const2 = lambda i, j, s: (0, 0)
    const3 = lambda i, j, s: (0, 0, 0)
    cast_specs = []
    for a in to_cast:
        assert a.shape[0] % (b * nj * 16) == 0
        cast_specs.append(pl.BlockSpec((a.shape[0] // (b * nj), a.shape[1]), lambda i, j, s: (i * nj + j, 0)))
    grid_spec = pltpu.PrefetchScalarGridSpec(
        num_scalar_prefetch=1,
        grid=(b, nj),
        in_specs=[
            pl.BlockSpec((1, tl, d), lambda i, j, s: (i, j, 0)),
            pl.BlockSpec((1, 1, 6 * d), lambda i, j, s: (i, 0, 0)),
            pl.BlockSpec((d, PROJ_W), const2),
            pl.BlockSpec((LANE, GLA_KW), const2),
            pl.BlockSpec((1, GLA_KW), const2),
            pl.BlockSpec((1, GLA_DV), const2),
            pl.BlockSpec((d, d), const2),
            pl.BlockSpec((1, d), const2),
            pl.BlockSpec((1, d), const2),
            pl.BlockSpec((tl, tl), const2),
            pl.BlockSpec((SWA_KV_HEADS, SWA_GROUP * P_QB, P_KW), const3),
            pl.BlockSpec((GLA_HEADS * CHUNK, GLA_KW), const2),
            pl.BlockSpec((GLA_HEADS * CHUNK, GLA_VW), const2),
            pl.BlockSpec((GLA_VW, GLA_KW), const2),
            pl.BlockSpec((CHUNK, GLA_HEADS * CHUNK), const2),
        ] + cast_specs,
        out_specs=[
            pl.BlockSpec((1, tl, d), lambda i, j, s: (i, j, 0)),
            pl.BlockSpec((1, GLA_HEADS, GLA_DK, GLA_DV), lambda i, j, s: (i, 0, 0, 0)),
            pl.BlockSpec((1, WINDOW, LANE), lambda i, j, s: (i, 0, 0)),
            pl.BlockSpec((1, WINDOW, LANE), lambda i, j, s: (i, 0, 0)),
        ] + cast_specs,
        scratch_shapes=[
            pltpu.VMEM((tl, PROJ_W), F32),
            pltpu.VMEM((tl, GLA_KW), BF16),
            pltpu.VMEM((tl, GLA_KW), BF16),
            pltpu.VMEM((tl, GLA_KW), BF16),
            pltpu.VMEM((tl, GLA_KW), BF16),
            pltpu.VMEM((tl, GLA_KW), BF16),
            pltpu.VMEM((tl, GLA_KW), F32),
            pltpu.VMEM((tl, GLA_VW), BF16),
            pltpu.VMEM((WINDOW + tl, LANE), BF16),
            pltpu.VMEM((WINDOW + tl, LANE), BF16),
            pltpu.VMEM((tl, d), BF16),
            pltpu.VMEM((GLA_VW, GLA_KW), F32),
        ],
    )
    return pl.pallas_call(
        functools.partial(_mixer_prompt_kernel, tl=tl),
        out_shape=[
            jax.ShapeDtypeStruct((b, l, d), F32),
            jax.ShapeDtypeStruct((b, GLA_HEADS, GLA_DK, GLA_DV), F32),
            jax.ShapeDtypeStruct((b, WINDOW, LANE), F32),
            jax.ShapeDtypeStruct((b, WINDOW, LANE), F32),
        ] + [jax.ShapeDtypeStruct(a.shape, BF16) for a in to_cast],
        grid_spec=grid_spec,
        compiler_params=pltpu.CompilerParams(dimension_semantics=("arbitrary", "arbitrary"),
                                             vmem_limit_bytes=VMEM_LIMIT),
        name="mixer_prompt",
    )(sinks, x, mod, win, a2w, a2b, gnw, wo, lng, lnb, *consts, *to_cast)


def _mixer_sample_kernel(sinks_ref, x_ref, mod_ref, s0_ref, kc_ref, vc_ref, win_ref, a2w_ref, a2b_ref, gnw_ref,
                         wo_ref, lng_ref, lnb_ref, tri_ref, bias_ref, mkk_ref, mv_ref, ms_ref, caus_ref,
                         x1_ref, s_out_ref, k_out_ref, v_out_ref,
                         proj_ref, qe_ref, qn_ref, ke_ref, kn_ref, kw_ref, eb_ref, vb_ref, kbuf, vbuf, mixed_ref,
                         sbt_ref, xm_ref, *, nb, s):
    nkeys = WINDOW + s
    d = D_MODEL
    for bb in range(nb):
        m = mod_ref[bb]
        xm_ref[bb * s:(bb + 1) * s, :] = x_ref[bb] * (1.0 + m[:, d:2 * d]) + m[:, 0:d]
    proj_ref[...] = _mm(xm_ref[...].astype(BF16), win_ref[...])
    loga = _log_gates(proj_ref, a2w_ref, a2b_ref)
    _gla_prep(loga, tri_ref, proj_ref, s, qe_ref, qn_ref, ke_ref, kn_ref, kw_ref, eb_ref, vb_ref)

    mkk = mkk_ref[...]
    mv = mv_ref[...]
    ms = ms_ref[...]
    caus = caus_ref[...]
    gnw = gnw_ref[...]

    for bb in range(nb):
        r0 = bb * s
        rows = slice(r0, r0 + s)
        sbt_ref[...] = jnp.zeros_like(sbt_ref)
        for hh in range(GLA_HEADS):
            sbt_ref[hh * GLA_DV:(hh + 1) * GLA_DV, hh * GLA_DK:(hh + 1) * GLA_DK] = s0_ref[bb, hh].T
        kbuf[0:WINDOW, :] = kc_ref[bb].astype(BF16)
        vbuf[0:WINDOW, :] = vc_ref[bb].astype(BF16)
        kbuf[WINDOW:nkeys, :] = proj_ref[rows, C_SK:C_SK + LANE].astype(BF16)
        vbuf[WINDOW:nkeys, :] = proj_ref[rows, C_SV:C_SV + LANE].astype(BF16)
        sc = _swa_scores(proj_ref[rows, C_SQ:C_SQ + SWA_W], kbuf[...], s)
        vwin = vbuf[...]
        blocks = _swa_merge(_swa_finish(sc[0], vwin, bias_ref[0], sinks_ref, 0, None, s),
                            _swa_finish(sc[1], vwin, bias_ref[1], sinks_ref, 1, None, s), s)
        for i, blk in enumerate(blocks):
            mixed_ref[rows, GLA_VW + i * LANE:GLA_VW + (i + 1) * LANE] = blk
        o = _gla_chunk(r0, s, qe_ref, qn_ref, ke_ref, kn_ref, kw_ref, eb_ref, vb_ref, sbt_ref, mkk, mv, ms, caus)
        og = _gla_post(o, proj_ref[rows, C_GR:C_GR + GLA_VW], gnw)
        for hh in range(GLA_HEADS):
            mixed_ref[rows, hh * GLA_DV:(hh + 1) * GLA_DV] = og[hh]
            s_out_ref[bb, hh] = sbt_ref[hh * GLA_DV:(hh + 1) * GLA_DV, hh * GLA_DK:(hh + 1) * GLA_DK].T
        k_out_ref[bb, 0:WINDOW - s, :] = kc_ref[bb, s:WINDOW, :]
        v_out_ref[bb, 0:WINDOW - s, :] = vc_ref[bb, s:WINDOW, :]
        k_out_ref[bb, WINDOW - s:WINDOW, :] = proj_ref[rows, C_SK:C_SK + LANE]
        v_out_ref[bb, WINDOW - s:WINDOW, :] = proj_ref[rows, C_SV:C_SV + LANE]

    mix = _mm(mixed_ref[...], wo_ref[...])
    lng = lng_ref[...]
    lnb = lnb_ref[...]
    for bb in range(nb):
        m = mod_ref[bb]
        y = DEEPNORM_ALPHA * x_ref[bb] + m[:, 2 * d:3 * d] * mix[bb * s:(bb + 1) * s, :]
        x1_ref[bb] = _layer_norm(y, lng, lnb)


def _mixer_sample(x, mod, s0, kc, vc, sinks, win, a2w, a2b, gnw, wo, lng, lnb, *, nb=SAMPLE_SEQS_PER_STEP):
    b, s, d = x.shape
    assert kc.shape[1] == WINDOW and s <= WINDOW
    rows = nb * s
    nkeys = WINDOW + s
    consts = _mixer_consts(rows, s, s, nkeys)
    const2 = lambda i, sk: (0, 0)
    const3 = lambda i, sk: (0, 0, 0)
    grid_spec = pltpu.PrefetchScalarGridSpec(
        num_scalar_prefetch=1,
        grid=(b // nb,),
        in_specs=[
            pl.BlockSpec((nb, s, d), lambda i, sk: (i, 0, 0)),
            pl.BlockSpec((nb, 1, 6 * d), lambda i, sk: (i, 0, 0)),
            pl.BlockSpec((nb, GLA_HEADS, GLA_DK, GLA_DV), lambda i, sk: (i, 0, 0, 0)),
            pl.BlockSpec((nb, WINDOW, LANE), lambda i, sk: (i, 0, 0)),
            pl.BlockSpec((nb, WINDOW, LANE), lambda i, sk: (i, 0, 0)),
            pl.BlockSpec((d, PROJ_W), const2),
            pl.BlockSpec((LANE, GLA_KW), const2),
            pl.BlockSpec((1, GLA_KW), const2),
            pl.BlockSpec((1, GLA_DV), const2),
            pl.BlockSpec((d, d), const2),
            pl.BlockSpec((1, d), const2),
            pl.BlockSpec((1, d), const2),
            pl.BlockSpec((rows, rows), const2),
            pl.BlockSpec((SWA_KV_HEADS, SWA_GROUP * s, nkeys), const3),
            pl.BlockSpec((GLA_HEADS * s, GLA_KW), const2),
            pl.BlockSpec((GLA_HEADS * s, GLA_VW), const2),
            pl.BlockSpec((GLA_VW, GLA_KW), const2),
            pl.BlockSpec((s, GLA_HEADS * s), const2),
        ],
        out_specs=[
            pl.BlockSpec((nb, s, d), lambda i, sk: (i, 0, 0)),
            pl.BlockSpec((nb, GLA_HEADS, GLA_DK, GLA_DV), lambda i, sk: (i, 0, 0, 0)),
            pl.BlockSpec((nb, WINDOW, LANE), lambda i, sk: (i, 0, 0)),
            pl.BlockSpec((nb, WINDOW, LANE), lambda i, sk: (i, 0, 0)),
        ],
        scratch_shapes=[
            pltpu.VMEM((rows, PROJ_W), F32),
            pltpu.VMEM((rows, GLA_KW), BF16),
            pltpu.VMEM((rows, GLA_KW), BF16),
            pltpu.VMEM((rows, GLA_KW), BF16),
            pltpu.VMEM((rows, GLA_KW), BF16),
            pltpu.VMEM((rows, GLA_KW), BF16),
            pltpu.VMEM((rows, GLA_KW), F32),
            pltpu.VMEM((rows, GLA_VW), BF16),
            pltpu.VMEM((nkeys, LANE), BF16),
            pltpu.VMEM((nkeys, LANE), BF16),
            pltpu.VMEM((rows, d), BF16),
            pltpu.VMEM((GLA_VW, GLA_KW), F32),
            pltpu.VMEM((rows, d), F32),
        ],
    )
    return pl.pallas_call(
        functools.partial(_mixer_sample_kernel, nb=nb, s=s),
        out_shape=[
            jax.ShapeDtypeStruct((b, s, d), F32),
            jax.ShapeDtypeStruct((b, GLA_HEADS, GLA_DK, GLA_DV), F32),
            jax.ShapeDtypeStruct((b, WINDOW, LANE), F32),
            jax.ShapeDtypeStruct((b, WINDOW, LANE), F32),
        ],
        grid_spec=grid_spec,
        compiler_params=pltpu.CompilerParams(dimension_semantics=("arbitrary",),
                                             vmem_limit_bytes=VMEM_LIMIT),
        name="mixer_sample",
    )(sinks, x, mod, s0, kc, vc, win, a2w, a2b, gnw, wo, lng, lnb, *consts)


MOE_BLK = 64


def _route_t(logits_t):
    t = logits_t.shape[1]
    row = lax.broadcasted_iota(jnp.int32, (EPG, t), 0).astype(F32)
    big = 99.0
    gl = jnp.where(row < N_GROUPS, logits_t[0:EPG, :], -jnp.inf)
    gmax = jnp.max(gl, axis=0, keepdims=True)
    grp = jnp.min(jnp.where(gl == gmax, row, big), axis=0, keepdims=True)
    p_grp = 1.0 / jnp.sum(jnp.exp(gl - gmax), axis=0, keepdims=True)
    el = jnp.zeros((EPG, t), F32)
    for g in range(N_GROUPS):
        el = el + jnp.where(grp == float(g), logits_t[R_EXP0 + EPG * g:R_EXP0 + EPG * (g + 1), :], 0.0)
    v1 = jnp.max(el, axis=0, keepdims=True)
    i1 = jnp.min(jnp.where(el == v1, row, big), axis=0, keepdims=True)
    el2 = jnp.where(row == i1, -jnp.inf, el)
    v2 = jnp.max(el2, axis=0, keepdims=True)
    i2 = jnp.min(jnp.where(el2 == v2, row, big), axis=0, keepdims=True)
    e2 = jnp.exp(v2 - v1)
    w1 = p_grp / (1.0 + e2)
    w2 = p_grp * e2 / (1.0 + e2)
    cw = jnp.where(row == i1, w1, 0.0) + jnp.where(row == i2, w2, 0.0)
    return grp, cw


MOE_PC = 256


def _moe_sort_kernel(x1_ref, mod_ref, wrt_ref, brt_ref, *refs, nb, r, nreal):
    outs = refs[-4:]

    @pl.when(pl.program_id(0) < nreal)
    def _():
        _moe_sort_tile(x1_ref, mod_ref, wrt_ref, brt_ref, *outs, nb=nb, r=r)

    @pl.when(pl.program_id(0) >= nreal)
    def _():
        for o in outs:
            o[...] = jnp.zeros_like(o)


def _moe_sort_tile(x1_ref, mod_ref, wrt_ref, brt_ref, xs_ref, cws_ref, pos_ref, cnt_ref, *, nb, r):
    tm = nb * r
    tmp = tm + N_GROUPS * MOE_BLK
    d = D_MODEL
    mod = mod_ref[...]
    t3 = x1_ref[...] * (1.0 + mod[:, :, 4 * d:5 * d]) + mod[:, :, 3 * d:4 * d]
    t = t3.reshape(tm, d).astype(BF16)
    grp, cw = _route_t(_mm_nt(wrt_ref[...], t) + brt_ref[...])
    row = lax.broadcasted_iota(jnp.int32, (EPG, tm), 0).astype(F32)
    onehot_g = jnp.where(row == grp, 1.0, 0.0)
    nch = tm // LANE
    strict = (lax.broadcasted_iota(jnp.int32, (LANE, LANE), 0)
              < lax.broadcasted_iota(jnp.int32, (LANE, LANE), 1)).astype(F32).astype(BF16)
    stacked = jnp.concatenate([onehot_g[:, c * LANE:(c + 1) * LANE] for c in range(nch)], axis=0)
    pref = _mm(stacked.astype(BF16), strict)
    tot = jnp.sum(stacked, axis=1, keepdims=True)
    cnt = jnp.zeros((EPG, 1), F32)
    ranks = []
    for c in range(nch):
        ranks.append(pref[c * EPG:(c + 1) * EPG, :] + cnt)
        cnt = cnt + tot[c * EPG:(c + 1) * EPG, :]
    rank = jnp.concatenate(ranks, axis=1)
    padded = jnp.floor((cnt + (MOE_BLK - 1)) * (1.0 / MOE_BLK)) * MOE_BLK
    rowc = lax.broadcasted_iota(jnp.int32, (EPG, 1), 0)
    off = jnp.zeros((EPG, 1), F32)
    for gg in range(N_GROUPS - 1):
        off = off + jnp.where(rowc > gg, padded[gg:gg + 1, :], 0.0)
    pos = jnp.sum(onehot_g * (off + rank), axis=0, keepdims=True)
    pos_ref[0] = jnp.broadcast_to(pos, (EPG, tm))
    cnt_ref[0] = jnp.broadcast_to(cnt, (EPG, LANE))
    cw_hi, cw_lo = _split_bf16(jnp.concatenate([cw, jnp.zeros((LANE - EPG, tm), F32)], axis=0).T)
    t_aug = jnp.concatenate([t, cw_hi, cw_lo], axis=1)
    pc = MOE_PC
    for c in range(tmp // pc):
        slot = (lax.broadcasted_iota(jnp.int32, (pc, tm), 0) + c * pc).astype(F32)
        perm = jnp.where(slot == pos, 1.0, 0.0).astype(BF16)
        moved = _mm(perm, t_aug)
        xs_ref[c * pc:(c + 1) * pc, :] = moved[:, 0:d].astype(BF16)
        cws_ref[c * pc:(c + 1) * pc, :] = moved[:, d:d + LANE] + moved[:, d + LANE:d + 2 * LANE]


MOE_BIG_UNITS = 4
MOE_BIG = MOE_BIG_UNITS * MOE_BLK
MOE_LAST_MAX = 2 * MOE_BIG_UNITS - 1


def _moe_expert_kernel(boff_ref, nblk_ref, nused_ref, xs_ref, cws_ref, wg_ref, wu_ref, wd_ref, ys_hbm,
                       ybuf, ylast, zbuf, sem, st_ref, *, ntiles, tmp):
    q = pl.program_id(0)
    grp = q // ntiles
    tile = q - grp * ntiles
    off = boff_ref[q]
    n = nblk_ref[q]
    nmain = jnp.maximum(lax.shift_right_logical(n, 2) - 1, 0)
    nlast = n - MOE_BIG_UNITS * nmain
    base = tile * tmp
    wd = wd_ref[0].reshape(EPG * EXPERT_FF, D_MODEL)

    def experts(rows):
        xb = xs_ref[rows, :]
        cwb = cws_ref[rows, :]
        hs = []
        for e in range(EPG):
            gg_ = _mm(xb, wg_ref[0, e])
            uu = _mm(xb, wu_ref[0, e])
            hs.append((gg_ * _sigmoid(gg_) * uu * cwb[:, e:e + 1]).astype(BF16))
        return _mm(jnp.concatenate(hs, axis=1), wd).astype(BF16)

    def big_copy(slot, blk):
        return pltpu.make_async_copy(
            ybuf.at[slot], ys_hbm.at[pl.ds(pl.multiple_of(base + blk * MOE_BLK, MOE_BLK), MOE_BIG), :], sem.at[slot])

    def last_copy(units, blk):
        rows = units * MOE_BLK
        return pltpu.make_async_copy(
            ylast.at[0:rows], ys_hbm.at[pl.ds(pl.multiple_of(base + blk * MOE_BLK, MOE_BLK), rows), :], sem.at[2])

    def wait_last():
        for u in range(1, MOE_LAST_MAX + 1):
            @pl.when(st_ref[1] == u)
            def _(u=u):
                last_copy(u, 0).wait()

    def zero_copy(blk):
        return pltpu.make_async_copy(
            zbuf, ys_hbm.at[pl.ds(pl.multiple_of(base + blk * MOE_BLK, MOE_BLK), MOE_BLK), :], sem.at[3])

    @pl.when(q == 0)
    def _():
        for i in range(3):
            st_ref[i] = 0
        zbuf[...] = jnp.zeros_like(zbuf)

    def body(k, carry):
        c = st_ref[0]
        slot = jnp.bitwise_and(c, 1)
        blk = off + MOE_BIG_UNITS * k
        y = experts(pl.ds(pl.multiple_of(blk * MOE_BLK, MOE_BLK), MOE_BIG))

        @pl.when(c >= 2)
        def _():
            big_copy(slot, 0).wait()

        ybuf[slot] = y
        big_copy(slot, blk).start()
        st_ref[0] = c + 1
        return carry

    lax.fori_loop(0, nmain, body, 0)

    @pl.when(nlast > 0)
    def _():
        wait_last()

    lblk = off + MOE_BIG_UNITS * nmain
    for u in range(1, MOE_LAST_MAX + 1):
        @pl.when(nlast == u)
        def _(u=u):
            ylast[0:u * MOE_BLK, :] = experts(pl.ds(pl.multiple_of(lblk * MOE_BLK, MOE_BLK), u * MOE_BLK))
            last_copy(u, lblk).start()
            st_ref[1] = u

    @pl.when(grp == N_GROUPS - 1)
    def _():
        def zfill(blk, carry):
            zero_copy(blk).start()
            return carry

        nz = tmp // MOE_BLK - nused_ref[tile]
        lax.fori_loop(nused_ref[tile], tmp // MOE_BLK, zfill, 0)
        st_ref[2] = st_ref[2] + nz

    @pl.when(q == pl.num_programs(0) - 1)
    def _():
        c = st_ref[0]

        @pl.when(c >= 2)
        def _():
            big_copy(jnp.bitwise_and(c, 1), 0).wait()

        @pl.when(c >= 1)
        def _():
            big_copy(jnp.bitwise_and(c - 1, 1), 0).wait()

        wait_last()

        def zwait(i, carry):
            zero_copy(0).wait()
            return carry

        lax.fori_loop(0, st_ref[2], zwait, 0)


def _moe_unsort_kernel(ys_ref, pos_ref, x1_ref, mod_ref, lng_ref, lnb_ref, out_ref, *, nb, r):
    tm = nb * r
    tmp = tm + N_GROUPS * MOE_BLK
    d = D_MODEL
    ysb = ys_ref[...]
    posc = jnp.broadcast_to(pos_ref[0][0:1, :], (LANE, tm)).T
    lng = lng_ref[...]
    lnb = lnb_ref[...]
    pc = MOE_PC
    for c in range(tm // pc):
        slot = lax.broadcasted_iota(jnp.int32, (pc, tmp), 1).astype(F32)
        unperm = jnp.where(slot == posc[c * pc:(c + 1) * pc, 0:1], 1.0, 0.0).astype(BF16)
        y = _mm(unperm, ysb)
        if nb == 1:
            x1c = x1_ref[0, c * pc:(c + 1) * pc, :]
            gt2 = mod_ref[0][:, 5 * d:6 * d]
            out_ref[0, c * pc:(c + 1) * pc, :] = _layer_norm(DEEPNORM_ALPHA * x1c + gt2 * y, lng, lnb)
        else:
            cb = pc // r
            x1c = x1_ref[c * cb:(c + 1) * cb]
            gt2 = mod_ref[c * cb:(c + 1) * cb][:, :, 5 * d:6 * d]
            yy = DEEPNORM_ALPHA * x1c + gt2 * y.reshape(cb, r, d)
            out_ref[c * cb:(c + 1) * cb] = _layer_norm(yy, lng, lnb)


def _moe(streams, wrt, brt, wg, wu, wd, lng, lnb):
    d = D_MODEL
    tm = streams[0][2] * streams[0][3]
    tmp = tm + N_GROUPS * MOE_BLK
    const2 = lambda i: (0, 0)
    params = pltpu.CompilerParams(dimension_semantics=("arbitrary",), vmem_limit_bytes=VMEM_LIMIT)
    geo = []
    ntiles = 0
    for x1, _, nb, r in streams:
        assert nb * r == tm
        tpb = x1.shape[1] // r
        n = (x1.shape[0] // nb) * tpb
        geo.append((tpb, n, ntiles))
        ntiles += n

    sorted_shapes = [jax.ShapeDtypeStruct((ntiles * tmp, d), BF16),
                     jax.ShapeDtypeStruct((ntiles * tmp, LANE), F32),
                     jax.ShapeDtypeStruct((ntiles, EPG, tm), F32),
                     jax.ShapeDtypeStruct((ntiles, EPG, LANE), F32)]
    bufs = []
    for (x1, mod, nb, r), (tpb, n, t0) in zip(streams, geo):
        steps = ntiles if not bufs else n
        xmap = lambda i, tpb=tpb, n=n: (jnp.minimum(i, n - 1) // tpb, jnp.minimum(i, n - 1) % tpb, 0)
        mmap = lambda i, tpb=tpb, n=n: (jnp.minimum(i, n - 1) // tpb, 0, 0)
        bufs = pl.pallas_call(
            functools.partial(_moe_sort_kernel, nb=nb, r=r, nreal=n),
            out_shape=sorted_shapes,
            grid=(steps,),
            in_specs=[
                pl.BlockSpec((nb, r, d), xmap),
                pl.BlockSpec((nb, 1, 6 * d), mmap),
                pl.BlockSpec((LANE, d), const2),
                pl.BlockSpec((LANE, 1), const2),
            ] + [pl.BlockSpec(memory_space=pl.ANY)] * len(bufs),
            out_specs=[pl.BlockSpec((tmp, d), lambda i, t0=t0: (i + t0, 0)),
                       pl.BlockSpec((tmp, LANE), lambda i, t0=t0: (i + t0, 0)),
                       pl.BlockSpec((1, EPG, tm), lambda i, t0=t0: (i + t0, 0, 0)),
                       pl.BlockSpec((1, EPG, LANE), lambda i, t0=t0: (i + t0, 0, 0))],
            input_output_aliases={4 + k: k for k in range(len(bufs))},
            compiler_params=params,
            name="moe_sort",
        )(x1, mod, wrt, brt, *bufs)
    xs, cws, pos, cnt = bufs

    nblk = ((cnt[:, :N_GROUPS, 0].astype(jnp.int32) + (MOE_BLK - 1)) // MOE_BLK)
    boff = jnp.cumsum(nblk, axis=1) - nblk
    nused = jnp.sum(nblk, axis=1).astype(jnp.int32)
    nblk_q = nblk.T.reshape(-1).astype(jnp.int32)
    boff_q = boff.T.reshape(-1).astype(jnp.int32)

    tmap = lambda q, bo, nk, nu: (q % ntiles, 0)
    wmap = lambda q, bo, nk, nu: (q // ntiles, 0, 0, 0)
    ys = pl.pallas_call(
        functools.partial(_moe_expert_kernel, ntiles=ntiles, tmp=tmp),
        out_shape=jax.ShapeDtypeStruct((ntiles * tmp, d), BF16),
        grid_spec=pltpu.PrefetchScalarGridSpec(
            num_scalar_prefetch=3,
            grid=(N_GROUPS * ntiles,),
            in_specs=[
                pl.BlockSpec((tmp, d), tmap),
                pl.BlockSpec((tmp, LANE), tmap),
                pl.BlockSpec((1, EPG, d, EXPERT_FF), wmap),
                pl.BlockSpec((1, EPG, d, EXPERT_FF), wmap),
                pl.BlockSpec((1, EPG, EXPERT_FF, d), wmap),
            ],
            out_specs=pl.BlockSpec(memory_space=pl.ANY),
            scratch_shapes=[
                pltpu.VMEM((2, MOE_BIG, d), BF16),
                pltpu.VMEM((MOE_LAST_MAX * MOE_BLK, d), BF16),
                pltpu.VMEM((MOE_BLK, d), BF16),
                pltpu.SemaphoreType.DMA((4,)),
                pltpu.SMEM((4,), jnp.int32),
            ],
        ),
        compiler_params=params,
        name="moe_experts",
    )(boff_q, nblk_q, nused, xs, cws, wg, wu, wd)

    outs = []
    for (x1, mod, nb, r), (tpb, n, t0) in zip(streams, geo):
        xmap = lambda i, tpb=tpb: (i // tpb, i % tpb, 0)
        mmap = lambda i, tpb=tpb: (i // tpb, 0, 0)
        outs.append(pl.pallas_call(
            functools.partial(_moe_unsort_kernel, nb=nb, r=r),
            out_shape=jax.ShapeDtypeStruct(x1.shape, F32),
            grid=(n,),
            in_specs=[
                pl.BlockSpec((tmp, d), lambda i, t0=t0: (i + t0, 0)),
                pl.BlockSpec((1, EPG, tm), lambda i, t0=t0: (i + t0, 0, 0)),
                pl.BlockSpec((nb, r, d), xmap),
                pl.BlockSpec((nb, 1, 6 * d), mmap),
                pl.BlockSpec((1, d), const2),
                pl.BlockSpec((1, d), const2),
            ],
            out_specs=pl.BlockSpec((nb, r, d), xmap),
            compiler_params=params,
            name="moe_unsort",
        )(ys, pos, x1, mod, lng, lnb))
    return outs


def kernel(x_prompt, x_sample, c_prompt, c_sample, state_gla, cache_swa_k, cache_swa_v, ada_w, ada_b, w_in,
           gla_a2_w, gla_a2_b, gla_norm_w, swa_sinks, w_o, ln1_g, ln1_b, router_g_w, router_g_b, router_e_w,
           router_e_b, moe_w_gate, moe_w_up, moe_w_down, ln2_g, ln2_b):
    assert ada_w.shape[0] == 1
    bp = x_prompt.shape[0]
    bs, ss, d = x_sample.shape
    lc = cache_swa_k.shape[2]

    w = w_in[0]
    zpad = jnp.zeros((d, LANE - GLA_RANK), F32)
    w_sq = w[:, 1552:2064].reshape(d, SWA_KV_HEADS, SWA_GROUP, SWA_DH).transpose(0, 2, 1, 3).reshape(d, SWA_W)
    win = jnp.concatenate([w[:, 0:1536], w_sq, w[:, 2064:2320], w[:, 1536:1552], zpad], axis=1).astype(BF16)
    a2w = jnp.concatenate([gla_a2_w[0], jnp.zeros((LANE - GLA_RANK, GLA_KW), F32)], axis=0).astype(BF16)
    a2b = gla_a2_b[0].reshape(1, GLA_KW)
    gnw = gla_norm_w[0].reshape(1, GLA_DV)
    wo_swa = w_o[0][GLA_VW:].reshape(SWA_KV_HEADS, SWA_GROUP, SWA_DH, d).transpose(1, 0, 2, 3).reshape(SWA_W, d)
    wo = jnp.concatenate([w_o[0][:GLA_VW], wo_swa], axis=0).astype(BF16)
    sinks = swa_sinks[0]
    wrt = jnp.concatenate([router_g_w[0], jnp.zeros((d, R_EXP0 - N_GROUPS), F32),
                           jnp.transpose(router_e_w[0], (1, 0, 2)).reshape(d, N_GROUPS * EPG),
                           jnp.zeros((d, LANE - R_EXP0 - N_GROUPS * EPG), F32)], axis=1).T.astype(BF16)
    brt = jnp.concatenate([router_g_b[0], jnp.zeros((R_EXP0 - N_GROUPS,), F32), router_e_b[0].reshape(-1),
                           jnp.zeros((LANE - R_EXP0 - N_GROUPS * EPG,), F32)]).reshape(LANE, 1)
    lng1, lnb1 = ln1_g[0].reshape(1, d), ln1_b[0].reshape(1, d)
    lng2, lnb2 = ln2_g[0].reshape(1, d), ln2_b[0].reshape(1, d)

    mod = _adaln(jnp.concatenate([c_prompt, c_sample], axis=0), ada_w[0], ada_b[0].reshape(1, 6 * d))
    mod = mod.reshape(bp + bs, 1, 6 * d)
    mod_p, mod_s = mod[:bp], mod[bp:]

    x1p, s_p, k_p, v_p, wg, wu, wd = _mixer_prompt(
        x_prompt, mod_p, sinks, win, a2w, a2b, gnw, wo, lng1, lnb1,
        [moe_w_gate[0].reshape(-1, EXPERT_FF), moe_w_up[0].reshape(-1, EXPERT_FF), moe_w_down[0].reshape(-1, d)])
    wg = wg.reshape(N_GROUPS, EPG, d, EXPERT_FF)
    wu = wu.reshape(N_GROUPS, EPG, d, EXPERT_FF)
    wd = wd.reshape(N_GROUPS, EPG, EXPERT_FF, d)
    x1s, s_s, k_s, v_s = _mixer_sample(
        x_sample, mod_s, state_gla[0], cache_swa_k[0].reshape(bs, lc, LANE), cache_swa_v[0].reshape(bs, lc, LANE),
        sinks, win, a2w, a2b, gnw, wo, lng1, lnb1)

    assert MOE_TILE % ss == 0 and bs % (MOE_TILE // ss) == 0 and x_prompt.shape[1] % MOE_TILE == 0
    yp, ys = _moe([(x1p, mod_p, 1, MOE_TILE), (x1s, mod_s, MOE_TILE // ss, ss)], wrt, brt, wg, wu, wd, lng2, lnb2)

    kv_shape_p = (1, bp, WINDOW, SWA_KV_HEADS, SWA_DH)
    kv_shape_s = (1, bs, lc, SWA_KV_HEADS, SWA_DH)
    return (yp, ys, s_p[None], k_p.reshape(kv_shape_p), v_p.reshape(kv_shape_p),
            s_s[None], k_s.reshape(kv_shape_s), v_s.reshape(kv_shape_s))
```

```python
import functools

import jax
import jax.numpy as jnp
import numpy as np
from jax import lax
from jax.experimental import pallas as pl
from jax.experimental.pallas import tpu as pltpu

F32 = jnp.float32
BF16 = jnp.bfloat16

D_MODEL = 1024
CHUNK = 64
GLA_HEADS = 4
GLA_DK = 64
GLA_DV = 128
GLA_KW = GLA_HEADS * GLA_DK
GLA_VW = GLA_HEADS * GLA_DV
GLA_RANK = 16
GLA_TAU = 16.0
SWA_Q_HEADS = 8
SWA_KV_HEADS = 2
SWA_GROUP = 4
SWA_DH = 64
SWA_W = SWA_Q_HEADS * SWA_DH
WINDOW = 128
N_GROUPS = 4
EPG = 8
EXPERT_FF = 256
DEEPNORM_ALPHA = 2.0 ** 0.25
LN_EPS = 1e-5
NEG_INF = -1e30

C_GQ, C_GK, C_GV, C_GR, C_SQ, C_SK, C_SV, C_GA = 0, 256, 512, 1024, 1536, 2048, 2176, 2304
PROJ_W = 2432
LANE = 128
R_EXP0 = 8

V7X_VMEM_BYTES = 64 * 1024 * 1024
VMEM_LIMIT = V7X_VMEM_BYTES * 7 // 8
MIXER_TILE = 512
SAMPLE_SEQS_PER_STEP = 8
MOE_TILE = 1024


def _mm(a, b):
    return jnp.dot(a, b, preferred_element_type=F32)


def _mm_nt(a, b):
    return lax.dot_general(a, b, (((1,), (1,)), ((), ())), preferred_element_type=F32)


def _mm_tn(a, b):
    return lax.dot_general(a, b, (((0,), (0,)), ((), ())), preferred_element_type=F32)


def _split_bf16(a):
    hi = a.astype(BF16)
    lo = (a - hi.astype(F32)).astype(BF16)
    return hi, lo


def _sigmoid(x):
    return 1.0 / (1.0 + jnp.exp(-x))


def _layer_norm(y, g, b):
    mu = jnp.mean(y, axis=-1, keepdims=True)
    d = y - mu
    var = jnp.mean(d * d, axis=-1, keepdims=True)
    return d * lax.rsqrt(var + LN_EPS) * g + b


def _adaln_kernel(c_ref, w_ref, b_ref, o_ref):
    c = c_ref[...]
    a = c * _sigmoid(c)
    a_hi, a_lo = _split_bf16(a)
    w_hi, w_lo = _split_bf16(w_ref[...])
    o_ref[...] = _mm(a_hi, w_hi) + (_mm(a_hi, w_lo) + _mm(a_lo, w_hi)) + b_ref[...]


def _adaln(c_all, ada_w, ada_b):
    n = c_all.shape[0]
    bn = 1024
    return pl.pallas_call(
        _adaln_kernel,
        out_shape=jax.ShapeDtypeStruct((n, 6 * D_MODEL), F32),
        grid=(6 * D_MODEL // bn,),
        in_specs=[pl.BlockSpec((n, D_MODEL), lambda j: (0, 0)),
                  pl.BlockSpec((D_MODEL, bn), lambda j: (0, j)),
                  pl.BlockSpec((1, bn), lambda j: (0, j))],
        out_specs=pl.BlockSpec((n, bn), lambda j: (0, j)),
        compiler_params=pltpu.CompilerParams(dimension_semantics=("arbitrary",), vmem_limit_bytes=VMEM_LIMIT),
        name="adaln",
    )(c_all, ada_w, ada_b)


def _mixer_consts(nrows, chunk, qb, kw):
    r = np.arange(nrows)
    tri = ((r[:, None] // chunk == r[None, :] // chunk) & (r[:, None] >= r[None, :])).astype(np.float32)
    hs = np.arange(GLA_HEADS * chunk)
    mkk = (hs[:, None] // chunk == np.arange(GLA_KW)[None, :] // GLA_DK).astype(np.float32)
    mv = (hs[:, None] // chunk == np.arange(GLA_VW)[None, :] // GLA_DV).astype(np.float32)
    ms = (np.arange(GLA_VW)[:, None] // GLA_DV == np.arange(GLA_KW)[None, :] // GLA_DK).astype(np.float32)
    caus = (np.arange(chunk)[:, None] >= (hs[None, :] % chunk)).astype(np.float32)
    t = np.arange(qb)
    kj = np.arange(kw)
    cs = (t // chunk) * chunk
    kpos = kj[None, :] - WINDOW
    vis = (kpos >= cs[:, None] - WINDOW) & (kpos < cs[:, None] + chunk)
    dist = np.abs(t[:, None] + WINDOW - kj[None, :]).astype(np.float32)
    bias = np.zeros((SWA_KV_HEADS, SWA_GROUP * qb, kw), np.float32)
    for hk in range(SWA_KV_HEADS):
        for g in range(SWA_GROUP):
            slope = np.float32(2.0 ** (-(hk * SWA_GROUP + g + 1)))
            bias[hk, g * qb:(g + 1) * qb] = np.where(vis, -slope * dist, np.float32(2.0 * NEG_INF))
    return (jnp.asarray(tri, BF16), jnp.asarray(bias), jnp.asarray(mkk, BF16), jnp.asarray(mv, BF16),
            jnp.asarray(ms), jnp.asarray(caus))


def _modulate(x, mod):
    return (x * (1.0 + mod[:, D_MODEL:2 * D_MODEL]) + mod[:, 0:D_MODEL]).astype(BF16)


def _log_gates(proj_ref, a2w_ref, a2b_ref):
    ga = proj_ref[:, C_GA:C_GA + LANE].astype(BF16)
    z = _mm(ga, a2w_ref[...]) + a2b_ref[...]
    lsig = -(jnp.maximum(-z, 0.0) + jnp.log(1.0 + jnp.exp(-jnp.abs(z))))
    return lsig * (1.0 / GLA_TAU)


def _gla_prep(loga, tri_ref, proj_ref, chunk, qe_ref, qn_ref, ke_ref, kn_ref, kw_ref, eb_ref, vb_ref):
    nrows = loga.shape[0]
    la_hi, la_lo = _split_bf16(loga)
    tri = tri_ref[...]
    b = _mm(tri, la_hi) + _mm(tri, la_lo)
    b_end = jnp.concatenate(
        [jnp.broadcast_to(b[c * chunk + chunk - 1:(c + 1) * chunk, :], (chunk, GLA_KW))
         for c in range(nrows // chunk)], axis=0)
    eb = jnp.exp(b)
    ebn = jnp.exp(-b)
    wk = jnp.exp(b_end - b)
    q = proj_ref[:, C_GQ:C_GQ + GLA_KW] * (GLA_DK ** -0.5)
    k = proj_ref[:, C_GK:C_GK + GLA_KW]
    eb_ref[...] = eb
    qe_ref[...] = (q * eb).astype(BF16)
    qn_ref[...] = (q * ebn).astype(BF16)
    ke_ref[...] = (k * eb).astype(BF16)
    kn_ref[...] = (k * ebn).astype(BF16)
    kw_ref[...] = (k * wk).astype(BF16)
    vb_ref[...] = proj_ref[:, C_GV:C_GV + GLA_VW].astype(BF16)


def _gla_chunk(r0, chunk, qe_ref, qn_ref, ke_ref, kn_ref, kw_ref, eb_ref, vb_ref, sbt_ref, mkk, mv, ms, caus):
    rows = slice(r0, r0 + chunk)
    qe = qe_ref[rows, :]
    qn = qn_ref[rows, :]
    zero = jnp.zeros((), BF16)
    kn4 = jnp.where(mkk != 0, jnp.concatenate([kn_ref[rows, :]] * GLA_HEADS, axis=0), zero)
    ke4 = jnp.where(mkk != 0, jnp.concatenate([ke_ref[rows, :]] * GLA_HEADS, axis=0), zero)
    a_lo = _mm_nt(qe, kn4)
    a_up = _mm_nt(qn, ke4)
    a = jnp.where(caus != 0.0, a_lo, a_up).astype(BF16)
    v = vb_ref[rows, :]
    v4 = jnp.where(mv != 0, jnp.concatenate([v] * GLA_HEADS, axis=0), zero)
    sbt = sbt_ref[...]
    o = _mm(a, v4) + _mm_nt(qe, sbt.astype(BF16))
    ut = _mm_tn(v, kw_ref[rows, :])
    g_end = eb_ref[r0 + chunk - 1:r0 + chunk, :]
    sbt_ref[...] = g_end * sbt + jnp.where(ms != 0.0, ut, 0.0)
    return o


def _gla_post(o, gr, gnw):
    res = []
    for h in range(GLA_HEADS):
        oh = o[:, h * GLA_DV:(h + 1) * GLA_DV]
        ms_ = jnp.mean(oh * oh, axis=-1, keepdims=True)
        og = oh * lax.rsqrt(ms_ + LN_EPS) * gnw
        r = gr[:, h * GLA_DV:(h + 1) * GLA_DV]
        res.append((og * (r * _sigmoid(r))).astype(BF16))
    return res


def _swa_scores(sq, kwin, qb):
    low = lax.broadcasted_iota(jnp.int32, (qb, LANE), 1) < SWA_DH
    res = []
    for hk in range(SWA_KV_HEADS):
        keep = low if hk == 0 else jnp.logical_not(low)
        qs = jnp.concatenate([jnp.where(keep, sq[:, g * LANE:(g + 1) * LANE] * (SWA_DH ** -0.5), 0.0)
                              for g in range(SWA_GROUP)], axis=0).astype(BF16)
        res.append(_mm_nt(qs, kwin))
    return res


def _swa_finish(s, vwin, bias, sinks_ref, hk, extra_valid, qb):
    ok = bias > -1e29
    if extra_valid is not None:
        ok = ok & extra_valid
    s = jnp.where(ok, s + bias, NEG_INF)
    ps, denoms = [], []
    for g in range(SWA_GROUP):
        sg = s[g * qb:(g + 1) * qb, :]
        sink = sinks_ref[hk * SWA_GROUP + g]
        m = jnp.maximum(jnp.max(sg, axis=-1, keepdims=True), sink)
        pg = jnp.exp(sg - m)
        denoms.append(jnp.sum(pg, axis=-1, keepdims=True) + jnp.exp(sink - m))
        ps.append(pg.astype(BF16))
    o = _mm(jnp.concatenate(ps, axis=0), vwin)
    return o / jnp.concatenate(denoms, axis=0)


def _swa_merge(o0, o1, qb):
    low = lax.broadcasted_iota(jnp.int32, (qb, LANE), 1) < SWA_DH
    return [jnp.where(low, o0[g * qb:(g + 1) * qb, :], o1[g * qb:(g + 1) * qb, :]).astype(BF16)
            for g in range(SWA_GROUP)]


def _out_proj_ln(x, gt1, mixed_ref, wo_ref, ln_g, ln_b):
    mix = _mm(mixed_ref[...], wo_ref[...])
    return _layer_norm(DEEPNORM_ALPHA * x + gt1 * mix, ln_g, ln_b)


P_QB = 128
P_KW = WINDOW + P_QB


def _mixer_prompt_kernel(sinks_ref, x_ref, mod_ref, win_ref, a2w_ref, a2b_ref, gnw_ref, wo_ref, lng_ref, lnb_ref,
                         tri_ref, bias_ref, mkk_ref, mv_ref, ms_ref, caus_ref, cast_in0, cast_in1, cast_in2,
                         x1_ref, s_out_ref, k_out_ref, v_out_ref, cast_out0, cast_out1, cast_out2,
                         proj_ref, qe_ref, qn_ref, ke_ref, kn_ref, kw_ref, eb_ref, vb_ref, kbuf, vbuf, mixed_ref,
                         sbt_ref, *, tl):
    j = pl.program_id(1)
    nj = pl.num_programs(1)
    d = D_MODEL

    for src, dst in ((cast_in0, cast_out0), (cast_in1, cast_out1), (cast_in2, cast_out2)):
        dst[...] = src[...].astype(BF16)

    @pl.when(j == 0)
    def _():
        sbt_ref[...] = jnp.zeros_like(sbt_ref)
        kbuf[0:WINDOW, :] = jnp.zeros((WINDOW, LANE), BF16)
        vbuf[0:WINDOW, :] = jnp.zeros((WINDOW, LANE), BF16)

    x = x_ref[0]
    mod = mod_ref[0]
    proj_ref[...] = _mm(_modulate(x, mod), win_ref[...])
    loga = _log_gates(proj_ref, a2w_ref, a2b_ref)
    _gla_prep(loga, tri_ref, proj_ref, CHUNK, qe_ref, qn_ref, ke_ref, kn_ref, kw_ref, eb_ref, vb_ref)
    kbuf[WINDOW:WINDOW + tl, :] = proj_ref[:, C_SK:C_SK + LANE].astype(BF16)
    vbuf[WINDOW:WINDOW + tl, :] = proj_ref[:, C_SV:C_SV + LANE].astype(BF16)

    mkk = mkk_ref[...]
    mv = mv_ref[...]
    ms = ms_ref[...]
    caus = caus_ref[...]
    gnw = gnw_ref[...]
    kj = lax.broadcasted_iota(jnp.int32, (SWA_GROUP * P_QB, P_KW), 1)
    first_valid = kj >= jnp.where(j > 0, 0, WINDOW)

    def scores(p):
        q0 = p * P_QB
        return _swa_scores(proj_ref[q0:q0 + P_QB, C_SQ:C_SQ + SWA_W], kbuf[q0:q0 + P_KW, :], P_QB)

    def gla(r0):
        o = _gla_chunk(r0, CHUNK, qe_ref, qn_ref, ke_ref, kn_ref, kw_ref, eb_ref, vb_ref, sbt_ref, mkk, mv, ms, caus)
        og = _gla_post(o, proj_ref[r0:r0 + CHUNK, C_GR:C_GR + GLA_VW], gnw)
        for hh in range(GLA_HEADS):
            mixed_ref[r0:r0 + CHUNK, hh * GLA_DV:(hh + 1) * GLA_DV] = og[hh]

    for p in range(tl // P_QB):
        q0 = p * P_QB
        s = scores(p)
        vwin = vbuf[q0:q0 + P_KW, :]
        extra = first_valid if p == 0 else None
        o0 = _swa_finish(s[0], vwin, bias_ref[0], sinks_ref, 0, extra, P_QB)
        o1 = _swa_finish(s[1], vwin, bias_ref[1], sinks_ref, 1, extra, P_QB)
        for i, blk in enumerate(_swa_merge(o0, o1, P_QB)):
            mixed_ref[q0:q0 + P_QB, GLA_VW + i * LANE:GLA_VW + (i + 1) * LANE] = blk
        gla(q0)
        gla(q0 + CHUNK)

    x1_ref[0] = _out_proj_ln(x, mod[:, 2 * d:3 * d], mixed_ref, wo_ref, lng_ref[...], lnb_ref[...])

    kbuf[0:WINDOW, :] = kbuf[tl:tl + WINDOW, :]
    vbuf[0:WINDOW, :] = vbuf[tl:tl + WINDOW, :]

    @pl.when(j == nj - 1)
    def _():
        for hh in range(GLA_HEADS):
            s_out_ref[0, hh] = sbt_ref[hh * GLA_DV:(hh + 1) * GLA_DV, hh * GLA_DK:(hh + 1) * GLA_DK].T
        k_out_ref[0] = proj_ref[tl - WINDOW:tl, C_SK:C_SK + LANE]
        v_out_ref[0] = proj_ref[tl - WINDOW:tl, C_SV:C_SV + LANE]


def _mixer_prompt(x, mod, sinks, win, a2w, a2b, gnw, wo, lng, lnb, to_cast, *, tl=MIXER_TILE):
    b, l, d = x.shape
    nj = l // tl
    consts = _mixer_consts(tl, CHUNK, P_QB, P_KW)
    const2 = lambda i, j, s: (0, 0)
    const3 = lambda i, j, s: (0, 0, 0)
    cast_specs = []
    for a in to_cast:
        assert a.shape[0] % (b * nj * 16) == 0
        cast_specs.append(pl.BlockSpec((a.shape[0] // (b * nj), a.shape[1]), lambda i, j, s: (i * nj + j, 0)))
    grid_spec = pltpu.PrefetchScalarGridSpec(
        num_scalar_prefetch=1,
        grid=(b, nj),
        in_specs=[
            pl.BlockSpec((1, tl, d), lambda i, j, s: (i, j, 0)),
            pl.BlockSpec((1, 1, 6 * d), lambda i, j, s: (i, 0, 0)),
            pl.BlockSpec((d, PROJ_W), const2),
            pl.BlockSpec((LANE, GLA_KW), const2),
            pl.BlockSpec((1, GLA_KW), const2),
            pl.BlockSpec((1, GLA_DV), const2),
            pl.BlockSpec((d, d), const2),
            pl.BlockSpec((1, d), const2),
            pl.BlockSpec((1, d), const2),
            pl.BlockSpec((tl, tl), const2),
            pl.BlockSpec((SWA_KV_HEADS, SWA_GROUP * P_QB, P_KW), const3),
            pl.BlockSpec((GLA_HEADS * CHUNK, GLA_KW), const2),
            pl.BlockSpec((GLA_HEADS * CHUNK, GLA_VW), const2),
            pl.BlockSpec((GLA_VW, GLA_KW), const2),
            pl.BlockSpec((CHUNK, GLA_HEADS * CHUNK), const2),
        ] + cast_specs,
        out_specs=[
            pl.BlockSpec((1, tl, d), lambda i, j, s: (i, j, 0)),
            pl.BlockSpec((1, GLA_HEADS, GLA_DK, GLA_DV), lambda i, j, s: (i, 0, 0, 0)),
            pl.BlockSpec((1, WINDOW, LANE), lambda i, j, s: (i, 0, 0)),
            pl.BlockSpec((1, WINDOW, LANE), lambda i, j, s: (i, 0, 0)),
        ] + cast_specs,
        scratch_shapes=[
            pltpu.VMEM((tl, PROJ_W), F32),
            pltpu.VMEM((tl, GLA_KW), BF16),
            pltpu.VMEM((tl, GLA_KW), BF16),
            pltpu.VMEM((tl, GLA_KW), BF16),
            pltpu.VMEM((tl, GLA_KW), BF16),
            pltpu.VMEM((tl, GLA_KW), BF16),
            pltpu.VMEM((tl, GLA_KW), F32),
            pltpu.VMEM((tl, GLA_VW), BF16),
            pltpu.VMEM((WINDOW + tl, LANE), BF16),
            pltpu.VMEM((WINDOW + tl, LANE), BF16),
            pltpu.VMEM((tl, d), BF16),
            pltpu.VMEM((GLA_VW, GLA_KW), F32),
        ],
    )
    return pl.pallas_call(
        functools.partial(_mixer_prompt_kernel, tl=tl),
        out_shape=[
            jax.ShapeDtypeStruct((b, l, d), F32),
            jax.ShapeDtypeStruct((b, GLA_HEADS, GLA_DK, GLA_DV), F32),
            jax.ShapeDtypeStruct((b, WINDOW, LANE), F32),
            jax.ShapeDtypeStruct((b, WINDOW, LANE), F32),
        ] + [jax.ShapeDtypeStruct(a.shape, BF16) for a in to_cast],
        grid_spec=grid_spec,
        compiler_params=pltpu.CompilerParams(dimension_semantics=("arbitrary", "arbitrary"),
                                             vmem_limit_bytes=VMEM_LIMIT),
        name="mixer_prompt",
    )(sinks, x, mod, win, a2w, a2b, gnw, wo, lng, lnb, *consts, *to_cast)


def _mixer_sample_kernel(sinks_ref, x_ref, mod_ref, s0_ref, kc_ref, vc_ref, win_ref, a2w_ref, a2b_ref, gnw_ref,
                         wo_ref, lng_ref, lnb_ref, tri_ref, bias_ref, mkk_ref, mv_ref, ms_ref, caus_ref,
                         x1_ref, s_out_ref, k_out_ref, v_out_ref,
                         proj_ref, qe_ref, qn_ref, ke_ref, kn_ref, kw_ref, eb_ref, vb_ref, kbuf, vbuf, mixed_ref,
                         sbt_ref, xm_ref, *, nb, s):
    nkeys = WINDOW + s
    d = D_MODEL
    for bb in range(nb):
        m = mod_ref[bb]
        xm_ref[bb * s:(bb + 1) * s, :] = x_ref[bb] * (1.0 + m[:, d:2 * d]) + m[:, 0:d]
    proj_ref[...] = _mm(xm_ref[...].astype(BF16), win_ref[...])
    loga = _log_gates(proj_ref, a2w_ref, a2b_ref)
    _gla_prep(loga, tri_ref, proj_ref, s, qe_ref, qn_ref, ke_ref, kn_ref, kw_ref, eb_ref, vb_ref)

    mkk = mkk_ref[...]
    mv = mv_ref[...]
    ms = ms_ref[...]
    caus = caus_ref[...]
    gnw = gnw_ref[...]

    for bb in range(nb):
        r0 = bb * s
        rows = slice(r0, r0 + s)
        sbt_ref[...] = jnp.zeros_like(sbt_ref)
        for hh in range(GLA_HEADS):
            sbt_ref[hh * GLA_DV:(hh + 1) * GLA_DV, hh * GLA_DK:(hh + 1) * GLA_DK] = s0_ref[bb, hh].T
        kbuf[0:WINDOW, :] = kc_ref[bb].astype(BF16)
        vbuf[0:WINDOW, :] = vc_ref[bb].astype(BF16)
        kbuf[WINDOW:nkeys, :] = proj_ref[rows, C_SK:C_SK + LANE].astype(BF16)
        vbuf[WINDOW:nkeys, :] = proj_ref[rows, C_SV:C_SV + LANE].astype(BF16)
        sc = _swa_scores(proj_ref[rows, C_SQ:C_SQ + SWA_W], kbuf[...], s)
        vwin = vbuf[...]
        blocks = _swa_merge(_swa_finish(sc[0], vwin, bias_ref[0], sinks_ref, 0, None, s),
                            _swa_finish(sc[1], vwin, bias_ref[1], sinks_ref, 1, None, s), s)
        for i, blk in enumerate(blocks):
            mixed_ref[rows, GLA_VW + i * LANE:GLA_VW + (i + 1) * LANE] = blk
        o = _gla_chunk(r0, s, qe_ref, qn_ref, ke_ref, kn_ref, kw_ref, eb_ref, vb_ref, sbt_ref, mkk, mv, ms, caus)
        og = _gla_post(o, proj_ref[rows, C_GR:C_GR + GLA_VW], gnw)
        for hh in range(GLA_HEADS):
            mixed_ref[rows, hh * GLA_DV:(hh + 1) * GLA_DV] = og[hh]
            s_out_ref[bb, hh] = sbt_ref[hh * GLA_DV:(hh + 1) * GLA_DV, hh * GLA_DK:(hh + 1) * GLA_DK].T
        k_out_ref[bb, 0:WINDOW - s, :] = kc_ref[bb, s:WINDOW, :]
        v_out_ref[bb, 0:WINDOW - s, :] = vc_ref[bb, s:WINDOW, :]
        k_out_ref[bb, WINDOW - s:WINDOW, :] = proj_ref[rows, C_SK:C_SK + LANE]
        v_out_ref[bb, WINDOW - s:WINDOW, :] = proj_ref[rows, C_SV:C_SV + LANE]

    mix = _mm(mixed_ref[...], wo_ref[...])
    lng = lng_ref[...]
    lnb = lnb_ref[...]
    for bb in range(nb):
        m = mod_ref[bb]
        y = DEEPNORM_ALPHA * x_ref[bb] + m[:, 2 * d:3 * d] * mix[bb * s:(bb + 1) * s, :]
        x1_ref[bb] = _layer_norm(y, lng, lnb)


def _mixer_sample(x, mod, s0, kc, vc, sinks, win, a2w, a2b, gnw, wo, lng, lnb, *, nb=SAMPLE_SEQS_PER_STEP):
    b, s, d = x.shape
    assert kc.shape[1] == WINDOW and s <= WINDOW
    rows = nb * s
    nkeys = WINDOW + s
    consts = _mixer_consts(rows, s, s, nkeys)
    const2 = lambda i, sk: (0, 0)
    const3 = lambda i, sk: (0, 0, 0)
    grid_spec = pltpu.PrefetchScalarGridSpec(
        num_scalar_prefetch=1,
        grid=(b // nb,),
        in_specs=[
            pl.BlockSpec((nb, s, d), lambda i, sk: (i, 0, 0)),
            pl.BlockSpec((nb, 1, 6 * d), lambda i, sk: (i, 0, 0)),
            pl.BlockSpec((nb, GLA_HEADS, GLA_DK, GLA_DV), lambda i, sk: (i, 0, 0, 0)),
            pl.BlockSpec((nb, WINDOW, LANE), lambda i, sk: (i, 0, 0)),
            pl.BlockSpec((nb, WINDOW, LANE), lambda i, sk: (i, 0, 0)),
            pl.BlockSpec((d, PROJ_W), const2),
            pl.BlockSpec((LANE, GLA_KW), const2),
            pl.BlockSpec((1, GLA_KW), const2),
            pl.BlockSpec((1, GLA_DV), const2),
            pl.BlockSpec((d, d), const2),
            pl.BlockSpec((1, d), const2),
            pl.BlockSpec((1, d), const2),
            pl.BlockSpec((rows, rows), const2),
            pl.BlockSpec((SWA_KV_HEADS, SWA_GROUP * s, nkeys), const3),
            pl.BlockSpec((GLA_HEADS * s, GLA_KW), const2),
            pl.BlockSpec((GLA_HEADS * s, GLA_VW), const2),
            pl.BlockSpec((GLA_VW, GLA_KW), const2),
            pl.BlockSpec((s, GLA_HEADS * s), const2),
        ],
        out_specs=[
            pl.BlockSpec((nb, s, d), lambda i, sk: (i, 0, 0)),
            pl.BlockSpec((nb, GLA_HEADS, GLA_DK, GLA_DV), lambda i, sk: (i, 0, 0, 0)),
            pl.BlockSpec((nb, WINDOW, LANE), lambda i, sk: (i, 0, 0)),
            pl.BlockSpec((nb, WINDOW, LANE), lambda i, sk: (i, 0, 0)),
        ],
        scratch_shapes=[
            pltpu.VMEM((rows, PROJ_W), F32),
            pltpu.VMEM((rows, GLA_KW), BF16),
            pltpu.VMEM((rows, GLA_KW), BF16),
            pltpu.VMEM((rows, GLA_KW), BF16),
            pltpu.VMEM((rows, GLA_KW), BF16),
            pltpu.VMEM((rows, GLA_KW), BF16),
            pltpu.VMEM((rows, GLA_KW), F32),
            pltpu.VMEM((rows, GLA_VW), BF16),
            pltpu.VMEM((nkeys, LANE), BF16),
            pltpu.VMEM((nkeys, LANE), BF16),
            pltpu.VMEM((rows, d), BF16),
            pltpu.VMEM((GLA_VW, GLA_KW), F32),
            pltpu.VMEM((rows, d), F32),
        ],
    )
    return pl.pallas_call(
        functools.partial(_mixer_sample_kernel, nb=nb, s=s),
        out_shape=[
            jax.ShapeDtypeStruct((b, s, d), F32),
            jax.ShapeDtypeStruct((b, GLA_HEADS, GLA_DK, GLA_DV), F32),
            jax.ShapeDtypeStruct((b, WINDOW, LANE), F32),
            jax.ShapeDtypeStruct((b, WINDOW, LANE), F32),
        ],
        grid_spec=grid_spec,
        compiler_params=pltpu.CompilerParams(dimension_semantics=("arbitrary",),
                                             vmem_limit_bytes=VMEM_LIMIT),
        name="mixer_sample",
    )(sinks, x, mod, s0, kc, vc, win, a2w, a2b, gnw, wo, lng, lnb, *consts)


MOE_BLK = 64


def _route_t(logits_t):
    t = logits_t.shape[1]
    row = lax.broadcasted_iota(jnp.int32, (EPG, t), 0).astype(F32)
    big = 99.0
    gl = jnp.where(row < N_GROUPS, logits_t[0:EPG, :], -jnp.inf)
    gmax = jnp.max(gl, axis=0, keepdims=True)
    grp = jnp.min(jnp.where(gl == gmax, row, big), axis=0, keepdims=True)
    p_grp = 1.0 / jnp.sum(jnp.exp(gl - gmax), axis=0, keepdims=True)
    el = jnp.zeros((EPG, t), F32)
    for g in range(N_GROUPS):
        el = el + jnp.where(grp == float(g), logits_t[R_EXP0 + EPG * g:R_EXP0 + EPG * (g + 1), :], 0.0)
    v1 = jnp.max(el, axis=0, keepdims=True)
    i1 = jnp.min(jnp.where(el == v1, row, big), axis=0, keepdims=True)
    el2 = jnp.where(row == i1, -jnp.inf, el)
    v2 = jnp.max(el2, axis=0, keepdims=True)
    i2 = jnp.min(jnp.where(el2 == v2, row, big), axis=0, keepdims=True)
    e2 = jnp.exp(v2 - v1)
    w1 = p_grp / (1.0 + e2)
    w2 = p_grp * e2 / (1.0 + e2)
    cw = jnp.where(row == i1, w1, 0.0) + jnp.where(row == i2, w2, 0.0)
    return grp, cw


MOE_PC = 256


def _moe_sort_kernel(x1_ref, mod_ref, wrt_ref, brt_ref, *refs, nb, r, nreal):
    outs = refs[-4:]

    @pl.when(pl.program_id(0) < nreal)
    def _():
        _moe_sort_tile(x1_ref, mod_ref, wrt_ref, brt_ref, *outs, nb=nb, r=r)

    @pl.when(pl.program_id(0) >= nreal)
    def _():
        for o in outs:
            o[...] = jnp.zeros_like(o)


def _moe_sort_tile(x1_ref, mod_ref, wrt_ref, brt_ref, xs_ref, cws_ref, pos_ref, cnt_ref, *, nb, r):
    tm = nb * r
    tmp = tm + N_GROUPS * MOE_BLK
    d = D_MODEL
    mod = mod_ref[...]
    t3 = x1_ref[...] * (1.0 + mod[:, :, 4 * d:5 * d]) + mod[:, :, 3 * d:4 * d]
    t = t3.reshape(tm, d).astype(BF16)
    grp, cw = _route_t(_mm_nt(wrt_ref[...], t) + brt_ref[...])
    row = lax.broadcasted_iota(jnp.int32, (EPG, tm), 0).astype(F32)
    onehot_g = jnp.where(row == grp, 1.0, 0.0)
    nch = tm // LANE
    strict = (lax.broadcasted_iota(jnp.int32, (LANE, LANE), 0)
              < lax.broadcasted_iota(jnp.int32, (LANE, LANE), 1)).astype(F32).astype(BF16)
    stacked = jnp.concatenate([onehot_g[:, c * LANE:(c + 1) * LANE] for c in range(nch)], axis=0)
    pref = _mm(stacked.astype(BF16), strict)
    tot = jnp.sum(stacked, axis=1, keepdims=True)
    cnt = jnp.zeros((EPG, 1), F32)
    ranks = []
    for c in range(nch):
        ranks.append(pref[c * EPG:(c + 1) * EPG, :] + cnt)
        cnt = cnt + tot[c * EPG:(c + 1) * EPG, :]
    rank = jnp.concatenate(ranks, axis=1)
    padded = jnp.floor((cnt + (MOE_BLK - 1)) * (1.0 / MOE_BLK)) * MOE_BLK
    rowc = lax.broadcasted_iota(jnp.int32, (EPG, 1), 0)
    off = jnp.zeros((EPG, 1), F32)
    for gg in range(N_GROUPS - 1):
        off = off + jnp.where(rowc > gg, padded[gg:gg + 1, :], 0.0)
    pos = jnp.sum(onehot_g * (off + rank), axis=0, keepdims=True)
    pos_ref[0] = jnp.broadcast_to(pos, (EPG, tm))
    cnt_ref[0] = jnp.broadcast_to(cnt, (EPG, LANE))
    cw_hi, cw_lo = _split_bf16(jnp.concatenate([cw, jnp.zeros((LANE - EPG, tm), F32)], axis=0).T)
    t_aug = jnp.concatenate([t, cw_hi, cw_lo], axis=1)
    pc = MOE_PC
    for c in range(tmp // pc):
        slot = (lax.broadcasted_iota(jnp.int32, (pc, tm), 0) + c * pc).astype(F32)
        perm = jnp.where(slot == pos, 1.0, 0.0).astype(BF16)
        moved = _mm(perm, t_aug)
        xs_ref[c * pc:(c + 1) * pc, :] = moved[:, 0:d].astype(BF16)
        cws_ref[c * pc:(c + 1) * pc, :] = moved[:, d:d + LANE] + moved[:, d + LANE:d + 2 * LANE]


MOE_BIG_UNITS = 4
MOE_BIG = MOE_BIG_UNITS * MOE_BLK
MOE_LAST_MAX = 2 * MOE_BIG_UNITS - 1


def _moe_expert_kernel(boff_ref, nblk_ref, nused_ref, xs_ref, cws_ref, wg_ref, wu_ref, wd_ref, ys_hbm,
                       ybuf, ylast, zbuf, sem, st_ref, *, ntiles, tmp):
    q = pl.program_id(0)
    grp = q // ntiles
    tile = q - grp * ntiles
    off = boff_ref[q]
    n = nblk_ref[q]
    nmain = jnp.maximum(lax.shift_right_logical(n, 2) - 1, 0)
    nlast = n - MOE_BIG_UNITS * nmain
    base = tile * tmp
    wd = wd_ref[0].reshape(EPG * EXPERT_FF, D_MODEL)

    def experts(rows):
        xb = xs_ref[rows, :]
        cwb = cws_ref[rows, :]
        hs = []
        for e in range(EPG):
            gg_ = _mm(xb, wg_ref[0, e])
            uu = _mm(xb, wu_ref[0, e])
            hs.append((gg_ * _sigmoid(gg_) * uu * cwb[:, e:e + 1]).astype(BF16))
        return _mm(jnp.concatenate(hs, axis=1), wd).astype(BF16)

    def big_copy(slot, blk):
        return pltpu.make_async_copy(
            ybuf.at[slot], ys_hbm.at[pl.ds(pl.multiple_of(base + blk * MOE_BLK, MOE_BLK), MOE_BIG), :], sem.at[slot])

    def last_copy(units, blk):
        rows = units * MOE_BLK
        return pltpu.make_async_copy(
            ylast.at[0:rows], ys_hbm.at[pl.ds(pl.multiple_of(base + blk * MOE_BLK, MOE_BLK), rows), :], sem.at[2])

    def wait_last():
        for u in range(1, MOE_LAST_MAX + 1):
            @pl.when(st_ref[1] == u)
            def _(u=u):
                last_copy(u, 0).wait()

    def zero_copy(blk):
        return pltpu.make_async_copy(
            zbuf, ys_hbm.at[pl.ds(pl.multiple_of(base + blk * MOE_BLK, MOE_BLK), MOE_BLK), :], sem.at[3])

    @pl.when(q == 0)
    def _():
        for i in range(3):
            st_ref[i] = 0
        zbuf[...] = jnp.zeros_like(zbuf)

    def body(k, carry):
        c = st_ref[0]
        slot = jnp.bitwise_and(c, 1)
        blk = off + MOE_BIG_UNITS * k
        y = experts(pl.ds(pl.multiple_of(blk * MOE_BLK, MOE_BLK), MOE_BIG))

        @pl.when(c >= 2)
        def _():
            big_copy(slot, 0).wait()

        ybuf[slot] = y
        big_copy(slot, blk).start()
        st_ref[0] = c + 1
        return carry

    lax.fori_loop(0, nmain, body, 0)

    @pl.when(nlast > 0)
    def _():
        wait_last()

    lblk = off + MOE_BIG_UNITS * nmain
    for u in range(1, MOE_LAST_MAX + 1):
        @pl.when(nlast == u)
        def _(u=u):
            ylast[0:u * MOE_BLK, :] = experts(pl.ds(pl.multiple_of(lblk * MOE_BLK, MOE_BLK), u * MOE_BLK))
            last_copy(u, lblk).start()
            st_ref[1] = u

    @pl.when(grp == N_GROUPS - 1)
    def _():
        def zfill(blk, carry):
            zero_copy(blk).start()
            return carry

        nz = tmp // MOE_BLK - nused_ref[tile]
        lax.fori_loop(nused_ref[tile], tmp // MOE_BLK, zfill, 0)
        st_ref[2] = st_ref[2] + nz

    @pl.when(q == pl.num_programs(0) - 1)
    def _():
        c = st_ref[0]

        @pl.when(c >= 2)
        def _():
            big_copy(jnp.bitwise_and(c, 1), 0).wait()

        @pl.when(c >= 1)
        def _():
            big_copy(jnp.bitwise_and(c - 1, 1), 0).wait()

        wait_last()

        def zwait(i, carry):
            zero_copy(0).wait()
            return carry

        lax.fori_loop(0, st_ref[2], zwait, 0)


def _moe_unsort_kernel(ys_ref, pos_ref, x1_ref, mod_ref, lng_ref, lnb_ref, out_ref, *, nb, r):
    tm = nb * r
    tmp = tm + N_GROUPS * MOE_BLK
    d = D_MODEL
    ysb = ys_ref[...]
    posc = jnp.broadcast_to(pos_ref[0][0:1, :], (LANE, tm)).T
    lng = lng_ref[...]
    lnb = lnb_ref[...]
    pc = MOE_PC
    for c in range(tm // pc):
        slot = lax.broadcasted_iota(jnp.int32, (pc, tmp), 1).astype(F32)
        unperm = jnp.where(slot == posc[c * pc:(c + 1) * pc, 0:1], 1.0, 0.0).astype(BF16)
        y = _mm(unperm, ysb)
        if nb == 1:
            x1c = x1_ref[0, c * pc:(c + 1) * pc, :]
            gt2 = mod_ref[0][:, 5 * d:6 * d]
            out_ref[0, c * pc:(c + 1) * pc, :] = _layer_norm(DEEPNORM_ALPHA * x1c + gt2 * y, lng, lnb)
        else:
            cb = pc // r
            x1c = x1_ref[c * cb:(c + 1) * cb]
            gt2 = mod_ref[c * cb:(c + 1) * cb][:, :, 5 * d:6 * d]
            yy = DEEPNORM_ALPHA * x1c + gt2 * y.reshape(cb, r, d)
            out_ref[c * cb:(c + 1) * cb] = _layer_norm(yy, lng, lnb)


def _moe(streams, wrt, brt, wg, wu, wd, lng, lnb):
    d = D_MODEL
    tm = streams[0][2] * streams[0][3]
    tmp = tm + N_GROUPS * MOE_BLK
    const2 = lambda i: (0, 0)
    params = pltpu.CompilerParams(dimension_semantics=("arbitrary",), vmem_limit_bytes=VMEM_LIMIT)
    geo = []
    ntiles = 0
    for x1, _, nb, r in streams:
        assert nb * r == tm
        tpb = x1.shape[1] // r
        n = (x1.shape[0] // nb) * tpb
        geo.append((tpb, n, ntiles))
        ntiles += n

    sorted_shapes = [jax.ShapeDtypeStruct((ntiles * tmp, d), BF16),
                     jax.ShapeDtypeStruct((ntiles * tmp, LANE), F32),
                     jax.ShapeDtypeStruct((ntiles, EPG, tm), F32),
                     jax.ShapeDtypeStruct((ntiles, EPG, LANE), F32)]
    bufs = []
    for (x1, mod, nb, r), (tpb, n, t0) in zip(streams, geo):
        steps = ntiles if not bufs else n
        xmap = lambda i, tpb=tpb, n=n: (jnp.minimum(i, n - 1) // tpb, jnp.minimum(i, n - 1) % tpb, 0)
        mmap = lambda i, tpb=tpb, n=n: (jnp.minimum(i, n - 1) // tpb, 0, 0)
        bufs = pl.pallas_call(
            functools.partial(_moe_sort_kernel, nb=nb, r=r, nreal=n),
            out_shape=sorted_shapes,
            grid=(steps,),
            in_specs=[
                pl.BlockSpec((nb, r, d), xmap),
                pl.BlockSpec((nb, 1, 6 * d), mmap),
                pl.BlockSpec((LANE, d), const2),
                pl.BlockSpec((LANE, 1), const2),
            ] + [pl.BlockSpec(memory_space=pl.ANY)] * len(bufs),
            out_specs=[pl.BlockSpec((tmp, d), lambda i, t0=t0: (i + t0, 0)),
                       pl.BlockSpec((tmp, LANE), lambda i, t0=t0: (i + t0, 0)),
                       pl.BlockSpec((1, EPG, tm), lambda i, t0=t0: (i + t0, 0, 0)),
                       pl.BlockSpec((1, EPG, LANE), lambda i, t0=t0: (i + t0, 0, 0))],
            input_output_aliases={4 + k: k for k in range(len(bufs))},
            compiler_params=params,
            name="moe_sort",
        )(x1, mod, wrt, brt, *bufs)
    xs, cws, pos, cnt = bufs

    nblk = ((cnt[:, :N_GROUPS, 0].astype(jnp.int32) + (MOE_BLK - 1)) // MOE_BLK)
    boff = jnp.cumsum(nblk, axis=1) - nblk
    nused = jnp.sum(nblk, axis=1).astype(jnp.int32)
    nblk_q = nblk.T.reshape(-1).astype(jnp.int32)
    boff_q = boff.T.reshape(-1).astype(jnp.int32)

    tmap = lambda q, bo, nk, nu: (q % ntiles, 0)
    wmap = lambda q, bo, nk, nu: (q // ntiles, 0, 0, 0)
    ys = pl.pallas_call(
        functools.partial(_moe_expert_kernel, ntiles=ntiles, tmp=tmp),
        out_shape=jax.ShapeDtypeStruct((ntiles * tmp, d), BF16),
        grid_spec=pltpu.PrefetchScalarGridSpec(
            num_scalar_prefetch=3,
            grid=(N_GROUPS * ntiles,),
            in_specs=[
                pl.BlockSpec((tmp, d), tmap),
                pl.BlockSpec((tmp, LANE), tmap),
                pl.BlockSpec((1, EPG, d, EXPERT_FF), wmap),
                pl.BlockSpec((1, EPG, d, EXPERT_FF), wmap),
                pl.BlockSpec((1, EPG, EXPERT_FF, d), wmap),
            ],
            out_specs=pl.BlockSpec(memory_space=pl.ANY),
            scratch_shapes=[
                pltpu.VMEM((2, MOE_BIG, d), BF16),
                pltpu.VMEM((MOE_LAST_MAX * MOE_BLK, d), BF16),
                pltpu.VMEM((MOE_BLK, d), BF16),
                pltpu.SemaphoreType.DMA((4,)),
                pltpu.SMEM((4,), jnp.int32),
            ],
        ),
        compiler_params=params,
        name="moe_experts",
    )(boff_q, nblk_q, nused, xs, cws, wg, wu, wd)

    outs = []
    for (x1, mod, nb, r), (tpb, n, t0) in zip(streams, geo):
        xmap = lambda i, tpb=tpb: (i // tpb, i % tpb, 0)
        mmap = lambda i, tpb=tpb: (i // tpb, 0, 0)
        outs.append(pl.pallas_call(
            functools.partial(_moe_unsort_kernel, nb=nb, r=r),
            out_shape=jax.ShapeDtypeStruct(x1.shape, F32),
            grid=(n,),
            in_specs=[
                pl.BlockSpec((tmp, d), lambda i, t0=t0: (i + t0, 0)),
                pl.BlockSpec((1, EPG, tm), lambda i, t0=t0: (i + t0, 0, 0)),
                pl.BlockSpec((nb, r, d), xmap),
                pl.BlockSpec((nb, 1, 6 * d), mmap),
                pl.BlockSpec((1, d), const2),
                pl.BlockSpec((1, d), const2),
            ],
            out_specs=pl.BlockSpec((nb, r, d), xmap),
            compiler_params=params,
            name="moe_unsort",
        )(ys, pos, x1, mod, lng, lnb))
    return outs


def kernel(x_prompt, x_sample, c_prompt, c_sample, state_gla, cache_swa_k, cache_swa_v, ada_w, ada_b, w_in,
           gla_a2_w, gla_a2_b, gla_norm_w, swa_sinks, w_o, ln1_g, ln1_b, router_g_w, router_g_b, router_e_w,
           router_e_b, moe_w_gate, moe_w_up, moe_w_down, ln2_g, ln2_b):
    assert ada_w.shape[0] == 1
    bp = x_prompt.shape[0]
    bs, ss, d = x_sample.shape
    lc = cache_swa_k.shape[2]

    w = w_in[0].astype(BF16)
    zpad = jnp.zeros((d, LANE - GLA_RANK), BF16)
    w_sq = w[:, 1552:2064].reshape(d, SWA_KV_HEADS, SWA_GROUP, SWA_DH).transpose(0, 2, 1, 3).reshape(d, SWA_W)
    win = jnp.concatenate([w[:, 0:1536], w_sq, w[:, 2064:2320], w[:, 1536:1552], zpad], axis=1)
    a2w = jnp.concatenate([gla_a2_w[0], jnp.zeros((LANE - GLA_RANK, GLA_KW), F32)], axis=0).astype(BF16)
    a2b = gla_a2_b[0].reshape(1, GLA_KW)
    gnw = gla_norm_w[0].reshape(1, GLA_DV)
    wo_b = w_o[0].astype(BF16)
    wo_swa = wo_b[GLA_VW:].reshape(SWA_KV_HEADS, SWA_GROUP, SWA_DH, d).transpose(1, 0, 2, 3).reshape(SWA_W, d)
    wo = jnp.concatenate([wo_b[:GLA_VW], wo_swa], axis=0)
    sinks = swa_sinks[0]
    wrt = jnp.concatenate([router_g_w[0], jnp.zeros((d, R_EXP0 - N_GROUPS), F32),
                           jnp.transpose(router_e_w[0], (1, 0, 2)).reshape(d, N_GROUPS * EPG),
                           jnp.zeros((d, LANE - R_EXP0 - N_GROUPS * EPG), F32)], axis=1).T.astype(BF16)
    brt = jnp.concatenate([router_g_b[0], jnp.zeros((R_EXP0 - N_GROUPS,), F32), router_e_b[0].reshape(-1),
                           jnp.zeros((LANE - R_EXP0 - N_GROUPS * EPG,), F32)]).reshape(LANE, 1)
    lng1, lnb1 = ln1_g[0].reshape(1, d), ln1_b[0].reshape(1, d)
    lng2, lnb2 = ln2_g[0].reshape(1, d), ln2_b[0].reshape(1, d)

    mod = _adaln(jnp.concatenate([c_prompt, c_sample], axis=0), ada_w[0], ada_b[0].reshape(1, 6 * d))
    mod = mod.reshape(bp + bs, 1, 6 * d)
    mod_p, mod_s = mod[:bp], mod[bp:]

    x1p, s_p, k_p, v_p, wg, wu, wd = _mixer_prompt(
        x_prompt, mod_p, sinks, win, a2w, a2b, gnw, wo, lng1, lnb1,
        [moe_w_gate[0].reshape(-1, EXPERT_FF), moe_w_up[0].reshape(-1, EXPERT_FF), moe_w_down[0].reshape(-1, d)])
    wg = wg.reshape(N_GROUPS, EPG, d, EXPERT_FF)
    wu = wu.reshape(N_GROUPS, EPG, d, EXPERT_FF)
    wd = wd.reshape(N_GROUPS, EPG, EXPERT_FF, d)
    x1s, s_s, k_s, v_s = _mixer_sample(
        x_sample, mod_s, state_gla[0], cache_swa_k[0].reshape(bs, lc, LANE), cache_swa_v[0].reshape(bs, lc, LANE),
        sinks, win, a2w, a2b, gnw, wo, lng1, lnb1)

    assert MOE_TILE % ss == 0 and bs % (MOE_TILE // ss) == 0 and x_prompt.shape[1] % MOE_TILE == 0
    yp, ys = _moe([(x1p, mod_p, 1, MOE_TILE), (x1s, mod_s, MOE_TILE // ss, ss)], wrt, brt, wg, wu, wd, lng2, lnb2)

    kv_shape_p = (1, bp, WINDOW, SWA_KV_HEADS, SWA_DH)
    kv_shape_s = (1, bs, lc, SWA_KV_HEADS, SWA_DH)
    return (yp, ys, s_p[None], k_p.reshape(kv_shape_p), v_p.reshape(kv_shape_p),
            s_s[None], k_s.reshape(kv_shape_s), v_s.reshape(kv_shape_s))
```
